```python
import math
import numpy as np
import jax
import jax.numpy as jnp
from jax import lax

D_MODEL = 1024
BATCH = 8
SEQ = 2048
DEPTH = 1

GRID_W = 64
CTX_LEN = 256
MIX_WIDTH = D_MODEL
MOD_CHUNKS = 6
EPS = 1e-6

ATTN_HEADS = 8
ATTN_KV_HEADS = 2
ATTN_HEAD_DIM = (MIX_WIDTH // 2) // ATTN_HEADS
ROPE_BASE = 10000.0
Q_BLOCK = 128

DN_HEADS = 4
DN_HEAD_DIM = (MIX_WIDTH // 2) // DN_HEADS
DN_CONV_W = 5
DN_CHUNK = 64

N_EXPERTS = 16
EC_CAPACITY_FACTOR = 2
EXPERT_FF = 1024

ATTN_Q_W = ATTN_HEADS * ATTN_HEAD_DIM
ATTN_KV_W = ATTN_KV_HEADS * ATTN_HEAD_DIM
DN_W = DN_HEADS * DN_HEAD_DIM
IN_SPLITS = (ATTN_Q_W, ATTN_KV_W, ATTN_KV_W, DN_W, DN_W, DN_W, DN_W, 2 * DN_HEADS, 2 * DN_HEADS)
IN_WIDTH = sum(IN_SPLITS)
IN_OFFSETS = tuple(int(v) for v in np.cumsum(IN_SPLITS)[:-1])

kernel_name = "hybrid_dit_gqa_gdn_ec_moe"


def rmsnorm(x, w):
    xf = x.astype(jnp.float32)
    xf = xf * lax.rsqrt(jnp.mean(xf * xf, axis=-1, keepdims=True) + EPS)
    return xf.astype(x.dtype) * w


def l2norm(x):
    xf = x.astype(jnp.float32)
    return (xf * lax.rsqrt(jnp.sum(xf * xf, axis=-1, keepdims=True) + EPS)).astype(x.dtype)


def modulate(h, shift, scale):
    return h * (1 + scale) + shift


def split_heads(t, n_heads):
    b, l, w = t.shape
    return t.reshape(b, l, n_heads, w // n_heads).transpose(0, 2, 1, 3)


def merge_heads(t):
    b, n, l, d = t.shape
    return t.transpose(0, 2, 1, 3).reshape(b, l, n * d)


def rope_axis(x, pos):
    m = x.shape[-1] // 2
    freqs = ROPE_BASE ** (-jnp.arange(m, dtype=jnp.float32) / m)
    ang = pos.astype(jnp.float32)[:, None] * freqs[None, :]
    cos = jnp.cos(ang).astype(x.dtype)
    sin = jnp.sin(ang).astype(x.dtype)
    x1, x2 = x[..., :m], x[..., m:]
    return jnp.concatenate([x1 * cos - x2 * sin, x2 * cos + x1 * sin], axis=-1)


def rope2d(x, rows, cols):
    h = x.shape[-1] // 2
    return jnp.concatenate([rope_axis(x[..., :h], rows), rope_axis(x[..., h:], cols)], axis=-1)


def attend_blocked(q, k, v):
    b, hq, lq, dh = q.shape
    hkv = k.shape[1]
    grp = hq // hkv
    nb = lq // Q_BLOCK
    qb = q.reshape(b, hkv, grp, nb, Q_BLOCK, dh).transpose(3, 0, 1, 2, 4, 5)
    scale = dh ** -0.5

    def one_block(qblk):
        s = jnp.einsum('bkgqd,bksd->bkgqs', qblk, k).astype(jnp.float32) * scale
        p = jax.nn.softmax(s, axis=-1).astype(v.dtype)
        return jnp.einsum('bkgqs,bksd->bkgqd', p, v)

    ob = lax.map(one_block, qb)
    return ob.transpose(1, 2, 3, 0, 4, 5).reshape(b, hq, lq, dh)


def short_conv(x, w):
    pad = DN_CONV_W // 2
    return lax.conv_general_dilated(
        x, w[:, None, :], window_strides=(1,), padding=[(pad, pad)],
        dimension_numbers=('NWC', 'WIO', 'NWC'), feature_group_count=x.shape[-1])


def delta_rule_chunked(q, k, v, g, beta, s0, need_output):
    f32 = jnp.float32
    b, h, l, dk = q.shape
    dv = v.shape[-1]
    n = l // DN_CHUNK
    q = q.astype(f32).reshape(b, h, n, DN_CHUNK, dk)
    k = k.astype(f32).reshape(b, h, n, DN_CHUNK, dk)
    v = v.astype(f32).reshape(b, h, n, DN_CHUNK, dv)
    g = g.astype(f32).reshape(b, h, n, DN_CHUNK)
    beta = beta.astype(f32).reshape(b, h, n, DN_CHUNK)

    g_cum = jnp.cumsum(g, axis=-1)
    tril = jnp.tril(jnp.ones((DN_CHUNK, DN_CHUNK), dtype=bool))
    eye = jnp.eye(DN_CHUNK, dtype=bool)
    decay = jnp.exp(jnp.where(tril, g_cum[..., :, None] - g_cum[..., None, :], -jnp.inf))
    k_beta = k * beta[..., None]
    a_strict = jnp.where(tril & ~eye, jnp.einsum('bhnid,bhnjd->bhnij', k_beta, k) * decay, 0.0)
    unit_lower = a_strict + eye.astype(f32)
    rhs = jnp.concatenate([v * beta[..., None], k_beta * jnp.exp(g_cum)[..., None]], axis=-1)
    sol = lax.linalg.triangular_solve(unit_lower, rhs, left_side=True, lower=True, unit_diagonal=True)
    u, w = sol[..., :dv], sol[..., dv:]
    k_dec = k * jnp.exp(g_cum[..., -1:] - g_cum)[..., None]
    g_last = jnp.exp(g_cum[..., -1])

    xs = [u, w, k_dec, g_last]
    if need_output:
        qk = jnp.einsum('bhnid,bhnjd->bhnij', q, k) * decay
        q_dec = q * jnp.exp(g_cum)[..., None]
        xs += [qk, q_dec]
    xs = tuple(jnp.moveaxis(t, 2, 0) for t in xs)

    def step(s, inp):
        u_c, w_c, kd_c, gl_c = inp[:4]
        v_new = u_c - jnp.einsum('bhck,bhkv->bhcv', w_c, s)
        s_next = s * gl_c[..., None, None] + jnp.einsum('bhck,bhcv->bhkv', kd_c, v_new)
        if need_output:
            qk_c, qd_c = inp[4:]
            o = jnp.einsum('bhck,bhkv->bhcv', qd_c, s) + jnp.einsum('bhij,bhjv->bhiv', qk_c, v_new)
            return s_next, o
        return s_next, None

    s_final, o = lax.scan(step, s0.astype(f32), xs)
    if need_output:
        o = jnp.moveaxis(o, 0, 2).reshape(b, h, l, dv)
    return o, s_final


def dn_gates(alpha, beta_logits, a_log, dt_bias):
    b, l, _ = alpha.shape
    alpha = alpha.astype(jnp.float32).reshape(b, l, 2, DN_HEADS).transpose(2, 0, 3, 1)
    beta = jax.nn.sigmoid(beta_logits.astype(jnp.float32).reshape(b, l, 2, DN_HEADS).transpose(2, 0, 3, 1))
    a = jnp.exp(a_log.astype(jnp.float32))[:, None, :, None]
    g = -a * jax.nn.softplus(alpha + dt_bias.astype(jnp.float32)[:, None, :, None])
    return g, beta


def token_mixer(h_lat, h_ctx, rows, cols, w_in, q_norm_w, k_norm_w, conv_w, a_log, dt_bias,
                o_norm_w, w_out, update_ctx):
    p_lat = h_lat @ w_in
    p_ctx = h_ctx @ w_in
    aq_l, ak_l, av_l, dq_l, dk_l, dv_l, dz_l, db_l, da_l = jnp.split(p_lat, IN_OFFSETS, axis=-1)
    aq_c, ak_c, av_c, dq_c, dk_c, dv_c, dz_c, db_c, da_c = jnp.split(p_ctx, IN_OFFSETS, axis=-1)

    def attn_qkv(aq, ak, av):
        q = rmsnorm(split_heads(aq, ATTN_HEADS), q_norm_w)
        k = rmsnorm(split_heads(ak, ATTN_KV_HEADS), k_norm_w)
        return q, k, split_heads(av, ATTN_KV_HEADS)

    q_l, k_l, v_l = attn_qkv(aq_l, ak_l, av_l)
    q_l = rope2d(q_l, rows, cols)
    k_l = rope2d(k_l, rows, cols)
    q_c, k_c, v_c = attn_qkv(aq_c, ak_c, av_c)
    k_all = jnp.concatenate([k_c, k_l], axis=2)
    v_all = jnp.concatenate([v_c, v_l], axis=2)
    att_lat = merge_heads(attend_blocked(q_l, k_all, v_all))

    def dn_qkv(dq, dk, dv):
        qkv = jax.nn.silu(short_conv(jnp.concatenate([dq, dk, dv], axis=-1), conv_w))
        q, k, v = jnp.split(qkv, 3, axis=-1)
        q = l2norm(split_heads(q, DN_HEADS)) * (DN_HEAD_DIM ** -0.5)
        k = l2norm(split_heads(k, DN_HEADS))
        return q, k, split_heads(v, DN_HEADS)

    dq_lh, dk_lh, dv_lh = dn_qkv(dq_l, dk_l, dv_l)
    dq_ch, dk_ch, dv_ch = dn_qkv(dq_c, dk_c, dv_c)
    g_l, b_l = dn_gates(da_l, db_l, a_log, dt_bias)
    g_c, b_c = dn_gates(da_c, db_c, a_log, dt_bias)
    batch = h_lat.shape[0]

    def seq_flip(t):
        return jnp.flip(t, axis=2)

    outs_lat, outs_ctx = [], []
    for d in range(2):
        ctx_in = (dq_ch, dk_ch, dv_ch, g_c[d], b_c[d])
        lat_in = (dq_lh, dk_lh, dv_lh, g_l[d], b_l[d])
        if d == 1:
            ctx_in = tuple(seq_flip(t) for t in ctx_in)
            lat_in = tuple(seq_flip(t) for t in lat_in)
        s0 = jnp.zeros((batch, DN_HEADS, DN_HEAD_DIM, DN_HEAD_DIM), jnp.float32)
        o_c, s_c = delta_rule_chunked(*ctx_in, s0, need_output=update_ctx)
        o_l, _ = delta_rule_chunked(*lat_in, s_c, need_output=True)
        outs_lat.append(seq_flip(o_l) if d == 1 else o_l)
        if update_ctx:
            outs_ctx.append(seq_flip(o_c) if d == 1 else o_c)

    def dn_output(o, dz):
        o = o.astype(dz.dtype).transpose(0, 2, 1, 3)
        z = dz.reshape(dz.shape[0], dz.shape[1], DN_HEADS, DN_HEAD_DIM)
        y = rmsnorm(o, o_norm_w) * jax.nn.silu(z)
        return y.reshape(dz.shape[0], dz.shape[1], DN_W)

    dn_lat = dn_output(outs_lat[0] + outs_lat[1], dz_l)
    y_lat = jnp.concatenate([att_lat, dn_lat], axis=-1) @ w_out

    y_ctx = None
    if update_ctx:
        att_ctx = merge_heads(attend_blocked(q_c, k_c, v_c))
        dn_ctx = dn_output(outs_ctx[0] + outs_ctx[1], dz_c)
        y_ctx = jnp.concatenate([att_ctx, dn_ctx], axis=-1) @ w_out
    return y_lat, y_ctx


def expert_choice_ffn(h, router_w, w_gate, w_up, w_down):
    b, t, d = h.shape
    cap = EC_CAPACITY_FACTOR * t // N_EXPERTS
    affinity = jax.nn.softmax((h @ router_w).astype(jnp.float32), axis=-1)
    gates, idx = lax.top_k(jnp.swapaxes(affinity, 1, 2), cap)
    xg = jax.vmap(lambda hb, ib: hb[ib])(h, idx)
    hid = jax.nn.silu(jnp.einsum('becd,edf->becf', xg, w_gate)) * jnp.einsum('becd,edf->becf', xg, w_up)
    y = jnp.einsum('becf,efd->becd', hid, w_down) * gates[..., None].astype(h.dtype)
    return jax.vmap(lambda yb, ib: jnp.zeros((t, d), yb.dtype).at[ib.reshape(-1)].add(yb.reshape(-1, d)))(y, idx)


def setup_inputs(seed: int = 0) -> dict:
    key = jax.random.key(seed)
    ks = jax.random.split(key, 20)
    f32 = jnp.float32

    def nrm(k, shape, scale):
        return jax.random.normal(k, shape, f32) * scale

    dt = jnp.exp(jax.random.uniform(ks[14], (DEPTH, 2, DN_HEADS), f32, math.log(1e-3), math.log(1e-1)))
    return {
        "x": nrm(ks[0], (BATCH, SEQ, D_MODEL), 1.0),
        "c": nrm(ks[1], (BATCH, D_MODEL), 1.0),
        "ctx": nrm(ks[2], (BATCH, CTX_LEN, D_MODEL), 1.0),
        "c_ctx": nrm(ks[3], (D_MODEL,), 1.0),
        "w_mod": nrm(ks[4], (DEPTH, D_MODEL, MOD_CHUNKS * D_MODEL), 0.5 * D_MODEL ** -0.5),
        "b_mod": nrm(ks[5], (DEPTH, MOD_CHUNKS * D_MODEL), 0.02),
        "norm1_w": 1.0 + nrm(ks[6], (DEPTH, D_MODEL), 0.02),
        "norm2_w": 1.0 + nrm(ks[7], (DEPTH, D_MODEL), 0.02),
        "w_in": nrm(ks[8], (DEPTH, D_MODEL, IN_WIDTH), D_MODEL ** -0.5),
        "q_norm_w": 1.0 + nrm(ks[9], (DEPTH, ATTN_HEAD_DIM), 0.02),
        "k_norm_w": 1.0 + nrm(ks[10], (DEPTH, ATTN_HEAD_DIM), 0.02),
        "conv_w": nrm(ks[11], (DEPTH, DN_CONV_W, 3 * DN_W), DN_CONV_W ** -0.5),
        "a_log": jnp.log(jax.random.uniform(ks[12], (DEPTH, 2, DN_HEADS), f32, 1.0, 16.0)),
        "dt_bias": dt + jnp.log(-jnp.expm1(-dt)),
        "o_norm_w": 1.0 + nrm(ks[13], (DEPTH, DN_HEAD_DIM), 0.02),
        "w_out": nrm(ks[15], (DEPTH, MIX_WIDTH, D_MODEL), MIX_WIDTH ** -0.5),
        "router_w": nrm(ks[16], (DEPTH, D_MODEL, N_EXPERTS), D_MODEL ** -0.5),
        "w_gate": nrm(ks[17], (DEPTH, N_EXPERTS, D_MODEL, EXPERT_FF), D_MODEL ** -0.5),
        "w_up": nrm(ks[18], (DEPTH, N_EXPERTS, D_MODEL, EXPERT_FF), D_MODEL ** -0.5),
        "w_down": nrm(ks[19], (DEPTH, N_EXPERTS, EXPERT_FF, D_MODEL), EXPERT_FF ** -0.5),
    }


def reference(x, c, ctx, c_ctx, w_mod, b_mod, norm1_w, norm2_w, w_in, q_norm_w, k_norm_w, conv_w,
              a_log, dt_bias, o_norm_w, w_out, router_w, w_gate, w_up, w_down):
    seq_len = x.shape[1]
    n_rows = seq_len // GRID_W
    rows = jnp.repeat(jnp.arange(n_rows, dtype=jnp.int32), GRID_W)
    cols = jnp.tile(jnp.arange(GRID_W, dtype=jnp.int32), n_rows)
    silu_c = jax.nn.silu(c)
    silu_cc = jax.nn.silu(c_ctx)
    for l in range(DEPTH):
        update_ctx = l < DEPTH - 1
        mod_lat = silu_c @ w_mod[l] + b_mod[l]
        mod_ctx = silu_cc @ w_mod[l] + b_mod[l]
        sh1, sc1, g1, sh2, sc2, g2 = jnp.split(mod_lat[:, None, :], MOD_CHUNKS, axis=-1)
        csh1, csc1, cg1, csh2, csc2, cg2 = jnp.split(mod_ctx, MOD_CHUNKS, axis=-1)

        h_lat = modulate(rmsnorm(x, norm1_w[l]), sh1, sc1)
        h_ctx = modulate(rmsnorm(ctx, norm1_w[l]), csh1, csc1)
        y_lat, y_ctx = token_mixer(h_lat, h_ctx, rows, cols, w_in[l], q_norm_w[l], k_norm_w[l], conv_w[l],
                                   a_log[l], dt_bias[l], o_norm_w[l], w_out[l], update_ctx)
        x = x + g1 * y_lat
        h2 = modulate(rmsnorm(x, norm2_w[l]), sh2, sc2)
        x = x + g2 * expert_choice_ffn(h2, router_w[l], w_gate[l], w_up[l], w_down[l])

        if update_ctx:
            ctx = ctx + cg1 * y_ctx
            h2c = modulate(rmsnorm(ctx, norm2_w[l]), csh2, csc2)
            ctx = ctx + cg2 * expert_choice_ffn(h2c, router_w[l], w_gate[l], w_up[l], w_down[l])
    return x
```

```python
import functools
import math

import numpy as np
import jax
import jax.numpy as jnp
from jax import lax
from jax.experimental import pallas as pl
from jax.experimental.pallas import tpu as pltpu

F32 = jnp.float32
BF16 = jnp.bfloat16
I32 = jnp.int32

GRID_W = 64
EPS = 1e-6
ATTN_HEADS = 8
ATTN_KV_HEADS = 2
ATTN_HEAD_DIM = 64
ROPE_BASE = 10000.0
DN_HEADS = 4
DN_HEAD_DIM = 128
DN_CONV_W = 5
N_EXPERTS = 16
EC_CAPACITY_FACTOR = 2

ATTN_Q_W = ATTN_HEADS * ATTN_HEAD_DIM
ATTN_KV_W = ATTN_KV_HEADS * ATTN_HEAD_DIM
DN_W = DN_HEADS * DN_HEAD_DIM
QK_W = ATTN_Q_W + ATTN_KV_W
ATT_W = ATTN_Q_W + 2 * ATTN_KV_W
N_GATES = 4 * DN_HEADS

LANES = 128
SUBLANES = 8
VMEM_LIMIT = 56 * 1024 * 1024
F32_MIN_EXP = -126
F32_MANTISSA_BITS = 23

TOK_TILE = 256
Q_TILE = 256
DN_CHUNK = 128
DN_GROUP = 6
DN_CONV_UNROLL = 2
GATHER_EXPERTS = 4
FF_TILE = 512
COMB_TILE = 512
MOD_ROWS = 16


def _params(*sem):
    return pltpu.CompilerParams(dimension_semantics=sem, vmem_limit_bytes=VMEM_LIMIT)


def _silu(v):
    return v * jax.nn.sigmoid(v)


def _dot(a, b):
    return jnp.dot(a, b, preferred_element_type=F32)


def _dot_nt(a, b):
    return lax.dot_general(a, b, (((1,), (1,)), ((), ())), preferred_element_type=F32)


def _mod_kernel(c_ref, w_ref, b_ref, o_ref):
    sc = _silu(c_ref[...]).astype(BF16)
    o_ref[...] = _dot(sc, w_ref[...].astype(BF16)) + b_ref[...]


def _mod_call(cc, w_mod, b_mod):
    d, n = w_mod.shape
    return pl.pallas_call(
        _mod_kernel,
        grid=(n // d,),
        in_specs=[pl.BlockSpec((MOD_ROWS, d), lambda i: (0, 0)),
                  pl.BlockSpec((d, d), lambda i: (0, i)),
                  pl.BlockSpec((1, d), lambda i: (0, i))],
        out_specs=pl.BlockSpec((MOD_ROWS, d), lambda i: (0, i)),
        out_shape=jax.ShapeDtypeStruct((MOD_ROWS, n), F32),
        compiler_params=_params("arbitrary"),
        name="mod",
    )(cc, w_mod, b_mod)


def _inproj_kernel(ctx_row, x_ref, ctx_ref, mod_ref, n1_ref, w_ref, bd_ref, qkw_ref, cos_ref, sin_ref,
                   alog_ref, dtb_ref, q_ref, kv_ref, dn_ref, z_ref, g_ref, gt_ref):
    d = x_ref.shape[-1]
    b = pl.program_id(0)
    is_ctx = pl.program_id(1) == 0
    xin = jnp.where(is_ctx, ctx_ref[0], x_ref[0])
    row = jnp.where(is_ctx, ctx_row, b)
    shift = mod_ref[pl.ds(row, 1), 0:d]
    scale = mod_ref[pl.ds(row, 1), d:2 * d]
    hn = xin * lax.rsqrt(jnp.mean(xin * xin, axis=-1, keepdims=True) + EPS) * n1_ref[...]
    h = (hn * (1.0 + scale) + shift).astype(BF16)
    p = _dot(h, w_ref[...])

    qk = p[:, 0:QK_W]
    sq = qk * qk
    hi = sq.astype(BF16)
    lo = (sq - hi.astype(F32)).astype(BF16)
    ms = (_dot(hi, bd_ref[...]) + _dot(lo, bd_ref[...])) * (1.0 / ATTN_HEAD_DIM)
    qkn = qk * lax.rsqrt(ms + EPS) * qkw_ref[...]

    lane = lax.broadcasted_iota(I32, (1, LANES), 1)
    first_half = (lane % 32) < 16
    cos = cos_ref[...]
    sin = sin_ref[...]
    rot = []
    for i in range(QK_W // LANES):
        blk = qkn[:, i * LANES:(i + 1) * LANES]
        partner = jnp.where(first_half, pltpu.roll(blk, LANES - 16, axis=1), pltpu.roll(blk, 16, axis=1))
        rot.append(blk * cos + partner * sin)
    for i in range(ATTN_Q_W // LANES):
        q_ref[0, :, i * LANES:(i + 1) * LANES] = (rot[i] * (ATTN_HEAD_DIM ** -0.5)).astype(BF16)

    low = lane < ATTN_HEAD_DIM
    k2 = rot[ATTN_Q_W // LANES]
    v2 = p[:, QK_W:ATT_W]
    for src, base in ((k2, 0), (v2, 2)):
        swapped = pltpu.roll(src, ATTN_HEAD_DIM, axis=1)
        kv_ref[0, 0, base + 0] = jnp.where(low, src, 0.0).astype(BF16)
        kv_ref[0, 0, base + 1] = jnp.where(low, 0.0, swapped).astype(BF16)
        kv_ref[0, 1, base + 0] = jnp.where(low, swapped, 0.0).astype(BF16)
        kv_ref[0, 1, base + 1] = jnp.where(low, 0.0, src).astype(BF16)

    dn_ref[0] = p[:, ATT_W:ATT_W + 3 * DN_W]
    z_ref[0] = p[:, ATT_W + 3 * DN_W:ATT_W + 4 * DN_W]

    gp = p[:, ATT_W + 4 * DN_W:]
    beta = jax.nn.sigmoid(gp)
    xa = gp + dtb_ref[...]
    softplus = jnp.maximum(xa, 0.0) + jnp.log1p(jnp.exp(-jnp.abs(xa)))
    decay = -jnp.exp(alog_ref[...]) * softplus
    gates = jnp.where(lane < N_GATES // 2, beta, jnp.where(lane < N_GATES, decay, 0.0))
    g_ref[0] = gates
    gt_ref[0] = gates.T[0:N_GATES, :]


def _inproj_call(x, ctx, mod, n1, w_pad, bd, qkw, cos_t, sin_t, alog_l, dtb_l):
    bsz, seq, d = x.shape
    n_ctx = ctx.shape[1]
    tot = n_ctx + seq
    t = TOK_TILE
    assert n_ctx == t and seq % t == 0
    nw = w_pad.shape[1]
    full = lambda shape: pl.BlockSpec(shape, lambda b, j: (0,) * len(shape))
    return pl.pallas_call(
        functools.partial(_inproj_kernel, bsz),
        grid=(bsz, tot // t),
        in_specs=[pl.BlockSpec((1, t, d), lambda b, j: (b, jnp.maximum(j - 1, 0), 0)),
                  pl.BlockSpec((1, t, d), lambda b, j: (b, 0, 0)),
                  full((MOD_ROWS, mod.shape[1])),
                  full((1, d)),
                  full((d, nw)),
                  full((QK_W, QK_W)),
                  full((1, QK_W)),
                  pl.BlockSpec((t, LANES), lambda b, j: (j, 0)),
                  pl.BlockSpec((t, LANES), lambda b, j: (j, 0)),
                  full((1, LANES)),
                  full((1, LANES))],
        out_specs=[pl.BlockSpec((1, t, ATTN_Q_W), lambda b, j: (b, j, 0)),
                   pl.BlockSpec((1, ATTN_KV_HEADS, 4, t, LANES), lambda b, j: (b, 0, 0, j, 0)),
                   pl.BlockSpec((1, t, 3 * DN_W), lambda b, j: (b, j, 0)),
                   pl.BlockSpec((1, t, DN_W), lambda b, j: (b, j, 0)),
                   pl.BlockSpec((1, t, LANES), lambda b, j: (b, j, 0)),
                   pl.BlockSpec((1, N_GATES, t), lambda b, j: (b, 0, j))],
        out_shape=[jax.ShapeDtypeStruct((bsz, tot, ATTN_Q_W), BF16),
                   jax.ShapeDtypeStruct((bsz, ATTN_KV_HEADS, 4, tot, LANES), BF16),
                   jax.ShapeDtypeStruct((bsz, tot, 3 * DN_W), F32),
                   jax.ShapeDtypeStruct((bsz, tot, DN_W), F32),
                   jax.ShapeDtypeStruct((bsz, tot, LANES), F32),
                   jax.ShapeDtypeStruct((bsz, N_GATES, tot), F32)],
        compiler_params=_params("arbitrary", "arbitrary"),
        name="inproj",
    )(x, ctx, mod, n1, w_pad, bd, qkw, cos_t, sin_t, alog_l, dtb_l)


def _attn_kernel(q_ref, kv_ref, o_ref):
    grp = ATTN_HEADS // ATTN_KV_HEADS
    for pair in range(grp // 2):
        qp = q_ref[0, :, pair * LANES:(pair + 1) * LANES]
        acc = None
        for parity in range(2):
            s = _dot_nt(qp, kv_ref[0, 0, parity])
            e = jnp.exp(s - jnp.max(s, axis=-1, keepdims=True))
            denom = jnp.sum(e, axis=-1, keepdims=True)
            o = _dot(e.astype(BF16), kv_ref[0, 0, 2 + parity]) / denom
            acc = o if acc is None else acc + o
        o_ref[0, :, pair * LANES:(pair + 1) * LANES] = acc.astype(BF16)


def _attn_call(q, kv, n_ctx):
    bsz, tot, _ = q.shape
    seq = tot - n_ctx
    tq = Q_TILE
    assert n_ctx % tq == 0 and seq % tq == 0
    off = n_ctx // tq
    gw = ATTN_Q_W // ATTN_KV_HEADS
    return pl.pallas_call(
        _attn_kernel,
        grid=(bsz, ATTN_KV_HEADS, seq // tq),
        in_specs=[pl.BlockSpec((1, tq, gw), lambda b, k, i: (b, i + off, k)),
                  pl.BlockSpec((1, 1, 4, tot, LANES), lambda b, k, i: (b, k, 0, 0, 0))],
        out_specs=pl.BlockSpec((1, tq, gw), lambda b, k, i: (b, i, k)),
        out_shape=jax.ShapeDtypeStruct((bsz, seq, ATTN_Q_W), BF16),
        compiler_params=_params("arbitrary", "arbitrary", "arbitrary"),
        name="attn",
    )(q, kv)


def _unit_tri_inverses_minus_eye(mats, ij_xor):
    n = mats[0].shape[0]
    corrs = [-jnp.where(ij_xor < 2, a, 0.0) for a in mats]
    m = 2
    while m < n:
        joins = (ij_xor >= m) & (ij_xor < 2 * m)
        xs = [jnp.where(joins, a, 0.0) for a in mats]
        ys = [x + _dot(corr.astype(BF16), x.astype(BF16)) for corr, x in zip(corrs, xs)]
        corrs = [corr - (y + _dot(y.astype(BF16), corr.astype(BF16))) for corr, y in zip(corrs, ys)]
        m *= 2
    return corrs


def _dn_kernel(n_ctx, dq_ref, dk_ref, dv_ref, cq_ref, ck_ref, cv_ref, g_ref, gt_ref, o_ref,
               xq_s, xk_s, xv_s, q_s, k_s, v_s, kk_s, qk_s, mq_s, n_s, op_s, gl_s):
    c = DN_CHUNK
    tot = dq_ref.shape[1]
    n_chunks = tot // c
    ctx_chunks = n_ctx // c
    lat_chunks = n_chunks - ctx_chunks
    pad = SUBLANES
    head = pl.program_id(1)
    dk = DN_HEAD_DIM

    zeros_pad = jnp.zeros((pad, dk), F32)
    lat0 = 2 * pad + n_ctx
    for src, dst in ((dq_ref, xq_s), (dk_ref, xk_s), (dv_ref, xv_s)):
        dst[0:pad, :] = zeros_pad
        dst[pad:pad + n_ctx, :] = src[0, 0:n_ctx, :]
        dst[pad + n_ctx:lat0, :] = zeros_pad
        dst[lat0:lat0 + tot - n_ctx, :] = src[0, n_ctx:tot, :]
        dst[lat0 + tot - n_ctx:lat0 + tot - n_ctx + pad, :] = zeros_pad

    half = DN_CONV_W // 2

    def conv_chunk(i, carry):
        r0 = pl.multiple_of(i * c, c)
        rp = r0 + jnp.where(i >= ctx_chunks, 2 * pad, pad)

        def conv(x_s, cw_ref):
            acc = None
            for s in range(DN_CONV_W):
                term = x_s[pl.ds(rp - half + s, c), :] * cw_ref[s:s + 1, :]
                acc = term if acc is None else acc + term
            return _silu(acc)

        qc = conv(xq_s, cq_ref)
        kc = conv(xk_s, ck_ref)
        vc = conv(xv_s, cv_ref)
        qc = qc * lax.rsqrt(jnp.sum(qc * qc, axis=-1, keepdims=True) + EPS) * (dk ** -0.5)
        kc = kc * lax.rsqrt(jnp.sum(kc * kc, axis=-1, keepdims=True) + EPS)
        q_s[pl.ds(r0, c), :] = qc
        k_s[pl.ds(r0, c), :] = kc
        v_s[pl.ds(r0, c), :] = vc
        kb = kc.astype(BF16)
        kk_s[i] = _dot_nt(kb, kb)
        qk_s[i] = _dot_nt(qc.astype(BF16), kb)
        return carry

    lax.fori_loop(0, n_chunks, conv_chunk, 0, unroll=DN_CONV_UNROLL)

    ii = lax.broadcasted_iota(I32, (c, c), 0)
    jj = lax.broadcasted_iota(I32, (c, c), 1)
    ij_xor = ii ^ jj
    lane = lax.broadcasted_iota(I32, (1, LANES), 1)

    def local_group(grp, carry):
        chains = []
        for k in range(DN_GROUP):
            i = grp * DN_GROUP + k
            r0 = pl.multiple_of(i * c, c)
            gates = g_ref[0, pl.ds(r0, c), :]
            kc = k_s[pl.ds(r0, c), :]
            qc = q_s[pl.ds(r0, c), :]
            vc = v_s[pl.ds(r0, c), :]
            kk = kk_s[i]
            qk = qk_s[i]
            kt = kc.T
            for d in range(2):
                beta_col = jnp.sum(jnp.where(lane == d * DN_HEADS + head, gates, 0.0), axis=-1, keepdims=True)
                g_col = jnp.sum(jnp.where(lane == N_GATES // 2 + d * DN_HEADS + head, gates, 0.0),
                                axis=-1, keepdims=True)
                g_row = gt_ref[0, pl.ds(N_GATES // 2 + d * DN_HEADS + head, 1), pl.ds(i, 1), :].reshape(1, c)
                incl = (jj <= ii) if d == 0 else (jj >= ii)
                strict = (jj < ii) if d == 0 else (jj > ii)
                incl_t = (ii <= jj) if d == 0 else (ii >= jj)
                gcum_col = jnp.sum(jnp.where(incl, g_row, 0.0), axis=1, keepdims=True)
                gcum_row = jnp.sum(jnp.where(incl_t, g_col, 0.0), axis=0, keepdims=True)
                total = jnp.sum(g_row, axis=1, keepdims=True)
                decay = jnp.exp(jnp.where(incl, gcum_col - gcum_row, -jnp.inf))
                e_col = jnp.exp(gcum_col)
                kdt = (kt * jnp.exp(total - gcum_row)).astype(BF16)
                qkm = (qk * decay).astype(BF16)
                chains.append(dict(
                    d=d, i=i,
                    a=jnp.where(strict, beta_col * kk * decay, 0.0),
                    rhs=jnp.concatenate([vc * beta_col, kc * (beta_col * e_col)], axis=1),
                    lhs=jnp.concatenate([kdt, qkm], axis=0),
                    qd=qc * e_col,
                    g_last=jnp.exp(total)))
        corrs = _unit_tri_inverses_minus_eye([ch["a"] for ch in chains], ij_xor)
        sols = [ch["rhs"] + _dot(corr.astype(BF16), ch["rhs"].astype(BF16)) for ch, corr in zip(chains, corrs)]
        prods = [_dot(ch["lhs"], sol.astype(BF16)) for ch, sol in zip(chains, sols)]
        for ch, r in zip(chains, prods):
            d, i = ch["d"], ch["i"]
            mq_s[d, i, 0:dk, :] = (-r[0:dk, dk:2 * dk]).astype(BF16)
            mq_s[d, i, dk:dk + c, :] = (ch["qd"] - r[dk:dk + c, dk:2 * dk]).astype(BF16)
            n_s[d, i] = r[0:dk, 0:dk]
            op_s[d, i] = r[dk:dk + c, 0:dk]
            gl_s[d, i] = jnp.broadcast_to(ch["g_last"], (SUBLANES, dk))
        return carry

    lax.fori_loop(0, n_chunks // DN_GROUP, local_group, 0)

    def scan(first_chunk, count, emit, states):
        rows = dk + c if emit else dk

        def step(t, st):
            new = []
            for d in range(2):
                i = first_chunk + t if d == 0 else first_chunk + count - 1 - t
                s = st[d]
                r = _dot(mq_s[d, i, 0:rows, :], s.astype(BF16))
                if emit:
                    ro = pl.multiple_of((i - ctx_chunks) * c, c)
                    o_ref[0, pl.ds(ro, c), :] += r[dk:dk + c] + op_s[d, i]
                new.append(s * gl_s[d, i][0:1, :] + r[0:dk] + n_s[d, i])
            return tuple(new)
        return lax.fori_loop(0, count, step, states)

    o_ref[...] = jnp.zeros(o_ref.shape, F32)
    zero_state = jnp.zeros((dk, dk), F32)
    states = scan(0, ctx_chunks, False, (zero_state, zero_state))
    scan(ctx_chunks, lat_chunks, True, states)


def _dn_call(dn, conv_w, gates, gates_t, n_ctx):
    bsz, tot, _ = dn.shape
    seq = tot - n_ctx
    c = DN_CHUNK
    assert n_ctx % c == 0 and seq % c == 0
    n_chunks = tot // c
    dk = DN_HEAD_DIM
    gt4 = gates_t.reshape(bsz, N_GATES, n_chunks, c)
    col = lambda off: pl.BlockSpec((1, tot, dk), lambda b, h: (b, 0, off + h))
    cw = lambda off: pl.BlockSpec((DN_CONV_W, dk), lambda b, h: (0, off + h))
    padded = tot + 3 * SUBLANES
    assert n_chunks % DN_GROUP == 0 and n_chunks % DN_CONV_UNROLL == 0
    return pl.pallas_call(
        functools.partial(_dn_kernel, n_ctx),
        grid=(bsz, DN_HEADS),
        in_specs=[col(0), col(DN_HEADS), col(2 * DN_HEADS),
                  cw(0), cw(DN_HEADS), cw(2 * DN_HEADS),
                  pl.BlockSpec((1, tot, LANES), lambda b, h: (b, 0, 0)),
                  pl.BlockSpec((1, N_GATES, n_chunks, c), lambda b, h: (b, 0, 0, 0))],
        out_specs=pl.BlockSpec((1, seq, dk), lambda b, h: (b, 0, h)),
        out_shape=jax.ShapeDtypeStruct((bsz, seq, DN_W), F32),
        scratch_shapes=[pltpu.VMEM((padded, dk), F32)] * 3
        + [pltpu.VMEM((tot, dk), F32)] * 3
        + [pltpu.VMEM((n_chunks, c, c), F32)] * 2
        + [pltpu.VMEM((2, n_chunks, dk + c, dk), BF16),
           pltpu.VMEM((2, n_chunks, dk, dk), F32),
           pltpu.VMEM((2, n_chunks, c, dk), F32),
           pltpu.VMEM((2, n_chunks, SUBLANES, dk), F32)],
        compiler_params=_params("arbitrary", "arbitrary"),
        name="dn",
    )(dn, dn, dn, conv_w, conv_w, conv_w, gates, gt4)


def _outproj_kernel(x_ref, att_ref, o_ref, z_ref, mod_ref, onw_ref, wo_ref, n2_ref, rw_ref,
                    x1_ref, h2_ref, afft_ref):
    d = x_ref.shape[-1]
    b = pl.program_id(0)
    gate1 = mod_ref[pl.ds(b, 1), 2 * d:3 * d]
    shift2 = mod_ref[pl.ds(b, 1), 3 * d:4 * d]
    scale2 = mod_ref[pl.ds(b, 1), 4 * d:5 * d]
    y = _dot(att_ref[0], wo_ref[0:ATTN_Q_W, :])
    for h in range(DN_HEADS):
        sl = slice(h * DN_HEAD_DIM, (h + 1) * DN_HEAD_DIM)
        oh = o_ref[0, :, sl]
        on = oh * lax.rsqrt(jnp.mean(oh * oh, axis=-1, keepdims=True) + EPS) * onw_ref[...]
        yh = (on * _silu(z_ref[0, :, sl])).astype(BF16)
        y = y + _dot(yh, wo_ref[ATTN_Q_W + h * DN_HEAD_DIM:ATTN_Q_W + (h + 1) * DN_HEAD_DIM, :])
    x1 = x_ref[0] + gate1 * y
    x1_ref[0] = x1
    hn = x1 * lax.rsqrt(jnp.mean(x1 * x1, axis=-1, keepdims=True) + EPS) * n2_ref[...]
    h2 = (hn * (1.0 + scale2) + shift2).astype(BF16)
    h2_ref[0] = h2
    logits = _dot(h2, rw_ref[...])
    lane = lax.broadcasted_iota(I32, (1, LANES), 1)
    logits = jnp.where(lane < N_EXPERTS, logits, -jnp.inf)
    e = jnp.exp(logits - jnp.max(logits, axis=-1, keepdims=True))
    aff = e / jnp.sum(e, axis=-1, keepdims=True)
    afft_ref[0] = aff.T[0:N_EXPERTS, :]


def _outproj_call(x, att, o_dn, z, mod, onw, wo, n2, rw, n_ctx):
    bsz, seq, d = x.shape
    t = TOK_TILE
    off = n_ctx // t
    full = lambda shape: pl.BlockSpec(shape, lambda b, j: (0,) * len(shape))
    return pl.pallas_call(
        _outproj_kernel,
        grid=(bsz, seq // t),
        in_specs=[pl.BlockSpec((1, t, d), lambda b, j: (b, j, 0)),
                  pl.BlockSpec((1, t, ATTN_Q_W), lambda b, j: (b, j, 0)),
                  pl.BlockSpec((1, t, DN_W), lambda b, j: (b, j, 0)),
                  pl.BlockSpec((1, t, DN_W), lambda b, j: (b, j + off, 0)),
                  full((MOD_ROWS, mod.shape[1])),
                  full((1, DN_HEAD_DIM)),
                  full(wo.shape),
                  full((1, d)),
                  full((d, LANES))],
        out_specs=[pl.BlockSpec((1, t, d), lambda b, j: (b, j, 0)),
                   pl.BlockSpec((1, t, d), lambda b, j: (b, j, 0)),
                   pl.BlockSpec((1, N_EXPERTS, t), lambda b, j: (b, 0, j))],
        out_shape=[jax.ShapeDtypeStruct((bsz, seq, d), F32),
                   jax.ShapeDtypeStruct((bsz, seq, d), BF16),
                   jax.ShapeDtypeStruct((bsz, N_EXPERTS, seq), F32)],
        compiler_params=_params("arbitrary", "arbitrary"),
        name="outproj",
    )(x, att, o_dn, z, mod, onw, wo, n2, rw)


def _route_kernel(cap, afft_ref, slot_ref, gate_ref, tok_ref):
    aff = afft_ref[0]
    n_e, n_t = aff.shape

    def enough(cand):
        return jnp.sum(jnp.where(aff >= cand, 1.0, 0.0), axis=-1, keepdims=True) >= cap

    tiny = 2.0 ** F32_MIN_EXP
    cur = jnp.full((n_e, 1), tiny, F32)
    any_normal = enough(cur)
    shift = 1 << (-F32_MIN_EXP).bit_length()
    while shift > 1:
        shift //= 2
        cand = cur * (2.0 ** shift)
        cur = jnp.where(enough(cand), cand, cur)

    def refine(_, state):
        cur, step = state
        cand = cur + step
        return jnp.where(enough(cand), cand, cur), step * 0.5

    cur, _ = lax.fori_loop(0, F32_MANTISSA_BITS, refine, (cur, cur * 0.5))
    thr = jnp.where(any_normal, cur, 0.0)
    need = cap - jnp.sum(jnp.where(aff > thr, 1.0, 0.0), axis=-1, keepdims=True)

    upper = (lax.broadcasted_iota(I32, (LANES, LANES), 0) < lax.broadcasted_iota(I32, (LANES, LANES), 1))
    upper = jnp.where(upper, 1.0, 0.0).astype(BF16)
    run = jnp.zeros((2 * n_e, 1), F32)
    for blk in range(n_t // LANES):
        sl = slice(blk * LANES, (blk + 1) * LANES)
        gt = aff[:, sl] > thr
        eq = aff[:, sl] == thr
        x = jnp.concatenate([jnp.where(gt, 1.0, 0.0), jnp.where(eq, 1.0, 0.0)], axis=0)
        cum = _dot(x.astype(BF16), upper) + run
        run = run + jnp.sum(x, axis=-1, keepdims=True)
        cum_gt = cum[0:n_e]
        cum_eq = cum[n_e:2 * n_e]
        sel = gt | (eq & (cum_eq < need))
        slot = jnp.where(sel, cum_gt + jnp.minimum(cum_eq, need), -1.0)
        slot_ref[0, :, sl] = slot.astype(I32)
        gate_ref[0, :, sl] = jnp.where(sel, aff[:, sl], 0.0)

    stacked = jnp.concatenate([slot_ref[0].astype(F32), gate_ref[0],
                               jnp.zeros((LANES - 2 * n_e, n_t), F32)], axis=0)
    tok_ref[0] = stacked.T


def _route_call(afft, cap):
    bsz, n_e, n_t = afft.shape
    row = pl.BlockSpec((1, n_e, n_t), lambda b: (b, 0, 0))
    return pl.pallas_call(
        functools.partial(_route_kernel, cap),
        grid=(bsz,),
        in_specs=[row],
        out_specs=[row, row, pl.BlockSpec((1, n_t, LANES), lambda b: (b, 0, 0))],
        out_shape=[jax.ShapeDtypeStruct((bsz, n_e, n_t), I32),
                   jax.ShapeDtypeStruct((bsz, n_e, n_t), F32),
                   jax.ShapeDtypeStruct((bsz, n_t, LANES), F32)],
        compiler_params=_params("arbitrary"),
        name="route",
    )(afft)


def _gather_kernel(cap, slot_ref, gate_ref, h_ref, xg_ref, gs_ref):
    ge = xg_ref.shape[0]
    n_t = h_ref.shape[1]
    e0 = pl.program_id(1) * ge
    j = lax.broadcasted_iota(I32, (cap, 1), 0)
    for i in range(ge):
        slot = slot_ref[0, pl.ds(e0 + i, 1), :]
        hit = slot == j
        xg_ref[i, 0] = _dot(jnp.where(hit, 1.0, 0.0).astype(BF16), h_ref[0]).astype(BF16)
        gate = gate_ref[0, pl.ds(e0 + i, 1), :]
        gs_ref[i, 0] = jnp.sum(jnp.where(hit, gate, 0.0), axis=-1, keepdims=True)


def _gather_call(slot, gate, h2, cap):
    bsz, n_e, n_t = slot.shape
    d = h2.shape[-1]
    ge = GATHER_EXPERTS
    row = pl.BlockSpec((1, n_e, n_t), lambda b, g: (b, 0, 0))
    return pl.pallas_call(
        functools.partial(_gather_kernel, cap),
        grid=(bsz, n_e // ge),
        in_specs=[row, row, pl.BlockSpec((1, n_t, d), lambda b, g: (b, 0, 0))],
        out_specs=[pl.BlockSpec((ge, 1, cap, d), lambda b, g: (g, b, 0, 0)),
                   pl.BlockSpec((ge, 1, cap, 1), lambda b, g: (g, b, 0, 0))],
        out_shape=[jax.ShapeDtypeStruct((n_e, bsz, cap, d), BF16),
                   jax.ShapeDtypeStruct((n_e, bsz, cap, 1), F32)],
        compiler_params=_params("arbitrary", "arbitrary"),
        name="gather",
    )(slot, gate, h2)


def _ffn_kernel(xg_ref, gs_ref, wg_ref, wu_ref, wd_ref, y_ref, acc_ref):
    f = pl.program_id(1)
    bsz, cap, d = xg_ref.shape[1:]
    xg = xg_ref[0].reshape(bsz * cap, d)
    hid = _silu(_dot(xg, wg_ref[0].astype(BF16))) * _dot(xg, wu_ref[0].astype(BF16))
    part = _dot(hid.astype(BF16), wd_ref[0].astype(BF16))

    @pl.when(f == 0)
    def _():
        acc_ref[...] = part

    @pl.when(f > 0)
    def _():
        acc_ref[...] += part

    @pl.when(f == pl.num_programs(1) - 1)
    def _():
        y = acc_ref[...] * gs_ref[0].reshape(bsz * cap, 1)
        y_ref[0] = y.astype(BF16).reshape(bsz, cap, d)


def _ffn_call(xg, gs, w_gate, w_up, w_down):
    n_e, bsz, cap, d = xg.shape
    ff = w_gate.shape[-1]
    tf = FF_TILE
    return pl.pallas_call(
        _ffn_kernel,
        grid=(n_e, ff // tf),
        in_specs=[pl.BlockSpec((1, bsz, cap, d), lambda e, f: (e, 0, 0, 0)),
                  pl.BlockSpec((1, bsz, cap, 1), lambda e, f: (e, 0, 0, 0)),
                  pl.BlockSpec((1, d, tf), lambda e, f: (e, 0, f)),
                  pl.BlockSpec((1, d, tf), lambda e, f: (e, 0, f)),
                  pl.BlockSpec((1, tf, d), lambda e, f: (e, f, 0))],
        out_specs=pl.BlockSpec((1, bsz, cap, d), lambda e, f: (e, 0, 0, 0)),
        out_shape=jax.ShapeDtypeStruct((n_e, bsz, cap, d), BF16),
        scratch_shapes=[pltpu.VMEM((bsz * cap, d), F32)],
        compiler_params=_params("arbitrary", "arbitrary"),
        name="ffn",
    )(xg, gs, w_gate, w_up, w_down)


def _combine_kernel(cap, x1_ref, tok_ref, y_ref, mod_ref, o_ref):
    d = x1_ref.shape[-1]
    n_e = y_ref.shape[0]
    b = pl.program_id(0)
    gate2 = mod_ref[pl.ds(b, 1), 5 * d:6 * d]
    tok = tok_ref[0]
    j = lax.broadcasted_iota(I32, (1, cap), 1)
    acc = None
    for e in range(n_e):
        slot = tok[:, e:e + 1].astype(I32)
        onehot = jnp.where(slot == j, 1.0, 0.0).astype(BF16)
        part = _dot(onehot, y_ref[e, 0])
        acc = part if acc is None else acc + part
    o_ref[0] = x1_ref[0] + gate2 * acc


def _combine_call(x1, tok, y, mod, cap):
    bsz, seq, d = x1.shape
    n_e = y.shape[0]
    t = COMB_TILE
    return pl.pallas_call(
        functools.partial(_combine_kernel, cap),
        grid=(bsz, seq // t),
        in_specs=[pl.BlockSpec((1, t, d), lambda b, j: (b, j, 0)),
                  pl.BlockSpec((1, t, LANES), lambda b, j: (b, j, 0)),
                  pl.BlockSpec((n_e, 1, cap, d), lambda b, j: (0, b, 0, 0)),
                  pl.BlockSpec((MOD_ROWS, mod.shape[1]), lambda b, j: (0, 0))],
        out_specs=pl.BlockSpec((1, t, d), lambda b, j: (b, j, 0)),
        out_shape=jax.ShapeDtypeStruct((bsz, seq, d), F32),
        compiler_params=_params("arbitrary", "arbitrary"),
        name="combine",
    )(x1, tok, y, mod)


def _rope_tables(n_ctx, seq):
    m = ATTN_HEAD_DIM // 4
    pos = jnp.arange(seq, dtype=jnp.int32)
    rows = (pos // GRID_W).astype(F32)
    cols = (pos % GRID_W).astype(F32)
    freqs = ROPE_BASE ** (-jnp.arange(m, dtype=F32) / m)
    ang_r = rows[:, None] * freqs[None, :]
    ang_c = cols[:, None] * freqs[None, :]
    cos_h = jnp.concatenate([jnp.cos(ang_r), jnp.cos(ang_r), jnp.cos(ang_c), jnp.cos(ang_c)], axis=-1)
    sin_h = jnp.concatenate([-jnp.sin(ang_r), jnp.sin(ang_r), -jnp.sin(ang_c), jnp.sin(ang_c)], axis=-1)
    reps = LANES // ATTN_HEAD_DIM
    cos_l = jnp.tile(cos_h, (1, reps))
    sin_l = jnp.tile(sin_h, (1, reps))
    cos_t = jnp.concatenate([jnp.ones((n_ctx, LANES), F32), cos_l], axis=0)
    sin_t = jnp.concatenate([jnp.zeros((n_ctx, LANES), F32), sin_l], axis=0)
    return cos_t, sin_t


def _lane_row(values, offset):
    return jnp.zeros((1, LANES), F32).at[0, offset:offset + values.shape[0]].set(values.astype(F32))


def kernel(x, c, ctx, c_ctx, w_mod, b_mod, norm1_w, norm2_w, w_in, q_norm_w, k_norm_w, conv_w, a_log, dt_bias,
           o_norm_w, w_out, router_w, w_gate, w_up, w_down):
    bsz, seq, d = x.shape
    n_ctx = ctx.shape[1]
    assert w_mod.shape[0] == 1, "single layer: the last layer's context outputs are never consumed"
    assert bsz < MOD_ROWS and N_EXPERTS == router_w.shape[-1]
    cap = EC_CAPACITY_FACTOR * seq // N_EXPERTS

    cc = jnp.concatenate([c, c_ctx[None, :], jnp.zeros((MOD_ROWS - bsz - 1, d), F32)], axis=0)
    mod = _mod_call(cc, w_mod[0], b_mod[0][None, :])

    w_pad = jnp.pad(w_in[0], ((0, 0), (0, LANES - N_GATES))).astype(BF16)
    seg = np.arange(QK_W) // ATTN_HEAD_DIM
    bd = jnp.asarray(seg[:, None] == seg[None, :], BF16)
    qkw = jnp.concatenate([jnp.tile(q_norm_w[0], ATTN_HEADS), jnp.tile(k_norm_w[0], ATTN_KV_HEADS)])[None, :]
    cos_t, sin_t = _rope_tables(n_ctx, seq)
    alog_l = _lane_row(a_log[0].reshape(-1), N_GATES // 2)
    dtb_l = _lane_row(dt_bias[0].reshape(-1), N_GATES // 2)
    q, kv, dn, z, gates, gates_t = _inproj_call(x, ctx, mod, norm1_w[0][None, :], w_pad, bd, qkw, cos_t, sin_t,
                                                alog_l, dtb_l)

    att = _attn_call(q, kv, n_ctx)
    o_dn = _dn_call(dn, conv_w[0], gates, gates_t, n_ctx)

    rw = jnp.pad(router_w[0], ((0, 0), (0, LANES - N_EXPERTS))).astype(BF16)
    x1, h2, afft = _outproj_call(x, att, o_dn, z, mod, o_norm_w[0][None, :], w_out[0].astype(BF16),
                                 norm2_w[0][None, :], rw, n_ctx)

    slot, gate, tok = _route_call(afft, cap)
    xg, gs = _gather_call(slot, gate, h2, cap)
    y = _ffn_call(xg, gs, w_gate[0], w_up[0], w_down[0])
    return _combine_call(x1, tok, y, mod, cap)
```

```python
import functools
import math

import numpy as np
import jax
import jax.numpy as jnp
from jax import lax
from jax.experimental import pallas as pl
from jax.experimental.pallas import tpu as pltpu

F32 = jnp.float32
BF16 = jnp.bfloat16
I32 = jnp.int32

GRID_W = 64
EPS = 1e-6
ATTN_HEADS = 8
ATTN_KV_HEADS = 2
ATTN_HEAD_DIM = 64
ROPE_BASE = 10000.0
DN_HEADS = 4
DN_HEAD_DIM = 128
DN_CONV_W = 5
N_EXPERTS = 16
EC_CAPACITY_FACTOR = 2

ATTN_Q_W = ATTN_HEADS * ATTN_HEAD_DIM
ATTN_KV_W = ATTN_KV_HEADS * ATTN_HEAD_DIM
DN_W = DN_HEADS * DN_HEAD_DIM
QK_W = ATTN_Q_W + ATTN_KV_W
ATT_W = ATTN_Q_W + 2 * ATTN_KV_W
N_GATES = 4 * DN_HEADS

LANES = 128
SUBLANES = 8
VMEM_LIMIT = 56 * 1024 * 1024
F32_MIN_EXP = -126
F32_MANTISSA_BITS = 23

TOK_TILE = 256
Q_TILE = 512
ATTN_KEY_CHUNK = 768
OUT_TILE = 512
DN_CHUNK = 128
DN_GROUP = 6
DN_CONV_UNROLL = 2
DN_HEADS_PER_STEP = 2
GATHER_EXPERTS = 4
FF_TILE = 512
COMB_TILE = 512
MOD_ROWS = 16


def _params(*sem):
    return pltpu.CompilerParams(dimension_semantics=sem, vmem_limit_bytes=VMEM_LIMIT)


def _silu(v):
    half = 0.5 * v
    return half + half * jnp.tanh(half)


def _dot(a, b):
    return jnp.dot(a, b, preferred_element_type=F32)


def _dot_nt(a, b):
    return lax.dot_general(a, b, (((1,), (1,)), ((), ())), preferred_element_type=F32)


def _mod_kernel(c_ref, w_ref, b_ref, o_ref):
    sc = _silu(c_ref[...]).astype(BF16)
    o_ref[...] = _dot(sc, w_ref[...].astype(BF16)) + b_ref[...]


def _mod_call(cc, w_mod, b_mod):
    d, n = w_mod.shape
    return pl.pallas_call(
        _mod_kernel,
        grid=(n // d,),
        in_specs=[pl.BlockSpec((MOD_ROWS, d), lambda i: (0, 0)),
                  pl.BlockSpec((d, d), lambda i: (0, i)),
                  pl.BlockSpec((1, d), lambda i: (0, i))],
        out_specs=pl.BlockSpec((MOD_ROWS, d), lambda i: (0, i)),
        out_shape=jax.ShapeDtypeStruct((MOD_ROWS, n), F32),
        compiler_params=_params("arbitrary"),
        name="mod",
    )(cc, w_mod, b_mod)


def _inproj_kernel(ctx_row, x_ref, ctx_ref, mod_ref, n1_ref, w_ref, bd_ref, qkw_ref, cos_ref, sin_ref,
                   alog_ref, dtb_ref, q_ref, kv_ref, dn_ref, z_ref, g_ref, gt_ref):
    d = x_ref.shape[-1]
    b = pl.program_id(0)
    is_ctx = pl.program_id(1) == 0
    xin = jnp.where(is_ctx, ctx_ref[0], x_ref[0])
    row = jnp.where(is_ctx, ctx_row, b)
    shift = mod_ref[pl.ds(row, 1), 0:d]
    scale = mod_ref[pl.ds(row, 1), d:2 * d]
    hn = xin * lax.rsqrt(jnp.mean(xin * xin, axis=-1, keepdims=True) + EPS) * n1_ref[...]
    h = (hn * (1.0 + scale) + shift).astype(BF16)
    p = _dot(h, w_ref[...])

    qk = p[:, 0:QK_W]
    sq = qk * qk
    hi = sq.astype(BF16)
    lo = (sq - hi.astype(F32)).astype(BF16)
    ms = (_dot(hi, bd_ref[...]) + _dot(lo, bd_ref[...])) * (1.0 / ATTN_HEAD_DIM)
    qkn = qk * lax.rsqrt(ms + EPS) * qkw_ref[...]

    lane = lax.broadcasted_iota(I32, (1, LANES), 1)
    first_half = (lane % 32) < 16
    cos = cos_ref[...]
    sin = sin_ref[...]
    rot = []
    for i in range(QK_W // LANES):
        blk = qkn[:, i * LANES:(i + 1) * LANES]
        partner = jnp.where(first_half, pltpu.roll(blk, LANES - 16, axis=1), pltpu.roll(blk, 16, axis=1))
        rot.append(blk * cos + partner * sin)
    for i in range(ATTN_Q_W // LANES):
        q_ref[0, :, i * LANES:(i + 1) * LANES] = (rot[i] * (ATTN_HEAD_DIM ** -0.5)).astype(BF16)

    low = lane < ATTN_HEAD_DIM
    k2 = rot[ATTN_Q_W // LANES]
    v2 = p[:, QK_W:ATT_W]
    for src, base in ((k2, 0), (v2, 2)):
        swapped = pltpu.roll(src, ATTN_HEAD_DIM, axis=1)
        kv_ref[0, 0, base + 0] = jnp.where(low, src, 0.0).astype(BF16)
        kv_ref[0, 0, base + 1] = jnp.where(low, 0.0, swapped).astype(BF16)
        kv_ref[0, 1, base + 0] = jnp.where(low, swapped, 0.0).astype(BF16)
        kv_ref[0, 1, base + 1] = jnp.where(low, 0.0, src).astype(BF16)

    dn_ref[0] = p[:, ATT_W:ATT_W + 3 * DN_W]
    z_ref[0] = p[:, ATT_W + 3 * DN_W:ATT_W + 4 * DN_W]

    gp = p[:, ATT_W + 4 * DN_W:]
    beta = jax.nn.sigmoid(gp)
    xa = gp + dtb_ref[...]
    softplus = jnp.maximum(xa, 0.0) + jnp.log1p(jnp.exp(-jnp.abs(xa)))
    decay = -jnp.exp(alog_ref[...]) * softplus
    gates = jnp.where(lane < N_GATES // 2, beta, jnp.where(lane < N_GATES, decay, 0.0))
    g_ref[0] = gates
    gt_ref[0] = gates.T[0:N_GATES, :]


def _inproj_call(x, ctx, mod, n1, w_pad, bd, qkw, cos_t, sin_t, alog_l, dtb_l):
    bsz, seq, d = x.shape
    n_ctx = ctx.shape[1]
    tot = n_ctx + seq
    t = TOK_TILE
    assert n_ctx == t and seq % t == 0
    nw = w_pad.shape[1]
    full = lambda shape: pl.BlockSpec(shape, lambda b, j: (0,) * len(shape))
    return pl.pallas_call(
        functools.partial(_inproj_kernel, bsz),
        grid=(bsz, tot // t),
        in_specs=[pl.BlockSpec((1, t, d), lambda b, j: (b, jnp.maximum(j - 1, 0), 0)),
                  pl.BlockSpec((1, t, d), lambda b, j: (b, 0, 0)),
                  full((MOD_ROWS, mod.shape[1])),
                  full((1, d)),
                  full((d, nw)),
                  full((QK_W, QK_W)),
                  full((1, QK_W)),
                  pl.BlockSpec((t, LANES), lambda b, j: (j, 0)),
                  pl.BlockSpec((t, LANES), lambda b, j: (j, 0)),
                  full((1, LANES)),
                  full((1, LANES))],
        out_specs=[pl.BlockSpec((1, t, ATTN_Q_W), lambda b, j: (b, jnp.maximum(j - 1, 0), 0)),
                   pl.BlockSpec((1, ATTN_KV_HEADS, 4, t, LANES), lambda b, j: (b, 0, 0, j, 0)),
                   pl.BlockSpec((1, t, 3 * DN_W), lambda b, j: (b, j, 0)),
                   pl.BlockSpec((1, t, DN_W), lambda b, j: (b, jnp.maximum(j - 1, 0), 0)),
                   pl.BlockSpec((1, t, LANES), lambda b, j: (b, j, 0)),
                   pl.BlockSpec((1, N_GATES, t), lambda b, j: (b, 0, j))],
        out_shape=[jax.ShapeDtypeStruct((bsz, seq, ATTN_Q_W), BF16),
                   jax.ShapeDtypeStruct((bsz, ATTN_KV_HEADS, 4, tot, LANES), BF16),
                   jax.ShapeDtypeStruct((bsz, tot, 3 * DN_W), F32),
                   jax.ShapeDtypeStruct((bsz, seq, DN_W), F32),
                   jax.ShapeDtypeStruct((bsz, tot, LANES), F32),
                   jax.ShapeDtypeStruct((bsz, N_GATES, tot), F32)],
        compiler_params=_params("arbitrary", "arbitrary"),
        name="inproj",
    )(x, ctx, mod, n1, w_pad, bd, qkw, cos_t, sin_t, alog_l, dtb_l)


def _attn_kernel(q_ref, kv_ref, o_ref):
    grp = ATTN_HEADS // ATTN_KV_HEADS

    def scores(h):
        qp = q_ref[0, :, (h // 2) * LANES:(h // 2 + 1) * LANES]
        return _dot_nt(qp, kv_ref[0, h // grp, h % 2])

    s_next = scores(0)
    acc = None
    for h in range(ATTN_HEADS):
        s = s_next
        if h + 1 < ATTN_HEADS:
            s_next = scores(h + 1)
        m = jnp.max(s, axis=-1, keepdims=True)
        n_keys = s.shape[-1]
        o = denom = None
        for k0 in range(0, n_keys, ATTN_KEY_CHUNK):
            e = jnp.exp(s[:, k0:k0 + ATTN_KEY_CHUNK] - m)
            part = _dot(e.astype(BF16), kv_ref[0, h // grp, 2 + h % 2, k0:k0 + ATTN_KEY_CHUNK, :])
            part_sum = jnp.sum(e, axis=-1, keepdims=True)
            o = part if o is None else o + part
            denom = part_sum if denom is None else denom + part_sum
        o = o / denom
        if h % 2 == 0:
            acc = o
        else:
            o_ref[0, :, (h // 2) * LANES:(h // 2 + 1) * LANES] = (acc + o).astype(BF16)


def _attn_call(q, kv):
    bsz, seq, _ = q.shape
    tot = kv.shape[3]
    tq = Q_TILE
    assert seq % tq == 0 and tot % ATTN_KEY_CHUNK == 0
    return pl.pallas_call(
        _attn_kernel,
        grid=(bsz, seq // tq),
        in_specs=[pl.BlockSpec((1, tq, ATTN_Q_W), lambda b, i: (b, i, 0)),
                  pl.BlockSpec((1, ATTN_KV_HEADS, 4, tot, LANES), lambda b, i: (b, 0, 0, 0, 0))],
        out_specs=pl.BlockSpec((1, tq, ATTN_Q_W), lambda b, i: (b, i, 0)),
        out_shape=jax.ShapeDtypeStruct((bsz, seq, ATTN_Q_W), BF16),
        compiler_params=_params("arbitrary", "arbitrary"),
        name="attn",
    )(q, kv)


def _unit_tri_inverses_minus_eye(mats, ij_xor):
    n = mats[0].shape[0]
    corrs = [-jnp.where(ij_xor < 2, a, 0.0) for a in mats]
    m = 2
    while m < n:
        joins = (ij_xor >= m) & (ij_xor < 2 * m)
        xs = [jnp.where(joins, a, 0.0) for a in mats]
        ys = [x + _dot(corr.astype(BF16), x.astype(BF16)) for corr, x in zip(corrs, xs)]
        corrs = [corr - (y + _dot(y.astype(BF16), corr.astype(BF16))) for corr, y in zip(corrs, ys)]
        m *= 2
    return corrs


def _dn_kernel(n_ctx, dq_ref, dk_ref, dv_ref, cq_ref, ck_ref, cv_ref, g_ref, gt_ref, o_ref,
               xq_s, xk_s, xv_s, q_s, k_s, v_s, kk_s, qk_s, mq_s, n_s, op_s, gl_s):
    c = DN_CHUNK
    tot = dq_ref.shape[1]
    n_chunks = tot // c
    ctx_chunks = n_ctx // c
    lat_chunks = n_chunks - ctx_chunks
    dk = DN_HEAD_DIM
    ii = lax.broadcasted_iota(I32, (c, c), 0)
    jj = lax.broadcasted_iota(I32, (c, c), 1)
    ij_xor = ii ^ jj
    lane = lax.broadcasted_iota(I32, (1, LANES), 1)

    for hh in range(DN_HEADS_PER_STEP):
        _dn_prepare_head(hh, pl.program_id(1) * DN_HEADS_PER_STEP + hh, n_ctx, ii, jj, ij_xor, lane,
                         dq_ref, dk_ref, dv_ref, cq_ref, ck_ref, cv_ref, g_ref, gt_ref,
                         xq_s, xk_s, xv_s, q_s, k_s, v_s, kk_s, qk_s, mq_s, n_s, op_s, gl_s)

    def scan(first_chunk, count, emit, states):
        rows = dk + c if emit else dk

        def step(t, st):
            chains = [(hh, d, first_chunk + t if d == 0 else first_chunk + count - 1 - t)
                      for hh in range(DN_HEADS_PER_STEP) for d in range(2)]
            prods = [_dot(mq_s[hh, d, i, 0:rows, :], s.astype(BF16)) for (hh, d, i), s in zip(chains, st)]
            new = []
            for (hh, d, i), s, r in zip(chains, st, prods):
                if emit:
                    ro = pl.multiple_of((i - ctx_chunks) * c, c)
                    o_ref[0, pl.ds(ro, c), hh * dk:(hh + 1) * dk] += r[dk:dk + c] + op_s[hh, d, i]
                new.append(s * gl_s[hh, d, i][0:1, :] + r[0:dk] + n_s[hh, d, i])
            return tuple(new)
        return lax.fori_loop(0, count, step, states)

    o_ref[...] = jnp.zeros(o_ref.shape, F32)
    zero_state = jnp.zeros((dk, dk), F32)
    states = scan(0, ctx_chunks, False, (zero_state,) * (2 * DN_HEADS_PER_STEP))
    scan(ctx_chunks, lat_chunks, True, states)


def _dn_prepare_head(hh, head, n_ctx, ii, jj, ij_xor, lane,
                     dq_ref, dk_ref, dv_ref, cq_ref, ck_ref, cv_ref, g_ref, gt_ref,
                     xq_s, xk_s, xv_s, q_s, k_s, v_s, kk_s, qk_s, mq_s, n_s, op_s, gl_s):
    c = DN_CHUNK
    tot = dq_ref.shape[1]
    n_chunks = tot // c
    ctx_chunks = n_ctx // c
    pad = SUBLANES
    dk = DN_HEAD_DIM
    half = DN_CONV_W // 2
    cols = slice(hh * dk, (hh + 1) * dk)

    zeros_pad = jnp.zeros((pad, dk), F32)
    lat0 = 2 * pad + n_ctx
    for src, dst in ((dq_ref, xq_s), (dk_ref, xk_s), (dv_ref, xv_s)):
        dst[0:pad, :] = zeros_pad
        dst[pad:pad + n_ctx, :] = src[0, 0:n_ctx, cols]
        dst[pad + n_ctx:lat0, :] = zeros_pad
        dst[lat0:lat0 + tot - n_ctx, :] = src[0, n_ctx:tot, cols]
        dst[lat0 + tot - n_ctx:lat0 + tot - n_ctx + pad, :] = zeros_pad

    def conv_chunk(i, carry):
        r0 = pl.multiple_of(i * c, c)
        rp = r0 + jnp.where(i >= ctx_chunks, 2 * pad, pad)

        def conv(x_s, cw_ref):
            acc = None
            for s in range(DN_CONV_W):
                term = x_s[pl.ds(rp - half + s, c), :] * cw_ref[s:s + 1, cols]
                acc = term if acc is None else acc + term
            return _silu(acc)

        qc = conv(xq_s, cq_ref)
        kc = conv(xk_s, ck_ref)
        vc = conv(xv_s, cv_ref)
        qc = qc * lax.rsqrt(jnp.sum(qc * qc, axis=-1, keepdims=True) + EPS) * (dk ** -0.5)
        kc = kc * lax.rsqrt(jnp.sum(kc * kc, axis=-1, keepdims=True) + EPS)
        q_s[pl.ds(r0, c), :] = qc
        k_s[pl.ds(r0, c), :] = kc
        v_s[pl.ds(r0, c), :] = vc
        kb = kc.astype(BF16)
        kk_s[i] = _dot_nt(kb, kb)
        qk_s[i] = _dot_nt(qc.astype(BF16), kb)
        return carry

    lax.fori_loop(0, n_chunks, conv_chunk, 0, unroll=DN_CONV_UNROLL)

    def local_group(grp, carry):
        chains = []
        for k in range(DN_GROUP):
            i = grp * DN_GROUP + k
            r0 = pl.multiple_of(i * c, c)
            gates = g_ref[0, pl.ds(r0, c), :]
            kc = k_s[pl.ds(r0, c), :]
            qc = q_s[pl.ds(r0, c), :]
            vc = v_s[pl.ds(r0, c), :]
            kk = kk_s[i]
            qk = qk_s[i]
            kt = kc.T
            for d in range(2):
                beta_col = jnp.sum(jnp.where(lane == d * DN_HEADS + head, gates, 0.0), axis=-1, keepdims=True)
                g_col = jnp.sum(jnp.where(lane == N_GATES // 2 + d * DN_HEADS + head, gates, 0.0),
                                axis=-1, keepdims=True)
                g_row = gt_ref[0, pl.ds(N_GATES // 2 + d * DN_HEADS + head, 1), pl.ds(i, 1), :].reshape(1, c)
                incl = (jj <= ii) if d == 0 else (jj >= ii)
                strict = (jj < ii) if d == 0 else (jj > ii)
                incl_t = (ii <= jj) if d == 0 else (ii >= jj)
                gcum_col = jnp.sum(jnp.where(incl, g_row, 0.0), axis=1, keepdims=True)
                gcum_row = jnp.sum(jnp.where(incl_t, g_col, 0.0), axis=0, keepdims=True)
                total = jnp.sum(g_row, axis=1, keepdims=True)
                decay = jnp.exp(jnp.where(incl, gcum_col - gcum_row, -jnp.inf))
                e_col = jnp.exp(gcum_col)
                kdt = (kt * jnp.exp(total - gcum_row)).astype(BF16)
                qkm = (qk * decay).astype(BF16)
                chains.append(dict(
                    d=d, i=i,
                    a=jnp.where(strict, beta_col * kk * decay, 0.0),
                    rhs=jnp.concatenate([vc * beta_col, kc * (beta_col * e_col)], axis=1),
                    lhs=jnp.concatenate([kdt, qkm], axis=0),
                    qd=qc * e_col,
                    g_last=jnp.exp(total)))
        corrs = _unit_tri_inverses_minus_eye([ch["a"] for ch in chains], ij_xor)
        sols = [ch["rhs"] + _dot(corr.astype(BF16), ch["rhs"].astype(BF16)) for ch, corr in zip(chains, corrs)]
        prods = [_dot(ch["lhs"], sol.astype(BF16)) for ch, sol in zip(chains, sols)]
        for ch, r in zip(chains, prods):
            d, i = ch["d"], ch["i"]
            mq_s[hh, d, i, 0:dk, :] = (-r[0:dk, dk:2 * dk]).astype(BF16)
            mq_s[hh, d, i, dk:dk + c, :] = (ch["qd"] - r[dk:dk + c, dk:2 * dk]).astype(BF16)
            n_s[hh, d, i] = r[0:dk, 0:dk]
            op_s[hh, d, i] = r[dk:dk + c, 0:dk]
            gl_s[hh, d, i] = jnp.broadcast_to(ch["g_last"], (SUBLANES, dk))
        return carry

    lax.fori_loop(0, n_chunks // DN_GROUP, local_group, 0)


def _dn_call(dn, conv_w, gates, gates_t, n_ctx):
    bsz, tot, _ = dn.shape
    seq = tot - n_ctx
    c = DN_CHUNK
    assert n_ctx % c == 0 and seq % c == 0
    n_chunks = tot // c
    dk = DN_HEAD_DIM
    gt4 = gates_t.reshape(bsz, N_GATES, n_chunks, c)
    hp = DN_HEADS_PER_STEP
    steps = DN_HEADS // hp
    col = lambda off: pl.BlockSpec((1, tot, hp * dk), lambda b, h: (b, 0, off + h))
    cw = lambda off: pl.BlockSpec((DN_CONV_W, hp * dk), lambda b, h: (0, off + h))
    padded = tot + 3 * SUBLANES
    assert n_chunks % DN_GROUP == 0 and n_chunks % DN_CONV_UNROLL == 0 and DN_HEADS % hp == 0
    return pl.pallas_call(
        functools.partial(_dn_kernel, n_ctx),
        grid=(bsz, steps),
        in_specs=[col(0), col(steps), col(2 * steps),
                  cw(0), cw(steps), cw(2 * steps),
                  pl.BlockSpec((1, tot, LANES), lambda b, h: (b, 0, 0)),
                  pl.BlockSpec((1, N_GATES, n_chunks, c), lambda b, h: (b, 0, 0, 0))],
        out_specs=pl.BlockSpec((1, seq, hp * dk), lambda b, h: (b, 0, h)),
        out_shape=jax.ShapeDtypeStruct((bsz, seq, DN_W), F32),
        scratch_shapes=[pltpu.VMEM((padded, dk), F32)] * 3
        + [pltpu.VMEM((tot, dk), F32)] * 3
        + [pltpu.VMEM((n_chunks, c, c), F32)] * 2
        + [pltpu.VMEM((hp, 2, n_chunks, dk + c, dk), BF16),
           pltpu.VMEM((hp, 2, n_chunks, dk, dk), F32),
           pltpu.VMEM((hp, 2, n_chunks, c, dk), F32),
           pltpu.VMEM((hp, 2, n_chunks, SUBLANES, dk), F32)],
        compiler_params=_params("arbitrary", "arbitrary"),
        name="dn",
    )(dn, dn, dn, conv_w, conv_w, conv_w, gates, gt4)


def _outproj_kernel(x_ref, att_ref, o_ref, z_ref, mod_ref, onw_ref, wo_ref, n2_ref, rw_ref,
                    x1_ref, h2_ref, afft_ref):
    d = x_ref.shape[-1]
    b = pl.program_id(0)
    gate1 = mod_ref[pl.ds(b, 1), 2 * d:3 * d]
    shift2 = mod_ref[pl.ds(b, 1), 3 * d:4 * d]
    scale2 = mod_ref[pl.ds(b, 1), 4 * d:5 * d]
    y = _dot(att_ref[0], wo_ref[0:ATTN_Q_W, :])
    for h in range(DN_HEADS):
        sl = slice(h * DN_HEAD_DIM, (h + 1) * DN_HEAD_DIM)
        oh = o_ref[0, :, sl]
        on = oh * lax.rsqrt(jnp.mean(oh * oh, axis=-1, keepdims=True) + EPS) * onw_ref[...]
        yh = (on * _silu(z_ref[0, :, sl])).astype(BF16)
        y = y + _dot(yh, wo_ref[ATTN_Q_W + h * DN_HEAD_DIM:ATTN_Q_W + (h + 1) * DN_HEAD_DIM, :])
    x1 = x_ref[0] + gate1 * y
    x1_ref[0] = x1
    hn = x1 * lax.rsqrt(jnp.mean(x1 * x1, axis=-1, keepdims=True) + EPS) * n2_ref[...]
    h2 = (hn * (1.0 + scale2) + shift2).astype(BF16)
    h2_ref[0] = h2
    logits = _dot(h2, rw_ref[...])
    lane = lax.broadcasted_iota(I32, (1, LANES), 1)
    logits = jnp.where(lane < N_EXPERTS, logits, -jnp.inf)
    e = jnp.exp(logits - jnp.max(logits, axis=-1, keepdims=True))
    aff = e / jnp.sum(e, axis=-1, keepdims=True)
    afft_ref[0] = aff.T[0:N_EXPERTS, :]


def _outproj_call(x, att, o_dn, z, mod, onw, wo, n2, rw):
    bsz, seq, d = x.shape
    t = OUT_TILE
    assert seq % t == 0
    full = lambda shape: pl.BlockSpec(shape, lambda b, j: (0,) * len(shape))
    return pl.pallas_call(
        _outproj_kernel,
        grid=(bsz, seq // t),
        in_specs=[pl.BlockSpec((1, t, d), lambda b, j: (b, j, 0)),
                  pl.BlockSpec((1, t, ATTN_Q_W), lambda b, j: (b, j, 0)),
                  pl.BlockSpec((1, t, DN_W), lambda b, j: (b, j, 0)),
                  pl.BlockSpec((1, t, DN_W), lambda b, j: (b, j, 0)),
                  full((MOD_ROWS, mod.shape[1])),
                  full((1, DN_HEAD_DIM)),
                  full(wo.shape),
                  full((1, d)),
                  full((d, LANES))],
        out_specs=[pl.BlockSpec((1, t, d), lambda b, j: (b, j, 0)),
                   pl.BlockSpec((1, t, d), lambda b, j: (b, j, 0)),
                   pl.BlockSpec((1, N_EXPERTS, t), lambda b, j: (b, 0, j))],
        out_shape=[jax.ShapeDtypeStruct((bsz, seq, d), F32),
                   jax.ShapeDtypeStruct((bsz, seq, d), BF16),
                   jax.ShapeDtypeStruct((bsz, N_EXPERTS, seq), F32)],
        compiler_params=_params("arbitrary", "arbitrary"),
        name="outproj",
    )(x, att, o_dn, z, mod, onw, wo, n2, rw)


def _route_kernel(cap, afft_ref, slot_ref, gate_ref, tok_ref):
    aff = afft_ref[0]
    n_e, n_t = aff.shape

    def enough(cand):
        return jnp.sum(jnp.where(aff >= cand, 1.0, 0.0), axis=-1, keepdims=True) >= cap

    tiny = 2.0 ** F32_MIN_EXP
    cur = jnp.full((n_e, 1), tiny, F32)
    any_normal = enough(cur)
    shift = 1 << (-F32_MIN_EXP).bit_length()
    while shift > 1:
        shift //= 2
        cand = cur * (2.0 ** shift)
        cur = jnp.where(enough(cand), cand, cur)

    def refine(_, state):
        cur, step = state
        cand = cur + step
        return jnp.where(enough(cand), cand, cur), step * 0.5

    cur, _ = lax.fori_loop(0, F32_MANTISSA_BITS, refine, (cur, cur * 0.5))
    thr = jnp.where(any_normal, cur, 0.0)
    need = cap - jnp.sum(jnp.where(aff > thr, 1.0, 0.0), axis=-1, keepdims=True)

    upper = (lax.broadcasted_iota(I32, (LANES, LANES), 0) < lax.broadcasted_iota(I32, (LANES, LANES), 1))
    upper = jnp.where(upper, 1.0, 0.0).astype(BF16)
    run = jnp.zeros((2 * n_e, 1), F32)
    for blk in range(n_t // LANES):
        sl = slice(blk * LANES, (blk + 1) * LANES)
        gt = aff[:, sl] > thr
        eq = aff[:, sl] == thr
        x = jnp.concatenate([jnp.where(gt, 1.0, 0.0), jnp.where(eq, 1.0, 0.0)], axis=0)
        cum = _dot(x.astype(BF16), upper) + run
        run = run + jnp.sum(x, axis=-1, keepdims=True)
        cum_gt = cum[0:n_e]
        cum_eq = cum[n_e:2 * n_e]
        sel = gt | (eq & (cum_eq < need))
        slot = jnp.where(sel, cum_gt + jnp.minimum(cum_eq, need), -1.0)
        slot_ref[0, :, sl] = slot.astype(I32)
        gate_ref[0, :, sl] = jnp.where(sel, aff[:, sl], 0.0)

    stacked = jnp.concatenate([slot_ref[0].astype(F32), gate_ref[0],
                               jnp.zeros((LANES - 2 * n_e, n_t), F32)], axis=0)
    tok_ref[0] = stacked.T


def _route_call(afft, cap):
    bsz, n_e, n_t = afft.shape
    row = pl.BlockSpec((1, n_e, n_t), lambda b: (b, 0, 0))
    return pl.pallas_call(
        functools.partial(_route_kernel, cap),
        grid=(bsz,),
        in_specs=[row],
        out_specs=[row, row, pl.BlockSpec((1, n_t, LANES), lambda b: (b, 0, 0))],
        out_shape=[jax.ShapeDtypeStruct((bsz, n_e, n_t), I32),
                   jax.ShapeDtypeStruct((bsz, n_e, n_t), F32),
                   jax.ShapeDtypeStruct((bsz, n_t, LANES), F32)],
        compiler_params=_params("arbitrary"),
        name="route",
    )(afft)


def _gather_kernel(cap, slot_ref, gate_ref, h_ref, xg_ref, gs_ref):
    ge = xg_ref.shape[0]
    n_t = h_ref.shape[1]
    e0 = pl.program_id(1) * ge
    j = lax.broadcasted_iota(I32, (cap, 1), 0)
    for i in range(ge):
        slot = slot_ref[0, pl.ds(e0 + i, 1), :]
        hit = slot == j
        xg_ref[i, 0] = _dot(jnp.where(hit, 1.0, 0.0).astype(BF16), h_ref[0]).astype(BF16)
        gate = gate_ref[0, pl.ds(e0 + i, 1), :]
        gs_ref[i, 0] = jnp.sum(jnp.where(hit, gate, 0.0), axis=-1, keepdims=True)


def _gather_call(slot, gate, h2, cap):
    bsz, n_e, n_t = slot.shape
    d = h2.shape[-1]
    ge = GATHER_EXPERTS
    row = pl.BlockSpec((1, n_e, n_t), lambda b, g: (b, 0, 0))
    return pl.pallas_call(
        functools.partial(_gather_kernel, cap),
        grid=(bsz, n_e // ge),
        in_specs=[row, row, pl.BlockSpec((1, n_t, d), lambda b, g: (b, 0, 0))],
        out_specs=[pl.BlockSpec((ge, 1, cap, d), lambda b, g: (g, b, 0, 0)),
                   pl.BlockSpec((ge, 1, cap, 1), lambda b, g: (g, b, 0, 0))],
        out_shape=[jax.ShapeDtypeStruct((n_e, bsz, cap, d), BF16),
                   jax.ShapeDtypeStruct((n_e, bsz, cap, 1), F32)],
        compiler_params=_params("arbitrary", "arbitrary"),
        name="gather",
    )(slot, gate, h2)


def _ffn_kernel(xg_ref, gs_ref, wg_ref, wu_ref, wd_ref, y_ref, acc_ref):
    f = pl.program_id(1)
    bsz, cap, d = xg_ref.shape[1:]
    xg = xg_ref[0].reshape(bsz * cap, d)
    hid = _silu(_dot(xg, wg_ref[0].astype(BF16))) * _dot(xg, wu_ref[0].astype(BF16))
    part = _dot(hid.astype(BF16), wd_ref[0].astype(BF16))

    @pl.when(f == 0)
    def _():
        acc_ref[...] = part

    @pl.when(f > 0)
    def _():
        acc_ref[...] += part

    @pl.when(f == pl.num_programs(1) - 1)
    def _():
        y = acc_ref[...] * gs_ref[0].reshape(bsz * cap, 1)
        y_ref[0] = y.astype(BF16).reshape(bsz, cap, d)


def _ffn_call(xg, gs, w_gate, w_up, w_down):
    n_e, bsz, cap, d = xg.shape
    ff = w_gate.shape[-1]
    tf = FF_TILE
    return pl.pallas_call(
        _ffn_kernel,
        grid=(n_e, ff // tf),
        in_specs=[pl.BlockSpec((1, bsz, cap, d), lambda e, f: (e, 0, 0, 0)),
                  pl.BlockSpec((1, bsz, cap, 1), lambda e, f: (e, 0, 0, 0)),
                  pl.BlockSpec((1, d, tf), lambda e, f: (e, 0, f)),
                  pl.BlockSpec((1, d, tf), lambda e, f: (e, 0, f)),
                  pl.BlockSpec((1, tf, d), lambda e, f: (e, f, 0))],
        out_specs=pl.BlockSpec((1, bsz, cap, d), lambda e, f: (e, 0, 0, 0)),
        out_shape=jax.ShapeDtypeStruct((n_e, bsz, cap, d), BF16),
        scratch_shapes=[pltpu.VMEM((bsz * cap, d), F32)],
        compiler_params=_params("arbitrary", "arbitrary"),
        name="ffn",
    )(xg, gs, w_gate, w_up, w_down)


def _combine_kernel(cap, x1_ref, tok_ref, y_ref, mod_ref, o_ref):
    d = x1_ref.shape[-1]
    n_e = y_ref.shape[0]
    b = pl.program_id(0)
    gate2 = mod_ref[pl.ds(b, 1), 5 * d:6 * d]
    tok = tok_ref[0]
    j = lax.broadcasted_iota(I32, (1, cap), 1)
    acc = None
    for e in range(n_e):
        slot = tok[:, e:e + 1].astype(I32)
        onehot = jnp.where(slot == j, 1.0, 0.0).astype(BF16)
        part = _dot(onehot, y_ref[e, 0])
        acc = part if acc is None else acc + part
    o_ref[0] = x1_ref[0] + gate2 * acc


def _combine_call(x1, tok, y, mod, cap):
    bsz, seq, d = x1.shape
    n_e = y.shape[0]
    t = COMB_TILE
    return pl.pallas_call(
        functools.partial(_combine_kernel, cap),
        grid=(bsz, seq // t),
        in_specs=[pl.BlockSpec((1, t, d), lambda b, j: (b, j, 0)),
                  pl.BlockSpec((1, t, LANES), lambda b, j: (b, j, 0)),
                  pl.BlockSpec((n_e, 1, cap, d), lambda b, j: (0, b, 0, 0)),
                  pl.BlockSpec((MOD_ROWS, mod.shape[1]), lambda b, j: (0, 0))],
        out_specs=pl.BlockSpec((1, t, d), lambda b, j: (b, j, 0)),
        out_shape=jax.ShapeDtypeStruct((bsz, seq, d), F32),
        compiler_params=_params("arbitrary", "arbitrary"),
        name="combine",
    )(x1, tok, y, mod)


def _rope_tables(n_ctx, seq):
    m = ATTN_HEAD_DIM // 4
    pos = jnp.arange(seq, dtype=jnp.int32)
    rows = (pos // GRID_W).astype(F32)
    cols = (pos % GRID_W).astype(F32)
    freqs = ROPE_BASE ** (-jnp.arange(m, dtype=F32) / m)
    ang_r = rows[:, None] * freqs[None, :]
    ang_c = cols[:, None] * freqs[None, :]
    cos_h = jnp.concatenate([jnp.cos(ang_r), jnp.cos(ang_r), jnp.cos(ang_c), jnp.cos(ang_c)], axis=-1)
    sin_h = jnp.concatenate([-jnp.sin(ang_r), jnp.sin(ang_r), -jnp.sin(ang_c), jnp.sin(ang_c)], axis=-1)
    reps = LANES // ATTN_HEAD_DIM
    cos_l = jnp.tile(cos_h, (1, reps))
    sin_l = jnp.tile(sin_h, (1, reps))
    cos_t = jnp.concatenate([jnp.ones((n_ctx, LANES), F32), cos_l], axis=0)
    sin_t = jnp.concatenate([jnp.zeros((n_ctx, LANES), F32), sin_l], axis=0)
    return cos_t, sin_t


def _lane_row(values, offset):
    return jnp.zeros((1, LANES), F32).at[0, offset:offset + values.shape[0]].set(values.astype(F32))


def kernel(x, c, ctx, c_ctx, w_mod, b_mod, norm1_w, norm2_w, w_in, q_norm_w, k_norm_w, conv_w, a_log, dt_bias,
           o_norm_w, w_out, router_w, w_gate, w_up, w_down):
    bsz, seq, d = x.shape
    n_ctx = ctx.shape[1]
    assert w_mod.shape[0] == 1, "single layer: the last layer's context outputs are never consumed"
    assert bsz < MOD_ROWS and N_EXPERTS == router_w.shape[-1]
    cap = EC_CAPACITY_FACTOR * seq // N_EXPERTS

    cc = jnp.concatenate([c, c_ctx[None, :], jnp.zeros((MOD_ROWS - bsz - 1, d), F32)], axis=0)
    mod = _mod_call(cc, w_mod[0], b_mod[0][None, :])

    w_pad = jnp.pad(w_in[0], ((0, 0), (0, LANES - N_GATES))).astype(BF16)
    seg = np.arange(QK_W) // ATTN_HEAD_DIM
    bd = jnp.asarray(seg[:, None] == seg[None, :], BF16)
    qkw = jnp.concatenate([jnp.tile(q_norm_w[0], ATTN_HEADS), jnp.tile(k_norm_w[0], ATTN_KV_HEADS)])[None, :]
    cos_t, sin_t = _rope_tables(n_ctx, seq)
    alog_l = _lane_row(a_log[0].reshape(-1), N_GATES // 2)
    dtb_l = _lane_row(dt_bias[0].reshape(-1), N_GATES // 2)
    q, kv, dn, z, gates, gates_t = _inproj_call(x, ctx, mod, norm1_w[0][None, :], w_pad, bd, qkw, cos_t, sin_t,
                                                alog_l, dtb_l)

    att = _attn_call(q, kv)
    o_dn = _dn_call(dn, conv_w[0], gates, gates_t, n_ctx)

    rw = jnp.pad(router_w[0], ((0, 0), (0, LANES - N_EXPERTS))).astype(BF16)
    x1, h2, afft = _outproj_call(x, att, o_dn, z, mod, o_norm_w[0][None, :], w_out[0].astype(BF16),
                                 norm2_w[0][None, :], rw)

    slot, gate, tok = _route_call(afft, cap)
    xg, gs = _gather_call(slot, gate, h2, cap)
    y = _ffn_call(xg, gs, w_gate[0], w_up[0], w_down[0])
    return _combine_call(x1, tok, y, mod, cap)
```

```python
import functools
import math

import numpy as np
import jax
import jax.numpy as jnp
from jax import lax
from jax.experimental import pallas as pl
from jax.experimental.pallas import tpu as pltpu

F32 = jnp.float32
BF16 = jnp.bfloat16
I32 = jnp.int32

GRID_W = 64
EPS = 1e-6
ATTN_HEADS = 8
ATTN_KV_HEADS = 2
ATTN_HEAD_DIM = 64
ROPE_BASE = 10000.0
DN_HEADS = 4
DN_HEAD_DIM = 128
DN_CONV_W = 5
N_EXPERTS = 16
EC_CAPACITY_FACTOR = 2

ATTN_Q_W = ATTN_HEADS * ATTN_HEAD_DIM
ATTN_KV_W = ATTN_KV_HEADS * ATTN_HEAD_DIM
DN_W = DN_HEADS * DN_HEAD_DIM
QK_W = ATTN_Q_W + ATTN_KV_W
ATT_W = ATTN_Q_W + 2 * ATTN_KV_W
N_GATES = 4 * DN_HEADS

LANES = 128
SUBLANES = 8
MXU_DIM = 256
VMEM_LIMIT = 56 * 1024 * 1024
F32_MIN_EXP = -126
F32_MANTISSA_BITS = 23

TOK_TILE = 512
Q_TILE = 512
ATTN_KEY_CHUNK = 1024
OUT_TILE = 512
DN_CHUNK = 128
DN_GROUP = 6
DN_CONV_UNROLL = 6
DN_HEADS_PER_STEP = 2
GATHER_EXPERTS = 4
FF_TILE = 512
COMB_TILE = 512
MOD_ROWS = 16


def _params(*sem):
    return pltpu.CompilerParams(dimension_semantics=sem, vmem_limit_bytes=VMEM_LIMIT)


def _silu(v):
    half = 0.5 * v
    return half + half * jnp.tanh(half)


def _dot(a, b):
    return jnp.dot(a, b, preferred_element_type=F32)


def _dot_nt(a, b):
    return lax.dot_general(a, b, (((1,), (1,)), ((), ())), preferred_element_type=F32)


def _mod_kernel(c_ref, w_ref, b_ref, o_ref):
    sc = _silu(c_ref[...]).astype(BF16)
    o_ref[...] = _dot(sc, w_ref[...].astype(BF16)) + b_ref[...]


def _mod_call(cc, w_mod, b_mod):
    d, n = w_mod.shape
    return pl.pallas_call(
        _mod_kernel,
        grid=(n // d,),
        in_specs=[pl.BlockSpec((MOD_ROWS, d), lambda i: (0, 0)),
                  pl.BlockSpec((d, d), lambda i: (0, i)),
                  pl.BlockSpec((1, d), lambda i: (0, i))],
        out_specs=pl.BlockSpec((MOD_ROWS, d), lambda i: (0, i)),
        out_shape=jax.ShapeDtypeStruct((MOD_ROWS, n), F32),
        compiler_params=_params("arbitrary"),
        name="mod",
    )(cc, w_mod, b_mod)


def _inproj_kernel(latent, ctx_row, x_ref, mod_ref, n1_ref, w_ref, bd_ref, qkw_ref, alog_ref, dtb_ref, *refs):
    if latent:
        cos_ref, sin_ref, kv_ref, dn_ref, g_ref, gt_ref, q_ref, z_ref = refs
    else:
        kv_ref, dn_ref, g_ref, gt_ref = refs
    d = x_ref.shape[-1]
    xin = x_ref[0]
    row = pl.program_id(0) if latent else ctx_row
    shift = mod_ref[pl.ds(row, 1), 0:d]
    scale = mod_ref[pl.ds(row, 1), d:2 * d]
    hn = xin * lax.rsqrt(jnp.mean(xin * xin, axis=-1, keepdims=True) + EPS) * n1_ref[...]
    h = (hn * (1.0 + scale) + shift).astype(BF16)
    p = _dot(h, w_ref[...])

    qk = p[:, 0:QK_W]
    sq = qk * qk
    hi = sq.astype(BF16)
    lo = (sq - hi.astype(F32)).astype(BF16)
    seg_w = bd_ref.shape[0]
    ms = []
    for c0 in range(0, QK_W, seg_w):
        w = min(seg_w, QK_W - c0)
        ones = bd_ref[0:w, 0:w]
        ms.append(_dot(hi[:, c0:c0 + w], ones) + _dot(lo[:, c0:c0 + w], ones))
    ms = jnp.concatenate(ms, axis=1) * (1.0 / ATTN_HEAD_DIM)
    qkn = qk * lax.rsqrt(ms + EPS) * qkw_ref[...]

    lane = lax.broadcasted_iota(I32, (1, LANES), 1)
    k_blk = ATTN_Q_W // LANES
    if latent:
        first_half = (lane % 32) < 16
        cos = cos_ref[...]
        sin = sin_ref[...]
        rot = []
        for i in range(QK_W // LANES):
            blk = qkn[:, i * LANES:(i + 1) * LANES]
            partner = jnp.where(first_half, pltpu.roll(blk, LANES - 16, axis=1), pltpu.roll(blk, 16, axis=1))
            rot.append(blk * cos + partner * sin)
        for i in range(k_blk):
            q_ref[0, :, i * LANES:(i + 1) * LANES] = (rot[i] * (ATTN_HEAD_DIM ** -0.5)).astype(BF16)
        k2 = rot[k_blk]
        z_ref[0] = p[:, ATT_W + 3 * DN_W:ATT_W + 4 * DN_W]
    else:
        k2 = qkn[:, k_blk * LANES:(k_blk + 1) * LANES]

    low = lane < ATTN_HEAD_DIM
    v2 = p[:, QK_W:ATT_W]
    for src, base in ((k2, 0), (v2, 2)):
        swapped = pltpu.roll(src, ATTN_HEAD_DIM, axis=1)
        kv_ref[0, 0, base + 0] = jnp.where(low, src, 0.0).astype(BF16)
        kv_ref[0, 0, base + 1] = jnp.where(low, 0.0, swapped).astype(BF16)
        kv_ref[0, 1, base + 0] = jnp.where(low, swapped, 0.0).astype(BF16)
        kv_ref[0, 1, base + 1] = jnp.where(low, 0.0, src).astype(BF16)

    dn_ref[0] = p[:, ATT_W:ATT_W + 3 * DN_W]

    gp = p[:, ATT_W + 4 * DN_W:]
    beta = jax.nn.sigmoid(gp)
    xa = gp + dtb_ref[...]
    softplus = jnp.maximum(xa, 0.0) + jnp.log1p(jnp.exp(-jnp.abs(xa)))
    decay = -jnp.exp(alog_ref[...]) * softplus
    gates = jnp.where(lane < N_GATES // 2, beta, jnp.where(lane < N_GATES, decay, 0.0))
    g_ref[0] = gates
    gt_ref[0] = gates.T[0:N_GATES, :]


def _inproj_call(rows3, ctx_row, mod, n1, w_pad, bd, qkw, alog_l, dtb_l, rope=None):
    latent = rope is not None
    grp, rows, d = rows3.shape
    t = TOK_TILE
    assert rows % t == 0
    nw = w_pad.shape[1]
    full = lambda shape: pl.BlockSpec(shape, lambda b, j: (0,) * len(shape))
    tile = lambda width: pl.BlockSpec((1, t, width), lambda b, j: (b, j, 0))
    in_specs = [tile(d), full((MOD_ROWS, mod.shape[1])), full((1, d)), full((d, nw)), full(bd.shape),
                full((1, QK_W)), full((1, LANES)), full((1, LANES))]
    out_specs = [pl.BlockSpec((1, ATTN_KV_HEADS, 4, t, LANES), lambda b, j: (b, 0, 0, j, 0)),
                 tile(3 * DN_W), tile(LANES), pl.BlockSpec((1, N_GATES, t), lambda b, j: (b, 0, j))]
    out_shape = [jax.ShapeDtypeStruct((grp, ATTN_KV_HEADS, 4, rows, LANES), BF16),
                 jax.ShapeDtypeStruct((grp, rows, 3 * DN_W), F32),
                 jax.ShapeDtypeStruct((grp, rows, LANES), F32),
                 jax.ShapeDtypeStruct((grp, N_GATES, rows), F32)]
    args = [rows3, mod, n1, w_pad, bd, qkw, alog_l, dtb_l]
    if latent:
        in_specs += [pl.BlockSpec((t, LANES), lambda b, j: (j, 0))] * 2
        out_specs += [tile(ATTN_Q_W), tile(DN_W)]
        out_shape += [jax.ShapeDtypeStruct((grp, rows, ATTN_Q_W), BF16), jax.ShapeDtypeStruct((grp, rows, DN_W), F32)]
        args += list(rope)
    return pl.pallas_call(
        functools.partial(_inproj_kernel, latent, ctx_row),
        grid=(grp, rows // t),
        in_specs=in_specs,
        out_specs=out_specs,
        out_shape=out_shape,
        compiler_params=_params("arbitrary", "arbitrary"),
        name="inproj_lat" if latent else "inproj_ctx",
    )(*args)


def _attn_kernel(q_ref, kvc_ref, kvl_ref, o_ref):
    grp = ATTN_HEADS // ATTN_KV_HEADS
    n_lat = kvl_ref.shape[3]

    def scores(h):
        qp = q_ref[0, :, (h // 2) * LANES:(h // 2 + 1) * LANES]
        return (_dot_nt(qp, kvc_ref[0, h // grp, h % 2]),
                _dot_nt(qp, kvl_ref[0, h // grp, h % 2]))

    s_next = scores(0)
    acc = None
    for h in range(ATTN_HEADS):
        s_ctx, s_lat = s_next
        if h + 1 < ATTN_HEADS:
            s_next = scores(h + 1)
        m = jnp.maximum(jnp.max(s_ctx, axis=-1, keepdims=True), jnp.max(s_lat, axis=-1, keepdims=True))
        stages = [(s_ctx, kvc_ref, 0, s_ctx.shape[-1])]
        stages += [(s_lat, kvl_ref, k0, ATTN_KEY_CHUNK) for k0 in range(0, n_lat, ATTN_KEY_CHUNK)]
        o = denom = None
        for s, v_ref, k0, width in stages:
            e = jnp.exp(s[:, k0:k0 + width] - m)
            part = _dot(e.astype(BF16), v_ref[0, h // grp, 2 + h % 2, k0:k0 + width, :])
            part_sum = jnp.sum(e, axis=-1, keepdims=True)
            o = part if o is None else o + part
            denom = part_sum if denom is None else denom + part_sum
        o = o / denom
        if h % 2 == 0:
            acc = o
        else:
            o_ref[0, :, (h // 2) * LANES:(h // 2 + 1) * LANES] = (acc + o).astype(BF16)


def _attn_call(q, kv_ctx, kv_lat, n_ctx):
    bsz, seq, _ = q.shape
    tq = Q_TILE
    assert seq % tq == 0 and seq % ATTN_KEY_CHUNK == 0
    return pl.pallas_call(
        _attn_kernel,
        grid=(bsz, seq // tq),
        in_specs=[pl.BlockSpec((1, tq, ATTN_Q_W), lambda b, i: (b, i, 0)),
                  pl.BlockSpec((1, ATTN_KV_HEADS, 4, n_ctx, LANES), lambda b, i: (0, 0, 0, b, 0)),
                  pl.BlockSpec((1, ATTN_KV_HEADS, 4, seq, LANES), lambda b, i: (b, 0, 0, 0, 0))],
        out_specs=pl.BlockSpec((1, tq, ATTN_Q_W), lambda b, i: (b, i, 0)),
        out_shape=jax.ShapeDtypeStruct((bsz, seq, ATTN_Q_W), BF16),
        compiler_params=_params("arbitrary", "arbitrary"),
        name="attn",
    )(q, kv_ctx, kv_lat)


def _unit_tri_inverses_minus_eye(mats, ij_xor):
    n = mats[0].shape[0]
    corrs = [-jnp.where(ij_xor < 2, a, 0.0) for a in mats]
    m = 2
    while m < n:
        joins = (ij_xor >= m) & (ij_xor < 2 * m)
        xs = [jnp.where(joins, a, 0.0) for a in mats]
        ys = [x + _dot(corr.astype(BF16), x.astype(BF16)) for corr, x in zip(corrs, xs)]
        corrs = [corr - (y + _dot(y.astype(BF16), corr.astype(BF16))) for corr, y in zip(corrs, ys)]
        m *= 2
    return corrs


def _dn_kernel(dqc_ref, dkc_ref, dvc_ref, dql_ref, dkl_ref, dvl_ref, cq_ref, ck_ref, cv_ref,
               gc_ref, gl_ref, gtc_ref, gtl_ref, o_ref,
               xq_s, xk_s, xv_s, q_s, k_s, v_s, kk_s, qk_s, g_s, gt_s, mq_s, n_s, op_s, gl_s):
    qkv_ctx_refs = (dqc_ref, dkc_ref, dvc_ref)
    qkv_lat_refs = (dql_ref, dkl_ref, dvl_ref)
    conv_refs = (cq_ref, ck_ref, cv_ref)
    x_scr = (xq_s, xk_s, xv_s)
    qkv_scr = (q_s, k_s, v_s)
    c = DN_CHUNK
    n_ctx = gc_ref.shape[1]
    ctx_chunks = n_ctx // c
    lat_chunks = gl_ref.shape[1] // c
    dk = DN_HEAD_DIM
    ii = lax.broadcasted_iota(I32, (c, c), 0)
    jj = lax.broadcasted_iota(I32, (c, c), 1)
    ij_xor = ii ^ jj
    lane = lax.broadcasted_iota(I32, (1, LANES), 1)

    g_s[0:n_ctx, :] = gc_ref[0]
    g_s[n_ctx:, :] = gl_ref[0]
    gt_s[:, 0:ctx_chunks, :] = gtc_ref[0]
    gt_s[:, ctx_chunks:, :] = gtl_ref[0]

    for hh in range(DN_HEADS_PER_STEP):
        _dn_prepare_head(hh, pl.program_id(1) * DN_HEADS_PER_STEP + hh, ii, jj, ij_xor, lane,
                         qkv_ctx_refs, qkv_lat_refs, conv_refs, g_s, gt_s,
                         x_scr, qkv_scr, kk_s, qk_s, mq_s, n_s, op_s, gl_s)

    def scan(first_chunk, count, emit, states):
        rows = dk + c if emit else dk

        def step(t, st):
            chains = [(hh, d, first_chunk + t if d == 0 else first_chunk + count - 1 - t)
                      for hh in range(DN_HEADS_PER_STEP) for d in range(2)]
            prods = [_dot(mq_s[hh, d, i, 0:rows, :], s.astype(BF16)) for (hh, d, i), s in zip(chains, st)]
            new = []
            for (hh, d, i), s, r in zip(chains, st, prods):
                if emit:
                    ro = pl.multiple_of((i - ctx_chunks) * c, c)
                    o_ref[0, pl.ds(ro, c), hh * dk:(hh + 1) * dk] += r[dk:dk + c] + op_s[hh, d, i]
                new.append(s * gl_s[hh, d, i][0:1, :] + r[0:dk] + n_s[hh, d, i])
            return tuple(new)
        return lax.fori_loop(0, count, step, states)

    o_ref[...] = jnp.zeros(o_ref.shape, F32)
    zero_state = jnp.zeros((dk, dk), F32)
    states = scan(0, ctx_chunks, False, (zero_state,) * (2 * DN_HEADS_PER_STEP))
    scan(ctx_chunks, lat_chunks, True, states)


def _dn_prepare_head(hh, head, ii, jj, ij_xor, lane, qkv_ctx_refs, qkv_lat_refs, conv_refs, g_s, gt_s,
                     x_scr, qkv_scr, kk_s, qk_s, mq_s, n_s, op_s, gl_s):
    c = DN_CHUNK
    n_ctx = qkv_ctx_refs[0].shape[1]
    seq = qkv_lat_refs[0].shape[1]
    n_chunks = (n_ctx + seq) // c
    ctx_chunks = n_ctx // c
    pad = SUBLANES
    dk = DN_HEAD_DIM
    half = DN_CONV_W // 2
    cols = slice(hh * dk, (hh + 1) * dk)
    xq_s, xk_s, xv_s = x_scr
    q_s, k_s, v_s = qkv_scr
    cq_ref, ck_ref, cv_ref = conv_refs

    zeros_pad = jnp.zeros((pad, dk), F32)
    lat0 = 2 * pad + n_ctx
    for src_ctx, src_lat, dst in zip(qkv_ctx_refs, qkv_lat_refs, x_scr):
        dst[0:pad, :] = zeros_pad
        dst[pad:pad + n_ctx, :] = src_ctx[0, :, cols]
        dst[pad + n_ctx:lat0, :] = zeros_pad
        dst[lat0:lat0 + seq, :] = src_lat[0, :, cols]
        dst[lat0 + seq:lat0 + seq + pad, :] = zeros_pad

    def conv_chunk(i, carry):
        r0 = pl.multiple_of(i * c, c)
        rp = r0 + jnp.where(i >= ctx_chunks, 2 * pad, pad)

        def conv(x_s, cw_ref):
            acc = None
            for s in range(DN_CONV_W):
                term = x_s[pl.ds(rp - half + s, c), :] * cw_ref[s:s + 1, cols]
                acc = term if acc is None else acc + term
            return _silu(acc)

        qc = conv(xq_s, cq_ref)
        kc = conv(xk_s, ck_ref)
        vc = conv(xv_s, cv_ref)
        qc = qc * lax.rsqrt(jnp.sum(qc * qc, axis=-1, keepdims=True) + EPS) * (dk ** -0.5)
        kc = kc * lax.rsqrt(jnp.sum(kc * kc, axis=-1, keepdims=True) + EPS)
        q_s[pl.ds(r0, c), :] = qc
        k_s[pl.ds(r0, c), :] = kc
        v_s[pl.ds(r0, c), :] = vc
        kb = kc.astype(BF16)
        kk_s[i] = _dot_nt(kb, kb)
        qk_s[i] = _dot_nt(qc.astype(BF16), kb)
        return carry

    lax.fori_loop(0, n_chunks, conv_chunk, 0, unroll=DN_CONV_UNROLL)

    def local_group(grp, carry):
        chains = []
        for k in range(DN_GROUP):
            i = grp * DN_GROUP + k
            r0 = pl.multiple_of(i * c, c)
            gates = g_s[pl.ds(r0, c), :]
            kc = k_s[pl.ds(r0, c), :]
            qc = q_s[pl.ds(r0, c), :]
            vc = v_s[pl.ds(r0, c), :]
            kk = kk_s[i]
            qk = qk_s[i]
            kt = kc.T
            for d in range(2):
                beta_col = jnp.sum(jnp.where(lane == d * DN_HEADS + head, gates, 0.0), axis=-1, keepdims=True)
                g_col = jnp.sum(jnp.where(lane == N_GATES // 2 + d * DN_HEADS + head, gates, 0.0),
                                axis=-1, keepdims=True)
                g_row = gt_s[pl.ds(N_GATES // 2 + d * DN_HEADS + head, 1), pl.ds(i, 1), :].reshape(1, c)
                incl = (jj <= ii) if d == 0 else (jj >= ii)
                strict = (jj < ii) if d == 0 else (jj > ii)
                incl_t = (ii <= jj) if d == 0 else (ii >= jj)
                gcum_col = jnp.sum(jnp.where(incl, g_row, 0.0), axis=1, keepdims=True)
                gcum_row = jnp.sum(jnp.where(incl_t, g_col, 0.0), axis=0, keepdims=True)
                total = jnp.sum(g_row, axis=1, keepdims=True)
                decay = jnp.exp(jnp.where(incl, gcum_col - gcum_row, -jnp.inf))
                e_col = jnp.exp(gcum_col)
                kdt = (kt * jnp.exp(total - gcum_row)).astype(BF16)
                qkm = (qk * decay).astype(BF16)
                chains.append(dict(
                    d=d, i=i,
                    a=jnp.where(strict, beta_col * kk * decay, 0.0),
                    rhs=jnp.concatenate([vc * beta_col, kc * (beta_col * e_col)], axis=1),
                    lhs=jnp.concatenate([kdt, qkm], axis=0),
                    qd=qc * e_col,
                    g_last=jnp.exp(total)))
        corrs = _unit_tri_inverses_minus_eye([ch["a"] for ch in chains], ij_xor)
        sols = [ch["rhs"] + _dot(corr.astype(BF16), ch["rhs"].astype(BF16)) for ch, corr in zip(chains, corrs)]
        prods = [_dot(ch["lhs"], sol.astype(BF16)) for ch, sol in zip(chains, sols)]
        for ch, r in zip(chains, prods):
            d, i = ch["d"], ch["i"]
            mq_s[hh, d, i, 0:dk, :] = (-r[0:dk, dk:2 * dk]).astype(BF16)
            mq_s[hh, d, i, dk:dk + c, :] = (ch["qd"] - r[dk:dk + c, dk:2 * dk]).astype(BF16)
            n_s[hh, d, i] = r[0:dk, 0:dk]
            op_s[hh, d, i] = r[dk:dk + c, 0:dk]
            gl_s[hh, d, i] = jnp.broadcast_to(ch["g_last"], (SUBLANES, dk))
        return carry

    lax.fori_loop(0, n_chunks // DN_GROUP, local_group, 0)


def _dn_call(dn_ctx, dn_lat, conv_w, g_ctx, g_lat, gt_ctx, gt_lat, n_ctx):
    bsz, seq, _ = dn_lat.shape
    tot = n_ctx + seq
    c = DN_CHUNK
    assert n_ctx % c == 0 and seq % c == 0
    n_chunks = tot // c
    ctx_chunks = n_ctx // c
    dk = DN_HEAD_DIM
    gtc4 = gt_ctx.reshape(N_GATES, bsz, ctx_chunks, c).transpose(1, 0, 2, 3)
    gtl4 = gt_lat.reshape(bsz, N_GATES, seq // c, c)
    hp = DN_HEADS_PER_STEP
    steps = DN_HEADS // hp
    col_ctx = lambda off: pl.BlockSpec((1, n_ctx, hp * dk), lambda b, h: (0, b, off + h))
    col_lat = lambda off: pl.BlockSpec((1, seq, hp * dk), lambda b, h: (b, 0, off + h))
    cw = lambda off: pl.BlockSpec((DN_CONV_W, hp * dk), lambda b, h: (0, off + h))
    padded = tot + 3 * SUBLANES
    assert n_chunks % DN_GROUP == 0 and n_chunks % DN_CONV_UNROLL == 0 and DN_HEADS % hp == 0
    return pl.pallas_call(
        _dn_kernel,
        grid=(bsz, steps),
        in_specs=[col_ctx(0), col_ctx(steps), col_ctx(2 * steps),
                  col_lat(0), col_lat(steps), col_lat(2 * steps),
                  cw(0), cw(steps), cw(2 * steps),
                  pl.BlockSpec((1, n_ctx, LANES), lambda b, h: (0, b, 0)),
                  pl.BlockSpec((1, seq, LANES), lambda b, h: (b, 0, 0)),
                  pl.BlockSpec((1, N_GATES, ctx_chunks, c), lambda b, h: (b, 0, 0, 0)),
                  pl.BlockSpec((1, N_GATES, seq // c, c), lambda b, h: (b, 0, 0, 0))],
        out_specs=pl.BlockSpec((1, seq, hp * dk), lambda b, h: (b, 0, h)),
        out_shape=jax.ShapeDtypeStruct((bsz, seq, DN_W), F32),
        scratch_shapes=[pltpu.VMEM((padded, dk), F32)] * 3
        + [pltpu.VMEM((tot, dk), F32)] * 3
        + [pltpu.VMEM((n_chunks, c, c), F32)] * 2
        + [pltpu.VMEM((tot, LANES), F32),
           pltpu.VMEM((N_GATES, n_chunks, c), F32)]
        + [pltpu.VMEM((hp, 2, n_chunks, dk + c, dk), BF16),
           pltpu.VMEM((hp, 2, n_chunks, dk, dk), F32),
           pltpu.VMEM((hp, 2, n_chunks, c, dk), F32),
           pltpu.VMEM((hp, 2, n_chunks, SUBLANES, dk), F32)],
        compiler_params=_params("arbitrary", "arbitrary"),
        name="dn",
    )(dn_ctx, dn_ctx, dn_ctx, dn_lat, dn_lat, dn_lat, conv_w, conv_w, conv_w, g_ctx, g_lat, gtc4, gtl4)


def _outproj_kernel(x_ref, att_ref, o_ref, z_ref, mod_ref, onw_ref, wo_ref, n2_ref, rw_ref,
                    x1_ref, h2_ref, afft_ref):
    d = x_ref.shape[-1]
    b = pl.program_id(0)
    gate1 = mod_ref[pl.ds(b, 1), 2 * d:3 * d]
    shift2 = mod_ref[pl.ds(b, 1), 3 * d:4 * d]
    scale2 = mod_ref[pl.ds(b, 1), 4 * d:5 * d]
    y = _dot(att_ref[0], wo_ref[0:ATTN_Q_W, :])
    for h in range(DN_HEADS):
        sl = slice(h * DN_HEAD_DIM, (h + 1) * DN_HEAD_DIM)
        oh = o_ref[0, :, sl]
        on = oh * lax.rsqrt(jnp.mean(oh * oh, axis=-1, keepdims=True) + EPS) * onw_ref[...]
        yh = (on * _silu(z_ref[0, :, sl])).astype(BF16)
        y = y + _dot(yh, wo_ref[ATTN_Q_W + h * DN_HEAD_DIM:ATTN_Q_W + (h + 1) * DN_HEAD_DIM, :])
    x1 = x_ref[0] + gate1 * y
    x1_ref[0] = x1
    hn = x1 * lax.rsqrt(jnp.mean(x1 * x1, axis=-1, keepdims=True) + EPS) * n2_ref[...]
    h2 = (hn * (1.0 + scale2) + shift2).astype(BF16)
    h2_ref[0] = h2
    logits = _dot(h2, rw_ref[...])
    lane = lax.broadcasted_iota(I32, (1, LANES), 1)
    logits = jnp.where(lane < N_EXPERTS, logits, -jnp.inf)
    e = jnp.exp(logits - jnp.max(logits, axis=-1, keepdims=True))
    aff = e / jnp.sum(e, axis=-1, keepdims=True)
    afft_ref[0] = aff.T[0:N_EXPERTS, :]


def _outproj_call(x, att, o_dn, z, mod, onw, wo, n2, rw):
    bsz, seq, d = x.shape
    t = OUT_TILE
    assert seq % t == 0
    full = lambda shape: pl.BlockSpec(shape, lambda b, j: (0,) * len(shape))
    return pl.pallas_call(
        _outproj_kernel,
        grid=(bsz, seq // t),
        in_specs=[pl.BlockSpec((1, t, d), lambda b, j: (b, j, 0)),
                  pl.BlockSpec((1, t, ATTN_Q_W), lambda b, j: (b, j, 0)),
                  pl.BlockSpec((1, t, DN_W), lambda b, j: (b, j, 0)),
                  pl.BlockSpec((1, t, DN_W), lambda b, j: (b, j, 0)),
                  full((MOD_ROWS, mod.shape[1])),
                  full((1, DN_HEAD_DIM)),
                  full(wo.shape),
                  full((1, d)),
                  full((d, LANES))],
        out_specs=[pl.BlockSpec((1, t, d), lambda b, j: (b, j, 0)),
                   pl.BlockSpec((1, t, d), lambda b, j: (b, j, 0)),
                   pl.BlockSpec((1, N_EXPERTS, t), lambda b, j: (b, 0, j))],
        out_shape=[jax.ShapeDtypeStruct((bsz, seq, d), F32),
                   jax.ShapeDtypeStruct((bsz, seq, d), BF16),
                   jax.ShapeDtypeStruct((bsz, N_EXPERTS, seq), F32)],
        compiler_params=_params("arbitrary", "arbitrary"),
        name="outproj",
    )(x, att, o_dn, z, mod, onw, wo, n2, rw)


def _route_kernel(cap, afft_ref, slot_ref, gate_ref, tok_ref):
    aff = afft_ref[0]
    n_e, n_t = aff.shape

    def enough(cand):
        return jnp.sum(jnp.where(aff >= cand, 1.0, 0.0), axis=-1, keepdims=True) >= cap

    tiny = 2.0 ** F32_MIN_EXP
    cur = jnp.full((n_e, 1), tiny, F32)
    any_normal = enough(cur)
    shift = 1 << (-F32_MIN_EXP).bit_length()
    while shift > 1:
        shift //= 2
        cand = cur * (2.0 ** shift)
        cur = jnp.where(enough(cand), cand, cur)

    def refine(_, state):
        cur, step = state
        cand = cur + step
        return jnp.where(enough(cand), cand, cur), step * 0.5

    cur, _ = lax.fori_loop(0, F32_MANTISSA_BITS, refine, (cur, cur * 0.5))
    thr = jnp.where(any_normal, cur, 0.0)
    need = cap - jnp.sum(jnp.where(aff > thr, 1.0, 0.0), axis=-1, keepdims=True)

    upper = (lax.broadcasted_iota(I32, (LANES, LANES), 0) < lax.broadcasted_iota(I32, (LANES, LANES), 1))
    upper = jnp.where(upper, 1.0, 0.0).astype(BF16)
    run = jnp.zeros((2 * n_e, 1), F32)
    for blk in range(n_t // LANES):
        sl = slice(blk * LANES, (blk + 1) * LANES)
        gt = aff[:, sl] > thr
        eq = aff[:, sl] == thr
        x = jnp.concatenate([jnp.where(gt, 1.0, 0.0), jnp.where(eq, 1.0, 0.0)], axis=0)
        cum = _dot(x.astype(BF16), upper) + run
        run = run + jnp.sum(x, axis=-1, keepdims=True)
        cum_gt = cum[0:n_e]
        cum_eq = cum[n_e:2 * n_e]
        sel = gt | (eq & (cum_eq < need))
        slot = jnp.where(sel, cum_gt + jnp.minimum(cum_eq, need), -1.0)
        slot_ref[0, :, sl] = slot.astype(I32)
        gate_ref[0, :, sl] = jnp.where(sel, aff[:, sl], 0.0)

    stacked = jnp.concatenate([slot_ref[0].astype(F32), gate_ref[0],
                               jnp.zeros((LANES - 2 * n_e, n_t), F32)], axis=0)
    tok_ref[0] = stacked.T


def _route_call(afft, cap):
    bsz, n_e, n_t = afft.shape
    row = pl.BlockSpec((1, n_e, n_t), lambda b: (b, 0, 0))
    return pl.pallas_call(
        functools.partial(_route_kernel, cap),
        grid=(bsz,),
        in_specs=[row],
        out_specs=[row, row, pl.BlockSpec((1, n_t, LANES), lambda b: (b, 0, 0))],
        out_shape=[jax.ShapeDtypeStruct((bsz, n_e, n_t), I32),
                   jax.ShapeDtypeStruct((bsz, n_e, n_t), F32),
                   jax.ShapeDtypeStruct((bsz, n_t, LANES), F32)],
        compiler_params=_params("arbitrary"),
        name="route",
    )(afft)


def _gather_kernel(cap, slot_ref, gate_ref, h_ref, xg_ref, gs_ref):
    ge = xg_ref.shape[0]
    n_t = h_ref.shape[1]
    e0 = pl.program_id(1) * ge
    j = lax.broadcasted_iota(I32, (cap, 1), 0)
    for i in range(ge):
        slot = slot_ref[0, pl.ds(e0 + i, 1), :]
        hit = slot == j
        xg_ref[i, 0] = _dot(jnp.where(hit, 1.0, 0.0).astype(BF16), h_ref[0]).astype(BF16)
        gate = gate_ref[0, pl.ds(e0 + i, 1), :]
        gs_ref[i, 0] = jnp.sum(jnp.where(hit, gate, 0.0), axis=-1, keepdims=True)


def _gather_call(slot, gate, h2, cap):
    bsz, n_e, n_t = slot.shape
    d = h2.shape[-1]
    ge = GATHER_EXPERTS
    row = pl.BlockSpec((1, n_e, n_t), lambda b, g: (b, 0, 0))
    return pl.pallas_call(
        functools.partial(_gather_kernel, cap),
        grid=(bsz, n_e // ge),
        in_specs=[row, row, pl.BlockSpec((1, n_t, d), lambda b, g: (b, 0, 0))],
        out_specs=[pl.BlockSpec((ge, 1, cap, d), lambda b, g: (g, b, 0, 0)),
                   pl.BlockSpec((ge, 1, cap, 1), lambda b, g: (g, b, 0, 0))],
        out_shape=[jax.ShapeDtypeStruct((n_e, bsz, cap, d), BF16),
                   jax.ShapeDtypeStruct((n_e, bsz, cap, 1), F32)],
        compiler_params=_params("arbitrary", "arbitrary"),
        name="gather",
    )(slot, gate, h2)


def _ffn_kernel(xg_ref, gs_ref, wg_ref, wu_ref, wd_ref, y_ref, acc_ref):
    f = pl.program_id(1)
    bsz, cap, d = xg_ref.shape[1:]
    xg = xg_ref[0].reshape(bsz * cap, d)
    hid = _silu(_dot(xg, wg_ref[0].astype(BF16))) * _dot(xg, wu_ref[0].astype(BF16))
    part = _dot(hid.astype(BF16), wd_ref[0].astype(BF16))

    @pl.when(f == 0)
    def _():
        acc_ref[...] = part

    @pl.when(f > 0)
    def _():
        acc_ref[...] += part

    @pl.when(f == pl.num_programs(1) - 1)
    def _():
        y = acc_ref[...] * gs_ref[0].reshape(bsz * cap, 1)
        y_ref[0] = y.astype(BF16).reshape(bsz, cap, d)


def _ffn_call(xg, gs, w_gate, w_up, w_down):
    n_e, bsz, cap, d = xg.shape
    ff = w_gate.shape[-1]
    tf = FF_TILE
    return pl.pallas_call(
        _ffn_kernel,
        grid=(n_e, ff // tf),
        in_specs=[pl.BlockSpec((1, bsz, cap, d), lambda e, f: (e, 0, 0, 0)),
                  pl.BlockSpec((1, bsz, cap, 1), lambda e, f: (e, 0, 0, 0)),
                  pl.BlockSpec((1, d, tf), lambda e, f: (e, 0, f)),
                  pl.BlockSpec((1, d, tf), lambda e, f: (e, 0, f)),
                  pl.BlockSpec((1, tf, d), lambda e, f: (e, f, 0))],
        out_specs=pl.BlockSpec((1, bsz, cap, d), lambda e, f: (e, 0, 0, 0)),
        out_shape=jax.ShapeDtypeStruct((n_e, bsz, cap, d), BF16),
        scratch_shapes=[pltpu.VMEM((bsz * cap, d), F32)],
        compiler_params=_params("arbitrary", "arbitrary"),
        name="ffn",
    )(xg, gs, w_gate, w_up, w_down)


def _combine_kernel(cap, x1_ref, tok_ref, y_ref, mod_ref, o_ref):
    d = x1_ref.shape[-1]
    n_e = y_ref.shape[0]
    b = pl.program_id(0)
    gate2 = mod_ref[pl.ds(b, 1), 5 * d:6 * d]
    tok = tok_ref[0]
    j = lax.broadcasted_iota(I32, (1, cap), 1)
    acc = None
    for e in range(n_e):
        slot = tok[:, e:e + 1].astype(I32)
        onehot = jnp.where(slot == j, 1.0, 0.0).astype(BF16)
        part = _dot(onehot, y_ref[e, 0])
        acc = part if acc is None else acc + part
    o_ref[0] = x1_ref[0] + gate2 * acc


def _combine_call(x1, tok, y, mod, cap):
    bsz, seq, d = x1.shape
    n_e = y.shape[0]
    t = COMB_TILE
    return pl.pallas_call(
        functools.partial(_combine_kernel, cap),
        grid=(bsz, seq // t),
        in_specs=[pl.BlockSpec((1, t, d), lambda b, j: (b, j, 0)),
                  pl.BlockSpec((1, t, LANES), lambda b, j: (b, j, 0)),
                  pl.BlockSpec((n_e, 1, cap, d), lambda b, j: (0, b, 0, 0)),
                  pl.BlockSpec((MOD_ROWS, mod.shape[1]), lambda b, j: (0, 0))],
        out_specs=pl.BlockSpec((1, t, d), lambda b, j: (b, j, 0)),
        out_shape=jax.ShapeDtypeStruct((bsz, seq, d), F32),
        compiler_params=_params("arbitrary", "arbitrary"),
        name="combine",
    )(x1, tok, y, mod)


def _rope_tables(seq):
    m = ATTN_HEAD_DIM // 4
    pos = jnp.arange(seq, dtype=jnp.int32)
    rows = (pos // GRID_W).astype(F32)
    cols = (pos % GRID_W).astype(F32)
    freqs = ROPE_BASE ** (-jnp.arange(m, dtype=F32) / m)
    ang_r = rows[:, None] * freqs[None, :]
    ang_c = cols[:, None] * freqs[None, :]
    cos_h = jnp.concatenate([jnp.cos(ang_r), jnp.cos(ang_r), jnp.cos(ang_c), jnp.cos(ang_c)], axis=-1)
    sin_h = jnp.concatenate([-jnp.sin(ang_r), jnp.sin(ang_r), -jnp.sin(ang_c), jnp.sin(ang_c)], axis=-1)
    reps = LANES // ATTN_HEAD_DIM
    return jnp.tile(cos_h, (1, reps)), jnp.tile(sin_h, (1, reps))


def _lane_row(values, offset):
    return jnp.zeros((1, LANES), F32).at[0, offset:offset + values.shape[0]].set(values.astype(F32))


def kernel(x, c, ctx, c_ctx, w_mod, b_mod, norm1_w, norm2_w, w_in, q_norm_w, k_norm_w, conv_w, a_log, dt_bias,
           o_norm_w, w_out, router_w, w_gate, w_up, w_down):
    bsz, seq, d = x.shape
    n_ctx = ctx.shape[1]
    assert w_mod.shape[0] == 1, "single layer: the last layer's context outputs are never consumed"
    assert bsz < MOD_ROWS and N_EXPERTS == router_w.shape[-1]
    cap = EC_CAPACITY_FACTOR * seq // N_EXPERTS

    cc = jnp.concatenate([c, c_ctx[None, :], jnp.zeros((MOD_ROWS - bsz - 1, d), F32)], axis=0)
    mod = _mod_call(cc, w_mod[0], b_mod[0][None, :])

    w_pad = jnp.pad(w_in[0], ((0, 0), (0, LANES - N_GATES))).astype(BF16)
    seg = np.arange(MXU_DIM) // ATTN_HEAD_DIM
    bd = jnp.asarray(seg[:, None] == seg[None, :], BF16)
    qkw = jnp.concatenate([jnp.tile(q_norm_w[0], ATTN_HEADS), jnp.tile(k_norm_w[0], ATTN_KV_HEADS)])[None, :]
    alog_l = _lane_row(a_log[0].reshape(-1), N_GATES // 2)
    dtb_l = _lane_row(dt_bias[0].reshape(-1), N_GATES // 2)
    shared = (mod, norm1_w[0][None, :], w_pad, bd, qkw, alog_l, dtb_l)
    kv_lat, dn_lat, g_lat, gt_lat, q, z = _inproj_call(x, bsz, *shared, rope=_rope_tables(seq))
    kv_ctx, dn_ctx, g_ctx, gt_ctx = _inproj_call(ctx.reshape(1, bsz * n_ctx, d), bsz, *shared)

    att = _attn_call(q, kv_ctx, kv_lat, n_ctx)
    o_dn = _dn_call(dn_ctx, dn_lat, conv_w[0], g_ctx, g_lat, gt_ctx, gt_lat, n_ctx)

    rw = jnp.pad(router_w[0], ((0, 0), (0, LANES - N_EXPERTS))).astype(BF16)
    x1, h2, afft = _outproj_call(x, att, o_dn, z, mod, o_norm_w[0][None, :], w_out[0].astype(BF16),
                                 norm2_w[0][None, :], rw)

    slot, gate, tok = _route_call(afft, cap)
    xg, gs = _gather_call(slot, gate, h2, cap)
    y = _ffn_call(xg, gs, w_gate[0], w_up[0], w_down[0])
    return _combine_call(x1, tok, y, mod, cap)
```

```python
import functools
import math

import numpy as np
import jax
import jax.numpy as jnp
from jax import lax
from jax.experimental import pallas as pl
from jax.experimental.pallas import tpu as pltpu

F32 = jnp.float32
BF16 = jnp.bfloat16
I32 = jnp.int32

GRID_W = 64
EPS = 1e-6
ATTN_HEADS = 8
ATTN_KV_HEADS = 2
ATTN_HEAD_DIM = 64
ROPE_BASE = 10000.0
DN_HEADS = 4
DN_HEAD_DIM = 128
DN_CONV_W = 5
N_EXPERTS = 16
EC_CAPACITY_FACTOR = 2

ATTN_Q_W = ATTN_HEADS * ATTN_HEAD_DIM
ATTN_KV_W = ATTN_KV_HEADS * ATTN_HEAD_DIM
DN_W = DN_HEADS * DN_HEAD_DIM
QK_W = ATTN_Q_W + ATTN_KV_W
ATT_W = ATTN_Q_W + 2 * ATTN_KV_W
N_GATES = 4 * DN_HEADS

LANES = 128
SUBLANES = 8
MXU_DIM = 256
VMEM_LIMIT = 56 * 1024 * 1024
F32_MIN_EXP = -126
F32_MANTISSA_BITS = 23

TOK_TILE = 512
Q_TILE = 512
ATTN_KEY_CHUNK = 1024
OUT_TILE = 512
DN_CHUNK = 128
DN_GROUP = 6
DN_CONV_UNROLL = 6
DN_HEADS_PER_STEP = 2
ROUTE_BLOCK = 256
SLOT_WINDOW = 64
BF16_ROWS = 16
FF_TILE = 512
MOD_ROWS = 16


def _params(*sem):
    return pltpu.CompilerParams(dimension_semantics=sem, vmem_limit_bytes=VMEM_LIMIT)


def _silu(v):
    half = 0.5 * v
    return half + half * jnp.tanh(half)


def _dot(a, b):
    return jnp.dot(a, b, preferred_element_type=F32)


def _dot_nt(a, b):
    return lax.dot_general(a, b, (((1,), (1,)), ((), ())), preferred_element_type=F32)


def _mod_kernel(c_ref, w_ref, b_ref, o_ref):
    sc = _silu(c_ref[...]).astype(BF16)
    o_ref[...] = _dot(sc, w_ref[...].astype(BF16)) + b_ref[...]


def _mod_call(cc, w_mod, b_mod):
    d, n = w_mod.shape
    return pl.pallas_call(
        _mod_kernel,
        grid=(n // d,),
        in_specs=[pl.BlockSpec((MOD_ROWS, d), lambda i: (0, 0)),
                  pl.BlockSpec((d, d), lambda i: (0, i)),
                  pl.BlockSpec((1, d), lambda i: (0, i))],
        out_specs=pl.BlockSpec((MOD_ROWS, d), lambda i: (0, i)),
        out_shape=jax.ShapeDtypeStruct((MOD_ROWS, n), F32),
        compiler_params=_params("arbitrary"),
        name="mod",
    )(cc, w_mod, b_mod)


def _inproj_kernel(latent, ctx_row, x_ref, mod_ref, n1_ref, w_ref, bd_ref, qkw_ref, alog_ref, dtb_ref, *refs):
    if latent:
        cos_ref, sin_ref, kv_ref, dn_ref, g_ref, gt_ref, q_ref, z_ref = refs
    else:
        kv_ref, dn_ref, g_ref, gt_ref = refs
    d = x_ref.shape[-1]
    xin = x_ref[0]
    row = pl.program_id(0) if latent else ctx_row
    shift = mod_ref[pl.ds(row, 1), 0:d]
    scale = mod_ref[pl.ds(row, 1), d:2 * d]
    hn = xin * lax.rsqrt(jnp.mean(xin * xin, axis=-1, keepdims=True) + EPS) * n1_ref[...]
    h = (hn * (1.0 + scale) + shift).astype(BF16)
    p = _dot(h, w_ref[...])

    qk = p[:, 0:QK_W]
    sq = qk * qk
    hi = sq.astype(BF16)
    lo = (sq - hi.astype(F32)).astype(BF16)
    seg_w = bd_ref.shape[0]
    ms = []
    for c0 in range(0, QK_W, seg_w):
        w = min(seg_w, QK_W - c0)
        ones = bd_ref[0:w, 0:w]
        ms.append(_dot(hi[:, c0:c0 + w], ones) + _dot(lo[:, c0:c0 + w], ones))
    ms = jnp.concatenate(ms, axis=1) * (1.0 / ATTN_HEAD_DIM)
    qkn = qk * lax.rsqrt(ms + EPS) * qkw_ref[...]

    lane = lax.broadcasted_iota(I32, (1, LANES), 1)
    k_blk = ATTN_Q_W // LANES
    if latent:
        first_half = (lane % 32) < 16
        cos = cos_ref[...]
        sin = sin_ref[...]
        rot = []
        for i in range(QK_W // LANES):
            blk = qkn[:, i * LANES:(i + 1) * LANES]
            partner = jnp.where(first_half, pltpu.roll(blk, LANES - 16, axis=1), pltpu.roll(blk, 16, axis=1))
            rot.append(blk * cos + partner * sin)
        for i in range(k_blk):
            q_ref[0, :, i * LANES:(i + 1) * LANES] = (rot[i] * (ATTN_HEAD_DIM ** -0.5)).astype(BF16)
        k2 = rot[k_blk]
        z_ref[0] = p[:, ATT_W + 3 * DN_W:ATT_W + 4 * DN_W]
    else:
        k2 = qkn[:, k_blk * LANES:(k_blk + 1) * LANES]

    low = lane < ATTN_HEAD_DIM
    v2 = p[:, QK_W:ATT_W]
    for src, base in ((k2, 0), (v2, 2)):
        swapped = pltpu.roll(src, ATTN_HEAD_DIM, axis=1)
        kv_ref[0, 0, base + 0] = jnp.where(low, src, 0.0).astype(BF16)
        kv_ref[0, 0, base + 1] = jnp.where(low, 0.0, swapped).astype(BF16)
        kv_ref[0, 1, base + 0] = jnp.where(low, swapped, 0.0).astype(BF16)
        kv_ref[0, 1, base + 1] = jnp.where(low, 0.0, src).astype(BF16)

    dn_ref[0] = p[:, ATT_W:ATT_W + 3 * DN_W]

    gp = p[:, ATT_W + 4 * DN_W:]
    beta = jax.nn.sigmoid(gp)
    xa = gp + dtb_ref[...]
    softplus = jnp.maximum(xa, 0.0) + jnp.log1p(jnp.exp(-jnp.abs(xa)))
    decay = -jnp.exp(alog_ref[...]) * softplus
    gates = jnp.where(lane < N_GATES // 2, beta, jnp.where(lane < N_GATES, decay, 0.0))
    g_ref[0] = gates
    gt_ref[0] = gates.T[0:N_GATES, :]


def _inproj_call(rows3, ctx_row, mod, n1, w_pad, bd, qkw, alog_l, dtb_l, rope=None):
    latent = rope is not None
    grp, rows, d = rows3.shape
    t = TOK_TILE
    assert rows % t == 0
    nw = w_pad.shape[1]
    full = lambda shape: pl.BlockSpec(shape, lambda b, j: (0,) * len(shape))
    tile = lambda width: pl.BlockSpec((1, t, width), lambda b, j: (b, j, 0))
    in_specs = [tile(d), full((MOD_ROWS, mod.shape[1])), full((1, d)), full((d, nw)), full(bd.shape),
                full((1, QK_W)), full((1, LANES)), full((1, LANES))]
    out_specs = [pl.BlockSpec((1, ATTN_KV_HEADS, 4, t, LANES), lambda b, j: (b, 0, 0, j, 0)),
                 tile(3 * DN_W), tile(LANES), pl.BlockSpec((1, N_GATES, t), lambda b, j: (b, 0, j))]
    out_shape = [jax.ShapeDtypeStruct((grp, ATTN_KV_HEADS, 4, rows, LANES), BF16),
                 jax.ShapeDtypeStruct((grp, rows, 3 * DN_W), F32),
                 jax.ShapeDtypeStruct((grp, rows, LANES), F32),
                 jax.ShapeDtypeStruct((grp, N_GATES, rows), F32)]
    args = [rows3, mod, n1, w_pad, bd, qkw, alog_l, dtb_l]
    if latent:
        in_specs += [pl.BlockSpec((t, LANES), lambda b, j: (j, 0))] * 2
        out_specs += [tile(ATTN_Q_W), tile(DN_W)]
        out_shape += [jax.ShapeDtypeStruct((grp, rows, ATTN_Q_W), BF16), jax.ShapeDtypeStruct((grp, rows, DN_W), F32)]
        args += list(rope)
    return pl.pallas_call(
        functools.partial(_inproj_kernel, latent, ctx_row),
        grid=(grp, rows // t),
        in_specs=in_specs,
        out_specs=out_specs,
        out_shape=out_shape,
        compiler_params=_params("arbitrary", "arbitrary"),
        name="inproj_lat" if latent else "inproj_ctx",
    )(*args)


def _attn_kernel(q_ref, kvc_ref, kvl_ref, o_ref):
    grp = ATTN_HEADS // ATTN_KV_HEADS
    n_lat = kvl_ref.shape[3]

    def scores(h):
        qp = q_ref[0, :, (h // 2) * LANES:(h // 2 + 1) * LANES]
        return (_dot_nt(qp, kvc_ref[0, h // grp, h % 2]),
                _dot_nt(qp, kvl_ref[0, h // grp, h % 2]))

    s_next = scores(0)
    acc = None
    for h in range(ATTN_HEADS):
        s_ctx, s_lat = s_next
        if h + 1 < ATTN_HEADS:
            s_next = scores(h + 1)
        m = jnp.maximum(jnp.max(s_ctx, axis=-1, keepdims=True), jnp.max(s_lat, axis=-1, keepdims=True))
        stages = [(s_ctx, kvc_ref, 0, s_ctx.shape[-1])]
        stages += [(s_lat, kvl_ref, k0, ATTN_KEY_CHUNK) for k0 in range(0, n_lat, ATTN_KEY_CHUNK)]
        o = denom = None
        for s, v_ref, k0, width in stages:
            e = jnp.exp(s[:, k0:k0 + width] - m)
            part = _dot(e.astype(BF16), v_ref[0, h // grp, 2 + h % 2, k0:k0 + width, :])
            part_sum = jnp.sum(e, axis=-1, keepdims=True)
            o = part if o is None else o + part
            denom = part_sum if denom is None else denom + part_sum
        o = o / denom
        if h % 2 == 0:
            acc = o
        else:
            o_ref[0, :, (h // 2) * LANES:(h // 2 + 1) * LANES] = (acc + o).astype(BF16)


def _attn_call(q, kv_ctx, kv_lat, n_ctx):
    bsz, seq, _ = q.shape
    tq = Q_TILE
    assert seq % tq == 0 and seq % ATTN_KEY_CHUNK == 0
    return pl.pallas_call(
        _attn_kernel,
        grid=(bsz, seq // tq),
        in_specs=[pl.BlockSpec((1, tq, ATTN_Q_W), lambda b, i: (b, i, 0)),
                  pl.BlockSpec((1, ATTN_KV_HEADS, 4, n_ctx, LANES), lambda b, i: (0, 0, 0, b, 0)),
                  pl.BlockSpec((1, ATTN_KV_HEADS, 4, seq, LANES), lambda b, i: (b, 0, 0, 0, 0))],
        out_specs=pl.BlockSpec((1, tq, ATTN_Q_W), lambda b, i: (b, i, 0)),
        out_shape=jax.ShapeDtypeStruct((bsz, seq, ATTN_Q_W), BF16),
        compiler_params=_params("arbitrary", "arbitrary"),
        name="attn",
    )(q, kv_ctx, kv_lat)


def _unit_tri_inverses_minus_eye(mats, ij_xor):
    n = mats[0].shape[0]
    corrs = [-jnp.where(ij_xor < 2, a, 0.0) for a in mats]
    m = 2
    while m < n:
        joins = (ij_xor >= m) & (ij_xor < 2 * m)
        xs = [jnp.where(joins, a, 0.0) for a in mats]
        ys = [x + _dot(corr.astype(BF16), x.astype(BF16)) for corr, x in zip(corrs, xs)]
        corrs = [corr - (y + _dot(y.astype(BF16), corr.astype(BF16))) for corr, y in zip(corrs, ys)]
        m *= 2
    return corrs


def _dn_kernel(dqc_ref, dkc_ref, dvc_ref, dql_ref, dkl_ref, dvl_ref, cq_ref, ck_ref, cv_ref,
               gc_ref, gl_ref, gtc_ref, gtl_ref, o_ref,
               xq_s, xk_s, xv_s, q_s, k_s, v_s, kk_s, qk_s, g_s, gt_s, mq_s, n_s, op_s, gl_s):
    qkv_ctx_refs = (dqc_ref, dkc_ref, dvc_ref)
    qkv_lat_refs = (dql_ref, dkl_ref, dvl_ref)
    conv_refs = (cq_ref, ck_ref, cv_ref)
    x_scr = (xq_s, xk_s, xv_s)
    qkv_scr = (q_s, k_s, v_s)
    c = DN_CHUNK
    n_ctx = gc_ref.shape[1]
    ctx_chunks = n_ctx // c
    lat_chunks = gl_ref.shape[1] // c
    dk = DN_HEAD_DIM
    ii = lax.broadcasted_iota(I32, (c, c), 0)
    jj = lax.broadcasted_iota(I32, (c, c), 1)
    ij_xor = ii ^ jj
    lane = lax.broadcasted_iota(I32, (1, LANES), 1)

    g_s[0:n_ctx, :] = gc_ref[0]
    g_s[n_ctx:, :] = gl_ref[0]
    gt_s[:, 0:ctx_chunks, :] = gtc_ref[0]
    gt_s[:, ctx_chunks:, :] = gtl_ref[0]

    for hh in range(DN_HEADS_PER_STEP):
        _dn_prepare_head(hh, pl.program_id(1) * DN_HEADS_PER_STEP + hh, ii, jj, ij_xor, lane,
                         qkv_ctx_refs, qkv_lat_refs, conv_refs, g_s, gt_s,
                         x_scr, qkv_scr, kk_s, qk_s, mq_s, n_s, op_s, gl_s)

    def scan(first_chunk, count, emit, states):
        rows = dk + c if emit else dk

        def step(t, st):
            chains = [(hh, d, first_chunk + t if d == 0 else first_chunk + count - 1 - t)
                      for hh in range(DN_HEADS_PER_STEP) for d in range(2)]
            prods = [_dot(mq_s[hh, d, i, 0:rows, :], s.astype(BF16)) for (hh, d, i), s in zip(chains, st)]
            new = []
            for (hh, d, i), s, r in zip(chains, st, prods):
                if emit:
                    ro = pl.multiple_of((i - ctx_chunks) * c, c)
                    o_ref[0, pl.ds(ro, c), hh * dk:(hh + 1) * dk] += r[dk:dk + c] + op_s[hh, d, i]
                new.append(s * gl_s[hh, d, i][0:1, :] + r[0:dk] + n_s[hh, d, i])
            return tuple(new)
        return lax.fori_loop(0, count, step, states)

    o_ref[...] = jnp.zeros(o_ref.shape, F32)
    zero_state = jnp.zeros((dk, dk), F32)
    states = scan(0, ctx_chunks, False, (zero_state,) * (2 * DN_HEADS_PER_STEP))
    scan(ctx_chunks, lat_chunks, True, states)


def _dn_prepare_head(hh, head, ii, jj, ij_xor, lane, qkv_ctx_refs, qkv_lat_refs, conv_refs, g_s, gt_s,
                     x_scr, qkv_scr, kk_s, qk_s, mq_s, n_s, op_s, gl_s):
    c = DN_CHUNK
    n_ctx = qkv_ctx_refs[0].shape[1]
    seq = qkv_lat_refs[0].shape[1]
    n_chunks = (n_ctx + seq) // c
    ctx_chunks = n_ctx // c
    pad = SUBLANES
    dk = DN_HEAD_DIM
    half = DN_CONV_W // 2
    cols = slice(hh * dk, (hh + 1) * dk)
    xq_s, xk_s, xv_s = x_scr
    q_s, k_s, v_s = qkv_scr
    cq_ref, ck_ref, cv_ref = conv_refs

    zeros_pad = jnp.zeros((pad, dk), F32)
    lat0 = 2 * pad + n_ctx
    for src_ctx, src_lat, dst in zip(qkv_ctx_refs, qkv_lat_refs, x_scr):
        dst[0:pad, :] = zeros_pad
        dst[pad:pad + n_ctx, :] = src_ctx[0, :, cols]
        dst[pad + n_ctx:lat0, :] = zeros_pad
        dst[lat0:lat0 + seq, :] = src_lat[0, :, cols]
        dst[lat0 + seq:lat0 + seq + pad, :] = zeros_pad

    def conv_chunk(i, carry):
        r0 = pl.multiple_of(i * c, c)
        rp = r0 + jnp.where(i >= ctx_chunks, 2 * pad, pad)

        def conv(x_s, cw_ref):
            acc = None
            for s in range(DN_CONV_W):
                term = x_s[pl.ds(rp - half + s, c), :] * cw_ref[s:s + 1, cols]
                acc = term if acc is None else acc + term
            return _silu(acc)

        qc = conv(xq_s, cq_ref)
        kc = conv(xk_s, ck_ref)
        vc = conv(xv_s, cv_ref)
        qc = qc * lax.rsqrt(jnp.sum(qc * qc, axis=-1, keepdims=True) + EPS) * (dk ** -0.5)
        kc = kc * lax.rsqrt(jnp.sum(kc * kc, axis=-1, keepdims=True) + EPS)
        q_s[pl.ds(r0, c), :] = qc
        k_s[pl.ds(r0, c), :] = kc
        v_s[pl.ds(r0, c), :] = vc
        kb = kc.astype(BF16)
        kk_s[i] = _dot_nt(kb, kb)
        qk_s[i] = _dot_nt(qc.astype(BF16), kb)
        return carry

    lax.fori_loop(0, n_chunks, conv_chunk, 0, unroll=DN_CONV_UNROLL)

    def local_group(grp, carry):
        chains = []
        for k in range(DN_GROUP):
            i = grp * DN_GROUP + k
            r0 = pl.multiple_of(i * c, c)
            gates = g_s[pl.ds(r0, c), :]
            kc = k_s[pl.ds(r0, c), :]
            qc = q_s[pl.ds(r0, c), :]
            vc = v_s[pl.ds(r0, c), :]
            kk = kk_s[i]
            qk = qk_s[i]
            kt = kc.T
            for d in range(2):
                beta_col = jnp.sum(jnp.where(lane == d * DN_HEADS + head, gates, 0.0), axis=-1, keepdims=True)
                g_col = jnp.sum(jnp.where(lane == N_GATES // 2 + d * DN_HEADS + head, gates, 0.0),
                                axis=-1, keepdims=True)
                g_row = gt_s[pl.ds(N_GATES // 2 + d * DN_HEADS + head, 1), pl.ds(i, 1), :].reshape(1, c)
                incl = (jj <= ii) if d == 0 else (jj >= ii)
                strict = (jj < ii) if d == 0 else (jj > ii)
                incl_t = (ii <= jj) if d == 0 else (ii >= jj)
                gcum_col = jnp.sum(jnp.where(incl, g_row, 0.0), axis=1, keepdims=True)
                gcum_row = jnp.sum(jnp.where(incl_t, g_col, 0.0), axis=0, keepdims=True)
                total = jnp.sum(g_row, axis=1, keepdims=True)
                decay = jnp.exp(jnp.where(incl, gcum_col - gcum_row, -jnp.inf))
                e_col = jnp.exp(gcum_col)
                kdt = (kt * jnp.exp(total - gcum_row)).astype(BF16)
                qkm = (qk * decay).astype(BF16)
                chains.append(dict(
                    d=d, i=i,
                    a=jnp.where(strict, beta_col * kk * decay, 0.0),
                    rhs=jnp.concatenate([vc * beta_col, kc * (beta_col * e_col)], axis=1),
                    lhs=jnp.concatenate([kdt, qkm], axis=0),
                    qd=qc * e_col,
                    g_last=jnp.exp(total)))
        corrs = _unit_tri_inverses_minus_eye([ch["a"] for ch in chains], ij_xor)
        sols = [ch["rhs"] + _dot(corr.astype(BF16), ch["rhs"].astype(BF16)) for ch, corr in zip(chains, corrs)]
        prods = [_dot(ch["lhs"], sol.astype(BF16)) for ch, sol in zip(chains, sols)]
        for ch, r in zip(chains, prods):
            d, i = ch["d"], ch["i"]
            mq_s[hh, d, i, 0:dk, :] = (-r[0:dk, dk:2 * dk]).astype(BF16)
            mq_s[hh, d, i, dk:dk + c, :] = (ch["qd"] - r[dk:dk + c, dk:2 * dk]).astype(BF16)
            n_s[hh, d, i] = r[0:dk, 0:dk]
            op_s[hh, d, i] = r[dk:dk + c, 0:dk]
            gl_s[hh, d, i] = jnp.broadcast_to(ch["g_last"], (SUBLANES, dk))
        return carry

    lax.fori_loop(0, n_chunks // DN_GROUP, local_group, 0)


def _dn_call(dn_ctx, dn_lat, conv_w, g_ctx, g_lat, gt_ctx, gt_lat, n_ctx):
    bsz, seq, _ = dn_lat.shape
    tot = n_ctx + seq
    c = DN_CHUNK
    assert n_ctx % c == 0 and seq % c == 0
    n_chunks = tot // c
    ctx_chunks = n_ctx // c
    dk = DN_HEAD_DIM
    gtc4 = gt_ctx.reshape(N_GATES, bsz, ctx_chunks, c).transpose(1, 0, 2, 3)
    gtl4 = gt_lat.reshape(bsz, N_GATES, seq // c, c)
    hp = DN_HEADS_PER_STEP
    steps = DN_HEADS // hp
    col_ctx = lambda off: pl.BlockSpec((1, n_ctx, hp * dk), lambda b, h: (0, b, off + h))
    col_lat = lambda off: pl.BlockSpec((1, seq, hp * dk), lambda b, h: (b, 0, off + h))
    cw = lambda off: pl.BlockSpec((DN_CONV_W, hp * dk), lambda b, h: (0, off + h))
    padded = tot + 3 * SUBLANES
    assert n_chunks % DN_GROUP == 0 and n_chunks % DN_CONV_UNROLL == 0 and DN_HEADS % hp == 0
    return pl.pallas_call(
        _dn_kernel,
        grid=(bsz, steps),
        in_specs=[col_ctx(0), col_ctx(steps), col_ctx(2 * steps),
                  col_lat(0), col_lat(steps), col_lat(2 * steps),
                  cw(0), cw(steps), cw(2 * steps),
                  pl.BlockSpec((1, n_ctx, LANES), lambda b, h: (0, b, 0)),
                  pl.BlockSpec((1, seq, LANES), lambda b, h: (b, 0, 0)),
                  pl.BlockSpec((1, N_GATES, ctx_chunks, c), lambda b, h: (b, 0, 0, 0)),
                  pl.BlockSpec((1, N_GATES, seq // c, c), lambda b, h: (b, 0, 0, 0))],
        out_specs=pl.BlockSpec((1, seq, hp * dk), lambda b, h: (b, 0, h)),
        out_shape=jax.ShapeDtypeStruct((bsz, seq, DN_W), F32),
        scratch_shapes=[pltpu.VMEM((padded, dk), F32)] * 3
        + [pltpu.VMEM((tot, dk), F32)] * 3
        + [pltpu.VMEM((n_chunks, c, c), F32)] * 2
        + [pltpu.VMEM((tot, LANES), F32),
           pltpu.VMEM((N_GATES, n_chunks, c), F32)]
        + [pltpu.VMEM((hp, 2, n_chunks, dk + c, dk), BF16),
           pltpu.VMEM((hp, 2, n_chunks, dk, dk), F32),
           pltpu.VMEM((hp, 2, n_chunks, c, dk), F32),
           pltpu.VMEM((hp, 2, n_chunks, SUBLANES, dk), F32)],
        compiler_params=_params("arbitrary", "arbitrary"),
        name="dn",
    )(dn_ctx, dn_ctx, dn_ctx, dn_lat, dn_lat, dn_lat, conv_w, conv_w, conv_w, g_ctx, g_lat, gtc4, gtl4)


def _outproj_kernel(x_ref, att_ref, o_ref, z_ref, mod_ref, onw_ref, wo_ref, n2_ref, rw_ref,
                    x1_ref, h2_ref, afft_ref):
    d = x_ref.shape[-1]
    b = pl.program_id(0)
    gate1 = mod_ref[pl.ds(b, 1), 2 * d:3 * d]
    shift2 = mod_ref[pl.ds(b, 1), 3 * d:4 * d]
    scale2 = mod_ref[pl.ds(b, 1), 4 * d:5 * d]
    y = _dot(att_ref[0], wo_ref[0:ATTN_Q_W, :])
    for h in range(DN_HEADS):
        sl = slice(h * DN_HEAD_DIM, (h + 1) * DN_HEAD_DIM)
        oh = o_ref[0, :, sl]
        on = oh * lax.rsqrt(jnp.mean(oh * oh, axis=-1, keepdims=True) + EPS) * onw_ref[...]
        yh = (on * _silu(z_ref[0, :, sl])).astype(BF16)
        y = y + _dot(yh, wo_ref[ATTN_Q_W + h * DN_HEAD_DIM:ATTN_Q_W + (h + 1) * DN_HEAD_DIM, :])
    x1 = x_ref[0] + gate1 * y
    x1_ref[0] = x1
    hn = x1 * lax.rsqrt(jnp.mean(x1 * x1, axis=-1, keepdims=True) + EPS) * n2_ref[...]
    h2 = (hn * (1.0 + scale2) + shift2).astype(BF16)
    h2_ref[0] = h2
    logits = _dot(h2, rw_ref[...])
    lane = lax.broadcasted_iota(I32, (1, LANES), 1)
    logits = jnp.where(lane < N_EXPERTS, logits, -jnp.inf)
    e = jnp.exp(logits - jnp.max(logits, axis=-1, keepdims=True))
    aff = e / jnp.sum(e, axis=-1, keepdims=True)
    afft_ref[0] = aff.T[0:N_EXPERTS, :]


def _outproj_call(x, att, o_dn, z, mod, onw, wo, n2, rw):
    bsz, seq, d = x.shape
    t = OUT_TILE
    assert seq % t == 0
    full = lambda shape: pl.BlockSpec(shape, lambda b, j: (0,) * len(shape))
    return pl.pallas_call(
        _outproj_kernel,
        grid=(bsz, seq // t),
        in_specs=[pl.BlockSpec((1, t, d), lambda b, j: (b, j, 0)),
                  pl.BlockSpec((1, t, ATTN_Q_W), lambda b, j: (b, j, 0)),
                  pl.BlockSpec((1, t, DN_W), lambda b, j: (b, j, 0)),
                  pl.BlockSpec((1, t, DN_W), lambda b, j: (b, j, 0)),
                  full((MOD_ROWS, mod.shape[1])),
                  full((1, DN_HEAD_DIM)),
                  full(wo.shape),
                  full((1, d)),
                  full((d, LANES))],
        out_specs=[pl.BlockSpec((1, t, d), lambda b, j: (b, j, 0)),
                   pl.BlockSpec((1, t, d), lambda b, j: (b, j, 0)),
                   pl.BlockSpec((1, N_EXPERTS, t), lambda b, j: (b, 0, j))],
        out_shape=[jax.ShapeDtypeStruct((bsz, seq, d), F32),
                   jax.ShapeDtypeStruct((bsz, seq, d), BF16),
                   jax.ShapeDtypeStruct((bsz, N_EXPERTS, seq), F32)],
        compiler_params=_params("arbitrary", "arbitrary"),
        name="outproj",
    )(x, att, o_dn, z, mod, onw, wo, n2, rw)


def _route_kernel(cap, afft_ref, slot_ref, gate_ref, tok_ref, bounds_ref):
    aff = afft_ref[0]
    n_e, n_t = aff.shape

    def enough(cand):
        return jnp.sum(jnp.where(aff >= cand, 1.0, 0.0), axis=-1, keepdims=True) >= cap

    tiny = 2.0 ** F32_MIN_EXP
    cur = jnp.full((n_e, 1), tiny, F32)
    any_normal = enough(cur)
    shift = 1 << (-F32_MIN_EXP).bit_length()
    while shift > 1:
        shift //= 2
        cand = cur * (2.0 ** shift)
        cur = jnp.where(enough(cand), cand, cur)

    def refine(_, state):
        cur, step = state
        cand = cur + step
        return jnp.where(enough(cand), cand, cur), step * 0.5

    cur, _ = lax.fori_loop(0, F32_MANTISSA_BITS, refine, (cur, cur * 0.5))
    thr = jnp.where(any_normal, cur, 0.0)
    need = cap - jnp.sum(jnp.where(aff > thr, 1.0, 0.0), axis=-1, keepdims=True)

    upper = (lax.broadcasted_iota(I32, (LANES, LANES), 0) < lax.broadcasted_iota(I32, (LANES, LANES), 1))
    upper = jnp.where(upper, 1.0, 0.0).astype(BF16)
    run = jnp.zeros((2 * n_e, 1), F32)
    lane = lax.broadcasted_iota(I32, (1, LANES), 1)
    bounds = jnp.zeros((n_e, LANES), F32)
    per_block = ROUTE_BLOCK // LANES
    for blk in range(n_t // LANES):
        sl = slice(blk * LANES, (blk + 1) * LANES)
        gt = aff[:, sl] > thr
        eq = aff[:, sl] == thr
        x = jnp.concatenate([jnp.where(gt, 1.0, 0.0), jnp.where(eq, 1.0, 0.0)], axis=0)
        cum = _dot(x.astype(BF16), upper) + run
        run = run + jnp.sum(x, axis=-1, keepdims=True)
        cum_gt = cum[0:n_e]
        cum_eq = cum[n_e:2 * n_e]
        sel = gt | (eq & (cum_eq < need))
        slot = jnp.where(sel, cum_gt + jnp.minimum(cum_eq, need), -1.0)
        slot_ref[0, :, sl] = slot.astype(I32)
        gate_ref[0, :, sl] = jnp.where(sel, aff[:, sl], 0.0)
        if (blk + 1) % per_block == 0:
            taken = run[0:n_e] + jnp.minimum(run[n_e:2 * n_e], need)
            bounds = jnp.where(lane == (blk + 1) // per_block, taken, bounds)
    bounds_ref[0] = bounds.astype(I32)

    stacked = jnp.concatenate([slot_ref[0].astype(F32), gate_ref[0],
                               jnp.zeros((LANES - 2 * n_e, n_t), F32)], axis=0)
    tok_ref[0] = stacked.T


def _route_call(afft, cap):
    bsz, n_e, n_t = afft.shape
    row = pl.BlockSpec((1, n_e, n_t), lambda b: (b, 0, 0))
    return pl.pallas_call(
        functools.partial(_route_kernel, cap),
        grid=(bsz,),
        in_specs=[row],
        out_specs=[row, row, pl.BlockSpec((1, n_t, LANES), lambda b: (b, 0, 0)),
                   pl.BlockSpec((1, n_e, LANES), lambda b: (b, 0, 0))],
        out_shape=[jax.ShapeDtypeStruct((bsz, n_e, n_t), I32),
                   jax.ShapeDtypeStruct((bsz, n_e, n_t), F32),
                   jax.ShapeDtypeStruct((bsz, n_t, LANES), F32),
                   jax.ShapeDtypeStruct((bsz, n_e, LANES), I32)],
        compiler_params=_params("arbitrary"),
        name="route",
    )(afft)


def _window_starts(cnt_ref, b, tb, n_e, cap):
    starts, fits = [], None
    for e in range(n_e):
        lo = cnt_ref[b, tb * n_e + e]
        hi = cnt_ref[b, (tb + 1) * n_e + e]
        start = jnp.minimum((lo // BF16_ROWS) * BF16_ROWS, cap - SLOT_WINDOW)
        ok = hi - start <= SLOT_WINDOW
        starts.append(pl.multiple_of(start, BF16_ROWS))
        fits = ok if fits is None else jnp.logical_and(fits, ok)
    return starts, fits


def _gather_kernel(cap, cnt_ref, slot_ref, gate_ref, h_ref, xg_ref, gs_ref):
    b = pl.program_id(0)
    n_e = slot_ref.shape[1]
    n_blk = slot_ref.shape[2]
    xg_ref[...] = jnp.zeros(xg_ref.shape, BF16)
    gs_ref[...] = jnp.zeros(gs_ref.shape, F32)

    def token_block(tb, carry):
        h_blk = h_ref[0, pl.ds(pl.multiple_of(tb * ROUTE_BLOCK, ROUTE_BLOCK), ROUTE_BLOCK), :]
        starts, fits = _window_starts(cnt_ref, b, tb, n_e, cap)

        def accumulate(window, first_rows):
            j = lax.broadcasted_iota(I32, (window, 1), 0)
            hits = [slot_ref[0, e, pl.ds(tb, 1), :] == first_rows[e] + j for e in range(n_e)]
            onehot = jnp.concatenate([jnp.where(hit, 1.0, 0.0).astype(BF16) for hit in hits], axis=0)
            rows = _dot(onehot, h_blk)
            for e in range(n_e):
                dst = pl.ds(first_rows[e], window)
                xg_ref[e, 0, dst, :] += rows[e * window:(e + 1) * window].astype(BF16)
                gate = gate_ref[0, e, pl.ds(tb, 1), :]
                gs_ref[e, 0, dst, :] += jnp.sum(jnp.where(hits[e], gate, 0.0), axis=-1, keepdims=True)

        @pl.when(fits)
        def _():
            accumulate(SLOT_WINDOW, starts)

        @pl.when(jnp.logical_not(fits))
        def _():
            accumulate(cap, [0] * n_e)

        return carry

    lax.fori_loop(0, n_blk, token_block, 0)


def _gather_call(cnt, slot, gate, h2, cap):
    bsz, n_e, n_t = slot.shape
    d = h2.shape[-1]
    n_blk = n_t // ROUTE_BLOCK
    blocked = lambda a: a.reshape(bsz, n_e, n_blk, ROUTE_BLOCK)
    row = pl.BlockSpec((1, n_e, n_blk, ROUTE_BLOCK), lambda b, cnt: (b, 0, 0, 0))
    return pl.pallas_call(
        functools.partial(_gather_kernel, cap),
        grid_spec=pltpu.PrefetchScalarGridSpec(
            num_scalar_prefetch=1,
            grid=(bsz,),
            in_specs=[row, row, pl.BlockSpec((1, n_t, d), lambda b, cnt: (b, 0, 0))],
            out_specs=[pl.BlockSpec((n_e, 1, cap, d), lambda b, cnt: (0, b, 0, 0)),
                       pl.BlockSpec((n_e, 1, cap, 1), lambda b, cnt: (0, b, 0, 0))]),
        out_shape=[jax.ShapeDtypeStruct((n_e, bsz, cap, d), BF16),
                   jax.ShapeDtypeStruct((n_e, bsz, cap, 1), F32)],
        compiler_params=_params("arbitrary"),
        name="gather",
    )(cnt, blocked(slot), blocked(gate), h2)


def _ffn_kernel(xg_ref, gs_ref, wg_ref, wu_ref, wd_ref, y_ref, acc_ref):
    f = pl.program_id(1)
    bsz, cap, d = xg_ref.shape[1:]
    xg = xg_ref[0].reshape(bsz * cap, d)
    hid = _silu(_dot(xg, wg_ref[0].astype(BF16))) * _dot(xg, wu_ref[0].astype(BF16))
    part = _dot(hid.astype(BF16), wd_ref[0].astype(BF16))

    @pl.when(f == 0)
    def _():
        acc_ref[...] = part

    @pl.when(f > 0)
    def _():
        acc_ref[...] += part

    @pl.when(f == pl.num_programs(1) - 1)
    def _():
        y = acc_ref[...] * gs_ref[0].reshape(bsz * cap, 1)
        y_ref[0] = y.astype(BF16).reshape(bsz, cap, d)


def _ffn_call(xg, gs, w_gate, w_up, w_down):
    n_e, bsz, cap, d = xg.shape
    ff = w_gate.shape[-1]
    tf = FF_TILE
    return pl.pallas_call(
        _ffn_kernel,
        grid=(n_e, ff // tf),
        in_specs=[pl.BlockSpec((1, bsz, cap, d), lambda e, f: (e, 0, 0, 0)),
                  pl.BlockSpec((1, bsz, cap, 1), lambda e, f: (e, 0, 0, 0)),
                  pl.BlockSpec((1, d, tf), lambda e, f: (e, 0, f)),
                  pl.BlockSpec((1, d, tf), lambda e, f: (e, 0, f)),
                  pl.BlockSpec((1, tf, d), lambda e, f: (e, f, 0))],
        out_specs=pl.BlockSpec((1, bsz, cap, d), lambda e, f: (e, 0, 0, 0)),
        out_shape=jax.ShapeDtypeStruct((n_e, bsz, cap, d), BF16),
        scratch_shapes=[pltpu.VMEM((bsz * cap, d), F32)],
        compiler_params=_params("arbitrary", "arbitrary"),
        name="ffn",
    )(xg, gs, w_gate, w_up, w_down)


def _combine_kernel(cap, cnt_ref, x1_ref, tok_ref, y_ref, mod_ref, o_ref):
    d = x1_ref.shape[-1]
    n_e = y_ref.shape[0]
    b = pl.program_id(0)
    tb = pl.program_id(1)
    gate2 = mod_ref[pl.ds(b, 1), 5 * d:6 * d]
    tok = tok_ref[0]
    starts, fits = _window_starts(cnt_ref, b, tb, n_e, cap)

    @pl.when(fits)
    def _():
        lane = lax.broadcasted_iota(I32, (1, LANES), 1)
        first = lane < SLOT_WINDOW
        groups = []
        for e in range(0, n_e, 2):
            slot = jnp.where(first, tok[:, e:e + 1], tok[:, e + 1:e + 2])
            target = jnp.where(first, starts[e] + lane, starts[e + 1] + lane - SLOT_WINDOW).astype(F32)
            groups.append(jnp.where(slot == target, 1.0, 0.0).astype(BF16))
        onehot = jnp.concatenate(groups, axis=1)
        rows = jnp.concatenate([y_ref[e, 0, pl.ds(starts[e], SLOT_WINDOW), :] for e in range(n_e)], axis=0)
        o_ref[0] = x1_ref[0] + gate2 * _dot(onehot, rows)

    @pl.when(jnp.logical_not(fits))
    def _():
        j = lax.broadcasted_iota(I32, (1, cap), 1).astype(F32)
        acc = None
        for e in range(n_e):
            onehot = jnp.where(tok[:, e:e + 1] == j, 1.0, 0.0).astype(BF16)
            part = _dot(onehot, y_ref[e, 0])
            acc = part if acc is None else acc + part
        o_ref[0] = x1_ref[0] + gate2 * acc


def _combine_call(cnt, x1, tok, y, mod, cap):
    bsz, seq, d = x1.shape
    n_e = y.shape[0]
    t = ROUTE_BLOCK
    assert 2 * SLOT_WINDOW == LANES and n_e % 2 == 0
    return pl.pallas_call(
        functools.partial(_combine_kernel, cap),
        grid_spec=pltpu.PrefetchScalarGridSpec(
            num_scalar_prefetch=1,
            grid=(bsz, seq // t),
            in_specs=[pl.BlockSpec((1, t, d), lambda b, j, cnt: (b, j, 0)),
                      pl.BlockSpec((1, t, LANES), lambda b, j, cnt: (b, j, 0)),
                      pl.BlockSpec((n_e, 1, cap, d), lambda b, j, cnt: (0, b, 0, 0)),
                      pl.BlockSpec((MOD_ROWS, mod.shape[1]), lambda b, j, cnt: (0, 0))],
            out_specs=pl.BlockSpec((1, t, d), lambda b, j, cnt: (b, j, 0))),
        out_shape=jax.ShapeDtypeStruct((bsz, seq, d), F32),
        compiler_params=_params("arbitrary", "arbitrary"),
        name="combine",
    )(cnt, x1, tok, y, mod)


def _rope_tables(seq):
    m = ATTN_HEAD_DIM // 4
    pos = jnp.arange(seq, dtype=jnp.int32)
    rows = (pos // GRID_W).astype(F32)
    cols = (pos % GRID_W).astype(F32)
    freqs = ROPE_BASE ** (-jnp.arange(m, dtype=F32) / m)
    ang_r = rows[:, None] * freqs[None, :]
    ang_c = cols[:, None] * freqs[None, :]
    cos_h = jnp.concatenate([jnp.cos(ang_r), jnp.cos(ang_r), jnp.cos(ang_c), jnp.cos(ang_c)], axis=-1)
    sin_h = jnp.concatenate([-jnp.sin(ang_r), jnp.sin(ang_r), -jnp.sin(ang_c), jnp.sin(ang_c)], axis=-1)
    reps = LANES // ATTN_HEAD_DIM
    return jnp.tile(cos_h, (1, reps)), jnp.tile(sin_h, (1, reps))


def _lane_row(values, offset):
    return jnp.zeros((1, LANES), F32).at[0, offset:offset + values.shape[0]].set(values.astype(F32))


def kernel(x, c, ctx, c_ctx, w_mod, b_mod, norm1_w, norm2_w, w_in, q_norm_w, k_norm_w, conv_w, a_log, dt_bias,
           o_norm_w, w_out, router_w, w_gate, w_up, w_down):
    bsz, seq, d = x.shape
    n_ctx = ctx.shape[1]
    assert w_mod.shape[0] == 1, "single layer: the last layer's context outputs are never consumed"
    assert bsz < MOD_ROWS and N_EXPERTS == router_w.shape[-1]
    cap = EC_CAPACITY_FACTOR * seq // N_EXPERTS

    cc = jnp.concatenate([c, c_ctx[None, :], jnp.zeros((MOD_ROWS - bsz - 1, d), F32)], axis=0)
    mod = _mod_call(cc, w_mod[0], b_mod[0][None, :])

    w_pad = jnp.pad(w_in[0], ((0, 0), (0, LANES - N_GATES))).astype(BF16)
    seg = np.arange(MXU_DIM) // ATTN_HEAD_DIM
    bd = jnp.asarray(seg[:, None] == seg[None, :], BF16)
    qkw = jnp.concatenate([jnp.tile(q_norm_w[0], ATTN_HEADS), jnp.tile(k_norm_w[0], ATTN_KV_HEADS)])[None, :]
    alog_l = _lane_row(a_log[0].reshape(-1), N_GATES // 2)
    dtb_l = _lane_row(dt_bias[0].reshape(-1), N_GATES // 2)
    shared = (mod, norm1_w[0][None, :], w_pad, bd, qkw, alog_l, dtb_l)
    kv_lat, dn_lat, g_lat, gt_lat, q, z = _inproj_call(x, bsz, *shared, rope=_rope_tables(seq))
    kv_ctx, dn_ctx, g_ctx, gt_ctx = _inproj_call(ctx.reshape(1, bsz * n_ctx, d), bsz, *shared)

    att = _attn_call(q, kv_ctx, kv_lat, n_ctx)
    o_dn = _dn_call(dn_ctx, dn_lat, conv_w[0], g_ctx, g_lat, gt_ctx, gt_lat, n_ctx)

    rw = jnp.pad(router_w[0], ((0, 0), (0, LANES - N_EXPERTS))).astype(BF16)
    x1, h2, afft = _outproj_call(x, att, o_dn, z, mod, o_norm_w[0][None, :], w_out[0].astype(BF16),
                                 norm2_w[0][None, :], rw)

    slot, gate, tok, bounds = _route_call(afft, cap)
    n_blk = seq // ROUTE_BLOCK
    cnt = bounds[:, :, 0:n_blk + 1].transpose(0, 2, 1).reshape(bsz, (n_blk + 1) * N_EXPERTS)
    xg, gs = _gather_call(cnt, slot, gate, h2, cap)
    y = _ffn_call(xg, gs, w_gate[0], w_up[0], w_down[0])
    return _combine_call(cnt, x1, tok, y, mod, cap)
```

```python
import functools
import math

import numpy as np
import jax
import jax.numpy as jnp
from jax import lax
from jax.experimental import pallas as pl
from jax.experimental.pallas import tpu as pltpu

F32 = jnp.float32
BF16 = jnp.bfloat16
I32 = jnp.int32

GRID_W = 64
EPS = 1e-6
ATTN_HEADS = 8
ATTN_KV_HEADS = 2
ATTN_HEAD_DIM = 64
ROPE_BASE = 10000.0
DN_HEADS = 4
DN_HEAD_DIM = 128
DN_CONV_W = 5
N_EXPERTS = 16
EC_CAPACITY_FACTOR = 2

ATTN_Q_W = ATTN_HEADS * ATTN_HEAD_DIM
ATTN_KV_W = ATTN_KV_HEADS * ATTN_HEAD_DIM
DN_W = DN_HEADS * DN_HEAD_DIM
QK_W = ATTN_Q_W + ATTN_KV_W
ATT_W = ATTN_Q_W + 2 * ATTN_KV_W
N_GATES = 4 * DN_HEADS

LANES = 128
SUBLANES = 8
MXU_DIM = 256
VMEM_LIMIT = 56 * 1024 * 1024
F32_MIN_EXP = -126
F32_MANTISSA_BITS = 23

TOK_TILE = 512
Q_TILE = 512
ATTN_KEY_CHUNK = 1024
OUT_TILE = 512
OUT_ROW_CHUNK = 128
DN_CHUNK = 128
DN_GROUP = 6
DN_CONV_UNROLL = 6
DN_HEADS_PER_STEP = 2
ROUTE_BLOCK = 256
SLOT_WINDOW = 64
BF16_ROWS = 16
FF_TILE = 512
FFN_ROW_CHUNK = 512
MOD_ROWS = 16


def _params(*sem):
    return pltpu.CompilerParams(dimension_semantics=sem, vmem_limit_bytes=VMEM_LIMIT)


def _silu(v):
    half = 0.5 * v
    return half + half * jnp.tanh(half)


def _dot(a, b):
    return jnp.dot(a, b, preferred_element_type=F32)


def _dot_nt(a, b):
    return lax.dot_general(a, b, (((1,), (1,)), ((), ())), preferred_element_type=F32)


def _mod_kernel(c_ref, w_ref, b_ref, o_ref):
    sc = _silu(c_ref[...]).astype(BF16)
    o_ref[...] = _dot(sc, w_ref[...].astype(BF16)) + b_ref[...]


def _mod_call(cc, w_mod, b_mod):
    d, n = w_mod.shape
    return pl.pallas_call(
        _mod_kernel,
        grid=(n // d,),
        in_specs=[pl.BlockSpec((MOD_ROWS, d), lambda i: (0, 0)),
                  pl.BlockSpec((d, d), lambda i: (0, i)),
                  pl.BlockSpec((1, d), lambda i: (0, i))],
        out_specs=pl.BlockSpec((MOD_ROWS, d), lambda i: (0, i)),
        out_shape=jax.ShapeDtypeStruct((MOD_ROWS, n), F32),
        compiler_params=_params("arbitrary"),
        name="mod",
    )(cc, w_mod, b_mod)


def _inproj_kernel(latent, ctx_row, x_ref, mod_ref, n1_ref, w_ref, bd_ref, qkw_ref, alog_ref, dtb_ref, *refs):
    if latent:
        cos_ref, sin_ref, kv_ref, dn_ref, g_ref, gt_ref, q_ref, z_ref = refs
    else:
        kv_ref, dn_ref, g_ref, gt_ref = refs
    d = x_ref.shape[-1]
    xin = x_ref[0]
    row = pl.program_id(0) if latent else ctx_row
    shift = mod_ref[pl.ds(row, 1), 0:d]
    scale = mod_ref[pl.ds(row, 1), d:2 * d]
    hn = xin * lax.rsqrt(jnp.mean(xin * xin, axis=-1, keepdims=True) + EPS) * n1_ref[...]
    h = (hn * (1.0 + scale) + shift).astype(BF16)
    p = _dot(h, w_ref[...])

    qk = p[:, 0:QK_W]
    sq = qk * qk
    hi = sq.astype(BF16)
    lo = (sq - hi.astype(F32)).astype(BF16)
    seg_w = bd_ref.shape[0]
    ms = []
    for c0 in range(0, QK_W, seg_w):
        w = min(seg_w, QK_W - c0)
        ones = bd_ref[0:w, 0:w]
        ms.append(_dot(hi[:, c0:c0 + w], ones) + _dot(lo[:, c0:c0 + w], ones))
    ms = jnp.concatenate(ms, axis=1) * (1.0 / ATTN_HEAD_DIM)
    qkn = qk * lax.rsqrt(ms + EPS) * qkw_ref[...]

    lane = lax.broadcasted_iota(I32, (1, LANES), 1)
    k_blk = ATTN_Q_W // LANES
    if latent:
        first_half = (lane % 32) < 16
        cos = cos_ref[...]
        sin = sin_ref[...]
        rot = []
        for i in range(QK_W // LANES):
            blk = qkn[:, i * LANES:(i + 1) * LANES]
            partner = jnp.where(first_half, pltpu.roll(blk, LANES - 16, axis=1), pltpu.roll(blk, 16, axis=1))
            rot.append(blk * cos + partner * sin)
        for i in range(k_blk):
            q_ref[0, :, i * LANES:(i + 1) * LANES] = (rot[i] * (ATTN_HEAD_DIM ** -0.5)).astype(BF16)
        k2 = rot[k_blk]
        z_ref[0] = p[:, ATT_W + 3 * DN_W:ATT_W + 4 * DN_W]
    else:
        k2 = qkn[:, k_blk * LANES:(k_blk + 1) * LANES]

    low = lane < ATTN_HEAD_DIM
    v2 = p[:, QK_W:ATT_W]
    for src, base in ((k2, 0), (v2, 2)):
        swapped = pltpu.roll(src, ATTN_HEAD_DIM, axis=1)
        kv_ref[0, 0, base + 0] = jnp.where(low, src, 0.0).astype(BF16)
        kv_ref[0, 0, base + 1] = jnp.where(low, 0.0, swapped).astype(BF16)
        kv_ref[0, 1, base + 0] = jnp.where(low, swapped, 0.0).astype(BF16)
        kv_ref[0, 1, base + 1] = jnp.where(low, 0.0, src).astype(BF16)

    dn_ref[0] = p[:, ATT_W:ATT_W + 3 * DN_W]

    gp = p[:, ATT_W + 4 * DN_W:]
    beta = jax.nn.sigmoid(gp)
    xa = gp + dtb_ref[...]
    softplus = jnp.maximum(xa, 0.0) + jnp.log1p(jnp.exp(-jnp.abs(xa)))
    decay = -jnp.exp(alog_ref[...]) * softplus
    gates = jnp.where(lane < N_GATES // 2, beta, jnp.where(lane < N_GATES, decay, 0.0))
    g_ref[0] = gates
    gt_ref[0] = gates.T[0:N_GATES, :]


def _inproj_call(rows3, ctx_row, mod, n1, w_pad, bd, qkw, alog_l, dtb_l, rope=None):
    latent = rope is not None
    grp, rows, d = rows3.shape
    t = TOK_TILE
    assert rows % t == 0
    nw = w_pad.shape[1]
    full = lambda shape: pl.BlockSpec(shape, lambda b, j: (0,) * len(shape))
    tile = lambda width: pl.BlockSpec((1, t, width), lambda b, j: (b, j, 0))
    in_specs = [tile(d), full((MOD_ROWS, mod.shape[1])), full((1, d)), full((d, nw)), full(bd.shape),
                full((1, QK_W)), full((1, LANES)), full((1, LANES))]
    out_specs = [pl.BlockSpec((1, ATTN_KV_HEADS, 4, t, LANES), lambda b, j: (b, 0, 0, j, 0)),
                 tile(3 * DN_W), tile(LANES), pl.BlockSpec((1, N_GATES, t), lambda b, j: (b, 0, j))]
    out_shape = [jax.ShapeDtypeStruct((grp, ATTN_KV_HEADS, 4, rows, LANES), BF16),
                 jax.ShapeDtypeStruct((grp, rows, 3 * DN_W), F32),
                 jax.ShapeDtypeStruct((grp, rows, LANES), F32),
                 jax.ShapeDtypeStruct((grp, N_GATES, rows), F32)]
    args = [rows3, mod, n1, w_pad, bd, qkw, alog_l, dtb_l]
    if latent:
        in_specs += [pl.BlockSpec((t, LANES), lambda b, j: (j, 0))] * 2
        out_specs += [tile(ATTN_Q_W), tile(DN_W)]
        out_shape += [jax.ShapeDtypeStruct((grp, rows, ATTN_Q_W), BF16), jax.ShapeDtypeStruct((grp, rows, DN_W), F32)]
        args += list(rope)
    return pl.pallas_call(
        functools.partial(_inproj_kernel, latent, ctx_row),
        grid=(grp, rows // t),
        in_specs=in_specs,
        out_specs=out_specs,
        out_shape=out_shape,
        compiler_params=_params("arbitrary", "arbitrary"),
        name="inproj_lat" if latent else "inproj_ctx",
    )(*args)


def _attn_kernel(q_ref, kvc_ref, kvl_ref, o_ref):
    grp = ATTN_HEADS // ATTN_KV_HEADS
    n_lat = kvl_ref.shape[3]

    def scores(h):
        qp = q_ref[0, :, (h // 2) * LANES:(h // 2 + 1) * LANES]
        return (_dot_nt(qp, kvc_ref[0, h // grp, h % 2]),
                _dot_nt(qp, kvl_ref[0, h // grp, h % 2]))

    s_next = scores(0)
    acc = None
    for h in range(ATTN_HEADS):
        s_ctx, s_lat = s_next
        if h + 1 < ATTN_HEADS:
            s_next = scores(h + 1)
        m = jnp.maximum(jnp.max(s_ctx, axis=-1, keepdims=True), jnp.max(s_lat, axis=-1, keepdims=True))
        stages = [(s_ctx, kvc_ref, 0, s_ctx.shape[-1])]
        stages += [(s_lat, kvl_ref, k0, ATTN_KEY_CHUNK) for k0 in range(0, n_lat, ATTN_KEY_CHUNK)]
        o = denom = None
        for s, v_ref, k0, width in stages:
            e = jnp.exp(s[:, k0:k0 + width] - m)
            part = _dot(e.astype(BF16), v_ref[0, h // grp, 2 + h % 2, k0:k0 + width, :])
            part_sum = jnp.sum(e, axis=-1, keepdims=True)
            o = part if o is None else o + part
            denom = part_sum if denom is None else denom + part_sum
        o = o / denom
        if h % 2 == 0:
            acc = o
        else:
            o_ref[0, :, (h // 2) * LANES:(h // 2 + 1) * LANES] = (acc + o).astype(BF16)


def _attn_call(q, kv_ctx, kv_lat, n_ctx):
    bsz, seq, _ = q.shape
    tq = Q_TILE
    assert seq % tq == 0 and seq % ATTN_KEY_CHUNK == 0
    return pl.pallas_call(
        _attn_kernel,
        grid=(bsz, seq // tq),
        in_specs=[pl.BlockSpec((1, tq, ATTN_Q_W), lambda b, i: (b, i, 0)),
                  pl.BlockSpec((1, ATTN_KV_HEADS, 4, n_ctx, LANES), lambda b, i: (0, 0, 0, b, 0)),
                  pl.BlockSpec((1, ATTN_KV_HEADS, 4, seq, LANES), lambda b, i: (b, 0, 0, 0, 0))],
        out_specs=pl.BlockSpec((1, tq, ATTN_Q_W), lambda b, i: (b, i, 0)),
        out_shape=jax.ShapeDtypeStruct((bsz, seq, ATTN_Q_W), BF16),
        compiler_params=_params("arbitrary", "arbitrary"),
        name="attn",
    )(q, kv_ctx, kv_lat)


def _unit_tri_inverses_minus_eye(mats, level_masks):
    mats = [a.astype(BF16) for a in mats]
    corrs = [-(a * level_masks[0]) for a in mats]
    for joins in level_masks[1:]:
        xs = [a * joins for a in mats]
        ys = [x.astype(F32) + _dot(corr, x) for corr, x in zip(corrs, xs)]
        corrs = [corr - (y + _dot(y.astype(BF16), corr)).astype(BF16) for corr, y in zip(corrs, ys)]
    return corrs


def _tri_level_masks(n):
    ij_xor = lax.broadcasted_iota(I32, (n, n), 0) ^ lax.broadcasted_iota(I32, (n, n), 1)
    masks, m = [], 1
    while m < n:
        masks.append(jnp.where((ij_xor >= m) & (ij_xor < 2 * m), 1.0, 0.0).astype(BF16))
        m *= 2
    return masks


def _dn_kernel(dqc_ref, dkc_ref, dvc_ref, dql_ref, dkl_ref, dvl_ref, cq_ref, ck_ref, cv_ref,
               gc_ref, gl_ref, gtc_ref, gtl_ref, o_ref,
               xq_s, xk_s, xv_s, q_s, k_s, v_s, kk_s, qk_s, g_s, gt_s, mq_s, n_s, op_s, gl_s):
    qkv_ctx_refs = (dqc_ref, dkc_ref, dvc_ref)
    qkv_lat_refs = (dql_ref, dkl_ref, dvl_ref)
    conv_refs = (cq_ref, ck_ref, cv_ref)
    x_scr = (xq_s, xk_s, xv_s)
    qkv_scr = (q_s, k_s, v_s)
    c = DN_CHUNK
    n_ctx = gc_ref.shape[1]
    ctx_chunks = n_ctx // c
    lat_chunks = gl_ref.shape[1] // c
    dk = DN_HEAD_DIM
    ii = lax.broadcasted_iota(I32, (c, c), 0)
    jj = lax.broadcasted_iota(I32, (c, c), 1)
    level_masks = _tri_level_masks(c)
    lane = lax.broadcasted_iota(I32, (1, LANES), 1)

    g_s[0:n_ctx, :] = gc_ref[0]
    g_s[n_ctx:, :] = gl_ref[0]
    gt_s[:, 0:ctx_chunks, :] = gtc_ref[0]
    gt_s[:, ctx_chunks:, :] = gtl_ref[0]

    for hh in range(DN_HEADS_PER_STEP):
        _dn_prepare_head(hh, pl.program_id(1) * DN_HEADS_PER_STEP + hh, ii, jj, level_masks, lane,
                         qkv_ctx_refs, qkv_lat_refs, conv_refs, g_s, gt_s,
                         x_scr, qkv_scr, kk_s, qk_s, mq_s, n_s, op_s, gl_s)

    def scan(first_chunk, count, emit, states):
        rows = dk + c if emit else dk

        def step(t, st):
            chains = [(hh, d, first_chunk + t if d == 0 else first_chunk + count - 1 - t)
                      for hh in range(DN_HEADS_PER_STEP) for d in range(2)]
            prods = [_dot(mq_s[hh, d, i, 0:rows, :], s.astype(BF16)) for (hh, d, i), s in zip(chains, st)]
            new = []
            for (hh, d, i), s, r in zip(chains, st, prods):
                if emit:
                    ro = pl.multiple_of((i - ctx_chunks) * c, c)
                    o_ref[0, pl.ds(ro, c), hh * dk:(hh + 1) * dk] += r[dk:dk + c] + op_s[hh, d, i]
                new.append(s * gl_s[hh, d, i][0:1, :] + r[0:dk] + n_s[hh, d, i])
            return tuple(new)
        return lax.fori_loop(0, count, step, states)

    o_ref[...] = jnp.zeros(o_ref.shape, F32)
    zero_state = jnp.zeros((dk, dk), F32)
    states = scan(0, ctx_chunks, False, (zero_state,) * (2 * DN_HEADS_PER_STEP))
    scan(ctx_chunks, lat_chunks, True, states)


def _dn_prepare_head(hh, head, ii, jj, level_masks, lane, qkv_ctx_refs, qkv_lat_refs, conv_refs, g_s, gt_s,
                     x_scr, qkv_scr, kk_s, qk_s, mq_s, n_s, op_s, gl_s):
    c = DN_CHUNK
    n_ctx = qkv_ctx_refs[0].shape[1]
    seq = qkv_lat_refs[0].shape[1]
    n_chunks = (n_ctx + seq) // c
    ctx_chunks = n_ctx // c
    pad = SUBLANES
    dk = DN_HEAD_DIM
    half = DN_CONV_W // 2
    cols = slice(hh * dk, (hh + 1) * dk)
    xq_s, xk_s, xv_s = x_scr
    q_s, k_s, v_s = qkv_scr
    cq_ref, ck_ref, cv_ref = conv_refs

    zeros_pad = jnp.zeros((pad, dk), F32)
    lat0 = 2 * pad + n_ctx
    for src_ctx, src_lat, dst in zip(qkv_ctx_refs, qkv_lat_refs, x_scr):
        dst[0:pad, :] = zeros_pad
        dst[pad:pad + n_ctx, :] = src_ctx[0, :, cols]
        dst[pad + n_ctx:lat0, :] = zeros_pad
        dst[lat0:lat0 + seq, :] = src_lat[0, :, cols]
        dst[lat0 + seq:lat0 + seq + pad, :] = zeros_pad

    def conv_chunk(i, carry):
        r0 = pl.multiple_of(i * c, c)
        rp = r0 + jnp.where(i >= ctx_chunks, 2 * pad, pad)

        def conv(x_s, cw_ref):
            acc = None
            for s in range(DN_CONV_W):
                term = x_s[pl.ds(rp - half + s, c), :] * cw_ref[s:s + 1, cols]
                acc = term if acc is None else acc + term
            return _silu(acc)

        qc = conv(xq_s, cq_ref)
        kc = conv(xk_s, ck_ref)
        vc = conv(xv_s, cv_ref)
        qc = qc * lax.rsqrt(jnp.sum(qc * qc, axis=-1, keepdims=True) + EPS) * (dk ** -0.5)
        kc = kc * lax.rsqrt(jnp.sum(kc * kc, axis=-1, keepdims=True) + EPS)
        q_s[pl.ds(r0, c), :] = qc
        k_s[pl.ds(r0, c), :] = kc
        v_s[pl.ds(r0, c), :] = vc
        kb = kc.astype(BF16)
        kk_s[i] = _dot_nt(kb, kb)
        qk_s[i] = _dot_nt(qc.astype(BF16), kb)
        return carry

    lax.fori_loop(0, n_chunks, conv_chunk, 0, unroll=DN_CONV_UNROLL)

    def local_group(grp, carry):
        chains = []
        for k in range(DN_GROUP):
            i = grp * DN_GROUP + k
            r0 = pl.multiple_of(i * c, c)
            gates = g_s[pl.ds(r0, c), :]
            kc = k_s[pl.ds(r0, c), :]
            qc = q_s[pl.ds(r0, c), :]
            vc = v_s[pl.ds(r0, c), :]
            kk = kk_s[i]
            qk = qk_s[i]
            kt = kc.T
            for d in range(2):
                beta_col = jnp.sum(jnp.where(lane == d * DN_HEADS + head, gates, 0.0), axis=-1, keepdims=True)
                g_col = jnp.sum(jnp.where(lane == N_GATES // 2 + d * DN_HEADS + head, gates, 0.0),
                                axis=-1, keepdims=True)
                g_row = gt_s[pl.ds(N_GATES // 2 + d * DN_HEADS + head, 1), pl.ds(i, 1), :].reshape(1, c)
                incl = (jj <= ii) if d == 0 else (jj >= ii)
                strict = (jj < ii) if d == 0 else (jj > ii)
                incl_t = (ii <= jj) if d == 0 else (ii >= jj)
                gcum_col = jnp.sum(jnp.where(incl, g_row, 0.0), axis=1, keepdims=True)
                gcum_row = jnp.sum(jnp.where(incl_t, g_col, 0.0), axis=0, keepdims=True)
                total = jnp.sum(g_row, axis=1, keepdims=True)
                decay = jnp.exp(jnp.where(incl, gcum_col - gcum_row, -jnp.inf))
                e_col = jnp.exp(gcum_col)
                kdt = (kt * jnp.exp(total - gcum_row)).astype(BF16)
                qkm = (qk * decay).astype(BF16)
                chains.append(dict(
                    d=d, i=i,
                    a=jnp.where(strict, beta_col * kk * decay, 0.0),
                    rhs=jnp.concatenate([vc * beta_col, kc * (beta_col * e_col)], axis=1),
                    lhs=jnp.concatenate([kdt, qkm], axis=0),
                    qd=qc * e_col,
                    g_last=jnp.exp(total)))
        corrs = _unit_tri_inverses_minus_eye([ch["a"] for ch in chains], level_masks)
        sols = [ch["rhs"] + _dot(corr, ch["rhs"].astype(BF16)) for ch, corr in zip(chains, corrs)]
        prods = [_dot(ch["lhs"], sol.astype(BF16)) for ch, sol in zip(chains, sols)]
        for ch, r in zip(chains, prods):
            d, i = ch["d"], ch["i"]
            mq_s[hh, d, i, 0:dk, :] = (-r[0:dk, dk:2 * dk]).astype(BF16)
            mq_s[hh, d, i, dk:dk + c, :] = (ch["qd"] - r[dk:dk + c, dk:2 * dk]).astype(BF16)
            n_s[hh, d, i] = r[0:dk, 0:dk]
            op_s[hh, d, i] = r[dk:dk + c, 0:dk]
            gl_s[hh, d, i] = jnp.broadcast_to(ch["g_last"], (SUBLANES, dk))
        return carry

    lax.fori_loop(0, n_chunks // DN_GROUP, local_group, 0)


def _dn_call(dn_ctx, dn_lat, conv_w, g_ctx, g_lat, gt_ctx, gt_lat, n_ctx):
    bsz, seq, _ = dn_lat.shape
    tot = n_ctx + seq
    c = DN_CHUNK
    assert n_ctx % c == 0 and seq % c == 0
    n_chunks = tot // c
    ctx_chunks = n_ctx // c
    dk = DN_HEAD_DIM
    gtc4 = gt_ctx.reshape(N_GATES, bsz, ctx_chunks, c).transpose(1, 0, 2, 3)
    gtl4 = gt_lat.reshape(bsz, N_GATES, seq // c, c)
    hp = DN_HEADS_PER_STEP
    steps = DN_HEADS // hp
    col_ctx = lambda off: pl.BlockSpec((1, n_ctx, hp * dk), lambda b, h: (0, b, off + h))
    col_lat = lambda off: pl.BlockSpec((1, seq, hp * dk), lambda b, h: (b, 0, off + h))
    cw = lambda off: pl.BlockSpec((DN_CONV_W, hp * dk), lambda b, h: (0, off + h))
    padded = tot + 3 * SUBLANES
    assert n_chunks % DN_GROUP == 0 and n_chunks % DN_CONV_UNROLL == 0 and DN_HEADS % hp == 0
    return pl.pallas_call(
        _dn_kernel,
        grid=(bsz, steps),
        in_specs=[col_ctx(0), col_ctx(steps), col_ctx(2 * steps),
                  col_lat(0), col_lat(steps), col_lat(2 * steps),
                  cw(0), cw(steps), cw(2 * steps),
                  pl.BlockSpec((1, n_ctx, LANES), lambda b, h: (0, b, 0)),
                  pl.BlockSpec((1, seq, LANES), lambda b, h: (b, 0, 0)),
                  pl.BlockSpec((1, N_GATES, ctx_chunks, c), lambda b, h: (b, 0, 0, 0)),
                  pl.BlockSpec((1, N_GATES, seq // c, c), lambda b, h: (b, 0, 0, 0))],
        out_specs=pl.BlockSpec((1, seq, hp * dk), lambda b, h: (b, 0, h)),
        out_shape=jax.ShapeDtypeStruct((bsz, seq, DN_W), F32),
        scratch_shapes=[pltpu.VMEM((padded, dk), F32)] * 3
        + [pltpu.VMEM((tot, dk), F32)] * 3
        + [pltpu.VMEM((n_chunks, c, c), F32)] * 2
        + [pltpu.VMEM((tot, LANES), F32),
           pltpu.VMEM((N_GATES, n_chunks, c), F32)]
        + [pltpu.VMEM((hp, 2, n_chunks, dk + c, dk), BF16),
           pltpu.VMEM((hp, 2, n_chunks, dk, dk), F32),
           pltpu.VMEM((hp, 2, n_chunks, c, dk), F32),
           pltpu.VMEM((hp, 2, n_chunks, SUBLANES, dk), F32)],
        compiler_params=_params("arbitrary", "arbitrary"),
        name="dn",
    )(dn_ctx, dn_ctx, dn_ctx, dn_lat, dn_lat, dn_lat, conv_w, conv_w, conv_w, g_ctx, g_lat, gtc4, gtl4)


def _outproj_kernel(x_ref, att_ref, o_ref, z_ref, mod_ref, onw_ref, wo_ref, n2_ref, rw_ref,
                    x1_ref, h2_ref, afft_ref):
    d = x_ref.shape[-1]
    b = pl.program_id(0)
    gate1 = mod_ref[pl.ds(b, 1), 2 * d:3 * d]
    shift2 = mod_ref[pl.ds(b, 1), 3 * d:4 * d]
    scale2 = mod_ref[pl.ds(b, 1), 4 * d:5 * d]
    lane = lax.broadcasted_iota(I32, (1, LANES), 1)
    n_rows = x_ref.shape[1]

    def mixed(rows):
        parts = [att_ref[0, rows, :]]
        for h in range(DN_HEADS):
            sl = slice(h * DN_HEAD_DIM, (h + 1) * DN_HEAD_DIM)
            oh = o_ref[0, rows, sl]
            on = oh * lax.rsqrt(jnp.mean(oh * oh, axis=-1, keepdims=True) + EPS) * onw_ref[...]
            parts.append((on * _silu(z_ref[0, rows, sl])).astype(BF16))
        return _dot(jnp.concatenate(parts, axis=1), wo_ref[...])

    def finish(rows, y):
        x1 = x_ref[0, rows, :] + gate1 * y
        x1_ref[0, rows, :] = x1
        hn = x1 * lax.rsqrt(jnp.mean(x1 * x1, axis=-1, keepdims=True) + EPS) * n2_ref[...]
        h2 = (hn * (1.0 + scale2) + shift2).astype(BF16)
        h2_ref[0, rows, :] = h2
        logits = _dot(h2, rw_ref[...])
        logits = jnp.where(lane < N_EXPERTS, logits, -jnp.inf)
        e = jnp.exp(logits - jnp.max(logits, axis=-1, keepdims=True))
        aff = e / jnp.sum(e, axis=-1, keepdims=True)
        afft_ref[0, :, rows] = aff.T[0:N_EXPERTS, :]

    chunks = [slice(r0, r0 + OUT_ROW_CHUNK) for r0 in range(0, n_rows, OUT_ROW_CHUNK)]
    y_next = mixed(chunks[0])
    for r, rows in enumerate(chunks):
        y = y_next
        if r + 1 < len(chunks):
            y_next = mixed(chunks[r + 1])
        finish(rows, y)


def _outproj_call(x, att, o_dn, z, mod, onw, wo, n2, rw):
    bsz, seq, d = x.shape
    t = OUT_TILE
    assert seq % t == 0
    full = lambda shape: pl.BlockSpec(shape, lambda b, j: (0,) * len(shape))
    return pl.pallas_call(
        _outproj_kernel,
        grid=(bsz, seq // t),
        in_specs=[pl.BlockSpec((1, t, d), lambda b, j: (b, j, 0)),
                  pl.BlockSpec((1, t, ATTN_Q_W), lambda b, j: (b, j, 0)),
                  pl.BlockSpec((1, t, DN_W), lambda b, j: (b, j, 0)),
                  pl.BlockSpec((1, t, DN_W), lambda b, j: (b, j, 0)),
                  full((MOD_ROWS, mod.shape[1])),
                  full((1, DN_HEAD_DIM)),
                  full(wo.shape),
                  full((1, d)),
                  full((d, LANES))],
        out_specs=[pl.BlockSpec((1, t, d), lambda b, j: (b, j, 0)),
                   pl.BlockSpec((1, t, d), lambda b, j: (b, j, 0)),
                   pl.BlockSpec((1, N_EXPERTS, t), lambda b, j: (b, 0, j))],
        out_shape=[jax.ShapeDtypeStruct((bsz, seq, d), F32),
                   jax.ShapeDtypeStruct((bsz, seq, d), BF16),
                   jax.ShapeDtypeStruct((bsz, N_EXPERTS, seq), F32)],
        compiler_params=_params("arbitrary", "arbitrary"),
        name="outproj",
    )(x, att, o_dn, z, mod, onw, wo, n2, rw)


def _route_kernel(cap, afft_ref, slot_ref, gate_ref, tok_ref, bounds_ref):
    aff = afft_ref[0]
    n_e, n_t = aff.shape

    def enough(cand):
        return jnp.sum(jnp.where(aff >= cand, 1.0, 0.0), axis=-1, keepdims=True) >= cap

    tiny = 2.0 ** F32_MIN_EXP
    cur = jnp.full((n_e, 1), tiny, F32)
    any_normal = enough(cur)
    shift = 1 << (-F32_MIN_EXP).bit_length()
    while shift > 1:
        shift //= 2
        cand = cur * (2.0 ** shift)
        cur = jnp.where(enough(cand), cand, cur)

    def refine(_, state):
        cur, step = state
        cand = cur + step
        return jnp.where(enough(cand), cand, cur), step * 0.5

    cur, _ = lax.fori_loop(0, F32_MANTISSA_BITS, refine, (cur, cur * 0.5))
    thr = jnp.where(any_normal, cur, 0.0)
    need = cap - jnp.sum(jnp.where(aff > thr, 1.0, 0.0), axis=-1, keepdims=True)

    upper = (lax.broadcasted_iota(I32, (LANES, LANES), 0) < lax.broadcasted_iota(I32, (LANES, LANES), 1))
    upper = jnp.where(upper, 1.0, 0.0).astype(BF16)
    run = jnp.zeros((2 * n_e, 1), F32)
    lane = lax.broadcasted_iota(I32, (1, LANES), 1)
    bounds = jnp.zeros((n_e, LANES), F32)
    per_block = ROUTE_BLOCK // LANES
    for blk in range(n_t // LANES):
        sl = slice(blk * LANES, (blk + 1) * LANES)
        gt = aff[:, sl] > thr
        eq = aff[:, sl] == thr
        x = jnp.concatenate([jnp.where(gt, 1.0, 0.0), jnp.where(eq, 1.0, 0.0)], axis=0)
        cum = _dot(x.astype(BF16), upper) + run
        run = run + jnp.sum(x, axis=-1, keepdims=True)
        cum_gt = cum[0:n_e]
        cum_eq = cum[n_e:2 * n_e]
        sel = gt | (eq & (cum_eq < need))
        slot = jnp.where(sel, cum_gt + jnp.minimum(cum_eq, need), -1.0)
        slot_ref[0, :, sl] = slot.astype(I32)
        gate_ref[0, :, sl] = jnp.where(sel, aff[:, sl], 0.0)
        if (blk + 1) % per_block == 0:
            taken = run[0:n_e] + jnp.minimum(run[n_e:2 * n_e], need)
            bounds = jnp.where(lane == (blk + 1) // per_block, taken, bounds)
    bounds_ref[0] = bounds.astype(I32)

    stacked = jnp.concatenate([slot_ref[0].astype(F32), gate_ref[0],
                               jnp.zeros((LANES - 2 * n_e, n_t), F32)], axis=0)
    tok_ref[0] = stacked.T


def _route_call(afft, cap):
    bsz, n_e, n_t = afft.shape
    row = pl.BlockSpec((1, n_e, n_t), lambda b: (b, 0, 0))
    return pl.pallas_call(
        functools.partial(_route_kernel, cap),
        grid=(bsz,),
        in_specs=[row],
        out_specs=[row, row, pl.BlockSpec((1, n_t, LANES), lambda b: (b, 0, 0)),
                   pl.BlockSpec((1, n_e, LANES), lambda b: (b, 0, 0))],
        out_shape=[jax.ShapeDtypeStruct((bsz, n_e, n_t), I32),
                   jax.ShapeDtypeStruct((bsz, n_e, n_t), F32),
                   jax.ShapeDtypeStruct((bsz, n_t, LANES), F32),
                   jax.ShapeDtypeStruct((bsz, n_e, LANES), I32)],
        compiler_params=_params("arbitrary"),
        name="route",
    )(afft)


def _window_starts(cnt_ref, b, tb, n_e, cap):
    starts, fits = [], None
    for e in range(n_e):
        lo = cnt_ref[b, tb * n_e + e]
        hi = cnt_ref[b, (tb + 1) * n_e + e]
        start = jnp.minimum((lo // BF16_ROWS) * BF16_ROWS, cap - SLOT_WINDOW)
        ok = hi - start <= SLOT_WINDOW
        starts.append(pl.multiple_of(start, BF16_ROWS))
        fits = ok if fits is None else jnp.logical_and(fits, ok)
    return starts, fits


def _gather_kernel(cap, cnt_ref, slot_ref, gate_ref, h_ref, xg_ref, gs_ref):
    b = pl.program_id(0)
    n_e = slot_ref.shape[1]
    n_blk = slot_ref.shape[2]
    xg_ref[...] = jnp.zeros(xg_ref.shape, BF16)
    gs_ref[...] = jnp.zeros(gs_ref.shape, F32)

    def token_block(tb, carry):
        h_blk = h_ref[0, pl.ds(pl.multiple_of(tb * ROUTE_BLOCK, ROUTE_BLOCK), ROUTE_BLOCK), :]
        starts, fits = _window_starts(cnt_ref, b, tb, n_e, cap)

        def accumulate(window, first_rows):
            j = lax.broadcasted_iota(I32, (window, 1), 0)
            hits = [slot_ref[0, e, pl.ds(tb, 1), :] == first_rows[e] + j for e in range(n_e)]
            onehot = jnp.concatenate([jnp.where(hit, 1.0, 0.0).astype(BF16) for hit in hits], axis=0)
            rows = _dot(onehot, h_blk)
            for e in range(n_e):
                dst = pl.ds(first_rows[e], window)
                xg_ref[e, 0, dst, :] += rows[e * window:(e + 1) * window].astype(BF16)
                gate = gate_ref[0, e, pl.ds(tb, 1), :]
                gs_ref[e, 0, dst, :] += jnp.sum(jnp.where(hits[e], gate, 0.0), axis=-1, keepdims=True)

        @pl.when(fits)
        def _():
            accumulate(SLOT_WINDOW, starts)

        @pl.when(jnp.logical_not(fits))
        def _():
            accumulate(cap, [0] * n_e)

        return carry

    lax.fori_loop(0, n_blk, token_block, 0)


def _gather_call(cnt, slot, gate, h2, cap):
    bsz, n_e, n_t = slot.shape
    d = h2.shape[-1]
    n_blk = n_t // ROUTE_BLOCK
    blocked = lambda a: a.reshape(bsz, n_e, n_blk, ROUTE_BLOCK)
    row = pl.BlockSpec((1, n_e, n_blk, ROUTE_BLOCK), lambda b, cnt: (b, 0, 0, 0))
    return pl.pallas_call(
        functools.partial(_gather_kernel, cap),
        grid_spec=pltpu.PrefetchScalarGridSpec(
            num_scalar_prefetch=1,
            grid=(bsz,),
            in_specs=[row, row, pl.BlockSpec((1, n_t, d), lambda b, cnt: (b, 0, 0))],
            out_specs=[pl.BlockSpec((n_e, 1, cap, d), lambda b, cnt: (0, b, 0, 0)),
                       pl.BlockSpec((n_e, 1, cap, 1), lambda b, cnt: (0, b, 0, 0))]),
        out_shape=[jax.ShapeDtypeStruct((n_e, bsz, cap, d), BF16),
                   jax.ShapeDtypeStruct((n_e, bsz, cap, 1), F32)],
        compiler_params=_params("arbitrary"),
        name="gather",
    )(cnt, blocked(slot), blocked(gate), h2)


def _ffn_kernel(xg_ref, gs_ref, wg_ref, wu_ref, wd_ref, y_ref, acc_ref):
    bsz, cap, d = xg_ref.shape[1:]
    per = FFN_ROW_CHUNK // cap
    n_chunks = bsz // per

    @pl.when(pl.program_id(1) == 0)
    def _():
        acc_ref[...] = jnp.zeros(acc_ref.shape, F32)

    wg = wg_ref[0].astype(BF16)
    wu = wu_ref[0].astype(BF16)
    wd = wd_ref[0].astype(BF16)

    def up(r):
        x = xg_ref[0, r * per:(r + 1) * per].reshape(per * cap, d)
        return _dot(x, wg), _dot(x, wu)

    nxt = up(0)
    for r in range(n_chunks):
        g, u = nxt
        if r + 1 < n_chunks:
            nxt = up(r + 1)
        rows = slice(r * per * cap, (r + 1) * per * cap)
        total = acc_ref[rows, :] + _dot((_silu(g) * u).astype(BF16), wd)
        acc_ref[rows, :] = total
        y = total * gs_ref[0, r * per:(r + 1) * per].reshape(per * cap, 1)
        y_ref[0, r * per:(r + 1) * per] = y.astype(BF16).reshape(per, cap, d)


def _ffn_call(xg, gs, w_gate, w_up, w_down):
    n_e, bsz, cap, d = xg.shape
    ff = w_gate.shape[-1]
    tf = FF_TILE
    return pl.pallas_call(
        _ffn_kernel,
        grid=(n_e, ff // tf),
        in_specs=[pl.BlockSpec((1, bsz, cap, d), lambda e, f: (e, 0, 0, 0)),
                  pl.BlockSpec((1, bsz, cap, 1), lambda e, f: (e, 0, 0, 0)),
                  pl.BlockSpec((1, d, tf), lambda e, f: (e, 0, f)),
                  pl.BlockSpec((1, d, tf), lambda e, f: (e, 0, f)),
                  pl.BlockSpec((1, tf, d), lambda e, f: (e, f, 0))],
        out_specs=pl.BlockSpec((1, bsz, cap, d), lambda e, f: (e, 0, 0, 0)),
        out_shape=jax.ShapeDtypeStruct((n_e, bsz, cap, d), BF16),
        scratch_shapes=[pltpu.VMEM((bsz * cap, d), F32)],
        compiler_params=_params("arbitrary", "arbitrary"),
        name="ffn",
    )(xg, gs, w_gate, w_up, w_down)


def _combine_kernel(cap, cnt_ref, x1_ref, tok_ref, y_ref, mod_ref, o_ref):
    d = x1_ref.shape[-1]
    n_e = y_ref.shape[0]
    b = pl.program_id(0)
    tb = pl.program_id(1)
    gate2 = mod_ref[pl.ds(b, 1), 5 * d:6 * d]
    tok = tok_ref[0]
    starts, fits = _window_starts(cnt_ref, b, tb, n_e, cap)

    @pl.when(fits)
    def _():
        lane = lax.broadcasted_iota(I32, (1, LANES), 1)
        first = lane < SLOT_WINDOW
        groups = []
        for e in range(0, n_e, 2):
            slot = jnp.where(first, tok[:, e:e + 1], tok[:, e + 1:e + 2])
            target = jnp.where(first, starts[e] + lane, starts[e + 1] + lane - SLOT_WINDOW).astype(F32)
            groups.append(jnp.where(slot == target, 1.0, 0.0).astype(BF16))
        onehot = jnp.concatenate(groups, axis=1)
        rows = jnp.concatenate([y_ref[e, 0, pl.ds(starts[e], SLOT_WINDOW), :] for e in range(n_e)], axis=0)
        o_ref[0] = x1_ref[0] + gate2 * _dot(onehot, rows)

    @pl.when(jnp.logical_not(fits))
    def _():
        j = lax.broadcasted_iota(I32, (1, cap), 1).astype(F32)
        acc = None
        for e in range(n_e):
            onehot = jnp.where(tok[:, e:e + 1] == j, 1.0, 0.0).astype(BF16)
            part = _dot(onehot, y_ref[e, 0])
            acc = part if acc is None else acc + part
        o_ref[0] = x1_ref[0] + gate2 * acc


def _combine_call(cnt, x1, tok, y, mod, cap):
    bsz, seq, d = x1.shape
    n_e = y.shape[0]
    t = ROUTE_BLOCK
    assert 2 * SLOT_WINDOW == LANES and n_e % 2 == 0
    return pl.pallas_call(
        functools.partial(_combine_kernel, cap),
        grid_spec=pltpu.PrefetchScalarGridSpec(
            num_scalar_prefetch=1,
            grid=(bsz, seq // t),
            in_specs=[pl.BlockSpec((1, t, d), lambda b, j, cnt: (b, j, 0)),
                      pl.BlockSpec((1, t, LANES), lambda b, j, cnt: (b, j, 0)),
                      pl.BlockSpec((n_e, 1, cap, d), lambda b, j, cnt: (0, b, 0, 0)),
                      pl.BlockSpec((MOD_ROWS, mod.shape[1]), lambda b, j, cnt: (0, 0))],
            out_specs=pl.BlockSpec((1, t, d), lambda b, j, cnt: (b, j, 0))),
        out_shape=jax.ShapeDtypeStruct((bsz, seq, d), F32),
        compiler_params=_params("arbitrary", "arbitrary"),
        name="combine",
    )(cnt, x1, tok, y, mod)


def _rope_tables(seq):
    m = ATTN_HEAD_DIM // 4
    pos = jnp.arange(seq, dtype=jnp.int32)
    rows = (pos // GRID_W).astype(F32)
    cols = (pos % GRID_W).astype(F32)
    freqs = ROPE_BASE ** (-jnp.arange(m, dtype=F32) / m)
    ang_r = rows[:, None] * freqs[None, :]
    ang_c = cols[:, None] * freqs[None, :]
    cos_h = jnp.concatenate([jnp.cos(ang_r), jnp.cos(ang_r), jnp.cos(ang_c), jnp.cos(ang_c)], axis=-1)
    sin_h = jnp.concatenate([-jnp.sin(ang_r), jnp.sin(ang_r), -jnp.sin(ang_c), jnp.sin(ang_c)], axis=-1)
    reps = LANES // ATTN_HEAD_DIM
    return jnp.tile(cos_h, (1, reps)), jnp.tile(sin_h, (1, reps))


def _lane_row(values, offset):
    return jnp.zeros((1, LANES), F32).at[0, offset:offset + values.shape[0]].set(values.astype(F32))


def kernel(x, c, ctx, c_ctx, w_mod, b_mod, norm1_w, norm2_w, w_in, q_norm_w, k_norm_w, conv_w, a_log, dt_bias,
           o_norm_w, w_out, router_w, w_gate, w_up, w_down):
    bsz, seq, d = x.shape
    n_ctx = ctx.shape[1]
    assert w_mod.shape[0] == 1, "single layer: the last layer's context outputs are never consumed"
    assert bsz < MOD_ROWS and N_EXPERTS == router_w.shape[-1]
    cap = EC_CAPACITY_FACTOR * seq // N_EXPERTS

    cc = jnp.concatenate([c, c_ctx[None, :], jnp.zeros((MOD_ROWS - bsz - 1, d), F32)], axis=0)
    mod = _mod_call(cc, w_mod[0], b_mod[0][None, :])

    w_pad = jnp.pad(w_in[0], ((0, 0), (0, LANES - N_GATES))).astype(BF16)
    seg = np.arange(MXU_DIM) // ATTN_HEAD_DIM
    bd = jnp.asarray(seg[:, None] == seg[None, :], BF16)
    qkw = jnp.concatenate([jnp.tile(q_norm_w[0], ATTN_HEADS), jnp.tile(k_norm_w[0], ATTN_KV_HEADS)])[None, :]
    alog_l = _lane_row(a_log[0].reshape(-1), N_GATES // 2)
    dtb_l = _lane_row(dt_bias[0].reshape(-1), N_GATES // 2)
    shared = (mod, norm1_w[0][None, :], w_pad, bd, qkw, alog_l, dtb_l)
    kv_lat, dn_lat, g_lat, gt_lat, q, z = _inproj_call(x, bsz, *shared, rope=_rope_tables(seq))
    kv_ctx, dn_ctx, g_ctx, gt_ctx = _inproj_call(ctx.reshape(1, bsz * n_ctx, d), bsz, *shared)

    att = _attn_call(q, kv_ctx, kv_lat, n_ctx)
    o_dn = _dn_call(dn_ctx, dn_lat, conv_w[0], g_ctx, g_lat, gt_ctx, gt_lat, n_ctx)

    rw = jnp.pad(router_w[0], ((0, 0), (0, LANES - N_EXPERTS))).astype(BF16)
    x1, h2, afft = _outproj_call(x, att, o_dn, z, mod, o_norm_w[0][None, :], w_out[0].astype(BF16),
                                 norm2_w[0][None, :], rw)

    slot, gate, tok, bounds = _route_call(afft, cap)
    n_blk = seq // ROUTE_BLOCK
    cnt = bounds[:, :, 0:n_blk + 1].transpose(0, 2, 1).reshape(bsz, (n_blk + 1) * N_EXPERTS)
    xg, gs = _gather_call(cnt, slot, gate, h2, cap)
    y = _ffn_call(xg, gs, w_gate[0], w_up[0], w_down[0])
    return _combine_call(cnt, x1, tok, y, mod, cap)
```

```python
import functools
import math

import numpy as np
import jax
import jax.numpy as jnp
from jax import lax
from jax.experimental import pallas as pl
from jax.experimental.pallas import tpu as pltpu

F32 = jnp.float32
BF16 = jnp.bfloat16
I32 = jnp.int32

GRID_W = 64
EPS = 1e-6
ATTN_HEADS = 8
ATTN_KV_HEADS = 2
ATTN_HEAD_DIM = 64
ROPE_BASE = 10000.0
DN_HEADS = 4
DN_HEAD_DIM = 128
DN_CONV_W = 5
N_EXPERTS = 16
EC_CAPACITY_FACTOR = 2

ATTN_Q_W = ATTN_HEADS * ATTN_HEAD_DIM
ATTN_KV_W = ATTN_KV_HEADS * ATTN_HEAD_DIM
DN_W = DN_HEADS * DN_HEAD_DIM
QK_W = ATTN_Q_W + ATTN_KV_W
ATT_W = ATTN_Q_W + 2 * ATTN_KV_W
N_GATES = 4 * DN_HEADS

LANES = 128
SUBLANES = 8
MXU_DIM = 256
VMEM_LIMIT = 56 * 1024 * 1024
F32_MIN_EXP = -126
F32_MANTISSA_BITS = 23

TOK_TILE = 512
IN_ROW_CHUNK = 128
Q_TILE = 512
ATTN_KEY_CHUNK = 1024
OUT_TILE = 512
OUT_ROW_CHUNK = 128
DN_CHUNK = 128
DN_GROUP = 9
DN_CONV_UNROLL = 6
DN_HEADS_PER_STEP = 2
ROUTE_BLOCK = 256
SLOT_WINDOW = 64
BF16_ROWS = 16
FF_TILE = 512
FFN_ROW_CHUNK = 512
MOD_ROWS = 16


def _params(*sem):
    return pltpu.CompilerParams(dimension_semantics=sem, vmem_limit_bytes=VMEM_LIMIT)


def _silu(v):
    half = 0.5 * v
    return half + half * jnp.tanh(half)


def _dot(a, b):
    return jnp.dot(a, b, preferred_element_type=F32)


def _dot_nt(a, b):
    return lax.dot_general(a, b, (((1,), (1,)), ((), ())), preferred_element_type=F32)


def _mod_kernel(c_ref, w_ref, b_ref, o_ref):
    sc = _silu(c_ref[...]).astype(BF16)
    o_ref[...] = _dot(sc, w_ref[...].astype(BF16)) + b_ref[...]


def _mod_call(cc, w_mod, b_mod):
    d, n = w_mod.shape
    return pl.pallas_call(
        _mod_kernel,
        grid=(n // d,),
        in_specs=[pl.BlockSpec((MOD_ROWS, d), lambda i: (0, 0)),
                  pl.BlockSpec((d, d), lambda i: (0, i)),
                  pl.BlockSpec((1, d), lambda i: (0, i))],
        out_specs=pl.BlockSpec((MOD_ROWS, d), lambda i: (0, i)),
        out_shape=jax.ShapeDtypeStruct((MOD_ROWS, n), F32),
        compiler_params=_params("arbitrary"),
        name="mod",
    )(cc, w_mod, b_mod)


def _inproj_kernel(latent, ctx_row, x_ref, mod_ref, n1_ref, w_ref, bd_ref, qkw_ref, alog_ref, dtb_ref, *refs):
    if latent:
        cos_ref, sin_ref, kv_ref, dn_ref, g_ref, gt_ref, q_ref, z_ref = refs
    else:
        kv_ref, dn_ref, g_ref, gt_ref = refs
    d = x_ref.shape[-1]
    n_rows = x_ref.shape[1]
    row = pl.program_id(0) if latent else ctx_row
    shift = mod_ref[pl.ds(row, 1), 0:d]
    scale = mod_ref[pl.ds(row, 1), d:2 * d]
    lane = lax.broadcasted_iota(I32, (1, LANES), 1)
    k_blk = ATTN_Q_W // LANES

    def project(rows):
        xin = x_ref[0, rows, :]
        hn = xin * lax.rsqrt(jnp.mean(xin * xin, axis=-1, keepdims=True) + EPS) * n1_ref[...]
        h = (hn * (1.0 + scale) + shift).astype(BF16)
        return _dot(h, w_ref[...])

    def finish(rows, p):
        qk = p[:, 0:QK_W]
        sq = qk * qk
        hi = sq.astype(BF16)
        lo = (sq - hi.astype(F32)).astype(BF16)
        seg_w = bd_ref.shape[0]
        ms = []
        for c0 in range(0, QK_W, seg_w):
            w = min(seg_w, QK_W - c0)
            ones = bd_ref[0:w, 0:w]
            ms.append(_dot(hi[:, c0:c0 + w], ones) + _dot(lo[:, c0:c0 + w], ones))
        ms = jnp.concatenate(ms, axis=1) * (1.0 / ATTN_HEAD_DIM)
        qkn = qk * lax.rsqrt(ms + EPS) * qkw_ref[...]

        if latent:
            first_half = (lane % 32) < 16
            cos = cos_ref[rows, :]
            sin = sin_ref[rows, :]
            rot = []
            for i in range(QK_W // LANES):
                blk = qkn[:, i * LANES:(i + 1) * LANES]
                partner = jnp.where(first_half, pltpu.roll(blk, LANES - 16, axis=1), pltpu.roll(blk, 16, axis=1))
                rot.append(blk * cos + partner * sin)
            for i in range(k_blk):
                q_ref[0, rows, i * LANES:(i + 1) * LANES] = (rot[i] * (ATTN_HEAD_DIM ** -0.5)).astype(BF16)
            k2 = rot[k_blk]
            z_ref[0, rows, :] = p[:, ATT_W + 3 * DN_W:ATT_W + 4 * DN_W]
        else:
            k2 = qkn[:, k_blk * LANES:(k_blk + 1) * LANES]

        low = lane < ATTN_HEAD_DIM
        v2 = p[:, QK_W:ATT_W]
        for src, base in ((k2, 0), (v2, 2)):
            swapped = pltpu.roll(src, ATTN_HEAD_DIM, axis=1)
            kv_ref[0, 0, base + 0, rows, :] = jnp.where(low, src, 0.0).astype(BF16)
            kv_ref[0, 0, base + 1, rows, :] = jnp.where(low, 0.0, swapped).astype(BF16)
            kv_ref[0, 1, base + 0, rows, :] = jnp.where(low, swapped, 0.0).astype(BF16)
            kv_ref[0, 1, base + 1, rows, :] = jnp.where(low, 0.0, src).astype(BF16)

        dn_ref[0, rows, :] = p[:, ATT_W:ATT_W + 3 * DN_W]

        gp = p[:, ATT_W + 4 * DN_W:]
        beta = jax.nn.sigmoid(gp)
        xa = gp + dtb_ref[...]
        softplus = jnp.maximum(xa, 0.0) + jnp.log1p(jnp.exp(-jnp.abs(xa)))
        decay = -jnp.exp(alog_ref[...]) * softplus
        gates = jnp.where(lane < N_GATES // 2, beta, jnp.where(lane < N_GATES, decay, 0.0))
        g_ref[0, rows, :] = gates
        gt_ref[0, :, rows] = gates.T[0:N_GATES, :]

    chunks = [slice(r0, r0 + IN_ROW_CHUNK) for r0 in range(0, n_rows, IN_ROW_CHUNK)]
    p_next = project(chunks[0])
    for r, rows in enumerate(chunks):
        p = p_next
        if r + 1 < len(chunks):
            p_next = project(chunks[r + 1])
        finish(rows, p)


def _inproj_call(rows3, ctx_row, mod, n1, w_pad, bd, qkw, alog_l, dtb_l, rope=None):
    latent = rope is not None
    grp, rows, d = rows3.shape
    t = TOK_TILE
    assert rows % t == 0
    nw = w_pad.shape[1]
    full = lambda shape: pl.BlockSpec(shape, lambda b, j: (0,) * len(shape))
    tile = lambda width: pl.BlockSpec((1, t, width), lambda b, j: (b, j, 0))
    in_specs = [tile(d), full((MOD_ROWS, mod.shape[1])), full((1, d)), full((d, nw)), full(bd.shape),
                full((1, QK_W)), full((1, LANES)), full((1, LANES))]
    out_specs = [pl.BlockSpec((1, ATTN_KV_HEADS, 4, t, LANES), lambda b, j: (b, 0, 0, j, 0)),
                 tile(3 * DN_W), tile(LANES), pl.BlockSpec((1, N_GATES, t), lambda b, j: (b, 0, j))]
    out_shape = [jax.ShapeDtypeStruct((grp, ATTN_KV_HEADS, 4, rows, LANES), BF16),
                 jax.ShapeDtypeStruct((grp, rows, 3 * DN_W), F32),
                 jax.ShapeDtypeStruct((grp, rows, LANES), F32),
                 jax.ShapeDtypeStruct((grp, N_GATES, rows), F32)]
    args = [rows3, mod, n1, w_pad, bd, qkw, alog_l, dtb_l]
    if latent:
        in_specs += [pl.BlockSpec((t, LANES), lambda b, j: (j, 0))] * 2
        out_specs += [tile(ATTN_Q_W), tile(DN_W)]
        out_shape += [jax.ShapeDtypeStruct((grp, rows, ATTN_Q_W), BF16), jax.ShapeDtypeStruct((grp, rows, DN_W), F32)]
        args += list(rope)
    return pl.pallas_call(
        functools.partial(_inproj_kernel, latent, ctx_row),
        grid=(grp, rows // t),
        in_specs=in_specs,
        out_specs=out_specs,
        out_shape=out_shape,
        compiler_params=_params("arbitrary", "arbitrary"),
        name="inproj_lat" if latent else "inproj_ctx",
    )(*args)


def _attn_kernel(q_ref, kvc_ref, kvl_ref, o_ref):
    grp = ATTN_HEADS // ATTN_KV_HEADS
    n_lat = kvl_ref.shape[3]

    def scores(h):
        qp = q_ref[0, :, (h // 2) * LANES:(h // 2 + 1) * LANES]
        return (_dot_nt(qp, kvc_ref[0, h // grp, h % 2]),
                _dot_nt(qp, kvl_ref[0, h // grp, h % 2]))

    s_next = scores(0)
    acc = None
    for h in range(ATTN_HEADS):
        s_ctx, s_lat = s_next
        if h + 1 < ATTN_HEADS:
            s_next = scores(h + 1)
        m = jnp.maximum(jnp.max(s_ctx, axis=-1, keepdims=True), jnp.max(s_lat, axis=-1, keepdims=True))
        stages = [(s_ctx, kvc_ref, 0, s_ctx.shape[-1])]
        stages += [(s_lat, kvl_ref, k0, ATTN_KEY_CHUNK) for k0 in range(0, n_lat, ATTN_KEY_CHUNK)]
        o = denom = None
        for s, v_ref, k0, width in stages:
            e = jnp.exp(s[:, k0:k0 + width] - m)
            part = _dot(e.astype(BF16), v_ref[0, h // grp, 2 + h % 2, k0:k0 + width, :])
            part_sum = jnp.sum(e, axis=-1, keepdims=True)
            o = part if o is None else o + part
            denom = part_sum if denom is None else denom + part_sum
        o = o / denom
        if h % 2 == 0:
            acc = o
        else:
            o_ref[0, :, (h // 2) * LANES:(h // 2 + 1) * LANES] = (acc + o).astype(BF16)


def _attn_call(q, kv_ctx, kv_lat, n_ctx):
    bsz, seq, _ = q.shape
    tq = Q_TILE
    assert seq % tq == 0 and seq % ATTN_KEY_CHUNK == 0
    return pl.pallas_call(
        _attn_kernel,
        grid=(bsz, seq // tq),
        in_specs=[pl.BlockSpec((1, tq, ATTN_Q_W), lambda b, i: (b, i, 0)),
                  pl.BlockSpec((1, ATTN_KV_HEADS, 4, n_ctx, LANES), lambda b, i: (0, 0, 0, b, 0)),
                  pl.BlockSpec((1, ATTN_KV_HEADS, 4, seq, LANES), lambda b, i: (b, 0, 0, 0, 0))],
        out_specs=pl.BlockSpec((1, tq, ATTN_Q_W), lambda b, i: (b, i, 0)),
        out_shape=jax.ShapeDtypeStruct((bsz, seq, ATTN_Q_W), BF16),
        compiler_params=_params("arbitrary", "arbitrary"),
        name="attn",
    )(q, kv_ctx, kv_lat)


def _unit_tri_inverses_minus_eye(mats, level_masks):
    strip_masks, last_join, left, right = level_masks
    n = mats[0].shape[0]
    half = n // 2

    def to_strip(m):
        return m[0:half, :] * left + m[half:, :] * right

    def from_strip(s):
        return jnp.concatenate([s * left, s * right], axis=0)

    mats = [a.astype(BF16) for a in mats]
    strips = [to_strip(a) for a in mats]
    corrs = [-(s * strip_masks[0]) for s in strips]
    for joins in strip_masks[1:]:
        xs = [s * joins for s in strips]
        ys = [x.astype(F32) + _dot(corr, from_strip(x)) for corr, x in zip(corrs, xs)]
        corrs = [corr - (y + _dot(y.astype(BF16), from_strip(corr))).astype(BF16) for corr, y in zip(corrs, ys)]
    corrs = [from_strip(corr) for corr in corrs]
    xs = [a * last_join for a in mats]
    ys = [x.astype(F32) + _dot(corr, x) for corr, x in zip(corrs, xs)]
    return [corr - (y + _dot(y.astype(BF16), corr)).astype(BF16) for corr, y in zip(corrs, ys)]


def _tri_level_masks(n):
    ij_xor = lax.broadcasted_iota(I32, (n, n), 0) ^ lax.broadcasted_iota(I32, (n, n), 1)
    half = n // 2
    lane = lax.broadcasted_iota(I32, (1, n), 1)
    left = jnp.where(lane < half, 1.0, 0.0).astype(BF16)
    right = jnp.where(lane >= half, 1.0, 0.0).astype(BF16)
    strip_masks, m = [], 1
    while m < half:
        full = jnp.where((ij_xor >= m) & (ij_xor < 2 * m), 1.0, 0.0).astype(BF16)
        strip_masks.append(full[0:half, :] * left + full[half:, :] * right)
        m *= 2
    last_join = jnp.where((ij_xor >= half) & (ij_xor < n), 1.0, 0.0).astype(BF16)
    return strip_masks, last_join, left, right


def _dn_kernel(dqc_ref, dkc_ref, dvc_ref, dql_ref, dkl_ref, dvl_ref, cq_ref, ck_ref, cv_ref,
               gc_ref, gl_ref, gtc_ref, gtl_ref, o_ref,
               xq_s, xk_s, xv_s, q_s, k_s, v_s, kk_s, qk_s, g_s, gt_s, mq_s, n_s, op_s, gl_s):
    qkv_ctx_refs = (dqc_ref, dkc_ref, dvc_ref)
    qkv_lat_refs = (dql_ref, dkl_ref, dvl_ref)
    conv_refs = (cq_ref, ck_ref, cv_ref)
    x_scr = (xq_s, xk_s, xv_s)
    qkv_scr = (q_s, k_s, v_s)
    c = DN_CHUNK
    n_ctx = gc_ref.shape[1]
    ctx_chunks = n_ctx // c
    lat_chunks = gl_ref.shape[1] // c
    dk = DN_HEAD_DIM
    ii = lax.broadcasted_iota(I32, (c, c), 0)
    jj = lax.broadcasted_iota(I32, (c, c), 1)
    level_masks = _tri_level_masks(c)
    lane = lax.broadcasted_iota(I32, (1, LANES), 1)

    g_s[0:n_ctx, :] = gc_ref[0]
    g_s[n_ctx:, :] = gl_ref[0]
    gt_s[:, 0:ctx_chunks, :] = gtc_ref[0]
    gt_s[:, ctx_chunks:, :] = gtl_ref[0]

    for hh in range(DN_HEADS_PER_STEP):
        _dn_prepare_head(hh, pl.program_id(1) * DN_HEADS_PER_STEP + hh, ii, jj, level_masks, lane,
                         qkv_ctx_refs, qkv_lat_refs, conv_refs, g_s, gt_s,
                         x_scr, qkv_scr, kk_s, qk_s, mq_s, n_s, op_s, gl_s)

    def scan(first_chunk, count, emit, states):
        rows = dk + c if emit else dk

        def step(t, st):
            chains = [(hh, d, first_chunk + t if d == 0 else first_chunk + count - 1 - t)
                      for hh in range(DN_HEADS_PER_STEP) for d in range(2)]
            prods = [_dot(mq_s[hh, d, i, 0:rows, :], s.astype(BF16)) for (hh, d, i), s in zip(chains, st)]
            new = []
            for (hh, d, i), s, r in zip(chains, st, prods):
                if emit:
                    ro = pl.multiple_of((i - ctx_chunks) * c, c)
                    o_ref[0, pl.ds(ro, c), hh * dk:(hh + 1) * dk] += r[dk:dk + c] + op_s[hh, d, i]
                new.append(s * gl_s[hh, d, i][0:1, :] + r[0:dk] + n_s[hh, d, i])
            return tuple(new)
        return lax.fori_loop(0, count, step, states)

    o_ref[...] = jnp.zeros(o_ref.shape, F32)
    zero_state = jnp.zeros((dk, dk), F32)
    states = scan(0, ctx_chunks, False, (zero_state,) * (2 * DN_HEADS_PER_STEP))
    scan(ctx_chunks, lat_chunks, True, states)


def _dn_prepare_head(hh, head, ii, jj, level_masks, lane, qkv_ctx_refs, qkv_lat_refs, conv_refs, g_s, gt_s,
                     x_scr, qkv_scr, kk_s, qk_s, mq_s, n_s, op_s, gl_s):
    c = DN_CHUNK
    n_ctx = qkv_ctx_refs[0].shape[1]
    seq = qkv_lat_refs[0].shape[1]
    n_chunks = (n_ctx + seq) // c
    ctx_chunks = n_ctx // c
    pad = SUBLANES
    dk = DN_HEAD_DIM
    half = DN_CONV_W // 2
    cols = slice(hh * dk, (hh + 1) * dk)
    xq_s, xk_s, xv_s = x_scr
    q_s, k_s, v_s = qkv_scr
    cq_ref, ck_ref, cv_ref = conv_refs

    zeros_pad = jnp.zeros((pad, dk), F32)
    lat0 = 2 * pad + n_ctx
    for src_ctx, src_lat, dst in zip(qkv_ctx_refs, qkv_lat_refs, x_scr):
        dst[0:pad, :] = zeros_pad
        dst[pad:pad + n_ctx, :] = src_ctx[0, :, cols]
        dst[pad + n_ctx:lat0, :] = zeros_pad
        dst[lat0:lat0 + seq, :] = src_lat[0, :, cols]
        dst[lat0 + seq:lat0 + seq + pad, :] = zeros_pad

    def conv_chunk(i, carry):
        r0 = pl.multiple_of(i * c, c)
        rp = r0 + jnp.where(i >= ctx_chunks, 2 * pad, pad)

        def conv(x_s, cw_ref):
            acc = None
            for s in range(DN_CONV_W):
                term = x_s[pl.ds(rp - half + s, c), :] * cw_ref[s:s + 1, cols]
                acc = term if acc is None else acc + term
            return _silu(acc)

        qc = conv(xq_s, cq_ref)
        kc = conv(xk_s, ck_ref)
        vc = conv(xv_s, cv_ref)
        qc = qc * lax.rsqrt(jnp.sum(qc * qc, axis=-1, keepdims=True) + EPS) * (dk ** -0.5)
        kc = kc * lax.rsqrt(jnp.sum(kc * kc, axis=-1, keepdims=True) + EPS)
        q_s[pl.ds(r0, c), :] = qc
        k_s[pl.ds(r0, c), :] = kc
        v_s[pl.ds(r0, c), :] = vc
        kb = kc.astype(BF16)
        kk_s[i] = _dot_nt(kb, kb)
        qk_s[i] = _dot_nt(qc.astype(BF16), kb)
        return carry

    lax.fori_loop(0, n_chunks, conv_chunk, 0, unroll=DN_CONV_UNROLL)

    def local_group(grp, carry):
        chains = []
        for k in range(DN_GROUP):
            i = grp * DN_GROUP + k
            r0 = pl.multiple_of(i * c, c)
            gates = g_s[pl.ds(r0, c), :]
            kc = k_s[pl.ds(r0, c), :]
            qc = q_s[pl.ds(r0, c), :]
            vc = v_s[pl.ds(r0, c), :]
            kk = kk_s[i]
            qk = qk_s[i]
            kt = kc.T
            for d in range(2):
                beta_col = jnp.sum(jnp.where(lane == d * DN_HEADS + head, gates, 0.0), axis=-1, keepdims=True)
                g_col = jnp.sum(jnp.where(lane == N_GATES // 2 + d * DN_HEADS + head, gates, 0.0),
                                axis=-1, keepdims=True)
                g_row = gt_s[pl.ds(N_GATES // 2 + d * DN_HEADS + head, 1), pl.ds(i, 1), :].reshape(1, c)
                incl = (jj <= ii) if d == 0 else (jj >= ii)
                strict = (jj < ii) if d == 0 else (jj > ii)
                incl_t = (ii <= jj) if d == 0 else (ii >= jj)
                gcum_col = jnp.sum(jnp.where(incl, g_row, 0.0), axis=1, keepdims=True)
                gcum_row = jnp.sum(jnp.where(incl_t, g_col, 0.0), axis=0, keepdims=True)
                total = jnp.sum(g_row, axis=1, keepdims=True)
                decay = jnp.exp(jnp.where(incl, gcum_col - gcum_row, -jnp.inf))
                e_col = jnp.exp(gcum_col)
                kdt = (kt * jnp.exp(total - gcum_row)).astype(BF16)
                qkm = (qk * decay).astype(BF16)
                chains.append(dict(
                    d=d, i=i,
                    a=jnp.where(strict, beta_col * kk * decay, 0.0),
                    rhs=jnp.concatenate([vc * beta_col, kc * (beta_col * e_col)], axis=1),
                    lhs=jnp.concatenate([kdt, qkm], axis=0),
                    qd=qc * e_col,
                    g_last=jnp.exp(total)))
        corrs = _unit_tri_inverses_minus_eye([ch["a"] for ch in chains], level_masks)
        sols = [ch["rhs"] + _dot(corr, ch["rhs"].astype(BF16)) for ch, corr in zip(chains, corrs)]
        prods = [_dot(ch["lhs"], sol.astype(BF16)) for ch, sol in zip(chains, sols)]
        for ch, r in zip(chains, prods):
            d, i = ch["d"], ch["i"]
            mq_s[hh, d, i, 0:dk, :] = (-r[0:dk, dk:2 * dk]).astype(BF16)
            mq_s[hh, d, i, dk:dk + c, :] = (ch["qd"] - r[dk:dk + c, dk:2 * dk]).astype(BF16)
            n_s[hh, d, i] = r[0:dk, 0:dk]
            op_s[hh, d, i] = r[dk:dk + c, 0:dk]
            gl_s[hh, d, i] = jnp.broadcast_to(ch["g_last"], (SUBLANES, dk))
        return carry

    lax.fori_loop(0, n_chunks // DN_GROUP, local_group, 0)


def _dn_call(dn_ctx, dn_lat, conv_w, g_ctx, g_lat, gt_ctx, gt_lat, n_ctx):
    bsz, seq, _ = dn_lat.shape
    tot = n_ctx + seq
    c = DN_CHUNK
    assert n_ctx % c == 0 and seq % c == 0
    n_chunks = tot // c
    ctx_chunks = n_ctx // c
    dk = DN_HEAD_DIM
    gtc4 = gt_ctx.reshape(N_GATES, bsz, ctx_chunks, c).transpose(1, 0, 2, 3)
    gtl4 = gt_lat.reshape(bsz, N_GATES, seq // c, c)
    hp = DN_HEADS_PER_STEP
    steps = DN_HEADS // hp
    col_ctx = lambda off: pl.BlockSpec((1, n_ctx, hp * dk), lambda b, h: (0, b, off + h))
    col_lat = lambda off: pl.BlockSpec((1, seq, hp * dk), lambda b, h: (b, 0, off + h))
    cw = lambda off: pl.BlockSpec((DN_CONV_W, hp * dk), lambda b, h: (0, off + h))
    padded = tot + 3 * SUBLANES
    assert n_chunks % DN_GROUP == 0 and n_chunks % DN_CONV_UNROLL == 0 and DN_HEADS % hp == 0
    return pl.pallas_call(
        _dn_kernel,
        grid=(bsz, steps),
        in_specs=[col_ctx(0), col_ctx(steps), col_ctx(2 * steps),
                  col_lat(0), col_lat(steps), col_lat(2 * steps),
                  cw(0), cw(steps), cw(2 * steps),
                  pl.BlockSpec((1, n_ctx, LANES), lambda b, h: (0, b, 0)),
                  pl.BlockSpec((1, seq, LANES), lambda b, h: (b, 0, 0)),
                  pl.BlockSpec((1, N_GATES, ctx_chunks, c), lambda b, h: (b, 0, 0, 0)),
                  pl.BlockSpec((1, N_GATES, seq // c, c), lambda b, h: (b, 0, 0, 0))],
        out_specs=pl.BlockSpec((1, seq, hp * dk), lambda b, h: (b, 0, h)),
        out_shape=jax.ShapeDtypeStruct((bsz, seq, DN_W), F32),
        scratch_shapes=[pltpu.VMEM((padded, dk), F32)] * 3
        + [pltpu.VMEM((tot, dk), F32)] * 3
        + [pltpu.VMEM((n_chunks, c, c), F32)] * 2
        + [pltpu.VMEM((tot, LANES), F32),
           pltpu.VMEM((N_GATES, n_chunks, c), F32)]
        + [pltpu.VMEM((hp, 2, n_chunks, dk + c, dk), BF16),
           pltpu.VMEM((hp, 2, n_chunks, dk, dk), F32),
           pltpu.VMEM((hp, 2, n_chunks, c, dk), F32),
           pltpu.VMEM((hp, 2, n_chunks, SUBLANES, dk), F32)],
        compiler_params=_params("arbitrary", "arbitrary"),
        name="dn",
    )(dn_ctx, dn_ctx, dn_ctx, dn_lat, dn_lat, dn_lat, conv_w, conv_w, conv_w, g_ctx, g_lat, gtc4, gtl4)


def _outproj_kernel(x_ref, att_ref, o_ref, z_ref, mod_ref, onw_ref, wo_ref, n2_ref, rw_ref,
                    x1_ref, h2_ref, afft_ref):
    d = x_ref.shape[-1]
    b = pl.program_id(0)
    gate1 = mod_ref[pl.ds(b, 1), 2 * d:3 * d]
    shift2 = mod_ref[pl.ds(b, 1), 3 * d:4 * d]
    scale2 = mod_ref[pl.ds(b, 1), 4 * d:5 * d]
    lane = lax.broadcasted_iota(I32, (1, LANES), 1)
    n_rows = x_ref.shape[1]

    def mixed(rows):
        parts = [att_ref[0, rows, :]]
        for h in range(DN_HEADS):
            sl = slice(h * DN_HEAD_DIM, (h + 1) * DN_HEAD_DIM)
            oh = o_ref[0, rows, sl]
            on = oh * lax.rsqrt(jnp.mean(oh * oh, axis=-1, keepdims=True) + EPS) * onw_ref[...]
            parts.append((on * _silu(z_ref[0, rows, sl])).astype(BF16))
        return _dot(jnp.concatenate(parts, axis=1), wo_ref[...])

    def finish(rows, y):
        x1 = x_ref[0, rows, :] + gate1 * y
        x1_ref[0, rows, :] = x1
        hn = x1 * lax.rsqrt(jnp.mean(x1 * x1, axis=-1, keepdims=True) + EPS) * n2_ref[...]
        h2 = (hn * (1.0 + scale2) + shift2).astype(BF16)
        h2_ref[0, rows, :] = h2
        logits = _dot(h2, rw_ref[...])
        logits = jnp.where(lane < N_EXPERTS, logits, -jnp.inf)
        e = jnp.exp(logits - jnp.max(logits, axis=-1, keepdims=True))
        aff = e / jnp.sum(e, axis=-1, keepdims=True)
        afft_ref[0, :, rows] = aff.T[0:N_EXPERTS, :]

    chunks = [slice(r0, r0 + OUT_ROW_CHUNK) for r0 in range(0, n_rows, OUT_ROW_CHUNK)]
    y_next = mixed(chunks[0])
    for r, rows in enumerate(chunks):
        y = y_next
        if r + 1 < len(chunks):
            y_next = mixed(chunks[r + 1])
        finish(rows, y)


def _outproj_call(x, att, o_dn, z, mod, onw, wo, n2, rw):
    bsz, seq, d = x.shape
    t = OUT_TILE
    assert seq % t == 0
    full = lambda shape: pl.BlockSpec(shape, lambda b, j: (0,) * len(shape))
    return pl.pallas_call(
        _outproj_kernel,
        grid=(bsz, seq // t),
        in_specs=[pl.BlockSpec((1, t, d), lambda b, j: (b, j, 0)),
                  pl.BlockSpec((1, t, ATTN_Q_W), lambda b, j: (b, j, 0)),
                  pl.BlockSpec((1, t, DN_W), lambda b, j: (b, j, 0)),
                  pl.BlockSpec((1, t, DN_W), lambda b, j: (b, j, 0)),
                  full((MOD_ROWS, mod.shape[1])),
                  full((1, DN_HEAD_DIM)),
                  full(wo.shape),
                  full((1, d)),
                  full((d, LANES))],
        out_specs=[pl.BlockSpec((1, t, d), lambda b, j: (b, j, 0)),
                   pl.BlockSpec((1, t, d), lambda b, j: (b, j, 0)),
                   pl.BlockSpec((1, N_EXPERTS, t), lambda b, j: (b, 0, j))],
        out_shape=[jax.ShapeDtypeStruct((bsz, seq, d), F32),
                   jax.ShapeDtypeStruct((bsz, seq, d), BF16),
                   jax.ShapeDtypeStruct((bsz, N_EXPERTS, seq), F32)],
        compiler_params=_params("arbitrary", "arbitrary"),
        name="outproj",
    )(x, att, o_dn, z, mod, onw, wo, n2, rw)


def _route_kernel(cap, afft_ref, slot_ref, gate_ref, tok_ref, bounds_ref):
    aff = afft_ref[0]
    n_e, n_t = aff.shape

    def enough(cand):
        return jnp.sum(jnp.where(aff >= cand, 1.0, 0.0), axis=-1, keepdims=True) >= cap

    tiny = 2.0 ** F32_MIN_EXP
    cur = jnp.full((n_e, 1), tiny, F32)
    any_normal = enough(cur)
    shift = 1 << (-F32_MIN_EXP).bit_length()
    while shift > 1:
        shift //= 2
        cand = cur * (2.0 ** shift)
        cur = jnp.where(enough(cand), cand, cur)

    def refine(_, state):
        cur, step = state
        cand = cur + step
        return jnp.where(enough(cand), cand, cur), step * 0.5

    cur, _ = lax.fori_loop(0, F32_MANTISSA_BITS, refine, (cur, cur * 0.5))
    thr = jnp.where(any_normal, cur, 0.0)
    need = cap - jnp.sum(jnp.where(aff > thr, 1.0, 0.0), axis=-1, keepdims=True)

    upper = (lax.broadcasted_iota(I32, (LANES, LANES), 0) < lax.broadcasted_iota(I32, (LANES, LANES), 1))
    upper = jnp.where(upper, 1.0, 0.0).astype(BF16)
    run = jnp.zeros((2 * n_e, 1), F32)
    lane = lax.broadcasted_iota(I32, (1, LANES), 1)
    bounds = jnp.zeros((n_e, LANES), F32)
    per_block = ROUTE_BLOCK // LANES
    for blk in range(n_t // LANES):
        sl = slice(blk * LANES, (blk + 1) * LANES)
        gt = aff[:, sl] > thr
        eq = aff[:, sl] == thr
        x = jnp.concatenate([jnp.where(gt, 1.0, 0.0), jnp.where(eq, 1.0, 0.0)], axis=0)
        cum = _dot(x.astype(BF16), upper) + run
        run = run + jnp.sum(x, axis=-1, keepdims=True)
        cum_gt = cum[0:n_e]
        cum_eq = cum[n_e:2 * n_e]
        sel = gt | (eq & (cum_eq < need))
        slot = jnp.where(sel, cum_gt + jnp.minimum(cum_eq, need), -1.0)
        slot_ref[0, :, sl] = slot.astype(I32)
        gate_ref[0, :, sl] = jnp.where(sel, aff[:, sl], 0.0)
        if (blk + 1) % per_block == 0:
            taken = run[0:n_e] + jnp.minimum(run[n_e:2 * n_e], need)
            bounds = jnp.where(lane == (blk + 1) // per_block, taken, bounds)
    bounds_ref[0] = bounds.astype(I32)

    stacked = jnp.concatenate([slot_ref[0].astype(F32), gate_ref[0],
                               jnp.zeros((LANES - 2 * n_e, n_t), F32)], axis=0)
    tok_ref[0] = stacked.T


def _route_call(afft, cap):
    bsz, n_e, n_t = afft.shape
    row = pl.BlockSpec((1, n_e, n_t), lambda b: (b, 0, 0))
    return pl.pallas_call(
        functools.partial(_route_kernel, cap),
        grid=(bsz,),
        in_specs=[row],
        out_specs=[row, row, pl.BlockSpec((1, n_t, LANES), lambda b: (b, 0, 0)),
                   pl.BlockSpec((1, n_e, LANES), lambda b: (b, 0, 0))],
        out_shape=[jax.ShapeDtypeStruct((bsz, n_e, n_t), I32),
                   jax.ShapeDtypeStruct((bsz, n_e, n_t), F32),
                   jax.ShapeDtypeStruct((bsz, n_t, LANES), F32),
                   jax.ShapeDtypeStruct((bsz, n_e, LANES), I32)],
        compiler_params=_params("arbitrary"),
        name="route",
    )(afft)


def _window_starts(cnt_ref, b, tb, n_e, cap):
    starts, fits = [], None
    for e in range(n_e):
        lo = cnt_ref[b, tb * n_e + e]
        hi = cnt_ref[b, (tb + 1) * n_e + e]
        start = jnp.minimum((lo // BF16_ROWS) * BF16_ROWS, cap - SLOT_WINDOW)
        ok = hi - start <= SLOT_WINDOW
        starts.append(pl.multiple_of(start, BF16_ROWS))
        fits = ok if fits is None else jnp.logical_and(fits, ok)
    return starts, fits


def _gather_kernel(cap, cnt_ref, slot_ref, gate_ref, h_ref, xg_ref, gs_ref):
    b = pl.program_id(0)
    n_e = slot_ref.shape[1]
    n_blk = slot_ref.shape[2]
    xg_ref[...] = jnp.zeros(xg_ref.shape, BF16)
    gs_ref[...] = jnp.zeros(gs_ref.shape, F32)

    def token_block(tb, carry):
        h_blk = h_ref[0, pl.ds(pl.multiple_of(tb * ROUTE_BLOCK, ROUTE_BLOCK), ROUTE_BLOCK), :]
        starts, fits = _window_starts(cnt_ref, b, tb, n_e, cap)

        def accumulate(window, first_rows):
            j = lax.broadcasted_iota(I32, (window, 1), 0)
            hits = [slot_ref[0, e, pl.ds(tb, 1), :] == first_rows[e] + j for e in range(n_e)]
            onehot = jnp.concatenate([jnp.where(hit, 1.0, 0.0).astype(BF16) for hit in hits], axis=0)
            rows = _dot(onehot, h_blk)
            for e in range(n_e):
                dst = pl.ds(first_rows[e], window)
                xg_ref[e, 0, dst, :] += rows[e * window:(e + 1) * window].astype(BF16)
                gate = gate_ref[0, e, pl.ds(tb, 1), :]
                gs_ref[e, 0, dst, :] += jnp.sum(jnp.where(hits[e], gate, 0.0), axis=-1, keepdims=True)

        @pl.when(fits)
        def _():
            accumulate(SLOT_WINDOW, starts)

        @pl.when(jnp.logical_not(fits))
        def _():
            accumulate(cap, [0] * n_e)

        return carry

    lax.fori_loop(0, n_blk, token_block, 0)


def _gather_call(cnt, slot, gate, h2, cap):
    bsz, n_e, n_t = slot.shape
    d = h2.shape[-1]
    n_blk = n_t // ROUTE_BLOCK
    blocked = lambda a: a.reshape(bsz, n_e, n_blk, ROUTE_BLOCK)
    row = pl.BlockSpec((1, n_e, n_blk, ROUTE_BLOCK), lambda b, cnt: (b, 0, 0, 0))
    return pl.pallas_call(
        functools.partial(_gather_kernel, cap),
        grid_spec=pltpu.PrefetchScalarGridSpec(
            num_scalar_prefetch=1,
            grid=(bsz,),
            in_specs=[row, row, pl.BlockSpec((1, n_t, d), lambda b, cnt: (b, 0, 0))],
            out_specs=[pl.BlockSpec((n_e, 1, cap, d), lambda b, cnt: (0, b, 0, 0)),
                       pl.BlockSpec((n_e, 1, cap, 1), lambda b, cnt: (0, b, 0, 0))]),
        out_shape=[jax.ShapeDtypeStruct((n_e, bsz, cap, d), BF16),
                   jax.ShapeDtypeStruct((n_e, bsz, cap, 1), F32)],
        compiler_params=_params("arbitrary"),
        name="gather",
    )(cnt, blocked(slot), blocked(gate), h2)


def _ffn_kernel(xg_ref, gs_ref, wg_ref, wu_ref, wd_ref, y_ref, acc_ref):
    bsz, cap, d = xg_ref.shape[1:]
    per = FFN_ROW_CHUNK // cap
    n_chunks = bsz // per

    @pl.when(pl.program_id(1) == 0)
    def _():
        acc_ref[...] = jnp.zeros(acc_ref.shape, F32)

    wg = wg_ref[0].astype(BF16)
    wu = wu_ref[0].astype(BF16)
    wd = wd_ref[0].astype(BF16)

    def up(r):
        x = xg_ref[0, r * per:(r + 1) * per].reshape(per * cap, d)
        return _dot(x, wg), _dot(x, wu)

    nxt = up(0)
    for r in range(n_chunks):
        g, u = nxt
        if r + 1 < n_chunks:
            nxt = up(r + 1)
        rows = slice(r * per * cap, (r + 1) * per * cap)
        total = acc_ref[rows, :] + _dot((_silu(g) * u).astype(BF16), wd)
        acc_ref[rows, :] = total
        y = total * gs_ref[0, r * per:(r + 1) * per].reshape(per * cap, 1)
        y_ref[0, r * per:(r + 1) * per] = y.astype(BF16).reshape(per, cap, d)


def _ffn_call(xg, gs, w_gate, w_up, w_down):
    n_e, bsz, cap, d = xg.shape
    ff = w_gate.shape[-1]
    tf = FF_TILE
    return pl.pallas_call(
        _ffn_kernel,
        grid=(n_e, ff // tf),
        in_specs=[pl.BlockSpec((1, bsz, cap, d), lambda e, f: (e, 0, 0, 0)),
                  pl.BlockSpec((1, bsz, cap, 1), lambda e, f: (e, 0, 0, 0)),
                  pl.BlockSpec((1, d, tf), lambda e, f: (e, 0, f)),
                  pl.BlockSpec((1, d, tf), lambda e, f: (e, 0, f)),
                  pl.BlockSpec((1, tf, d), lambda e, f: (e, f, 0))],
        out_specs=pl.BlockSpec((1, bsz, cap, d), lambda e, f: (e, 0, 0, 0)),
        out_shape=jax.ShapeDtypeStruct((n_e, bsz, cap, d), BF16),
        scratch_shapes=[pltpu.VMEM((bsz * cap, d), F32)],
        compiler_params=_params("arbitrary", "arbitrary"),
        name="ffn",
    )(xg, gs, w_gate, w_up, w_down)


def _combine_kernel(cap, cnt_ref, x1_ref, tok_ref, y_ref, mod_ref, o_ref):
    d = x1_ref.shape[-1]
    n_e = y_ref.shape[0]
    b = pl.program_id(0)
    tb = pl.program_id(1)
    gate2 = mod_ref[pl.ds(b, 1), 5 * d:6 * d]
    tok = tok_ref[0]
    starts, fits = _window_starts(cnt_ref, b, tb, n_e, cap)

    @pl.when(fits)
    def _():
        lane = lax.broadcasted_iota(I32, (1, LANES), 1)
        first = lane < SLOT_WINDOW
        groups = []
        for e in range(0, n_e, 2):
            slot = jnp.where(first, tok[:, e:e + 1], tok[:, e + 1:e + 2])
            target = jnp.where(first, starts[e] + lane, starts[e + 1] + lane - SLOT_WINDOW).astype(F32)
            groups.append(jnp.where(slot == target, 1.0, 0.0).astype(BF16))
        onehot = jnp.concatenate(groups, axis=1)
        rows = jnp.concatenate([y_ref[e, 0, pl.ds(starts[e], SLOT_WINDOW), :] for e in range(n_e)], axis=0)
        o_ref[0] = x1_ref[0] + gate2 * _dot(onehot, rows)

    @pl.when(jnp.logical_not(fits))
    def _():
        j = lax.broadcasted_iota(I32, (1, cap), 1).astype(F32)
        acc = None
        for e in range(n_e):
            onehot = jnp.where(tok[:, e:e + 1] == j, 1.0, 0.0).astype(BF16)
            part = _dot(onehot, y_ref[e, 0])
            acc = part if acc is None else acc + part
        o_ref[0] = x1_ref[0] + gate2 * acc


def _combine_call(cnt, x1, tok, y, mod, cap):
    bsz, seq, d = x1.shape
    n_e = y.shape[0]
    t = ROUTE_BLOCK
    assert 2 * SLOT_WINDOW == LANES and n_e % 2 == 0
    return pl.pallas_call(
        functools.partial(_combine_kernel, cap),
        grid_spec=pltpu.PrefetchScalarGridSpec(
            num_scalar_prefetch=1,
            grid=(bsz, seq // t),
            in_specs=[pl.BlockSpec((1, t, d), lambda b, j, cnt: (b, j, 0)),
                      pl.BlockSpec((1, t, LANES), lambda b, j, cnt: (b, j, 0)),
                      pl.BlockSpec((n_e, 1, cap, d), lambda b, j, cnt: (0, b, 0, 0)),
                      pl.BlockSpec((MOD_ROWS, mod.shape[1]), lambda b, j, cnt: (0, 0))],
            out_specs=pl.BlockSpec((1, t, d), lambda b, j, cnt: (b, j, 0))),
        out_shape=jax.ShapeDtypeStruct((bsz, seq, d), F32),
        compiler_params=_params("arbitrary", "arbitrary"),
        name="combine",
    )(cnt, x1, tok, y, mod)


def _rope_tables(seq):
    m = ATTN_HEAD_DIM // 4
    pos = jnp.arange(seq, dtype=jnp.int32)
    rows = (pos // GRID_W).astype(F32)
    cols = (pos % GRID_W).astype(F32)
    freqs = ROPE_BASE ** (-jnp.arange(m, dtype=F32) / m)
    ang_r = rows[:, None] * freqs[None, :]
    ang_c = cols[:, None] * freqs[None, :]
    cos_h = jnp.concatenate([jnp.cos(ang_r), jnp.cos(ang_r), jnp.cos(ang_c), jnp.cos(ang_c)], axis=-1)
    sin_h = jnp.concatenate([-jnp.sin(ang_r), jnp.sin(ang_r), -jnp.sin(ang_c), jnp.sin(ang_c)], axis=-1)
    reps = LANES // ATTN_HEAD_DIM
    return jnp.tile(cos_h, (1, reps)), jnp.tile(sin_h, (1, reps))


def _lane_row(values, offset):
    return jnp.zeros((1, LANES), F32).at[0, offset:offset + values.shape[0]].set(values.astype(F32))


def kernel(x, c, ctx, c_ctx, w_mod, b_mod, norm1_w, norm2_w, w_in, q_norm_w, k_norm_w, conv_w, a_log, dt_bias,
           o_norm_w, w_out, router_w, w_gate, w_up, w_down):
    bsz, seq, d = x.shape
    n_ctx = ctx.shape[1]
    assert w_mod.shape[0] == 1, "single layer: the last layer's context outputs are never consumed"
    assert bsz < MOD_ROWS and N_EXPERTS == router_w.shape[-1]
    cap = EC_CAPACITY_FACTOR * seq // N_EXPERTS

    cc = jnp.concatenate([c, c_ctx[None, :], jnp.zeros((MOD_ROWS - bsz - 1, d), F32)], axis=0)
    mod = _mod_call(cc, w_mod[0], b_mod[0][None, :])

    w_pad = jnp.pad(w_in[0], ((0, 0), (0, LANES - N_GATES))).astype(BF16)
    seg = np.arange(MXU_DIM) // ATTN_HEAD_DIM
    bd = jnp.asarray(seg[:, None] == seg[None, :], BF16)
    qkw = jnp.concatenate([jnp.tile(q_norm_w[0], ATTN_HEADS), jnp.tile(k_norm_w[0], ATTN_KV_HEADS)])[None, :]
    alog_l = _lane_row(a_log[0].reshape(-1), N_GATES // 2)
    dtb_l = _lane_row(dt_bias[0].reshape(-1), N_GATES // 2)
    shared = (mod, norm1_w[0][None, :], w_pad, bd, qkw, alog_l, dtb_l)
    kv_lat, dn_lat, g_lat, gt_lat, q, z = _inproj_call(x, bsz, *shared, rope=_rope_tables(seq))
    kv_ctx, dn_ctx, g_ctx, gt_ctx = _inproj_call(ctx.reshape(1, bsz * n_ctx, d), bsz, *shared)

    att = _attn_call(q, kv_ctx, kv_lat, n_ctx)
    o_dn = _dn_call(dn_ctx, dn_lat, conv_w[0], g_ctx, g_lat, gt_ctx, gt_lat, n_ctx)

    rw = jnp.pad(router_w[0], ((0, 0), (0, LANES - N_EXPERTS))).astype(BF16)
    x1, h2, afft = _outproj_call(x, att, o_dn, z, mod, o_norm_w[0][None, :], w_out[0].astype(BF16),
                                 norm2_w[0][None, :], rw)

    slot, gate, tok, bounds = _route_call(afft, cap)
    n_blk = seq // ROUTE_BLOCK
    cnt = bounds[:, :, 0:n_blk + 1].transpose(0, 2, 1).reshape(bsz, (n_blk + 1) * N_EXPERTS)
    xg, gs = _gather_call(cnt, slot, gate, h2, cap)
    y = _ffn_call(xg, gs, w_gate[0], w_up[0], w_down[0])
    return _combine_call(cnt, x1, tok, y, mod, cap)
```

```python
import functools
import math

import numpy as np
import jax
import jax.numpy as jnp
from jax import lax
from jax.experimental import pallas as pl
from jax.experimental.pallas import tpu as pltpu

F32 = jnp.float32
BF16 = jnp.bfloat16
I32 = jnp.int32

GRID_W = 64
EPS = 1e-6
ATTN_HEADS = 8
ATTN_KV_HEADS = 2
ATTN_HEAD_DIM = 64
ROPE_BASE = 10000.0
DN_HEADS = 4
DN_HEAD_DIM = 128
DN_CONV_W = 5
N_EXPERTS = 16
EC_CAPACITY_FACTOR = 2

ATTN_Q_W = ATTN_HEADS * ATTN_HEAD_DIM
ATTN_KV_W = ATTN_KV_HEADS * ATTN_HEAD_DIM
DN_W = DN_HEADS * DN_HEAD_DIM
QK_W = ATTN_Q_W + ATTN_KV_W
ATT_W = ATTN_Q_W + 2 * ATTN_KV_W
N_GATES = 4 * DN_HEADS

LANES = 128
SUBLANES = 8
MXU_DIM = 256
VMEM_LIMIT = 56 * 1024 * 1024
F32_MIN_EXP = -126
F32_MANTISSA_BITS = 23

TOK_TILE = 512
IN_ROW_CHUNK = 128
Q_TILE = 512
ATTN_KEY_CHUNK = 1024
OUT_TILE = 512
OUT_ROW_CHUNK = 128
DN_CHUNK = 128
DN_GROUP = 9
DN_CONV_UNROLL = 6
DN_HEADS_PER_STEP = 2
ROUTE_BLOCK = 256
SLOT_WINDOW = 64
BF16_ROWS = 16
FF_TILE = 512
FFN_ROW_CHUNK = 512
MOD_ROWS = 16


def _params(*sem):
    return pltpu.CompilerParams(dimension_semantics=sem, vmem_limit_bytes=VMEM_LIMIT)


def _silu(v):
    half = 0.5 * v
    return half + half * jnp.tanh(half)


def _dot(a, b):
    return jnp.dot(a, b, preferred_element_type=F32)


def _dot_nt(a, b):
    return lax.dot_general(a, b, (((1,), (1,)), ((), ())), preferred_element_type=F32)


def _mod_kernel(c_ref, w_ref, b_ref, o_ref):
    sc = _silu(c_ref[...]).astype(BF16)
    o_ref[...] = _dot(sc, w_ref[...].astype(BF16)) + b_ref[...]


def _mod_call(cc, w_mod, b_mod):
    d, n = w_mod.shape
    return pl.pallas_call(
        _mod_kernel,
        grid=(n // d,),
        in_specs=[pl.BlockSpec((MOD_ROWS, d), lambda i: (0, 0)),
                  pl.BlockSpec((d, d), lambda i: (0, i)),
                  pl.BlockSpec((1, d), lambda i: (0, i))],
        out_specs=pl.BlockSpec((MOD_ROWS, d), lambda i: (0, i)),
        out_shape=jax.ShapeDtypeStruct((MOD_ROWS, n), F32),
        compiler_params=_params("arbitrary"),
        name="mod",
    )(cc, w_mod, b_mod)


def _inproj_kernel(latent, ctx_row, x_ref, mod_ref, n1_ref, w_ref, bd_ref, qkw_ref, alog_ref, dtb_ref, *refs):
    if latent:
        cos_ref, sin_ref, kv_ref, dn_ref, g_ref, gt_ref, q_ref, z_ref = refs
    else:
        kv_ref, dn_ref, g_ref, gt_ref = refs
    d = x_ref.shape[-1]
    n_rows = x_ref.shape[1]
    row = pl.program_id(0) if latent else ctx_row
    shift = mod_ref[pl.ds(row, 1), 0:d]
    scale = mod_ref[pl.ds(row, 1), d:2 * d]
    lane = lax.broadcasted_iota(I32, (1, LANES), 1)
    k_blk = ATTN_Q_W // LANES

    def project(rows):
        xin = x_ref[0, rows, :]
        hn = xin * lax.rsqrt(jnp.mean(xin * xin, axis=-1, keepdims=True) + EPS) * n1_ref[...]
        h = (hn * (1.0 + scale) + shift).astype(BF16)
        return _dot(h, w_ref[...])

    def finish(rows, p):
        qk = p[:, 0:QK_W]
        sq = qk * qk
        hi = sq.astype(BF16)
        lo = (sq - hi.astype(F32)).astype(BF16)
        seg_w = bd_ref.shape[0]
        ms = []
        for c0 in range(0, QK_W, seg_w):
            w = min(seg_w, QK_W - c0)
            ones = bd_ref[0:w, 0:w]
            ms.append(_dot(hi[:, c0:c0 + w], ones) + _dot(lo[:, c0:c0 + w], ones))
        ms = jnp.concatenate(ms, axis=1) * (1.0 / ATTN_HEAD_DIM)
        qkn = qk * lax.rsqrt(ms + EPS) * qkw_ref[...]

        if latent:
            first_half = (lane % 32) < 16
            cos = cos_ref[rows, :]
            sin = sin_ref[rows, :]
            rot = []
            for i in range(QK_W // LANES):
                blk = qkn[:, i * LANES:(i + 1) * LANES]
                partner = jnp.where(first_half, pltpu.roll(blk, LANES - 16, axis=1), pltpu.roll(blk, 16, axis=1))
                rot.append(blk * cos + partner * sin)
            for i in range(k_blk):
                q_ref[0, rows, i * LANES:(i + 1) * LANES] = (rot[i] * (ATTN_HEAD_DIM ** -0.5)).astype(BF16)
            k2 = rot[k_blk]
            z_ref[0, rows, :] = p[:, ATT_W + 3 * DN_W:ATT_W + 4 * DN_W]
        else:
            k2 = qkn[:, k_blk * LANES:(k_blk + 1) * LANES]

        low = lane < ATTN_HEAD_DIM
        v2 = p[:, QK_W:ATT_W]
        for src, base in ((k2, 0), (v2, 2)):
            swapped = pltpu.roll(src, ATTN_HEAD_DIM, axis=1)
            kv_ref[0, 0, base + 0, rows, :] = jnp.where(low, src, 0.0).astype(BF16)
            kv_ref[0, 0, base + 1, rows, :] = jnp.where(low, 0.0, swapped).astype(BF16)
            kv_ref[0, 1, base + 0, rows, :] = jnp.where(low, swapped, 0.0).astype(BF16)
            kv_ref[0, 1, base + 1, rows, :] = jnp.where(low, 0.0, src).astype(BF16)

        dn_ref[0, rows, :] = p[:, ATT_W:ATT_W + 3 * DN_W]

        gp = p[:, ATT_W + 4 * DN_W:]
        beta = jax.nn.sigmoid(gp)
        xa = gp + dtb_ref[...]
        softplus = jnp.maximum(xa, 0.0) + jnp.log1p(jnp.exp(-jnp.abs(xa)))
        decay = -jnp.exp(alog_ref[...]) * softplus
        gates = jnp.where(lane < N_GATES // 2, beta, jnp.where(lane < N_GATES, decay, 0.0))
        g_ref[0, rows, :] = gates
        gt_ref[0, :, rows] = gates.T[0:N_GATES, :]

    chunks = [slice(r0, r0 + IN_ROW_CHUNK) for r0 in range(0, n_rows, IN_ROW_CHUNK)]
    p_next = project(chunks[0])
    for r, rows in enumerate(chunks):
        p = p_next
        if r + 1 < len(chunks):
            p_next = project(chunks[r + 1])
        finish(rows, p)


def _inproj_call(rows3, ctx_row, mod, n1, w_pad, bd, qkw, alog_l, dtb_l, rope=None):
    latent = rope is not None
    grp, rows, d = rows3.shape
    t = TOK_TILE
    assert rows % t == 0
    nw = w_pad.shape[1]
    full = lambda shape: pl.BlockSpec(shape, lambda b, j: (0,) * len(shape))
    tile = lambda width: pl.BlockSpec((1, t, width), lambda b, j: (b, j, 0))
    in_specs = [tile(d), full((MOD_ROWS, mod.shape[1])), full((1, d)), full((d, nw)), full(bd.shape),
                full((1, QK_W)), full((1, LANES)), full((1, LANES))]
    out_specs = [pl.BlockSpec((1, ATTN_KV_HEADS, 4, t, LANES), lambda b, j: (b, 0, 0, j, 0)),
                 tile(3 * DN_W), tile(LANES), pl.BlockSpec((1, N_GATES, t), lambda b, j: (b, 0, j))]
    out_shape = [jax.ShapeDtypeStruct((grp, ATTN_KV_HEADS, 4, rows, LANES), BF16),
                 jax.ShapeDtypeStruct((grp, rows, 3 * DN_W), F32),
                 jax.ShapeDtypeStruct((grp, rows, LANES), F32),
                 jax.ShapeDtypeStruct((grp, N_GATES, rows), F32)]
    args = [rows3, mod, n1, w_pad, bd, qkw, alog_l, dtb_l]
    if latent:
        in_specs += [pl.BlockSpec((t, LANES), lambda b, j: (j, 0))] * 2
        out_specs += [tile(ATTN_Q_W), tile(DN_W)]
        out_shape += [jax.ShapeDtypeStruct((grp, rows, ATTN_Q_W), BF16), jax.ShapeDtypeStruct((grp, rows, DN_W), F32)]
        args += list(rope)
    return pl.pallas_call(
        functools.partial(_inproj_kernel, latent, ctx_row),
        grid=(grp, rows // t),
        in_specs=in_specs,
        out_specs=out_specs,
        out_shape=out_shape,
        compiler_params=_params("arbitrary", "arbitrary"),
        name="inproj_lat" if latent else "inproj_ctx",
    )(*args)


def _attn_kernel(q_ref, kvc_ref, kvl_ref, o_ref):
    grp = ATTN_HEADS // ATTN_KV_HEADS
    n_lat = kvl_ref.shape[3]

    def scores(h):
        qp = q_ref[0, :, (h // 2) * LANES:(h // 2 + 1) * LANES]
        return (_dot_nt(qp, kvc_ref[0, h // grp, h % 2]),
                _dot_nt(qp, kvl_ref[0, h // grp, h % 2]))

    s_next = scores(0)
    acc = None
    for h in range(ATTN_HEADS):
        s_ctx, s_lat = s_next
        if h + 1 < ATTN_HEADS:
            s_next = scores(h + 1)
        m = jnp.maximum(jnp.max(s_ctx, axis=-1, keepdims=True), jnp.max(s_lat, axis=-1, keepdims=True))
        stages = [(s_ctx, kvc_ref, 0, s_ctx.shape[-1])]
        stages += [(s_lat, kvl_ref, k0, ATTN_KEY_CHUNK) for k0 in range(0, n_lat, ATTN_KEY_CHUNK)]
        o = denom = None
        for s, v_ref, k0, width in stages:
            e = jnp.exp(s[:, k0:k0 + width] - m)
            part = _dot(e.astype(BF16), v_ref[0, h // grp, 2 + h % 2, k0:k0 + width, :])
            part_sum = jnp.sum(e, axis=-1, keepdims=True)
            o = part if o is None else o + part
            denom = part_sum if denom is None else denom + part_sum
        o = o / denom
        if h % 2 == 0:
            acc = o
        else:
            o_ref[0, :, (h // 2) * LANES:(h // 2 + 1) * LANES] = (acc + o).astype(BF16)


def _attn_call(q, kv_ctx, kv_lat, n_ctx):
    bsz, seq, _ = q.shape
    tq = Q_TILE
    assert seq % tq == 0 and seq % ATTN_KEY_CHUNK == 0
    return pl.pallas_call(
        _attn_kernel,
        grid=(bsz, seq // tq),
        in_specs=[pl.BlockSpec((1, tq, ATTN_Q_W), lambda b, i: (b, i, 0)),
                  pl.BlockSpec((1, ATTN_KV_HEADS, 4, n_ctx, LANES), lambda b, i: (0, 0, 0, b, 0)),
                  pl.BlockSpec((1, ATTN_KV_HEADS, 4, seq, LANES), lambda b, i: (b, 0, 0, 0, 0))],
        out_specs=pl.BlockSpec((1, tq, ATTN_Q_W), lambda b, i: (b, i, 0)),
        out_shape=jax.ShapeDtypeStruct((bsz, seq, ATTN_Q_W), BF16),
        compiler_params=_params("arbitrary", "arbitrary"),
        name="attn",
    )(q, kv_ctx, kv_lat)


def _unit_tri_inverses_minus_eye(mats, level_masks):
    strip_masks, last_join, left, right = level_masks
    n = mats[0].shape[0]
    half = n // 2

    def to_strip(m):
        return m[0:half, :] * left + m[half:, :] * right

    def from_strip(s):
        return jnp.concatenate([s * left, s * right], axis=0)

    mats = [a.astype(BF16) for a in mats]
    strips = [to_strip(a) for a in mats]
    corrs = [-(s * strip_masks[0]) for s in strips]
    for joins in strip_masks[1:]:
        xs = [s * joins for s in strips]
        ys = [x.astype(F32) + _dot(corr, from_strip(x)) for corr, x in zip(corrs, xs)]
        corrs = [corr - (y + _dot(y.astype(BF16), from_strip(corr))).astype(BF16) for corr, y in zip(corrs, ys)]
    corrs = [from_strip(corr) for corr in corrs]
    xs = [a * last_join for a in mats]
    ys = [x.astype(F32) + _dot(corr, x) for corr, x in zip(corrs, xs)]
    return [corr - (y + _dot(y.astype(BF16), corr)).astype(BF16) for corr, y in zip(corrs, ys)]


def _tri_level_masks(n):
    ij_xor = lax.broadcasted_iota(I32, (n, n), 0) ^ lax.broadcasted_iota(I32, (n, n), 1)
    half = n // 2
    lane = lax.broadcasted_iota(I32, (1, n), 1)
    left = jnp.where(lane < half, 1.0, 0.0).astype(BF16)
    right = jnp.where(lane >= half, 1.0, 0.0).astype(BF16)
    strip_masks, m = [], 1
    while m < half:
        full = jnp.where((ij_xor >= m) & (ij_xor < 2 * m), 1.0, 0.0).astype(BF16)
        strip_masks.append(full[0:half, :] * left + full[half:, :] * right)
        m *= 2
    last_join = jnp.where((ij_xor >= half) & (ij_xor < n), 1.0, 0.0).astype(BF16)
    return strip_masks, last_join, left, right


def _dn_kernel(dqc_ref, dkc_ref, dvc_ref, dql_ref, dkl_ref, dvl_ref, cq_ref, ck_ref, cv_ref,
               gc_ref, gl_ref, gtc_ref, gtl_ref, o_ref,
               xq_s, xk_s, xv_s, q_s, k_s, v_s, kk_s, qk_s, g_s, gt_s, mq_s, n_s, op_s, gl_s):
    qkv_ctx_refs = (dqc_ref, dkc_ref, dvc_ref)
    qkv_lat_refs = (dql_ref, dkl_ref, dvl_ref)
    conv_refs = (cq_ref, ck_ref, cv_ref)
    x_scr = (xq_s, xk_s, xv_s)
    qkv_scr = (q_s, k_s, v_s)
    c = DN_CHUNK
    n_ctx = gc_ref.shape[1]
    ctx_chunks = n_ctx // c
    lat_chunks = gl_ref.shape[1] // c
    dk = DN_HEAD_DIM
    ii = lax.broadcasted_iota(I32, (c, c), 0)
    jj = lax.broadcasted_iota(I32, (c, c), 1)
    level_masks = _tri_level_masks(c)
    lane = lax.broadcasted_iota(I32, (1, LANES), 1)

    g_s[0:n_ctx, :] = gc_ref[0]
    g_s[n_ctx:, :] = gl_ref[0]
    gt_s[:, 0:ctx_chunks, :] = gtc_ref[0]
    gt_s[:, ctx_chunks:, :] = gtl_ref[0]

    for hh in range(DN_HEADS_PER_STEP):
        _dn_prepare_head(hh, pl.program_id(1) * DN_HEADS_PER_STEP + hh, ii, jj, level_masks, lane,
                         qkv_ctx_refs, qkv_lat_refs, conv_refs, g_s, gt_s,
                         x_scr, qkv_scr, kk_s, qk_s, mq_s, n_s, op_s, gl_s)

    def scan(first_chunk, count, emit, states):
        rows = dk + c if emit else dk

        def step(t, st):
            chains = [(hh, d, first_chunk + t if d == 0 else first_chunk + count - 1 - t)
                      for hh in range(DN_HEADS_PER_STEP) for d in range(2)]
            prods = [_dot(mq_s[hh, d, i, 0:rows, :], s.astype(BF16)) for (hh, d, i), s in zip(chains, st)]
            new = []
            for (hh, d, i), s, r in zip(chains, st, prods):
                if emit:
                    ro = pl.multiple_of((i - ctx_chunks) * c, c)
                    o_ref[0, pl.ds(ro, c), hh * dk:(hh + 1) * dk] += r[dk:dk + c] + op_s[hh, d, i]
                new.append(s * gl_s[hh, d, i][0:1, :] + r[0:dk] + n_s[hh, d, i])
            return tuple(new)
        return lax.fori_loop(0, count, step, states)

    o_ref[...] = jnp.zeros(o_ref.shape, F32)
    zero_state = jnp.zeros((dk, dk), F32)
    states = scan(0, ctx_chunks, False, (zero_state,) * (2 * DN_HEADS_PER_STEP))
    scan(ctx_chunks, lat_chunks, True, states)


def _dn_prepare_head(hh, head, ii, jj, level_masks, lane, qkv_ctx_refs, qkv_lat_refs, conv_refs, g_s, gt_s,
                     x_scr, qkv_scr, kk_s, qk_s, mq_s, n_s, op_s, gl_s):
    c = DN_CHUNK
    n_ctx = qkv_ctx_refs[0].shape[1]
    seq = qkv_lat_refs[0].shape[1]
    n_chunks = (n_ctx + seq) // c
    ctx_chunks = n_ctx // c
    pad = SUBLANES
    dk = DN_HEAD_DIM
    half = DN_CONV_W // 2
    cols = slice(hh * dk, (hh + 1) * dk)
    xq_s, xk_s, xv_s = x_scr
    q_s, k_s, v_s = qkv_scr
    cq_ref, ck_ref, cv_ref = conv_refs

    zeros_pad = jnp.zeros((pad, dk), F32)
    lat0 = 2 * pad + n_ctx
    for src_ctx, src_lat, dst in zip(qkv_ctx_refs, qkv_lat_refs, x_scr):
        dst[0:pad, :] = zeros_pad
        dst[pad:pad + n_ctx, :] = src_ctx[0, :, cols]
        dst[pad + n_ctx:lat0, :] = zeros_pad
        dst[lat0:lat0 + seq, :] = src_lat[0, :, cols]
        dst[lat0 + seq:lat0 + seq + pad, :] = zeros_pad

    def conv_chunk(i, carry):
        r0 = pl.multiple_of(i * c, c)
        rp = r0 + jnp.where(i >= ctx_chunks, 2 * pad, pad)

        def conv(x_s, cw_ref):
            acc = None
            for s in range(DN_CONV_W):
                term = x_s[pl.ds(rp - half + s, c), :] * cw_ref[s:s + 1, cols]
                acc = term if acc is None else acc + term
            return _silu(acc)

        qc = conv(xq_s, cq_ref)
        kc = conv(xk_s, ck_ref)
        vc = conv(xv_s, cv_ref)
        qc = qc * lax.rsqrt(jnp.sum(qc * qc, axis=-1, keepdims=True) + EPS) * (dk ** -0.5)
        kc = kc * lax.rsqrt(jnp.sum(kc * kc, axis=-1, keepdims=True) + EPS)
        q_s[pl.ds(r0, c), :] = qc
        k_s[pl.ds(r0, c), :] = kc
        v_s[pl.ds(r0, c), :] = vc
        kb = kc.astype(BF16)
        kk_s[i] = _dot_nt(kb, kb)
        qk_s[i] = _dot_nt(qc.astype(BF16), kb)
        return carry

    lax.fori_loop(0, n_chunks, conv_chunk, 0, unroll=DN_CONV_UNROLL)

    def local_group(grp, carry):
        chains = []
        for k in range(DN_GROUP):
            i = grp * DN_GROUP + k
            r0 = pl.multiple_of(i * c, c)
            gates = g_s[pl.ds(r0, c), :]
            kc = k_s[pl.ds(r0, c), :]
            qc = q_s[pl.ds(r0, c), :]
            vc = v_s[pl.ds(r0, c), :]
            kk = kk_s[i]
            qk = qk_s[i]
            kt = kc.T
            for d in range(2):
                beta_col = jnp.sum(jnp.where(lane == d * DN_HEADS + head, gates, 0.0), axis=-1, keepdims=True)
                g_col = jnp.sum(jnp.where(lane == N_GATES // 2 + d * DN_HEADS + head, gates, 0.0),
                                axis=-1, keepdims=True)
                g_row = gt_s[pl.ds(N_GATES // 2 + d * DN_HEADS + head, 1), pl.ds(i, 1), :].reshape(1, c)
                incl = (jj <= ii) if d == 0 else (jj >= ii)
                strict = (jj < ii) if d == 0 else (jj > ii)
                incl_t = (ii <= jj) if d == 0 else (ii >= jj)
                gcum_col = jnp.sum(jnp.where(incl, g_row, 0.0), axis=1, keepdims=True)
                gcum_row = jnp.sum(jnp.where(incl_t, g_col, 0.0), axis=0, keepdims=True)
                total = jnp.sum(g_row, axis=1, keepdims=True)
                decay = jnp.exp(jnp.where(incl, gcum_col - gcum_row, -jnp.inf))
                e_col = jnp.exp(gcum_col)
                kdt = (kt * jnp.exp(total - gcum_row)).astype(BF16)
                qkm = (qk * decay).astype(BF16)
                chains.append(dict(
                    d=d, i=i,
                    a=jnp.where(strict, beta_col * kk * decay, 0.0),
                    rhs=jnp.concatenate([vc * beta_col, kc * (beta_col * e_col)], axis=1),
                    lhs=jnp.concatenate([kdt, qkm], axis=0),
                    qd=qc * e_col,
                    g_last=jnp.exp(total)))
        corrs = _unit_tri_inverses_minus_eye([ch["a"] for ch in chains], level_masks)
        sols = [ch["rhs"] + _dot(corr, ch["rhs"].astype(BF16)) for ch, corr in zip(chains, corrs)]
        prods = [_dot(ch["lhs"], sol.astype(BF16)) for ch, sol in zip(chains, sols)]
        for ch, r in zip(chains, prods):
            d, i = ch["d"], ch["i"]
            mq_s[hh, d, i, 0:dk, :] = (-r[0:dk, dk:2 * dk]).astype(BF16)
            mq_s[hh, d, i, dk:dk + c, :] = (ch["qd"] - r[dk:dk + c, dk:2 * dk]).astype(BF16)
            n_s[hh, d, i] = r[0:dk, 0:dk]
            op_s[hh, d, i] = r[dk:dk + c, 0:dk]
            gl_s[hh, d, i] = jnp.broadcast_to(ch["g_last"], (SUBLANES, dk))
        return carry

    lax.fori_loop(0, n_chunks // DN_GROUP, local_group, 0)


def _dn_call(dn_ctx, dn_lat, conv_w, g_ctx, g_lat, gt_ctx, gt_lat, n_ctx):
    bsz, seq, _ = dn_lat.shape
    tot = n_ctx + seq
    c = DN_CHUNK
    assert n_ctx % c == 0 and seq % c == 0
    n_chunks = tot // c
    ctx_chunks = n_ctx // c
    dk = DN_HEAD_DIM
    gtc4 = gt_ctx.reshape(N_GATES, bsz, ctx_chunks, c).transpose(1, 0, 2, 3)
    gtl4 = gt_lat.reshape(bsz, N_GATES, seq // c, c)
    hp = DN_HEADS_PER_STEP
    steps = DN_HEADS // hp
    col_ctx = lambda off: pl.BlockSpec((1, n_ctx, hp * dk), lambda b, h: (0, b, off + h))
    col_lat = lambda off: pl.BlockSpec((1, seq, hp * dk), lambda b, h: (b, 0, off + h))
    cw = lambda off: pl.BlockSpec((DN_CONV_W, hp * dk), lambda b, h: (0, off + h))
    padded = tot + 3 * SUBLANES
    assert n_chunks % DN_GROUP == 0 and n_chunks % DN_CONV_UNROLL == 0 and DN_HEADS % hp == 0
    return pl.pallas_call(
        _dn_kernel,
        grid=(bsz, steps),
        in_specs=[col_ctx(0), col_ctx(steps), col_ctx(2 * steps),
                  col_lat(0), col_lat(steps), col_lat(2 * steps),
                  cw(0), cw(steps), cw(2 * steps),
                  pl.BlockSpec((1, n_ctx, LANES), lambda b, h: (0, b, 0)),
                  pl.BlockSpec((1, seq, LANES), lambda b, h: (b, 0, 0)),
                  pl.BlockSpec((1, N_GATES, ctx_chunks, c), lambda b, h: (b, 0, 0, 0)),
                  pl.BlockSpec((1, N_GATES, seq // c, c), lambda b, h: (b, 0, 0, 0))],
        out_specs=pl.BlockSpec((1, seq, hp * dk), lambda b, h: (b, 0, h)),
        out_shape=jax.ShapeDtypeStruct((bsz, seq, DN_W), F32),
        scratch_shapes=[pltpu.VMEM((padded, dk), F32)] * 3
        + [pltpu.VMEM((tot, dk), F32)] * 3
        + [pltpu.VMEM((n_chunks, c, c), F32)] * 2
        + [pltpu.VMEM((tot, LANES), F32),
           pltpu.VMEM((N_GATES, n_chunks, c), F32)]
        + [pltpu.VMEM((hp, 2, n_chunks, dk + c, dk), BF16),
           pltpu.VMEM((hp, 2, n_chunks, dk, dk), F32),
           pltpu.VMEM((hp, 2, n_chunks, c, dk), F32),
           pltpu.VMEM((hp, 2, n_chunks, SUBLANES, dk), F32)],
        compiler_params=_params("arbitrary", "arbitrary"),
        name="dn",
    )(dn_ctx, dn_ctx, dn_ctx, dn_lat, dn_lat, dn_lat, conv_w, conv_w, conv_w, g_ctx, g_lat, gtc4, gtl4)


def _outproj_kernel(x_ref, att_ref, o_ref, z_ref, mod_ref, onw_ref, wo_ref, n2_ref, rw_ref,
                    x1_ref, h2_ref, afft_ref):
    d = x_ref.shape[-1]
    b = pl.program_id(0)
    gate1 = mod_ref[pl.ds(b, 1), 2 * d:3 * d]
    shift2 = mod_ref[pl.ds(b, 1), 3 * d:4 * d]
    scale2 = mod_ref[pl.ds(b, 1), 4 * d:5 * d]
    lane = lax.broadcasted_iota(I32, (1, LANES), 1)
    n_rows = x_ref.shape[1]

    def mixed(rows):
        parts = [att_ref[0, rows, :]]
        for h in range(DN_HEADS):
            sl = slice(h * DN_HEAD_DIM, (h + 1) * DN_HEAD_DIM)
            oh = o_ref[0, rows, sl]
            on = oh * lax.rsqrt(jnp.mean(oh * oh, axis=-1, keepdims=True) + EPS) * onw_ref[...]
            parts.append((on * _silu(z_ref[0, rows, sl])).astype(BF16))
        return _dot(jnp.concatenate(parts, axis=1), wo_ref[...])

    def finish(rows, y):
        x1 = x_ref[0, rows, :] + gate1 * y
        x1_ref[0, rows, :] = x1
        hn = x1 * lax.rsqrt(jnp.mean(x1 * x1, axis=-1, keepdims=True) + EPS) * n2_ref[...]
        h2 = (hn * (1.0 + scale2) + shift2).astype(BF16)
        h2_ref[0, rows, :] = h2
        logits = _dot(h2, rw_ref[...])
        logits = jnp.where(lane < N_EXPERTS, logits, -jnp.inf)
        e = jnp.exp(logits - jnp.max(logits, axis=-1, keepdims=True))
        aff = e / jnp.sum(e, axis=-1, keepdims=True)
        afft_ref[0, :, rows] = aff.T[0:N_EXPERTS, :]

    chunks = [slice(r0, r0 + OUT_ROW_CHUNK) for r0 in range(0, n_rows, OUT_ROW_CHUNK)]
    y_next = mixed(chunks[0])
    for r, rows in enumerate(chunks):
        y = y_next
        if r + 1 < len(chunks):
            y_next = mixed(chunks[r + 1])
        finish(rows, y)


def _outproj_call(x, att, o_dn, z, mod, onw, wo, n2, rw):
    bsz, seq, d = x.shape
    t = OUT_TILE
    assert seq % t == 0
    full = lambda shape: pl.BlockSpec(shape, lambda b, j: (0,) * len(shape))
    return pl.pallas_call(
        _outproj_kernel,
        grid=(bsz, seq // t),
        in_specs=[pl.BlockSpec((1, t, d), lambda b, j: (b, j, 0)),
                  pl.BlockSpec((1, t, ATTN_Q_W), lambda b, j: (b, j, 0)),
                  pl.BlockSpec((1, t, DN_W), lambda b, j: (b, j, 0)),
                  pl.BlockSpec((1, t, DN_W), lambda b, j: (b, j, 0)),
                  full((MOD_ROWS, mod.shape[1])),
                  full((1, DN_HEAD_DIM)),
                  full(wo.shape),
                  full((1, d)),
                  full((d, LANES))],
        out_specs=[pl.BlockSpec((1, t, d), lambda b, j: (b, j, 0)),
                   pl.BlockSpec((1, t, d), lambda b, j: (b, j, 0)),
                   pl.BlockSpec((1, N_EXPERTS, t), lambda b, j: (b, 0, j))],
        out_shape=[jax.ShapeDtypeStruct((bsz, seq, d), F32),
                   jax.ShapeDtypeStruct((bsz, seq, d), BF16),
                   jax.ShapeDtypeStruct((bsz, N_EXPERTS, seq), F32)],
        compiler_params=_params("arbitrary", "arbitrary"),
        name="outproj",
    )(x, att, o_dn, z, mod, onw, wo, n2, rw)


def _route_kernel(cap, afft_ref, slot_ref, gate_ref, tok_ref, bounds_ref):
    n_b, n_exp, n_t = afft_ref.shape
    n_e = n_b * n_exp
    aff = afft_ref[...].reshape(n_e, n_t)

    def enough(cand):
        return jnp.sum(jnp.where(aff >= cand, 1.0, 0.0), axis=-1, keepdims=True) >= cap

    tiny = 2.0 ** F32_MIN_EXP
    cur = jnp.full((n_e, 1), tiny, F32)
    any_normal = enough(cur)
    shift = 1 << (-F32_MIN_EXP).bit_length()
    while shift > 1:
        shift //= 2
        cand = cur * (2.0 ** shift)
        cur = jnp.where(enough(cand), cand, cur)

    def refine(_, state):
        cur, step = state
        cand = cur + step
        return jnp.where(enough(cand), cand, cur), step * 0.5

    cur, _ = lax.fori_loop(0, F32_MANTISSA_BITS, refine, (cur, cur * 0.5))
    thr = jnp.where(any_normal, cur, 0.0)
    need = cap - jnp.sum(jnp.where(aff > thr, 1.0, 0.0), axis=-1, keepdims=True)

    upper = (lax.broadcasted_iota(I32, (LANES, LANES), 0) < lax.broadcasted_iota(I32, (LANES, LANES), 1))
    upper = jnp.where(upper, 1.0, 0.0).astype(BF16)
    run = jnp.zeros((2 * n_e, 1), F32)
    lane = lax.broadcasted_iota(I32, (1, LANES), 1)
    bounds = jnp.zeros((n_e, LANES), F32)
    per_block = ROUTE_BLOCK // LANES
    for blk in range(n_t // LANES):
        sl = slice(blk * LANES, (blk + 1) * LANES)
        gt = aff[:, sl] > thr
        eq = aff[:, sl] == thr
        x = jnp.concatenate([jnp.where(gt, 1.0, 0.0), jnp.where(eq, 1.0, 0.0)], axis=0)
        cum = _dot(x.astype(BF16), upper) + run
        run = run + jnp.sum(x, axis=-1, keepdims=True)
        cum_gt = cum[0:n_e]
        cum_eq = cum[n_e:2 * n_e]
        sel = gt | (eq & (cum_eq < need))
        slot = jnp.where(sel, cum_gt + jnp.minimum(cum_eq, need), -1.0)
        slot_ref[:, :, sl] = slot.astype(I32).reshape(n_b, n_exp, LANES)
        gate_ref[:, :, sl] = jnp.where(sel, aff[:, sl], 0.0).reshape(n_b, n_exp, LANES)
        if (blk + 1) % per_block == 0:
            taken = run[0:n_e] + jnp.minimum(run[n_e:2 * n_e], need)
            bounds = jnp.where(lane == (blk + 1) // per_block, taken, bounds)
    bounds_ref[...] = bounds.astype(I32).reshape(n_b, n_exp, LANES)

    for b in range(n_b):
        stacked = jnp.concatenate([slot_ref[b].astype(F32), gate_ref[b],
                                   jnp.zeros((LANES - 2 * n_exp, n_t), F32)], axis=0)
        tok_ref[b] = stacked.T


def _route_call(afft, cap):
    bsz, n_e, n_t = afft.shape
    row = pl.BlockSpec((bsz, n_e, n_t), lambda i: (0, 0, 0))
    return pl.pallas_call(
        functools.partial(_route_kernel, cap),
        grid=(1,),
        in_specs=[row],
        out_specs=[row, row, pl.BlockSpec((bsz, n_t, LANES), lambda i: (0, 0, 0)),
                   pl.BlockSpec((bsz, n_e, LANES), lambda i: (0, 0, 0))],
        out_shape=[jax.ShapeDtypeStruct((bsz, n_e, n_t), I32),
                   jax.ShapeDtypeStruct((bsz, n_e, n_t), F32),
                   jax.ShapeDtypeStruct((bsz, n_t, LANES), F32),
                   jax.ShapeDtypeStruct((bsz, n_e, LANES), I32)],
        compiler_params=_params("arbitrary"),
        name="route",
    )(afft)


def _window_starts(cnt_ref, b, tb, n_e, cap):
    starts, fits = [], None
    for e in range(n_e):
        lo = cnt_ref[b, tb * n_e + e]
        hi = cnt_ref[b, (tb + 1) * n_e + e]
        start = jnp.minimum((lo // BF16_ROWS) * BF16_ROWS, cap - SLOT_WINDOW)
        ok = hi - start <= SLOT_WINDOW
        starts.append(pl.multiple_of(start, BF16_ROWS))
        fits = ok if fits is None else jnp.logical_and(fits, ok)
    return starts, fits


def _gather_kernel(cap, cnt_ref, slot_ref, gate_ref, h_ref, xg_ref, gs_ref):
    b = pl.program_id(0)
    n_e = slot_ref.shape[1]
    n_blk = slot_ref.shape[2]
    xg_ref[...] = jnp.zeros(xg_ref.shape, BF16)
    gs_ref[...] = jnp.zeros(gs_ref.shape, F32)

    def token_block(tb, carry):
        h_blk = h_ref[0, pl.ds(pl.multiple_of(tb * ROUTE_BLOCK, ROUTE_BLOCK), ROUTE_BLOCK), :]
        starts, fits = _window_starts(cnt_ref, b, tb, n_e, cap)

        def accumulate(window, first_rows):
            j = lax.broadcasted_iota(I32, (window, 1), 0)
            hits = [slot_ref[0, e, pl.ds(tb, 1), :] == first_rows[e] + j for e in range(n_e)]
            onehot = jnp.concatenate([jnp.where(hit, 1.0, 0.0).astype(BF16) for hit in hits], axis=0)
            rows = _dot(onehot, h_blk)
            for e in range(n_e):
                dst = pl.ds(first_rows[e], window)
                xg_ref[e, 0, dst, :] += rows[e * window:(e + 1) * window].astype(BF16)
                gate = gate_ref[0, e, pl.ds(tb, 1), :]
                gs_ref[e, 0, dst, :] += jnp.sum(jnp.where(hits[e], gate, 0.0), axis=-1, keepdims=True)

        @pl.when(fits)
        def _():
            accumulate(SLOT_WINDOW, starts)

        @pl.when(jnp.logical_not(fits))
        def _():
            accumulate(cap, [0] * n_e)

        return carry

    lax.fori_loop(0, n_blk, token_block, 0)


def _gather_call(cnt, slot, gate, h2, cap):
    bsz, n_e, n_t = slot.shape
    d = h2.shape[-1]
    n_blk = n_t // ROUTE_BLOCK
    blocked = lambda a: a.reshape(bsz, n_e, n_blk, ROUTE_BLOCK)
    row = pl.BlockSpec((1, n_e, n_blk, ROUTE_BLOCK), lambda b, cnt: (b, 0, 0, 0))
    return pl.pallas_call(
        functools.partial(_gather_kernel, cap),
        grid_spec=pltpu.PrefetchScalarGridSpec(
            num_scalar_prefetch=1,
            grid=(bsz,),
            in_specs=[row, row, pl.BlockSpec((1, n_t, d), lambda b, cnt: (b, 0, 0))],
            out_specs=[pl.BlockSpec((n_e, 1, cap, d), lambda b, cnt: (0, b, 0, 0)),
                       pl.BlockSpec((n_e, 1, cap, 1), lambda b, cnt: (0, b, 0, 0))]),
        out_shape=[jax.ShapeDtypeStruct((n_e, bsz, cap, d), BF16),
                   jax.ShapeDtypeStruct((n_e, bsz, cap, 1), F32)],
        compiler_params=_params("arbitrary"),
        name="gather",
    )(cnt, blocked(slot), blocked(gate), h2)


def _ffn_kernel(xg_ref, gs_ref, wg_ref, wu_ref, wd_ref, y_ref, acc_ref):
    bsz, cap, d = xg_ref.shape[1:]
    per = FFN_ROW_CHUNK // cap
    n_chunks = bsz // per

    @pl.when(pl.program_id(1) == 0)
    def _():
        acc_ref[...] = jnp.zeros(acc_ref.shape, F32)

    wg = wg_ref[0].astype(BF16)
    wu = wu_ref[0].astype(BF16)
    wd = wd_ref[0].astype(BF16)

    def up(r):
        x = xg_ref[0, r * per:(r + 1) * per].reshape(per * cap, d)
        return _dot(x, wg), _dot(x, wu)

    nxt = up(0)
    for r in range(n_chunks):
        g, u = nxt
        if r + 1 < n_chunks:
            nxt = up(r + 1)
        rows = slice(r * per * cap, (r + 1) * per * cap)
        total = acc_ref[rows, :] + _dot((_silu(g) * u).astype(BF16), wd)
        acc_ref[rows, :] = total
        y = total * gs_ref[0, r * per:(r + 1) * per].reshape(per * cap, 1)
        y_ref[0, r * per:(r + 1) * per] = y.astype(BF16).reshape(per, cap, d)


def _ffn_call(xg, gs, w_gate, w_up, w_down):
    n_e, bsz, cap, d = xg.shape
    ff = w_gate.shape[-1]
    tf = FF_TILE
    return pl.pallas_call(
        _ffn_kernel,
        grid=(n_e, ff // tf),
        in_specs=[pl.BlockSpec((1, bsz, cap, d), lambda e, f: (e, 0, 0, 0)),
                  pl.BlockSpec((1, bsz, cap, 1), lambda e, f: (e, 0, 0, 0)),
                  pl.BlockSpec((1, d, tf), lambda e, f: (e, 0, f)),
                  pl.BlockSpec((1, d, tf), lambda e, f: (e, 0, f)),
                  pl.BlockSpec((1, tf, d), lambda e, f: (e, f, 0))],
        out_specs=pl.BlockSpec((1, bsz, cap, d), lambda e, f: (e, 0, 0, 0)),
        out_shape=jax.ShapeDtypeStruct((n_e, bsz, cap, d), BF16),
        scratch_shapes=[pltpu.VMEM((bsz * cap, d), F32)],
        compiler_params=_params("arbitrary", "arbitrary"),
        name="ffn",
    )(xg, gs, w_gate, w_up, w_down)


def _combine_kernel(cap, cnt_ref, x1_ref, tok_ref, y_ref, mod_ref, o_ref):
    d = x1_ref.shape[-1]
    n_e = y_ref.shape[0]
    b = pl.program_id(0)
    tb = pl.program_id(1)
    gate2 = mod_ref[pl.ds(b, 1), 5 * d:6 * d]
    tok = tok_ref[0]
    starts, fits = _window_starts(cnt_ref, b, tb, n_e, cap)

    @pl.when(fits)
    def _():
        lane = lax.broadcasted_iota(I32, (1, LANES), 1)
        first = lane < SLOT_WINDOW
        groups = []
        for e in range(0, n_e, 2):
            slot = jnp.where(first, tok[:, e:e + 1], tok[:, e + 1:e + 2])
            target = jnp.where(first, starts[e] + lane, starts[e + 1] + lane - SLOT_WINDOW).astype(F32)
            groups.append(jnp.where(slot == target, 1.0, 0.0).astype(BF16))
        onehot = jnp.concatenate(groups, axis=1)
        rows = jnp.concatenate([y_ref[e, 0, pl.ds(starts[e], SLOT_WINDOW), :] for e in range(n_e)], axis=0)
        o_ref[0] = x1_ref[0] + gate2 * _dot(onehot, rows)

    @pl.when(jnp.logical_not(fits))
    def _():
        j = lax.broadcasted_iota(I32, (1, cap), 1).astype(F32)
        acc = None
        for e in range(n_e):
            onehot = jnp.where(tok[:, e:e + 1] == j, 1.0, 0.0).astype(BF16)
            part = _dot(onehot, y_ref[e, 0])
            acc = part if acc is None else acc + part
        o_ref[0] = x1_ref[0] + gate2 * acc


def _combine_call(cnt, x1, tok, y, mod, cap):
    bsz, seq, d = x1.shape
    n_e = y.shape[0]
    t = ROUTE_BLOCK
    assert 2 * SLOT_WINDOW == LANES and n_e % 2 == 0
    return pl.pallas_call(
        functools.partial(_combine_kernel, cap),
        grid_spec=pltpu.PrefetchScalarGridSpec(
            num_scalar_prefetch=1,
            grid=(bsz, seq // t),
            in_specs=[pl.BlockSpec((1, t, d), lambda b, j, cnt: (b, j, 0)),
                      pl.BlockSpec((1, t, LANES), lambda b, j, cnt: (b, j, 0)),
                      pl.BlockSpec((n_e, 1, cap, d), lambda b, j, cnt: (0, b, 0, 0)),
                      pl.BlockSpec((MOD_ROWS, mod.shape[1]), lambda b, j, cnt: (0, 0))],
            out_specs=pl.BlockSpec((1, t, d), lambda b, j, cnt: (b, j, 0))),
        out_shape=jax.ShapeDtypeStruct((bsz, seq, d), F32),
        compiler_params=_params("arbitrary", "arbitrary"),
        name="combine",
    )(cnt, x1, tok, y, mod)


def _rope_tables(seq):
    m = ATTN_HEAD_DIM // 4
    pos = jnp.arange(seq, dtype=jnp.int32)
    rows = (pos // GRID_W).astype(F32)
    cols = (pos % GRID_W).astype(F32)
    freqs = ROPE_BASE ** (-jnp.arange(m, dtype=F32) / m)
    ang_r = rows[:, None] * freqs[None, :]
    ang_c = cols[:, None] * freqs[None, :]
    cos_h = jnp.concatenate([jnp.cos(ang_r), jnp.cos(ang_r), jnp.cos(ang_c), jnp.cos(ang_c)], axis=-1)
    sin_h = jnp.concatenate([-jnp.sin(ang_r), jnp.sin(ang_r), -jnp.sin(ang_c), jnp.sin(ang_c)], axis=-1)
    reps = LANES // ATTN_HEAD_DIM
    return jnp.tile(cos_h, (1, reps)), jnp.tile(sin_h, (1, reps))


def _lane_row(values, offset):
    return jnp.zeros((1, LANES), F32).at[0, offset:offset + values.shape[0]].set(values.astype(F32))


def kernel(x, c, ctx, c_ctx, w_mod, b_mod, norm1_w, norm2_w, w_in, q_norm_w, k_norm_w, conv_w, a_log, dt_bias,
           o_norm_w, w_out, router_w, w_gate, w_up, w_down):
    bsz, seq, d = x.shape
    n_ctx = ctx.shape[1]
    assert w_mod.shape[0] == 1, "single layer: the last layer's context outputs are never consumed"
    assert bsz < MOD_ROWS and N_EXPERTS == router_w.shape[-1]
    cap = EC_CAPACITY_FACTOR * seq // N_EXPERTS

    cc = jnp.concatenate([c, c_ctx[None, :], jnp.zeros((MOD_ROWS - bsz - 1, d), F32)], axis=0)
    mod = _mod_call(cc, w_mod[0], b_mod[0][None, :])

    w_pad = jnp.concatenate([w_in[0].astype(BF16), jnp.zeros((d, LANES - N_GATES), BF16)], axis=1)
    seg = np.arange(MXU_DIM) // ATTN_HEAD_DIM
    bd = jnp.asarray(seg[:, None] == seg[None, :], BF16)
    qkw = jnp.concatenate([jnp.tile(q_norm_w[0], ATTN_HEADS), jnp.tile(k_norm_w[0], ATTN_KV_HEADS)])[None, :]
    alog_l = _lane_row(a_log[0].reshape(-1), N_GATES // 2)
    dtb_l = _lane_row(dt_bias[0].reshape(-1), N_GATES // 2)
    shared = (mod, norm1_w[0][None, :], w_pad, bd, qkw, alog_l, dtb_l)
    kv_lat, dn_lat, g_lat, gt_lat, q, z = _inproj_call(x, bsz, *shared, rope=_rope_tables(seq))
    kv_ctx, dn_ctx, g_ctx, gt_ctx = _inproj_call(ctx.reshape(1, bsz * n_ctx, d), bsz, *shared)

    att = _attn_call(q, kv_ctx, kv_lat, n_ctx)
    o_dn = _dn_call(dn_ctx, dn_lat, conv_w[0], g_ctx, g_lat, gt_ctx, gt_lat, n_ctx)

    rw = jnp.pad(router_w[0], ((0, 0), (0, LANES - N_EXPERTS))).astype(BF16)
    x1, h2, afft = _outproj_call(x, att, o_dn, z, mod, o_norm_w[0][None, :], w_out[0].astype(BF16),
                                 norm2_w[0][None, :], rw)

    slot, gate, tok, bounds = _route_call(afft, cap)
    n_blk = seq // ROUTE_BLOCK
    cnt = bounds[:, :, 0:n_blk + 1].transpose(0, 2, 1).reshape(bsz, (n_blk + 1) * N_EXPERTS)
    xg, gs = _gather_call(cnt, slot, gate, h2, cap)
    y = _ffn_call(xg, gs, w_gate[0], w_up[0], w_down[0])
    return _combine_call(cnt, x1, tok, y, mod, cap)
```

```python
import functools
import math

import numpy as np
import jax
import jax.numpy as jnp
from jax import lax
from jax.experimental import pallas as pl
from jax.experimental.pallas import tpu as pltpu

F32 = jnp.float32
BF16 = jnp.bfloat16
I32 = jnp.int32

GRID_W = 64
EPS = 1e-6
ATTN_HEADS = 8
ATTN_KV_HEADS = 2
ATTN_HEAD_DIM = 64
ROPE_BASE = 10000.0
DN_HEADS = 4
DN_HEAD_DIM = 128
DN_CONV_W = 5
N_EXPERTS = 16
EC_CAPACITY_FACTOR = 2

ATTN_Q_W = ATTN_HEADS * ATTN_HEAD_DIM
ATTN_KV_W = ATTN_KV_HEADS * ATTN_HEAD_DIM
DN_W = DN_HEADS * DN_HEAD_DIM
QK_W = ATTN_Q_W + ATTN_KV_W
ATT_W = ATTN_Q_W + 2 * ATTN_KV_W
N_GATES = 4 * DN_HEADS

LANES = 128
SUBLANES = 8
MXU_DIM = 256
VMEM_LIMIT = 56 * 1024 * 1024
F32_MIN_EXP = -126
F32_MANTISSA_BITS = 23

TOK_TILE = 512
IN_ROW_CHUNK = 128
Q_TILE = 512
ATTN_KEY_CHUNK = 1024
OUT_TILE = 512
OUT_ROW_CHUNK = 128
DN_CHUNK = 128
DN_GROUP = 9
DN_CONV_UNROLL = 6
DN_HEADS_PER_STEP = 2
ROUTE_BLOCK = 256
SLOT_WINDOW = 64
BF16_ROWS = 16
FFN_ROW_CHUNK = 512
MOD_ROWS = 16


def _params(*sem):
    return pltpu.CompilerParams(dimension_semantics=sem, vmem_limit_bytes=VMEM_LIMIT)


def _silu(v):
    half = 0.5 * v
    return half + half * jnp.tanh(half)


def _dot(a, b):
    return jnp.dot(a, b, preferred_element_type=F32)


def _dot_nt(a, b):
    return lax.dot_general(a, b, (((1,), (1,)), ((), ())), preferred_element_type=F32)


def _mod_kernel(c_ref, w_ref, b_ref, o_ref):
    sc = _silu(c_ref[...]).astype(BF16)
    o_ref[...] = _dot(sc, w_ref[...].astype(BF16)) + b_ref[...]


def _mod_call(cc, w_mod, b_mod):
    d, n = w_mod.shape
    return pl.pallas_call(
        _mod_kernel,
        grid=(n // d,),
        in_specs=[pl.BlockSpec((MOD_ROWS, d), lambda i: (0, 0)),
                  pl.BlockSpec((d, d), lambda i: (0, i)),
                  pl.BlockSpec((1, d), lambda i: (0, i))],
        out_specs=pl.BlockSpec((MOD_ROWS, d), lambda i: (0, i)),
        out_shape=jax.ShapeDtypeStruct((MOD_ROWS, n), F32),
        compiler_params=_params("arbitrary"),
        name="mod",
    )(cc, w_mod, b_mod)


def _inproj_kernel(latent, ctx_row, x_ref, mod_ref, n1_ref, w_ref, bd_ref, qkw_ref, alog_ref, dtb_ref, *refs):
    if latent:
        cos_ref, sin_ref, kv_ref, dn_ref, g_ref, gt_ref, q_ref, z_ref = refs
    else:
        kv_ref, dn_ref, g_ref, gt_ref = refs
    d = x_ref.shape[-1]
    n_rows = x_ref.shape[1]
    row = pl.program_id(0) if latent else ctx_row
    shift = mod_ref[pl.ds(row, 1), 0:d]
    scale = mod_ref[pl.ds(row, 1), d:2 * d]
    lane = lax.broadcasted_iota(I32, (1, LANES), 1)
    k_blk = ATTN_Q_W // LANES

    def project(rows):
        xin = x_ref[0, rows, :]
        hn = xin * lax.rsqrt(jnp.mean(xin * xin, axis=-1, keepdims=True) + EPS) * n1_ref[...]
        h = (hn * (1.0 + scale) + shift).astype(BF16)
        return _dot(h, w_ref[...])

    def finish(rows, p):
        qk = p[:, 0:QK_W]
        sq = qk * qk
        hi = sq.astype(BF16)
        lo = (sq - hi.astype(F32)).astype(BF16)
        seg_w = bd_ref.shape[0]
        ms = []
        for c0 in range(0, QK_W, seg_w):
            w = min(seg_w, QK_W - c0)
            ones = bd_ref[0:w, 0:w]
            ms.append(_dot(hi[:, c0:c0 + w], ones) + _dot(lo[:, c0:c0 + w], ones))
        ms = jnp.concatenate(ms, axis=1) * (1.0 / ATTN_HEAD_DIM)
        qkn = qk * lax.rsqrt(ms + EPS) * qkw_ref[...]

        if latent:
            first_half = (lane % 32) < 16
            cos = cos_ref[rows, :]
            sin = sin_ref[rows, :]
            rot = []
            for i in range(QK_W // LANES):
                blk = qkn[:, i * LANES:(i + 1) * LANES]
                partner = jnp.where(first_half, pltpu.roll(blk, LANES - 16, axis=1), pltpu.roll(blk, 16, axis=1))
                rot.append(blk * cos + partner * sin)
            for i in range(k_blk):
                q_ref[0, rows, i * LANES:(i + 1) * LANES] = (rot[i] * (ATTN_HEAD_DIM ** -0.5)).astype(BF16)
            k2 = rot[k_blk]
            z_ref[0, rows, :] = p[:, ATT_W + 3 * DN_W:ATT_W + 4 * DN_W]
        else:
            k2 = qkn[:, k_blk * LANES:(k_blk + 1) * LANES]

        low = lane < ATTN_HEAD_DIM
        v2 = p[:, QK_W:ATT_W]
        for src, base in ((k2, 0), (v2, 2)):
            swapped = pltpu.roll(src, ATTN_HEAD_DIM, axis=1)
            kv_ref[0, 0, base + 0, rows, :] = jnp.where(low, src, 0.0).astype(BF16)
            kv_ref[0, 0, base + 1, rows, :] = jnp.where(low, 0.0, swapped).astype(BF16)
            kv_ref[0, 1, base + 0, rows, :] = jnp.where(low, swapped, 0.0).astype(BF16)
            kv_ref[0, 1, base + 1, rows, :] = jnp.where(low, 0.0, src).astype(BF16)

        dn_ref[0, rows, :] = p[:, ATT_W:ATT_W + 3 * DN_W]

        gp = p[:, ATT_W + 4 * DN_W:]
        beta = jax.nn.sigmoid(gp)
        xa = gp + dtb_ref[...]
        softplus = jnp.maximum(xa, 0.0) + jnp.log1p(jnp.exp(-jnp.abs(xa)))
        decay = -jnp.exp(alog_ref[...]) * softplus
        gates = jnp.where(lane < N_GATES // 2, beta, jnp.where(lane < N_GATES, decay, 0.0))
        g_ref[0, rows, :] = gates
        gt_ref[0, :, rows] = gates.T[0:N_GATES, :]

    chunks = [slice(r0, r0 + IN_ROW_CHUNK) for r0 in range(0, n_rows, IN_ROW_CHUNK)]
    p_next = project(chunks[0])
    for r, rows in enumerate(chunks):
        p = p_next
        if r + 1 < len(chunks):
            p_next = project(chunks[r + 1])
        finish(rows, p)


def _inproj_call(rows3, ctx_row, mod, n1, w_pad, bd, qkw, alog_l, dtb_l, rope=None):
    latent = rope is not None
    grp, rows, d = rows3.shape
    t = TOK_TILE
    assert rows % t == 0
    nw = w_pad.shape[1]
    full = lambda shape: pl.BlockSpec(shape, lambda b, j: (0,) * len(shape))
    tile = lambda width: pl.BlockSpec((1, t, width), lambda b, j: (b, j, 0))
    in_specs = [tile(d), full((MOD_ROWS, mod.shape[1])), full((1, d)), full((d, nw)), full(bd.shape),
                full((1, QK_W)), full((1, LANES)), full((1, LANES))]
    out_specs = [pl.BlockSpec((1, ATTN_KV_HEADS, 4, t, LANES), lambda b, j: (b, 0, 0, j, 0)),
                 tile(3 * DN_W), tile(LANES), pl.BlockSpec((1, N_GATES, t), lambda b, j: (b, 0, j))]
    out_shape = [jax.ShapeDtypeStruct((grp, ATTN_KV_HEADS, 4, rows, LANES), BF16),
                 jax.ShapeDtypeStruct((grp, rows, 3 * DN_W), F32),
                 jax.ShapeDtypeStruct((grp, rows, LANES), F32),
                 jax.ShapeDtypeStruct((grp, N_GATES, rows), F32)]
    args = [rows3, mod, n1, w_pad, bd, qkw, alog_l, dtb_l]
    if latent:
        in_specs += [pl.BlockSpec((t, LANES), lambda b, j: (j, 0))] * 2
        out_specs += [tile(ATTN_Q_W), tile(DN_W)]
        out_shape += [jax.ShapeDtypeStruct((grp, rows, ATTN_Q_W), BF16), jax.ShapeDtypeStruct((grp, rows, DN_W), F32)]
        args += list(rope)
    return pl.pallas_call(
        functools.partial(_inproj_kernel, latent, ctx_row),
        grid=(grp, rows // t),
        in_specs=in_specs,
        out_specs=out_specs,
        out_shape=out_shape,
        compiler_params=_params("arbitrary", "arbitrary"),
        name="inproj_lat" if latent else "inproj_ctx",
    )(*args)


def _attn_kernel(q_ref, kvc_ref, kvl_ref, o_ref):
    grp = ATTN_HEADS // ATTN_KV_HEADS
    n_lat = kvl_ref.shape[3]

    def scores(h):
        qp = q_ref[0, :, (h // 2) * LANES:(h // 2 + 1) * LANES]
        return (_dot_nt(qp, kvc_ref[0, h // grp, h % 2]),
                _dot_nt(qp, kvl_ref[0, h // grp, h % 2]))

    s_next = scores(0)
    acc = None
    for h in range(ATTN_HEADS):
        s_ctx, s_lat = s_next
        if h + 1 < ATTN_HEADS:
            s_next = scores(h + 1)
        m = jnp.maximum(jnp.max(s_ctx, axis=-1, keepdims=True), jnp.max(s_lat, axis=-1, keepdims=True))
        stages = [(s_ctx, kvc_ref, 0, s_ctx.shape[-1])]
        stages += [(s_lat, kvl_ref, k0, ATTN_KEY_CHUNK) for k0 in range(0, n_lat, ATTN_KEY_CHUNK)]
        o = denom = None
        for s, v_ref, k0, width in stages:
            e = jnp.exp(s[:, k0:k0 + width] - m)
            part = _dot(e.astype(BF16), v_ref[0, h // grp, 2 + h % 2, k0:k0 + width, :])
            part_sum = jnp.sum(e, axis=-1, keepdims=True)
            o = part if o is None else o + part
            denom = part_sum if denom is None else denom + part_sum
        o = o / denom
        if h % 2 == 0:
            acc = o
        else:
            o_ref[0, :, (h // 2) * LANES:(h // 2 + 1) * LANES] = (acc + o).astype(BF16)


def _attn_call(q, kv_ctx, kv_lat, n_ctx):
    bsz, seq, _ = q.shape
    tq = Q_TILE
    assert seq % tq == 0 and seq % ATTN_KEY_CHUNK == 0
    return pl.pallas_call(
        _attn_kernel,
        grid=(bsz, seq // tq),
        in_specs=[pl.BlockSpec((1, tq, ATTN_Q_W), lambda b, i: (b, i, 0)),
                  pl.BlockSpec((1, ATTN_KV_HEADS, 4, n_ctx, LANES), lambda b, i: (0, 0, 0, b, 0)),
                  pl.BlockSpec((1, ATTN_KV_HEADS, 4, seq, LANES), lambda b, i: (b, 0, 0, 0, 0))],
        out_specs=pl.BlockSpec((1, tq, ATTN_Q_W), lambda b, i: (b, i, 0)),
        out_shape=jax.ShapeDtypeStruct((bsz, seq, ATTN_Q_W), BF16),
        compiler_params=_params("arbitrary", "arbitrary"),
        name="attn",
    )(q, kv_ctx, kv_lat)


def _unit_tri_inverses_minus_eye(mats, level_masks):
    strip_masks, last_join, left, right = level_masks
    n = mats[0].shape[0]
    half = n // 2

    def to_strip(m):
        return m[0:half, :] * left + m[half:, :] * right

    def from_strip(s):
        return jnp.concatenate([s * left, s * right], axis=0)

    mats = [a.astype(BF16) for a in mats]
    strips = [to_strip(a) for a in mats]
    corrs = [-(s * strip_masks[0]) for s in strips]
    for joins in strip_masks[1:]:
        xs = [s * joins for s in strips]
        ys = [x.astype(F32) + _dot(corr, from_strip(x)) for corr, x in zip(corrs, xs)]
        corrs = [corr - (y + _dot(y.astype(BF16), from_strip(corr))).astype(BF16) for corr, y in zip(corrs, ys)]
    corrs = [from_strip(corr) for corr in corrs]
    xs = [a * last_join for a in mats]
    ys = [x.astype(F32) + _dot(corr, x) for corr, x in zip(corrs, xs)]
    return [corr - (y + _dot(y.astype(BF16), corr)).astype(BF16) for corr, y in zip(corrs, ys)]


def _tri_level_masks(n):
    ij_xor = lax.broadcasted_iota(I32, (n, n), 0) ^ lax.broadcasted_iota(I32, (n, n), 1)
    half = n // 2
    lane = lax.broadcasted_iota(I32, (1, n), 1)
    left = jnp.where(lane < half, 1.0, 0.0).astype(BF16)
    right = jnp.where(lane >= half, 1.0, 0.0).astype(BF16)
    strip_masks, m = [], 1
    while m < half:
        full = jnp.where((ij_xor >= m) & (ij_xor < 2 * m), 1.0, 0.0).astype(BF16)
        strip_masks.append(full[0:half, :] * left + full[half:, :] * right)
        m *= 2
    last_join = jnp.where((ij_xor >= half) & (ij_xor < n), 1.0, 0.0).astype(BF16)
    return strip_masks, last_join, left, right


def _dn_kernel(dqc_ref, dkc_ref, dvc_ref, dql_ref, dkl_ref, dvl_ref, cq_ref, ck_ref, cv_ref,
               gc_ref, gl_ref, gtc_ref, gtl_ref, o_ref,
               xq_s, xk_s, xv_s, q_s, k_s, v_s, kk_s, qk_s, g_s, gt_s, mq_s, n_s, op_s, gl_s):
    qkv_ctx_refs = (dqc_ref, dkc_ref, dvc_ref)
    qkv_lat_refs = (dql_ref, dkl_ref, dvl_ref)
    conv_refs = (cq_ref, ck_ref, cv_ref)
    x_scr = (xq_s, xk_s, xv_s)
    qkv_scr = (q_s, k_s, v_s)
    c = DN_CHUNK
    n_ctx = gc_ref.shape[1]
    ctx_chunks = n_ctx // c
    lat_chunks = gl_ref.shape[1] // c
    dk = DN_HEAD_DIM
    ii = lax.broadcasted_iota(I32, (c, c), 0)
    jj = lax.broadcasted_iota(I32, (c, c), 1)
    level_masks = _tri_level_masks(c)
    lane = lax.broadcasted_iota(I32, (1, LANES), 1)

    g_s[0:n_ctx, :] = gc_ref[0]
    g_s[n_ctx:, :] = gl_ref[0]
    gt_s[:, 0:ctx_chunks, :] = gtc_ref[0]
    gt_s[:, ctx_chunks:, :] = gtl_ref[0]

    for hh in range(DN_HEADS_PER_STEP):
        _dn_prepare_head(hh, pl.program_id(1) * DN_HEADS_PER_STEP + hh, ii, jj, level_masks, lane,
                         qkv_ctx_refs, qkv_lat_refs, conv_refs, g_s, gt_s,
                         x_scr, qkv_scr, kk_s, qk_s, mq_s, n_s, op_s, gl_s)

    def scan(first_chunk, count, emit, states):
        rows = dk + c if emit else dk

        def step(t, st):
            chains = [(hh, d, first_chunk + t if d == 0 else first_chunk + count - 1 - t)
                      for hh in range(DN_HEADS_PER_STEP) for d in range(2)]
            prods = [_dot(mq_s[hh, d, i, 0:rows, :], s.astype(BF16)) for (hh, d, i), s in zip(chains, st)]
            new = []
            for (hh, d, i), s, r in zip(chains, st, prods):
                if emit:
                    ro = pl.multiple_of((i - ctx_chunks) * c, c)
                    o_ref[0, pl.ds(ro, c), hh * dk:(hh + 1) * dk] += r[dk:dk + c] + op_s[hh, d, i]
                new.append(s * gl_s[hh, d, i][0:1, :] + r[0:dk] + n_s[hh, d, i])
            return tuple(new)
        return lax.fori_loop(0, count, step, states)

    o_ref[...] = jnp.zeros(o_ref.shape, F32)
    zero_state = jnp.zeros((dk, dk), F32)
    states = scan(0, ctx_chunks, False, (zero_state,) * (2 * DN_HEADS_PER_STEP))
    scan(ctx_chunks, lat_chunks, True, states)


def _dn_prepare_head(hh, head, ii, jj, level_masks, lane, qkv_ctx_refs, qkv_lat_refs, conv_refs, g_s, gt_s,
                     x_scr, qkv_scr, kk_s, qk_s, mq_s, n_s, op_s, gl_s):
    c = DN_CHUNK
    n_ctx = qkv_ctx_refs[0].shape[1]
    seq = qkv_lat_refs[0].shape[1]
    n_chunks = (n_ctx + seq) // c
    ctx_chunks = n_ctx // c
    pad = SUBLANES
    dk = DN_HEAD_DIM
    half = DN_CONV_W // 2
    cols = slice(hh * dk, (hh + 1) * dk)
    xq_s, xk_s, xv_s = x_scr
    q_s, k_s, v_s = qkv_scr
    cq_ref, ck_ref, cv_ref = conv_refs

    zeros_pad = jnp.zeros((pad, dk), F32)
    lat0 = 2 * pad + n_ctx
    for src_ctx, src_lat, dst in zip(qkv_ctx_refs, qkv_lat_refs, x_scr):
        dst[0:pad, :] = zeros_pad
        dst[pad:pad + n_ctx, :] = src_ctx[0, :, cols]
        dst[pad + n_ctx:lat0, :] = zeros_pad
        dst[lat0:lat0 + seq, :] = src_lat[0, :, cols]
        dst[lat0 + seq:lat0 + seq + pad, :] = zeros_pad

    def conv_chunk(i, carry):
        r0 = pl.multiple_of(i * c, c)
        rp = r0 + jnp.where(i >= ctx_chunks, 2 * pad, pad)

        def conv(x_s, cw_ref):
            acc = None
            for s in range(DN_CONV_W):
                term = x_s[pl.ds(rp - half + s, c), :] * cw_ref[s:s + 1, cols]
                acc = term if acc is None else acc + term
            return _silu(acc)

        qc = conv(xq_s, cq_ref)
        kc = conv(xk_s, ck_ref)
        vc = conv(xv_s, cv_ref)
        qc = qc * lax.rsqrt(jnp.sum(qc * qc, axis=-1, keepdims=True) + EPS) * (dk ** -0.5)
        kc = kc * lax.rsqrt(jnp.sum(kc * kc, axis=-1, keepdims=True) + EPS)
        q_s[pl.ds(r0, c), :] = qc
        k_s[pl.ds(r0, c), :] = kc
        v_s[pl.ds(r0, c), :] = vc
        kb = kc.astype(BF16)
        kk_s[i] = _dot_nt(kb, kb)
        qk_s[i] = _dot_nt(qc.astype(BF16), kb)
        return carry

    lax.fori_loop(0, n_chunks, conv_chunk, 0, unroll=DN_CONV_UNROLL)

    def local_group(grp, carry):
        chains = []
        for k in range(DN_GROUP):
            i = grp * DN_GROUP + k
            r0 = pl.multiple_of(i * c, c)
            gates = g_s[pl.ds(r0, c), :]
            kc = k_s[pl.ds(r0, c), :]
            qc = q_s[pl.ds(r0, c), :]
            vc = v_s[pl.ds(r0, c), :]
            kk = kk_s[i]
            qk = qk_s[i]
            kt = kc.T
            for d in range(2):
                beta_col = jnp.sum(jnp.where(lane == d * DN_HEADS + head, gates, 0.0), axis=-1, keepdims=True)
                g_col = jnp.sum(jnp.where(lane == N_GATES // 2 + d * DN_HEADS + head, gates, 0.0),
                                axis=-1, keepdims=True)
                g_row = gt_s[pl.ds(N_GATES // 2 + d * DN_HEADS + head, 1), pl.ds(i, 1), :].reshape(1, c)
                incl = (jj <= ii) if d == 0 else (jj >= ii)
                strict = (jj < ii) if d == 0 else (jj > ii)
                incl_t = (ii <= jj) if d == 0 else (ii >= jj)
                gcum_col = jnp.sum(jnp.where(incl, g_row, 0.0), axis=1, keepdims=True)
                gcum_row = jnp.sum(jnp.where(incl_t, g_col, 0.0), axis=0, keepdims=True)
                total = jnp.sum(g_row, axis=1, keepdims=True)
                decay = jnp.exp(jnp.where(incl, gcum_col - gcum_row, -jnp.inf))
                e_col = jnp.exp(gcum_col)
                kdt = (kt * jnp.exp(total - gcum_row)).astype(BF16)
                qkm = (qk * decay).astype(BF16)
                chains.append(dict(
                    d=d, i=i,
                    a=jnp.where(strict, beta_col * kk * decay, 0.0),
                    rhs=jnp.concatenate([vc * beta_col, kc * (beta_col * e_col)], axis=1),
                    lhs=jnp.concatenate([kdt, qkm], axis=0),
                    qd=qc * e_col,
                    g_last=jnp.exp(total)))
        corrs = _unit_tri_inverses_minus_eye([ch["a"] for ch in chains], level_masks)
        sols = [ch["rhs"] + _dot(corr, ch["rhs"].astype(BF16)) for ch, corr in zip(chains, corrs)]
        prods = [_dot(ch["lhs"], sol.astype(BF16)) for ch, sol in zip(chains, sols)]
        for ch, r in zip(chains, prods):
            d, i = ch["d"], ch["i"]
            mq_s[hh, d, i, 0:dk, :] = (-r[0:dk, dk:2 * dk]).astype(BF16)
            mq_s[hh, d, i, dk:dk + c, :] = (ch["qd"] - r[dk:dk + c, dk:2 * dk]).astype(BF16)
            n_s[hh, d, i] = r[0:dk, 0:dk]
            op_s[hh, d, i] = r[dk:dk + c, 0:dk]
            gl_s[hh, d, i] = jnp.broadcast_to(ch["g_last"], (SUBLANES, dk))
        return carry

    lax.fori_loop(0, n_chunks // DN_GROUP, local_group, 0)


def _dn_call(dn_ctx, dn_lat, conv_w, g_ctx, g_lat, gt_ctx, gt_lat, n_ctx):
    bsz, seq, _ = dn_lat.shape
    tot = n_ctx + seq
    c = DN_CHUNK
    assert n_ctx % c == 0 and seq % c == 0
    n_chunks = tot // c
    ctx_chunks = n_ctx // c
    dk = DN_HEAD_DIM
    gtc4 = gt_ctx.reshape(N_GATES, bsz, ctx_chunks, c).transpose(1, 0, 2, 3)
    gtl4 = gt_lat.reshape(bsz, N_GATES, seq // c, c)
    hp = DN_HEADS_PER_STEP
    steps = DN_HEADS // hp
    col_ctx = lambda off: pl.BlockSpec((1, n_ctx, hp * dk), lambda b, h: (0, b, off + h))
    col_lat = lambda off: pl.BlockSpec((1, seq, hp * dk), lambda b, h: (b, 0, off + h))
    cw = lambda off: pl.BlockSpec((DN_CONV_W, hp * dk), lambda b, h: (0, off + h))
    padded = tot + 3 * SUBLANES
    assert n_chunks % DN_GROUP == 0 and n_chunks % DN_CONV_UNROLL == 0 and DN_HEADS % hp == 0
    return pl.pallas_call(
        _dn_kernel,
        grid=(bsz, steps),
        in_specs=[col_ctx(0), col_ctx(steps), col_ctx(2 * steps),
                  col_lat(0), col_lat(steps), col_lat(2 * steps),
                  cw(0), cw(steps), cw(2 * steps),
                  pl.BlockSpec((1, n_ctx, LANES), lambda b, h: (0, b, 0)),
                  pl.BlockSpec((1, seq, LANES), lambda b, h: (b, 0, 0)),
                  pl.BlockSpec((1, N_GATES, ctx_chunks, c), lambda b, h: (b, 0, 0, 0)),
                  pl.BlockSpec((1, N_GATES, seq // c, c), lambda b, h: (b, 0, 0, 0))],
        out_specs=pl.BlockSpec((1, seq, hp * dk), lambda b, h: (b, 0, h)),
        out_shape=jax.ShapeDtypeStruct((bsz, seq, DN_W), F32),
        scratch_shapes=[pltpu.VMEM((padded, dk), F32)] * 3
        + [pltpu.VMEM((tot, dk), F32)] * 3
        + [pltpu.VMEM((n_chunks, c, c), F32)] * 2
        + [pltpu.VMEM((tot, LANES), F32),
           pltpu.VMEM((N_GATES, n_chunks, c), F32)]
        + [pltpu.VMEM((hp, 2, n_chunks, dk + c, dk), BF16),
           pltpu.VMEM((hp, 2, n_chunks, dk, dk), F32),
           pltpu.VMEM((hp, 2, n_chunks, c, dk), F32),
           pltpu.VMEM((hp, 2, n_chunks, SUBLANES, dk), F32)],
        compiler_params=_params("arbitrary", "arbitrary"),
        name="dn",
    )(dn_ctx, dn_ctx, dn_ctx, dn_lat, dn_lat, dn_lat, conv_w, conv_w, conv_w, g_ctx, g_lat, gtc4, gtl4)


def _outproj_kernel(x_ref, att_ref, o_ref, z_ref, mod_ref, onw_ref, wo_ref, n2_ref, rw_ref,
                    x1_ref, h2_ref, afft_ref):
    d = x_ref.shape[-1]
    b = pl.program_id(0)
    gate1 = mod_ref[pl.ds(b, 1), 2 * d:3 * d]
    shift2 = mod_ref[pl.ds(b, 1), 3 * d:4 * d]
    scale2 = mod_ref[pl.ds(b, 1), 4 * d:5 * d]
    lane = lax.broadcasted_iota(I32, (1, LANES), 1)
    n_rows = x_ref.shape[1]

    def mixed(rows):
        parts = [att_ref[0, rows, :]]
        for h in range(DN_HEADS):
            sl = slice(h * DN_HEAD_DIM, (h + 1) * DN_HEAD_DIM)
            oh = o_ref[0, rows, sl]
            on = oh * lax.rsqrt(jnp.mean(oh * oh, axis=-1, keepdims=True) + EPS) * onw_ref[...]
            parts.append((on * _silu(z_ref[0, rows, sl])).astype(BF16))
        return _dot(jnp.concatenate(parts, axis=1), wo_ref[...])

    def finish(rows, y):
        x1 = x_ref[0, rows, :] + gate1 * y
        x1_ref[0, rows, :] = x1
        hn = x1 * lax.rsqrt(jnp.mean(x1 * x1, axis=-1, keepdims=True) + EPS) * n2_ref[...]
        h2 = (hn * (1.0 + scale2) + shift2).astype(BF16)
        h2_ref[0, rows, :] = h2
        logits = _dot(h2, rw_ref[...])
        logits = jnp.where(lane < N_EXPERTS, logits, -jnp.inf)
        e = jnp.exp(logits - jnp.max(logits, axis=-1, keepdims=True))
        aff = e / jnp.sum(e, axis=-1, keepdims=True)
        afft_ref[0, :, rows] = aff.T[0:N_EXPERTS, :]

    chunks = [slice(r0, r0 + OUT_ROW_CHUNK) for r0 in range(0, n_rows, OUT_ROW_CHUNK)]
    y_next = mixed(chunks[0])
    for r, rows in enumerate(chunks):
        y = y_next
        if r + 1 < len(chunks):
            y_next = mixed(chunks[r + 1])
        finish(rows, y)


def _outproj_call(x, att, o_dn, z, mod, onw, wo, n2, rw):
    bsz, seq, d = x.shape
    t = OUT_TILE
    assert seq % t == 0
    full = lambda shape: pl.BlockSpec(shape, lambda b, j: (0,) * len(shape))
    return pl.pallas_call(
        _outproj_kernel,
        grid=(bsz, seq // t),
        in_specs=[pl.BlockSpec((1, t, d), lambda b, j: (b, j, 0)),
                  pl.BlockSpec((1, t, ATTN_Q_W), lambda b, j: (b, j, 0)),
                  pl.BlockSpec((1, t, DN_W), lambda b, j: (b, j, 0)),
                  pl.BlockSpec((1, t, DN_W), lambda b, j: (b, j, 0)),
                  full((MOD_ROWS, mod.shape[1])),
                  full((1, DN_HEAD_DIM)),
                  full(wo.shape),
                  full((1, d)),
                  full((d, LANES))],
        out_specs=[pl.BlockSpec((1, t, d), lambda b, j: (b, j, 0)),
                   pl.BlockSpec((1, t, d), lambda b, j: (b, j, 0)),
                   pl.BlockSpec((1, N_EXPERTS, t), lambda b, j: (b, 0, j))],
        out_shape=[jax.ShapeDtypeStruct((bsz, seq, d), F32),
                   jax.ShapeDtypeStruct((bsz, seq, d), BF16),
                   jax.ShapeDtypeStruct((bsz, N_EXPERTS, seq), F32)],
        compiler_params=_params("arbitrary", "arbitrary"),
        name="outproj",
    )(x, att, o_dn, z, mod, onw, wo, n2, rw)


def _route_kernel(cap, afft_ref, slot_ref, gate_ref, tok_ref, bounds_ref):
    n_b, n_exp, n_t = afft_ref.shape
    n_e = n_b * n_exp
    aff = afft_ref[...].reshape(n_e, n_t)

    def enough(cand):
        return jnp.sum(jnp.where(aff >= cand, 1.0, 0.0), axis=-1, keepdims=True) >= cap

    tiny = 2.0 ** F32_MIN_EXP
    cur = jnp.full((n_e, 1), tiny, F32)
    any_normal = enough(cur)
    shift = 1 << (-F32_MIN_EXP).bit_length()
    while shift > 1:
        shift //= 2
        cand = cur * (2.0 ** shift)
        cur = jnp.where(enough(cand), cand, cur)

    def refine(_, state):
        cur, step = state
        cand = cur + step
        return jnp.where(enough(cand), cand, cur), step * 0.5

    cur, _ = lax.fori_loop(0, F32_MANTISSA_BITS, refine, (cur, cur * 0.5))
    thr = jnp.where(any_normal, cur, 0.0)
    need = cap - jnp.sum(jnp.where(aff > thr, 1.0, 0.0), axis=-1, keepdims=True)

    upper = (lax.broadcasted_iota(I32, (LANES, LANES), 0) < lax.broadcasted_iota(I32, (LANES, LANES), 1))
    upper = jnp.where(upper, 1.0, 0.0).astype(BF16)
    run = jnp.zeros((2 * n_e, 1), F32)
    lane = lax.broadcasted_iota(I32, (1, LANES), 1)
    bounds = jnp.zeros((n_e, LANES), F32)
    per_block = ROUTE_BLOCK // LANES
    for blk in range(n_t // LANES):
        sl = slice(blk * LANES, (blk + 1) * LANES)
        gt = aff[:, sl] > thr
        eq = aff[:, sl] == thr
        x = jnp.concatenate([jnp.where(gt, 1.0, 0.0), jnp.where(eq, 1.0, 0.0)], axis=0)
        cum = _dot(x.astype(BF16), upper) + run
        run = run + jnp.sum(x, axis=-1, keepdims=True)
        cum_gt = cum[0:n_e]
        cum_eq = cum[n_e:2 * n_e]
        sel = gt | (eq & (cum_eq < need))
        slot = jnp.where(sel, cum_gt + jnp.minimum(cum_eq, need), -1.0)
        slot_ref[:, :, sl] = slot.astype(I32).reshape(n_b, n_exp, LANES)
        gate_ref[:, :, sl] = jnp.where(sel, aff[:, sl], 0.0).reshape(n_b, n_exp, LANES)
        if (blk + 1) % per_block == 0:
            taken = run[0:n_e] + jnp.minimum(run[n_e:2 * n_e], need)
            bounds = jnp.where(lane == (blk + 1) // per_block, taken, bounds)
    bounds_ref[...] = bounds.astype(I32).reshape(n_b, n_exp, LANES)

    for b in range(n_b):
        stacked = jnp.concatenate([slot_ref[b].astype(F32), gate_ref[b],
                                   jnp.zeros((LANES - 2 * n_exp, n_t), F32)], axis=0)
        tok_ref[b] = stacked.T


def _route_call(afft, cap):
    bsz, n_e, n_t = afft.shape
    row = pl.BlockSpec((bsz, n_e, n_t), lambda i: (0, 0, 0))
    return pl.pallas_call(
        functools.partial(_route_kernel, cap),
        grid=(1,),
        in_specs=[row],
        out_specs=[row, row, pl.BlockSpec((bsz, n_t, LANES), lambda i: (0, 0, 0)),
                   pl.BlockSpec((bsz, n_e, LANES), lambda i: (0, 0, 0))],
        out_shape=[jax.ShapeDtypeStruct((bsz, n_e, n_t), I32),
                   jax.ShapeDtypeStruct((bsz, n_e, n_t), F32),
                   jax.ShapeDtypeStruct((bsz, n_t, LANES), F32),
                   jax.ShapeDtypeStruct((bsz, n_e, LANES), I32)],
        compiler_params=_params("arbitrary"),
        name="route",
    )(afft)


def _window_starts(cnt_ref, b, tb, n_e, cap):
    starts, fits = [], None
    for e in range(n_e):
        lo = cnt_ref[b, tb * n_e + e]
        hi = cnt_ref[b, (tb + 1) * n_e + e]
        start = jnp.minimum((lo // BF16_ROWS) * BF16_ROWS, cap - SLOT_WINDOW)
        ok = hi - start <= SLOT_WINDOW
        starts.append(pl.multiple_of(start, BF16_ROWS))
        fits = ok if fits is None else jnp.logical_and(fits, ok)
    return starts, fits


def _gather_kernel(cap, cnt_ref, slot_ref, gate_ref, h_ref, xg_ref, gs_ref):
    b = pl.program_id(0)
    n_e = slot_ref.shape[1]
    n_blk = slot_ref.shape[2]
    xg_ref[...] = jnp.zeros(xg_ref.shape, BF16)
    gs_ref[...] = jnp.zeros(gs_ref.shape, F32)

    def token_block(tb, carry):
        h_blk = h_ref[0, pl.ds(pl.multiple_of(tb * ROUTE_BLOCK, ROUTE_BLOCK), ROUTE_BLOCK), :]
        starts, fits = _window_starts(cnt_ref, b, tb, n_e, cap)

        def accumulate(window, first_rows):
            j = lax.broadcasted_iota(I32, (window, 1), 0)
            hits = [slot_ref[0, e, pl.ds(tb, 1), :] == first_rows[e] + j for e in range(n_e)]
            onehot = jnp.concatenate([jnp.where(hit, 1.0, 0.0).astype(BF16) for hit in hits], axis=0)
            rows = _dot(onehot, h_blk)
            for e in range(n_e):
                dst = pl.ds(first_rows[e], window)
                xg_ref[e, 0, dst, :] += rows[e * window:(e + 1) * window].astype(BF16)
                gate = gate_ref[0, e, pl.ds(tb, 1), :]
                gs_ref[e, 0, dst, :] += jnp.sum(jnp.where(hits[e], gate, 0.0), axis=-1, keepdims=True)

        @pl.when(fits)
        def _():
            accumulate(SLOT_WINDOW, starts)

        @pl.when(jnp.logical_not(fits))
        def _():
            accumulate(cap, [0] * n_e)

        return carry

    lax.fori_loop(0, n_blk, token_block, 0)


def _gather_call(cnt, slot, gate, h2, cap):
    bsz, n_e, n_t = slot.shape
    d = h2.shape[-1]
    n_blk = n_t // ROUTE_BLOCK
    blocked = lambda a: a.reshape(bsz, n_e, n_blk, ROUTE_BLOCK)
    row = pl.BlockSpec((1, n_e, n_blk, ROUTE_BLOCK), lambda b, cnt: (b, 0, 0, 0))
    return pl.pallas_call(
        functools.partial(_gather_kernel, cap),
        grid_spec=pltpu.PrefetchScalarGridSpec(
            num_scalar_prefetch=1,
            grid=(bsz,),
            in_specs=[row, row, pl.BlockSpec((1, n_t, d), lambda b, cnt: (b, 0, 0))],
            out_specs=[pl.BlockSpec((n_e, 1, cap, d), lambda b, cnt: (0, b, 0, 0)),
                       pl.BlockSpec((n_e, 1, cap, 1), lambda b, cnt: (0, b, 0, 0))]),
        out_shape=[jax.ShapeDtypeStruct((n_e, bsz, cap, d), BF16),
                   jax.ShapeDtypeStruct((n_e, bsz, cap, 1), F32)],
        compiler_params=_params("arbitrary"),
        name="gather",
    )(cnt, blocked(slot), blocked(gate), h2)


def _ffn_kernel(xg_ref, gs_ref, wg_ref, wu_ref, wd_ref, y_ref):
    bsz, cap, d = xg_ref.shape[1:]
    per = FFN_ROW_CHUNK // cap
    n_chunks = bsz // per
    wg = wg_ref[0].astype(BF16)
    wu = wu_ref[0].astype(BF16)
    wd = wd_ref[0].astype(BF16)

    def up(r):
        x = xg_ref[0, r * per:(r + 1) * per].reshape(per * cap, d)
        return _dot(x, wg), _dot(x, wu)

    nxt = up(0)
    for r in range(n_chunks):
        g, u = nxt
        if r + 1 < n_chunks:
            nxt = up(r + 1)
        y = _dot((_silu(g) * u).astype(BF16), wd) * gs_ref[0, r * per:(r + 1) * per].reshape(per * cap, 1)
        y_ref[0, r * per:(r + 1) * per] = y.astype(BF16).reshape(per, cap, d)


def _ffn_call(xg, gs, w_gate, w_up, w_down):
    n_e, bsz, cap, d = xg.shape
    ff = w_gate.shape[-1]
    rows = pl.BlockSpec((1, bsz, cap, d), lambda e: (e, 0, 0, 0))
    return pl.pallas_call(
        _ffn_kernel,
        grid=(n_e,),
        in_specs=[rows,
                  pl.BlockSpec((1, bsz, cap, 1), lambda e: (e, 0, 0, 0)),
                  pl.BlockSpec((1, d, ff), lambda e: (e, 0, 0)),
                  pl.BlockSpec((1, d, ff), lambda e: (e, 0, 0)),
                  pl.BlockSpec((1, ff, d), lambda e: (e, 0, 0))],
        out_specs=rows,
        out_shape=jax.ShapeDtypeStruct((n_e, bsz, cap, d), BF16),
        compiler_params=_params("arbitrary"),
        name="ffn",
    )(xg, gs, w_gate, w_up, w_down)


def _combine_kernel(cap, cnt_ref, x1_ref, tok_ref, y_ref, mod_ref, o_ref):
    d = x1_ref.shape[-1]
    n_e = y_ref.shape[0]
    b = pl.program_id(0)
    tb = pl.program_id(1)
    gate2 = mod_ref[pl.ds(b, 1), 5 * d:6 * d]
    tok = tok_ref[0]
    starts, fits = _window_starts(cnt_ref, b, tb, n_e, cap)

    @pl.when(fits)
    def _():
        lane = lax.broadcasted_iota(I32, (1, LANES), 1)
        first = lane < SLOT_WINDOW
        groups = []
        for e in range(0, n_e, 2):
            slot = jnp.where(first, tok[:, e:e + 1], tok[:, e + 1:e + 2])
            target = jnp.where(first, starts[e] + lane, starts[e + 1] + lane - SLOT_WINDOW).astype(F32)
            groups.append(jnp.where(slot == target, 1.0, 0.0).astype(BF16))
        onehot = jnp.concatenate(groups, axis=1)
        rows = jnp.concatenate([y_ref[e, 0, pl.ds(starts[e], SLOT_WINDOW), :] for e in range(n_e)], axis=0)
        o_ref[0] = x1_ref[0] + gate2 * _dot(onehot, rows)

    @pl.when(jnp.logical_not(fits))
    def _():
        j = lax.broadcasted_iota(I32, (1, cap), 1).astype(F32)
        acc = None
        for e in range(n_e):
            onehot = jnp.where(tok[:, e:e + 1] == j, 1.0, 0.0).astype(BF16)
            part = _dot(onehot, y_ref[e, 0])
            acc = part if acc is None else acc + part
        o_ref[0] = x1_ref[0] + gate2 * acc


def _combine_call(cnt, x1, tok, y, mod, cap):
    bsz, seq, d = x1.shape
    n_e = y.shape[0]
    t = ROUTE_BLOCK
    assert 2 * SLOT_WINDOW == LANES and n_e % 2 == 0
    return pl.pallas_call(
        functools.partial(_combine_kernel, cap),
        grid_spec=pltpu.PrefetchScalarGridSpec(
            num_scalar_prefetch=1,
            grid=(bsz, seq // t),
            in_specs=[pl.BlockSpec((1, t, d), lambda b, j, cnt: (b, j, 0)),
                      pl.BlockSpec((1, t, LANES), lambda b, j, cnt: (b, j, 0)),
                      pl.BlockSpec((n_e, 1, cap, d), lambda b, j, cnt: (0, b, 0, 0)),
                      pl.BlockSpec((MOD_ROWS, mod.shape[1]), lambda b, j, cnt: (0, 0))],
            out_specs=pl.BlockSpec((1, t, d), lambda b, j, cnt: (b, j, 0))),
        out_shape=jax.ShapeDtypeStruct((bsz, seq, d), F32),
        compiler_params=_params("arbitrary", "arbitrary"),
        name="combine",
    )(cnt, x1, tok, y, mod)


def _rope_tables(seq):
    m = ATTN_HEAD_DIM // 4
    pos = jnp.arange(seq, dtype=jnp.int32)
    rows = (pos // GRID_W).astype(F32)
    cols = (pos % GRID_W).astype(F32)
    freqs = ROPE_BASE ** (-jnp.arange(m, dtype=F32) / m)
    ang_r = rows[:, None] * freqs[None, :]
    ang_c = cols[:, None] * freqs[None, :]
    cos_h = jnp.concatenate([jnp.cos(ang_r), jnp.cos(ang_r), jnp.cos(ang_c), jnp.cos(ang_c)], axis=-1)
    sin_h = jnp.concatenate([-jnp.sin(ang_r), jnp.sin(ang_r), -jnp.sin(ang_c), jnp.sin(ang_c)], axis=-1)
    reps = LANES // ATTN_HEAD_DIM
    return jnp.tile(cos_h, (1, reps)), jnp.tile(sin_h, (1, reps))


def _lane_row(values, offset):
    return jnp.zeros((1, LANES), F32).at[0, offset:offset + values.shape[0]].set(values.astype(F32))


def kernel(x, c, ctx, c_ctx, w_mod, b_mod, norm1_w, norm2_w, w_in, q_norm_w, k_norm_w, conv_w, a_log, dt_bias,
           o_norm_w, w_out, router_w, w_gate, w_up, w_down):
    bsz, seq, d = x.shape
    n_ctx = ctx.shape[1]
    assert w_mod.shape[0] == 1, "single layer: the last layer's context outputs are never consumed"
    assert bsz < MOD_ROWS and N_EXPERTS == router_w.shape[-1]
    cap = EC_CAPACITY_FACTOR * seq // N_EXPERTS

    cc = jnp.concatenate([c, c_ctx[None, :], jnp.zeros((MOD_ROWS - bsz - 1, d), F32)], axis=0)
    mod = _mod_call(cc, w_mod[0], b_mod[0][None, :])

    w_pad = jnp.concatenate([w_in[0].astype(BF16), jnp.zeros((d, LANES - N_GATES), BF16)], axis=1)
    seg = np.arange(MXU_DIM) // ATTN_HEAD_DIM
    bd = jnp.asarray(seg[:, None] == seg[None, :], BF16)
    qkw = jnp.concatenate([jnp.tile(q_norm_w[0], ATTN_HEADS), jnp.tile(k_norm_w[0], ATTN_KV_HEADS)])[None, :]
    alog_l = _lane_row(a_log[0].reshape(-1), N_GATES // 2)
    dtb_l = _lane_row(dt_bias[0].reshape(-1), N_GATES // 2)
    shared = (mod, norm1_w[0][None, :], w_pad, bd, qkw, alog_l, dtb_l)
    kv_lat, dn_lat, g_lat, gt_lat, q, z = _inproj_call(x, bsz, *shared, rope=_rope_tables(seq))
    kv_ctx, dn_ctx, g_ctx, gt_ctx = _inproj_call(ctx.reshape(1, bsz * n_ctx, d), bsz, *shared)

    att = _attn_call(q, kv_ctx, kv_lat, n_ctx)
    o_dn = _dn_call(dn_ctx, dn_lat, conv_w[0], g_ctx, g_lat, gt_ctx, gt_lat, n_ctx)

    rw = jnp.pad(router_w[0], ((0, 0), (0, LANES - N_EXPERTS))).astype(BF16)
    x1, h2, afft = _outproj_call(x, att, o_dn, z, mod, o_norm_w[0][None, :], w_out[0].astype(BF16),
                                 norm2_w[0][None, :], rw)

    slot, gate, tok, bounds = _route_call(afft, cap)
    n_blk = seq // ROUTE_BLOCK
    cnt = bounds[:, :, 0:n_blk + 1].transpose(0, 2, 1).reshape(bsz, (n_blk + 1) * N_EXPERTS)
    xg, gs = _gather_call(cnt, slot, gate, h2, cap)
    y = _ffn_call(xg, gs, w_gate[0], w_up[0], w_down[0])
    return _combine_call(cnt, x1, tok, y, mod, cap)
```

```python
import functools
import math

import numpy as np
import jax
import jax.numpy as jnp
from jax import lax
from jax.experimental import pallas as pl
from jax.experimental.pallas import tpu as pltpu

F32 = jnp.float32
BF16 = jnp.bfloat16
I32 = jnp.int32

GRID_W = 64
EPS = 1e-6
ATTN_HEADS = 8
ATTN_KV_HEADS = 2
ATTN_HEAD_DIM = 64
ROPE_BASE = 10000.0
DN_HEADS = 4
DN_HEAD_DIM = 128
DN_CONV_W = 5
N_EXPERTS = 16
EC_CAPACITY_FACTOR = 2

ATTN_Q_W = ATTN_HEADS * ATTN_HEAD_DIM
ATTN_KV_W = ATTN_KV_HEADS * ATTN_HEAD_DIM
DN_W = DN_HEADS * DN_HEAD_DIM
QK_W = ATTN_Q_W + ATTN_KV_W
ATT_W = ATTN_Q_W + 2 * ATTN_KV_W
N_GATES = 4 * DN_HEADS

LANES = 128
SUBLANES = 8
MXU_DIM = 256
VMEM_LIMIT = 56 * 1024 * 1024
F32_MIN_EXP = -126
F32_MANTISSA_BITS = 23

TOK_TILE = 512
IN_ROW_CHUNK = 128
Q_TILE = 512
ATTN_KEY_CHUNK = 1024
OUT_TILE = 512
OUT_ROW_CHUNK = 128
DN_CHUNK = 128
DN_GROUP = 9
DN_CONV_UNROLL = 6
DN_HEADS_PER_STEP = 2
ROUTE_BLOCK = 256
SLOT_WINDOW = 64
BF16_ROWS = 16
FFN_ROW_CHUNK = 512
WCAST_ROWS = 256
MOD_ROWS = 16


def _params(*sem):
    return pltpu.CompilerParams(dimension_semantics=sem, vmem_limit_bytes=VMEM_LIMIT)


def _silu(v):
    half = 0.5 * v
    return half + half * jnp.tanh(half)


def _dot(a, b):
    return jnp.dot(a, b, preferred_element_type=F32)


def _dot_nt(a, b):
    return lax.dot_general(a, b, (((1,), (1,)), ((), ())), preferred_element_type=F32)


def _wcast_kernel(w_ref, o_ref):
    n_in = w_ref.shape[1]
    whole = (n_in // LANES) * LANES
    o_ref[:, 0:whole] = w_ref[:, 0:whole].astype(BF16)
    o_ref[:, whole:] = jnp.zeros((o_ref.shape[0], o_ref.shape[1] - whole), BF16)
    o_ref[:, whole:n_in] = w_ref[:, whole:n_in].astype(BF16)


def _wcast_call(w):
    d, n_in = w.shape
    n_out = -(-n_in // LANES) * LANES
    rows = WCAST_ROWS
    assert d % rows == 0
    return pl.pallas_call(
        _wcast_kernel,
        grid=(d // rows,),
        in_specs=[pl.BlockSpec((rows, n_in), lambda i: (i, 0))],
        out_specs=pl.BlockSpec((rows, n_out), lambda i: (i, 0)),
        out_shape=jax.ShapeDtypeStruct((d, n_out), BF16),
        compiler_params=_params("arbitrary"),
        name="wcast",
    )(w)


def _mod_kernel(c_ref, w_ref, b_ref, o_ref):
    sc = _silu(c_ref[...]).astype(BF16)
    o_ref[...] = _dot(sc, w_ref[...].astype(BF16)) + b_ref[...]


def _mod_call(cc, w_mod, b_mod):
    d, n = w_mod.shape
    return pl.pallas_call(
        _mod_kernel,
        grid=(n // d,),
        in_specs=[pl.BlockSpec((MOD_ROWS, d), lambda i: (0, 0)),
                  pl.BlockSpec((d, d), lambda i: (0, i)),
                  pl.BlockSpec((1, d), lambda i: (0, i))],
        out_specs=pl.BlockSpec((MOD_ROWS, d), lambda i: (0, i)),
        out_shape=jax.ShapeDtypeStruct((MOD_ROWS, n), F32),
        compiler_params=_params("arbitrary"),
        name="mod",
    )(cc, w_mod, b_mod)


def _inproj_kernel(latent, ctx_row, x_ref, mod_ref, n1_ref, w_ref, bd_ref, qkw_ref, alog_ref, dtb_ref, *refs):
    if latent:
        cos_ref, sin_ref, kv_ref, dn_ref, g_ref, gt_ref, q_ref, z_ref = refs
    else:
        kv_ref, dn_ref, g_ref, gt_ref = refs
    d = x_ref.shape[-1]
    n_rows = x_ref.shape[1]
    row = pl.program_id(0) if latent else ctx_row
    shift = mod_ref[pl.ds(row, 1), 0:d]
    scale = mod_ref[pl.ds(row, 1), d:2 * d]
    lane = lax.broadcasted_iota(I32, (1, LANES), 1)
    k_blk = ATTN_Q_W // LANES

    def project(rows):
        xin = x_ref[0, rows, :]
        hn = xin * lax.rsqrt(jnp.mean(xin * xin, axis=-1, keepdims=True) + EPS) * n1_ref[...]
        h = (hn * (1.0 + scale) + shift).astype(BF16)
        return _dot(h, w_ref[...])

    def finish(rows, p):
        qk = p[:, 0:QK_W]
        sq = qk * qk
        hi = sq.astype(BF16)
        lo = (sq - hi.astype(F32)).astype(BF16)
        seg_w = bd_ref.shape[0]
        ms = []
        for c0 in range(0, QK_W, seg_w):
            w = min(seg_w, QK_W - c0)
            ones = bd_ref[0:w, 0:w]
            ms.append(_dot(hi[:, c0:c0 + w], ones) + _dot(lo[:, c0:c0 + w], ones))
        ms = jnp.concatenate(ms, axis=1) * (1.0 / ATTN_HEAD_DIM)
        qkn = qk * lax.rsqrt(ms + EPS) * qkw_ref[...]

        if latent:
            first_half = (lane % 32) < 16
            cos = cos_ref[rows, :]
            sin = sin_ref[rows, :]
            rot = []
            for i in range(QK_W // LANES):
                blk = qkn[:, i * LANES:(i + 1) * LANES]
                partner = jnp.where(first_half, pltpu.roll(blk, LANES - 16, axis=1), pltpu.roll(blk, 16, axis=1))
                rot.append(blk * cos + partner * sin)
            for i in range(k_blk):
                q_ref[0, rows, i * LANES:(i + 1) * LANES] = (rot[i] * (ATTN_HEAD_DIM ** -0.5)).astype(BF16)
            k2 = rot[k_blk]
            z_ref[0, rows, :] = p[:, ATT_W + 3 * DN_W:ATT_W + 4 * DN_W]
        else:
            k2 = qkn[:, k_blk * LANES:(k_blk + 1) * LANES]

        low = lane < ATTN_HEAD_DIM
        v2 = p[:, QK_W:ATT_W]
        for src, base in ((k2, 0), (v2, 2)):
            swapped = pltpu.roll(src, ATTN_HEAD_DIM, axis=1)
            kv_ref[0, 0, base + 0, rows, :] = jnp.where(low, src, 0.0).astype(BF16)
            kv_ref[0, 0, base + 1, rows, :] = jnp.where(low, 0.0, swapped).astype(BF16)
            kv_ref[0, 1, base + 0, rows, :] = jnp.where(low, swapped, 0.0).astype(BF16)
            kv_ref[0, 1, base + 1, rows, :] = jnp.where(low, 0.0, src).astype(BF16)

        dn_ref[0, rows, :] = p[:, ATT_W:ATT_W + 3 * DN_W]

        gp = p[:, ATT_W + 4 * DN_W:]
        beta = jax.nn.sigmoid(gp)
        xa = gp + dtb_ref[...]
        softplus = jnp.maximum(xa, 0.0) + jnp.log1p(jnp.exp(-jnp.abs(xa)))
        decay = -jnp.exp(alog_ref[...]) * softplus
        gates = jnp.where(lane < N_GATES // 2, beta, jnp.where(lane < N_GATES, decay, 0.0))
        g_ref[0, rows, :] = gates
        gt_ref[0, :, rows] = gates.T[0:N_GATES, :]

    chunks = [slice(r0, r0 + IN_ROW_CHUNK) for r0 in range(0, n_rows, IN_ROW_CHUNK)]
    p_next = project(chunks[0])
    for r, rows in enumerate(chunks):
        p = p_next
        if r + 1 < len(chunks):
            p_next = project(chunks[r + 1])
        finish(rows, p)


def _inproj_call(rows3, ctx_row, mod, n1, w_pad, bd, qkw, alog_l, dtb_l, rope=None):
    latent = rope is not None
    grp, rows, d = rows3.shape
    t = TOK_TILE
    assert rows % t == 0
    nw = w_pad.shape[1]
    full = lambda shape: pl.BlockSpec(shape, lambda b, j: (0,) * len(shape))
    tile = lambda width: pl.BlockSpec((1, t, width), lambda b, j: (b, j, 0))
    in_specs = [tile(d), full((MOD_ROWS, mod.shape[1])), full((1, d)), full((d, nw)), full(bd.shape),
                full((1, QK_W)), full((1, LANES)), full((1, LANES))]
    out_specs = [pl.BlockSpec((1, ATTN_KV_HEADS, 4, t, LANES), lambda b, j: (b, 0, 0, j, 0)),
                 tile(3 * DN_W), tile(LANES), pl.BlockSpec((1, N_GATES, t), lambda b, j: (b, 0, j))]
    out_shape = [jax.ShapeDtypeStruct((grp, ATTN_KV_HEADS, 4, rows, LANES), BF16),
                 jax.ShapeDtypeStruct((grp, rows, 3 * DN_W), F32),
                 jax.ShapeDtypeStruct((grp, rows, LANES), F32),
                 jax.ShapeDtypeStruct((grp, N_GATES, rows), F32)]
    args = [rows3, mod, n1, w_pad, bd, qkw, alog_l, dtb_l]
    if latent:
        in_specs += [pl.BlockSpec((t, LANES), lambda b, j: (j, 0))] * 2
        out_specs += [tile(ATTN_Q_W), tile(DN_W)]
        out_shape += [jax.ShapeDtypeStruct((grp, rows, ATTN_Q_W), BF16), jax.ShapeDtypeStruct((grp, rows, DN_W), F32)]
        args += list(rope)
    return pl.pallas_call(
        functools.partial(_inproj_kernel, latent, ctx_row),
        grid=(grp, rows // t),
        in_specs=in_specs,
        out_specs=out_specs,
        out_shape=out_shape,
        compiler_params=_params("arbitrary", "arbitrary"),
        name="inproj_lat" if latent else "inproj_ctx",
    )(*args)


def _attn_kernel(q_ref, kvc_ref, kvl_ref, o_ref):
    grp = ATTN_HEADS // ATTN_KV_HEADS
    n_lat = kvl_ref.shape[3]

    def scores(h):
        qp = q_ref[0, :, (h // 2) * LANES:(h // 2 + 1) * LANES]
        return (_dot_nt(qp, kvc_ref[0, h // grp, h % 2]),
                _dot_nt(qp, kvl_ref[0, h // grp, h % 2]))

    s_next = scores(0)
    acc = None
    for h in range(ATTN_HEADS):
        s_ctx, s_lat = s_next
        if h + 1 < ATTN_HEADS:
            s_next = scores(h + 1)
        m = jnp.maximum(jnp.max(s_ctx, axis=-1, keepdims=True), jnp.max(s_lat, axis=-1, keepdims=True))
        stages = [(s_ctx, kvc_ref, 0, s_ctx.shape[-1])]
        stages += [(s_lat, kvl_ref, k0, ATTN_KEY_CHUNK) for k0 in range(0, n_lat, ATTN_KEY_CHUNK)]
        o = denom = None
        for s, v_ref, k0, width in stages:
            e = jnp.exp(s[:, k0:k0 + width] - m)
            part = _dot(e.astype(BF16), v_ref[0, h // grp, 2 + h % 2, k0:k0 + width, :])
            part_sum = jnp.sum(e, axis=-1, keepdims=True)
            o = part if o is None else o + part
            denom = part_sum if denom is None else denom + part_sum
        o = o / denom
        if h % 2 == 0:
            acc = o
        else:
            o_ref[0, :, (h // 2) * LANES:(h // 2 + 1) * LANES] = (acc + o).astype(BF16)


def _attn_call(q, kv_ctx, kv_lat, n_ctx):
    bsz, seq, _ = q.shape
    tq = Q_TILE
    assert seq % tq == 0 and seq % ATTN_KEY_CHUNK == 0
    return pl.pallas_call(
        _attn_kernel,
        grid=(bsz, seq // tq),
        in_specs=[pl.BlockSpec((1, tq, ATTN_Q_W), lambda b, i: (b, i, 0)),
                  pl.BlockSpec((1, ATTN_KV_HEADS, 4, n_ctx, LANES), lambda b, i: (0, 0, 0, b, 0)),
                  pl.BlockSpec((1, ATTN_KV_HEADS, 4, seq, LANES), lambda b, i: (b, 0, 0, 0, 0))],
        out_specs=pl.BlockSpec((1, tq, ATTN_Q_W), lambda b, i: (b, i, 0)),
        out_shape=jax.ShapeDtypeStruct((bsz, seq, ATTN_Q_W), BF16),
        compiler_params=_params("arbitrary", "arbitrary"),
        name="attn",
    )(q, kv_ctx, kv_lat)


def _unit_tri_inverses_minus_eye(mats, level_masks):
    strip_masks, last_join, left, right = level_masks
    n = mats[0].shape[0]
    half = n // 2

    def to_strip(m):
        return m[0:half, :] * left + m[half:, :] * right

    def from_strip(s):
        return jnp.concatenate([s * left, s * right], axis=0)

    mats = [a.astype(BF16) for a in mats]
    strips = [to_strip(a) for a in mats]
    corrs = [-(s * strip_masks[0]) for s in strips]
    for joins in strip_masks[1:]:
        xs = [s * joins for s in strips]
        ys = [x.astype(F32) + _dot(corr, from_strip(x)) for corr, x in zip(corrs, xs)]
        corrs = [corr - (y + _dot(y.astype(BF16), from_strip(corr))).astype(BF16) for corr, y in zip(corrs, ys)]
    corrs = [from_strip(corr) for corr in corrs]
    xs = [a * last_join for a in mats]
    ys = [x.astype(F32) + _dot(corr, x) for corr, x in zip(corrs, xs)]
    return [corr - (y + _dot(y.astype(BF16), corr)).astype(BF16) for corr, y in zip(corrs, ys)]


def _tri_level_masks(n):
    ij_xor = lax.broadcasted_iota(I32, (n, n), 0) ^ lax.broadcasted_iota(I32, (n, n), 1)
    half = n // 2
    lane = lax.broadcasted_iota(I32, (1, n), 1)
    left = jnp.where(lane < half, 1.0, 0.0).astype(BF16)
    right = jnp.where(lane >= half, 1.0, 0.0).astype(BF16)
    strip_masks, m = [], 1
    while m < half:
        full = jnp.where((ij_xor >= m) & (ij_xor < 2 * m), 1.0, 0.0).astype(BF16)
        strip_masks.append(full[0:half, :] * left + full[half:, :] * right)
        m *= 2
    last_join = jnp.where((ij_xor >= half) & (ij_xor < n), 1.0, 0.0).astype(BF16)
    return strip_masks, last_join, left, right


def _dn_kernel(dqc_ref, dkc_ref, dvc_ref, dql_ref, dkl_ref, dvl_ref, cq_ref, ck_ref, cv_ref,
               gc_ref, gl_ref, gtc_ref, gtl_ref, o_ref,
               xq_s, xk_s, xv_s, q_s, k_s, v_s, kk_s, qk_s, g_s, gt_s, mq_s, n_s, op_s, gl_s):
    qkv_ctx_refs = (dqc_ref, dkc_ref, dvc_ref)
    qkv_lat_refs = (dql_ref, dkl_ref, dvl_ref)
    conv_refs = (cq_ref, ck_ref, cv_ref)
    x_scr = (xq_s, xk_s, xv_s)
    qkv_scr = (q_s, k_s, v_s)
    c = DN_CHUNK
    n_ctx = gc_ref.shape[1]
    ctx_chunks = n_ctx // c
    lat_chunks = gl_ref.shape[1] // c
    dk = DN_HEAD_DIM
    ii = lax.broadcasted_iota(I32, (c, c), 0)
    jj = lax.broadcasted_iota(I32, (c, c), 1)
    level_masks = _tri_level_masks(c)
    lane = lax.broadcasted_iota(I32, (1, LANES), 1)

    g_s[0:n_ctx, :] = gc_ref[0]
    g_s[n_ctx:, :] = gl_ref[0]
    gt_s[:, 0:ctx_chunks, :] = gtc_ref[0]
    gt_s[:, ctx_chunks:, :] = gtl_ref[0]

    for hh in range(DN_HEADS_PER_STEP):
        _dn_prepare_head(hh, pl.program_id(1) * DN_HEADS_PER_STEP + hh, ii, jj, level_masks, lane,
                         qkv_ctx_refs, qkv_lat_refs, conv_refs, g_s, gt_s,
                         x_scr, qkv_scr, kk_s, qk_s, mq_s, n_s, op_s, gl_s)

    def scan(first_chunk, count, emit, states):
        rows = dk + c if emit else dk

        def step(t, st):
            chains = [(hh, d, first_chunk + t if d == 0 else first_chunk + count - 1 - t)
                      for hh in range(DN_HEADS_PER_STEP) for d in range(2)]
            prods = [_dot(mq_s[hh, d, i, 0:rows, :], s.astype(BF16)) for (hh, d, i), s in zip(chains, st)]
            new = []
            for (hh, d, i), s, r in zip(chains, st, prods):
                if emit:
                    ro = pl.multiple_of((i - ctx_chunks) * c, c)
                    o_ref[0, pl.ds(ro, c), hh * dk:(hh + 1) * dk] += r[dk:dk + c] + op_s[hh, d, i]
                new.append(s * gl_s[hh, d, i][0:1, :] + r[0:dk] + n_s[hh, d, i])
            return tuple(new)
        return lax.fori_loop(0, count, step, states)

    o_ref[...] = jnp.zeros(o_ref.shape, F32)
    zero_state = jnp.zeros((dk, dk), F32)
    states = scan(0, ctx_chunks, False, (zero_state,) * (2 * DN_HEADS_PER_STEP))
    scan(ctx_chunks, lat_chunks, True, states)


def _dn_prepare_head(hh, head, ii, jj, level_masks, lane, qkv_ctx_refs, qkv_lat_refs, conv_refs, g_s, gt_s,
                     x_scr, qkv_scr, kk_s, qk_s, mq_s, n_s, op_s, gl_s):
    c = DN_CHUNK
    n_ctx = qkv_ctx_refs[0].shape[1]
    seq = qkv_lat_refs[0].shape[1]
    n_chunks = (n_ctx + seq) // c
    ctx_chunks = n_ctx // c
    pad = SUBLANES
    dk = DN_HEAD_DIM
    half = DN_CONV_W // 2
    cols = slice(hh * dk, (hh + 1) * dk)
    xq_s, xk_s, xv_s = x_scr
    q_s, k_s, v_s = qkv_scr
    cq_ref, ck_ref, cv_ref = conv_refs

    zeros_pad = jnp.zeros((pad, dk), F32)
    lat0 = 2 * pad + n_ctx
    for src_ctx, src_lat, dst in zip(qkv_ctx_refs, qkv_lat_refs, x_scr):
        dst[0:pad, :] = zeros_pad
        dst[pad:pad + n_ctx, :] = src_ctx[0, :, cols]
        dst[pad + n_ctx:lat0, :] = zeros_pad
        dst[lat0:lat0 + seq, :] = src_lat[0, :, cols]
        dst[lat0 + seq:lat0 + seq + pad, :] = zeros_pad

    def conv_chunk(i, carry):
        r0 = pl.multiple_of(i * c, c)
        rp = r0 + jnp.where(i >= ctx_chunks, 2 * pad, pad)

        def conv(x_s, cw_ref):
            acc = None
            for s in range(DN_CONV_W):
                term = x_s[pl.ds(rp - half + s, c), :] * cw_ref[s:s + 1, cols]
                acc = term if acc is None else acc + term
            return _silu(acc)

        qc = conv(xq_s, cq_ref)
        kc = conv(xk_s, ck_ref)
        vc = conv(xv_s, cv_ref)
        qc = qc * lax.rsqrt(jnp.sum(qc * qc, axis=-1, keepdims=True) + EPS) * (dk ** -0.5)
        kc = kc * lax.rsqrt(jnp.sum(kc * kc, axis=-1, keepdims=True) + EPS)
        q_s[pl.ds(r0, c), :] = qc
        k_s[pl.ds(r0, c), :] = kc
        v_s[pl.ds(r0, c), :] = vc
        kb = kc.astype(BF16)
        kk_s[i] = _dot_nt(kb, kb)
        qk_s[i] = _dot_nt(qc.astype(BF16), kb)
        return carry

    lax.fori_loop(0, n_chunks, conv_chunk, 0, unroll=DN_CONV_UNROLL)

    def local_group(grp, carry):
        chains = []
        for k in range(DN_GROUP):
            i = grp * DN_GROUP + k
            r0 = pl.multiple_of(i * c, c)
            gates = g_s[pl.ds(r0, c), :]
            kc = k_s[pl.ds(r0, c), :]
            qc = q_s[pl.ds(r0, c), :]
            vc = v_s[pl.ds(r0, c), :]
            kk = kk_s[i]
            qk = qk_s[i]
            kt = kc.T
            for d in range(2):
                beta_col = jnp.sum(jnp.where(lane == d * DN_HEADS + head, gates, 0.0), axis=-1, keepdims=True)
                g_col = jnp.sum(jnp.where(lane == N_GATES // 2 + d * DN_HEADS + head, gates, 0.0),
                                axis=-1, keepdims=True)
                g_row = gt_s[pl.ds(N_GATES // 2 + d * DN_HEADS + head, 1), pl.ds(i, 1), :].reshape(1, c)
                incl = (jj <= ii) if d == 0 else (jj >= ii)
                strict = (jj < ii) if d == 0 else (jj > ii)
                incl_t = (ii <= jj) if d == 0 else (ii >= jj)
                gcum_col = jnp.sum(jnp.where(incl, g_row, 0.0), axis=1, keepdims=True)
                gcum_row = jnp.sum(jnp.where(incl_t, g_col, 0.0), axis=0, keepdims=True)
                total = jnp.sum(g_row, axis=1, keepdims=True)
                decay = jnp.exp(jnp.where(incl, gcum_col - gcum_row, -jnp.inf))
                e_col = jnp.exp(gcum_col)
                kdt = (kt * jnp.exp(total - gcum_row)).astype(BF16)
                qkm = (qk * decay).astype(BF16)
                chains.append(dict(
                    d=d, i=i,
                    a=jnp.where(strict, beta_col * kk * decay, 0.0),
                    rhs=jnp.concatenate([vc * beta_col, kc * (beta_col * e_col)], axis=1),
                    lhs=jnp.concatenate([kdt, qkm], axis=0),
                    qd=qc * e_col,
                    g_last=jnp.exp(total)))
        corrs = _unit_tri_inverses_minus_eye([ch["a"] for ch in chains], level_masks)
        sols = [ch["rhs"] + _dot(corr, ch["rhs"].astype(BF16)) for ch, corr in zip(chains, corrs)]
        prods = [_dot(ch["lhs"], sol.astype(BF16)) for ch, sol in zip(chains, sols)]
        for ch, r in zip(chains, prods):
            d, i = ch["d"], ch["i"]
            mq_s[hh, d, i, 0:dk, :] = (-r[0:dk, dk:2 * dk]).astype(BF16)
            mq_s[hh, d, i, dk:dk + c, :] = (ch["qd"] - r[dk:dk + c, dk:2 * dk]).astype(BF16)
            n_s[hh, d, i] = r[0:dk, 0:dk]
            op_s[hh, d, i] = r[dk:dk + c, 0:dk]
            gl_s[hh, d, i] = jnp.broadcast_to(ch["g_last"], (SUBLANES, dk))
        return carry

    lax.fori_loop(0, n_chunks // DN_GROUP, local_group, 0)


def _dn_call(dn_ctx, dn_lat, conv_w, g_ctx, g_lat, gt_ctx, gt_lat, n_ctx):
    bsz, seq, _ = dn_lat.shape
    tot = n_ctx + seq
    c = DN_CHUNK
    assert n_ctx % c == 0 and seq % c == 0
    n_chunks = tot // c
    ctx_chunks = n_ctx // c
    dk = DN_HEAD_DIM
    gtc4 = gt_ctx.reshape(N_GATES, bsz, ctx_chunks, c).transpose(1, 0, 2, 3)
    gtl4 = gt_lat.reshape(bsz, N_GATES, seq // c, c)
    hp = DN_HEADS_PER_STEP
    steps = DN_HEADS // hp
    col_ctx = lambda off: pl.BlockSpec((1, n_ctx, hp * dk), lambda b, h: (0, b, off + h))
    col_lat = lambda off: pl.BlockSpec((1, seq, hp * dk), lambda b, h: (b, 0, off + h))
    cw = lambda off: pl.BlockSpec((DN_CONV_W, hp * dk), lambda b, h: (0, off + h))
    padded = tot + 3 * SUBLANES
    assert n_chunks % DN_GROUP == 0 and n_chunks % DN_CONV_UNROLL == 0 and DN_HEADS % hp == 0
    return pl.pallas_call(
        _dn_kernel,
        grid=(bsz, steps),
        in_specs=[col_ctx(0), col_ctx(steps), col_ctx(2 * steps),
                  col_lat(0), col_lat(steps), col_lat(2 * steps),
                  cw(0), cw(steps), cw(2 * steps),
                  pl.BlockSpec((1, n_ctx, LANES), lambda b, h: (0, b, 0)),
                  pl.BlockSpec((1, seq, LANES), lambda b, h: (b, 0, 0)),
                  pl.BlockSpec((1, N_GATES, ctx_chunks, c), lambda b, h: (b, 0, 0, 0)),
                  pl.BlockSpec((1, N_GATES, seq // c, c), lambda b, h: (b, 0, 0, 0))],
        out_specs=pl.BlockSpec((1, seq, hp * dk), lambda b, h: (b, 0, h)),
        out_shape=jax.ShapeDtypeStruct((bsz, seq, DN_W), F32),
        scratch_shapes=[pltpu.VMEM((padded, dk), F32)] * 3
        + [pltpu.VMEM((tot, dk), F32)] * 3
        + [pltpu.VMEM((n_chunks, c, c), F32)] * 2
        + [pltpu.VMEM((tot, LANES), F32),
           pltpu.VMEM((N_GATES, n_chunks, c), F32)]
        + [pltpu.VMEM((hp, 2, n_chunks, dk + c, dk), BF16),
           pltpu.VMEM((hp, 2, n_chunks, dk, dk), F32),
           pltpu.VMEM((hp, 2, n_chunks, c, dk), F32),
           pltpu.VMEM((hp, 2, n_chunks, SUBLANES, dk), F32)],
        compiler_params=_params("arbitrary", "arbitrary"),
        name="dn",
    )(dn_ctx, dn_ctx, dn_ctx, dn_lat, dn_lat, dn_lat, conv_w, conv_w, conv_w, g_ctx, g_lat, gtc4, gtl4)


def _outproj_kernel(x_ref, att_ref, o_ref, z_ref, mod_ref, onw_ref, wo_ref, n2_ref, rw_ref,
                    x1_ref, h2_ref, afft_ref):
    d = x_ref.shape[-1]
    b = pl.program_id(0)
    gate1 = mod_ref[pl.ds(b, 1), 2 * d:3 * d]
    shift2 = mod_ref[pl.ds(b, 1), 3 * d:4 * d]
    scale2 = mod_ref[pl.ds(b, 1), 4 * d:5 * d]
    lane = lax.broadcasted_iota(I32, (1, LANES), 1)
    n_rows = x_ref.shape[1]

    def mixed(rows):
        parts = [att_ref[0, rows, :]]
        for h in range(DN_HEADS):
            sl = slice(h * DN_HEAD_DIM, (h + 1) * DN_HEAD_DIM)
            oh = o_ref[0, rows, sl]
            on = oh * lax.rsqrt(jnp.mean(oh * oh, axis=-1, keepdims=True) + EPS) * onw_ref[...]
            parts.append((on * _silu(z_ref[0, rows, sl])).astype(BF16))
        return _dot(jnp.concatenate(parts, axis=1), wo_ref[...])

    def finish(rows, y):
        x1 = x_ref[0, rows, :] + gate1 * y
        x1_ref[0, rows, :] = x1
        hn = x1 * lax.rsqrt(jnp.mean(x1 * x1, axis=-1, keepdims=True) + EPS) * n2_ref[...]
        h2 = (hn * (1.0 + scale2) + shift2).astype(BF16)
        h2_ref[0, rows, :] = h2
        logits = _dot(h2, rw_ref[...])
        logits = jnp.where(lane < N_EXPERTS, logits, -jnp.inf)
        e = jnp.exp(logits - jnp.max(logits, axis=-1, keepdims=True))
        aff = e / jnp.sum(e, axis=-1, keepdims=True)
        afft_ref[0, :, rows] = aff.T[0:N_EXPERTS, :]

    chunks = [slice(r0, r0 + OUT_ROW_CHUNK) for r0 in range(0, n_rows, OUT_ROW_CHUNK)]
    y_next = mixed(chunks[0])
    for r, rows in enumerate(chunks):
        y = y_next
        if r + 1 < len(chunks):
            y_next = mixed(chunks[r + 1])
        finish(rows, y)


def _outproj_call(x, att, o_dn, z, mod, onw, wo, n2, rw):
    bsz, seq, d = x.shape
    t = OUT_TILE
    assert seq % t == 0
    full = lambda shape: pl.BlockSpec(shape, lambda b, j: (0,) * len(shape))
    return pl.pallas_call(
        _outproj_kernel,
        grid=(bsz, seq // t),
        in_specs=[pl.BlockSpec((1, t, d), lambda b, j: (b, j, 0)),
                  pl.BlockSpec((1, t, ATTN_Q_W), lambda b, j: (b, j, 0)),
                  pl.BlockSpec((1, t, DN_W), lambda b, j: (b, j, 0)),
                  pl.BlockSpec((1, t, DN_W), lambda b, j: (b, j, 0)),
                  full((MOD_ROWS, mod.shape[1])),
                  full((1, DN_HEAD_DIM)),
                  full(wo.shape),
                  full((1, d)),
                  full((d, LANES))],
        out_specs=[pl.BlockSpec((1, t, d), lambda b, j: (b, j, 0)),
                   pl.BlockSpec((1, t, d), lambda b, j: (b, j, 0)),
                   pl.BlockSpec((1, N_EXPERTS, t), lambda b, j: (b, 0, j))],
        out_shape=[jax.ShapeDtypeStruct((bsz, seq, d), F32),
                   jax.ShapeDtypeStruct((bsz, seq, d), BF16),
                   jax.ShapeDtypeStruct((bsz, N_EXPERTS, seq), F32)],
        compiler_params=_params("arbitrary", "arbitrary"),
        name="outproj",
    )(x, att, o_dn, z, mod, onw, wo, n2, rw)


def _route_kernel(cap, afft_ref, slot_ref, gate_ref, tok_ref, bounds_ref):
    n_b, n_exp, n_t = afft_ref.shape
    n_e = n_b * n_exp
    aff = afft_ref[...].reshape(n_e, n_t)

    def enough(cand):
        return jnp.sum(jnp.where(aff >= cand, 1.0, 0.0), axis=-1, keepdims=True) >= cap

    tiny = 2.0 ** F32_MIN_EXP
    cur = jnp.full((n_e, 1), tiny, F32)
    any_normal = enough(cur)
    shift = 1 << (-F32_MIN_EXP).bit_length()
    while shift > 1:
        shift //= 2
        cand = cur * (2.0 ** shift)
        cur = jnp.where(enough(cand), cand, cur)

    def refine(_, state):
        cur, step = state
        cand = cur + step
        return jnp.where(enough(cand), cand, cur), step * 0.5

    cur, _ = lax.fori_loop(0, F32_MANTISSA_BITS, refine, (cur, cur * 0.5))
    thr = jnp.where(any_normal, cur, 0.0)
    need = cap - jnp.sum(jnp.where(aff > thr, 1.0, 0.0), axis=-1, keepdims=True)

    upper = (lax.broadcasted_iota(I32, (LANES, LANES), 0) < lax.broadcasted_iota(I32, (LANES, LANES), 1))
    upper = jnp.where(upper, 1.0, 0.0).astype(BF16)
    run = jnp.zeros((2 * n_e, 1), F32)
    lane = lax.broadcasted_iota(I32, (1, LANES), 1)
    bounds = jnp.zeros((n_e, LANES), F32)
    per_block = ROUTE_BLOCK // LANES
    for blk in range(n_t // LANES):
        sl = slice(blk * LANES, (blk + 1) * LANES)
        gt = aff[:, sl] > thr
        eq = aff[:, sl] == thr
        x = jnp.concatenate([jnp.where(gt, 1.0, 0.0), jnp.where(eq, 1.0, 0.0)], axis=0)
        cum = _dot(x.astype(BF16), upper) + run
        run = run + jnp.sum(x, axis=-1, keepdims=True)
        cum_gt = cum[0:n_e]
        cum_eq = cum[n_e:2 * n_e]
        sel = gt | (eq & (cum_eq < need))
        slot = jnp.where(sel, cum_gt + jnp.minimum(cum_eq, need), -1.0)
        slot_ref[:, :, sl] = slot.astype(I32).reshape(n_b, n_exp, LANES)
        gate_ref[:, :, sl] = jnp.where(sel, aff[:, sl], 0.0).reshape(n_b, n_exp, LANES)
        if (blk + 1) % per_block == 0:
            taken = run[0:n_e] + jnp.minimum(run[n_e:2 * n_e], need)
            bounds = jnp.where(lane == (blk + 1) // per_block, taken, bounds)
    bounds_ref[...] = bounds.astype(I32).reshape(n_b, n_exp, LANES)

    for b in range(n_b):
        stacked = jnp.concatenate([slot_ref[b].astype(F32), gate_ref[b],
                                   jnp.zeros((LANES - 2 * n_exp, n_t), F32)], axis=0)
        tok_ref[b] = stacked.T


def _route_call(afft, cap):
    bsz, n_e, n_t = afft.shape
    row = pl.BlockSpec((bsz, n_e, n_t), lambda i: (0, 0, 0))
    return pl.pallas_call(
        functools.partial(_route_kernel, cap),
        grid=(1,),
        in_specs=[row],
        out_specs=[row, row, pl.BlockSpec((bsz, n_t, LANES), lambda i: (0, 0, 0)),
                   pl.BlockSpec((bsz, n_e, LANES), lambda i: (0, 0, 0))],
        out_shape=[jax.ShapeDtypeStruct((bsz, n_e, n_t), I32),
                   jax.ShapeDtypeStruct((bsz, n_e, n_t), F32),
                   jax.ShapeDtypeStruct((bsz, n_t, LANES), F32),
                   jax.ShapeDtypeStruct((bsz, n_e, LANES), I32)],
        compiler_params=_params("arbitrary"),
        name="route",
    )(afft)


def _window_starts(cnt_ref, b, tb, n_e, cap):
    starts, fits = [], None
    for e in range(n_e):
        lo = cnt_ref[b, tb * n_e + e]
        hi = cnt_ref[b, (tb + 1) * n_e + e]
        start = jnp.minimum((lo // BF16_ROWS) * BF16_ROWS, cap - SLOT_WINDOW)
        ok = hi - start <= SLOT_WINDOW
        starts.append(pl.multiple_of(start, BF16_ROWS))
        fits = ok if fits is None else jnp.logical_and(fits, ok)
    return starts, fits


def _gather_kernel(cap, cnt_ref, slot_ref, gate_ref, h_ref, xg_ref, gs_ref):
    b = pl.program_id(0)
    n_e = slot_ref.shape[1]
    n_blk = slot_ref.shape[2]
    xg_ref[...] = jnp.zeros(xg_ref.shape, BF16)
    gs_ref[...] = jnp.zeros(gs_ref.shape, F32)

    def token_block(tb, carry):
        h_blk = h_ref[0, pl.ds(pl.multiple_of(tb * ROUTE_BLOCK, ROUTE_BLOCK), ROUTE_BLOCK), :]
        starts, fits = _window_starts(cnt_ref, b, tb, n_e, cap)

        def accumulate(window, first_rows):
            j = lax.broadcasted_iota(I32, (window, 1), 0)
            hits = [slot_ref[0, e, pl.ds(tb, 1), :] == first_rows[e] + j for e in range(n_e)]
            onehot = jnp.concatenate([jnp.where(hit, 1.0, 0.0).astype(BF16) for hit in hits], axis=0)
            rows = _dot(onehot, h_blk)
            for e in range(n_e):
                dst = pl.ds(first_rows[e], window)
                xg_ref[e, 0, dst, :] += rows[e * window:(e + 1) * window].astype(BF16)
                gate = gate_ref[0, e, pl.ds(tb, 1), :]
                gs_ref[e, 0, dst, :] += jnp.sum(jnp.where(hits[e], gate, 0.0), axis=-1, keepdims=True)

        @pl.when(fits)
        def _():
            accumulate(SLOT_WINDOW, starts)

        @pl.when(jnp.logical_not(fits))
        def _():
            accumulate(cap, [0] * n_e)

        return carry

    lax.fori_loop(0, n_blk, token_block, 0)


def _gather_call(cnt, slot, gate, h2, cap):
    bsz, n_e, n_t = slot.shape
    d = h2.shape[-1]
    n_blk = n_t // ROUTE_BLOCK
    blocked = lambda a: a.reshape(bsz, n_e, n_blk, ROUTE_BLOCK)
    row = pl.BlockSpec((1, n_e, n_blk, ROUTE_BLOCK), lambda b, cnt: (b, 0, 0, 0))
    return pl.pallas_call(
        functools.partial(_gather_kernel, cap),
        grid_spec=pltpu.PrefetchScalarGridSpec(
            num_scalar_prefetch=1,
            grid=(bsz,),
            in_specs=[row, row, pl.BlockSpec((1, n_t, d), lambda b, cnt: (b, 0, 0))],
            out_specs=[pl.BlockSpec((n_e, 1, cap, d), lambda b, cnt: (0, b, 0, 0)),
                       pl.BlockSpec((n_e, 1, cap, 1), lambda b, cnt: (0, b, 0, 0))]),
        out_shape=[jax.ShapeDtypeStruct((n_e, bsz, cap, d), BF16),
                   jax.ShapeDtypeStruct((n_e, bsz, cap, 1), F32)],
        compiler_params=_params("arbitrary"),
        name="gather",
    )(cnt, blocked(slot), blocked(gate), h2)


def _ffn_kernel(xg_ref, gs_ref, wg_ref, wu_ref, wd_ref, y_ref):
    bsz, cap, d = xg_ref.shape[1:]
    per = FFN_ROW_CHUNK // cap
    n_chunks = bsz // per
    wg = wg_ref[0].astype(BF16)
    wu = wu_ref[0].astype(BF16)
    wd = wd_ref[0].astype(BF16)

    def up(r):
        x = xg_ref[0, r * per:(r + 1) * per].reshape(per * cap, d)
        return _dot(x, wg), _dot(x, wu)

    nxt = up(0)
    for r in range(n_chunks):
        g, u = nxt
        if r + 1 < n_chunks:
            nxt = up(r + 1)
        y = _dot((_silu(g) * u).astype(BF16), wd) * gs_ref[0, r * per:(r + 1) * per].reshape(per * cap, 1)
        y_ref[0, r * per:(r + 1) * per] = y.astype(BF16).reshape(per, cap, d)


def _ffn_call(xg, gs, w_gate, w_up, w_down):
    n_e, bsz, cap, d = xg.shape
    ff = w_gate.shape[-1]
    rows = pl.BlockSpec((1, bsz, cap, d), lambda e: (e, 0, 0, 0))
    return pl.pallas_call(
        _ffn_kernel,
        grid=(n_e,),
        in_specs=[rows,
                  pl.BlockSpec((1, bsz, cap, 1), lambda e: (e, 0, 0, 0)),
                  pl.BlockSpec((1, d, ff), lambda e: (e, 0, 0)),
                  pl.BlockSpec((1, d, ff), lambda e: (e, 0, 0)),
                  pl.BlockSpec((1, ff, d), lambda e: (e, 0, 0))],
        out_specs=rows,
        out_shape=jax.ShapeDtypeStruct((n_e, bsz, cap, d), BF16),
        compiler_params=_params("arbitrary"),
        name="ffn",
    )(xg, gs, w_gate, w_up, w_down)


def _combine_kernel(cap, cnt_ref, x1_ref, tok_ref, y_ref, mod_ref, o_ref):
    d = x1_ref.shape[-1]
    n_e = y_ref.shape[0]
    b = pl.program_id(0)
    tb = pl.program_id(1)
    gate2 = mod_ref[pl.ds(b, 1), 5 * d:6 * d]
    tok = tok_ref[0]
    starts, fits = _window_starts(cnt_ref, b, tb, n_e, cap)

    @pl.when(fits)
    def _():
        lane = lax.broadcasted_iota(I32, (1, LANES), 1)
        first = lane < SLOT_WINDOW
        groups = []
        for e in range(0, n_e, 2):
            slot = jnp.where(first, tok[:, e:e + 1], tok[:, e + 1:e + 2])
            target = jnp.where(first, starts[e] + lane, starts[e + 1] + lane - SLOT_WINDOW).astype(F32)
            groups.append(jnp.where(slot == target, 1.0, 0.0).astype(BF16))
        onehot = jnp.concatenate(groups, axis=1)
        rows = jnp.concatenate([y_ref[e, 0, pl.ds(starts[e], SLOT_WINDOW), :] for e in range(n_e)], axis=0)
        o_ref[0] = x1_ref[0] + gate2 * _dot(onehot, rows)

    @pl.when(jnp.logical_not(fits))
    def _():
        j = lax.broadcasted_iota(I32, (1, cap), 1).astype(F32)
        acc = None
        for e in range(n_e):
            onehot = jnp.where(tok[:, e:e + 1] == j, 1.0, 0.0).astype(BF16)
            part = _dot(onehot, y_ref[e, 0])
            acc = part if acc is None else acc + part
        o_ref[0] = x1_ref[0] + gate2 * acc


def _combine_call(cnt, x1, tok, y, mod, cap):
    bsz, seq, d = x1.shape
    n_e = y.shape[0]
    t = ROUTE_BLOCK
    assert 2 * SLOT_WINDOW == LANES and n_e % 2 == 0
    return pl.pallas_call(
        functools.partial(_combine_kernel, cap),
        grid_spec=pltpu.PrefetchScalarGridSpec(
            num_scalar_prefetch=1,
            grid=(bsz, seq // t),
            in_specs=[pl.BlockSpec((1, t, d), lambda b, j, cnt: (b, j, 0)),
                      pl.BlockSpec((1, t, LANES), lambda b, j, cnt: (b, j, 0)),
                      pl.BlockSpec((n_e, 1, cap, d), lambda b, j, cnt: (0, b, 0, 0)),
                      pl.BlockSpec((MOD_ROWS, mod.shape[1]), lambda b, j, cnt: (0, 0))],
            out_specs=pl.BlockSpec((1, t, d), lambda b, j, cnt: (b, j, 0))),
        out_shape=jax.ShapeDtypeStruct((bsz, seq, d), F32),
        compiler_params=_params("arbitrary", "arbitrary"),
        name="combine",
    )(cnt, x1, tok, y, mod)


def _rope_tables(seq):
    m = ATTN_HEAD_DIM // 4
    pos = jnp.arange(seq, dtype=jnp.int32)
    rows = (pos // GRID_W).astype(F32)
    cols = (pos % GRID_W).astype(F32)
    freqs = ROPE_BASE ** (-jnp.arange(m, dtype=F32) / m)
    ang_r = rows[:, None] * freqs[None, :]
    ang_c = cols[:, None] * freqs[None, :]
    cos_h = jnp.concatenate([jnp.cos(ang_r), jnp.cos(ang_r), jnp.cos(ang_c), jnp.cos(ang_c)], axis=-1)
    sin_h = jnp.concatenate([-jnp.sin(ang_r), jnp.sin(ang_r), -jnp.sin(ang_c), jnp.sin(ang_c)], axis=-1)
    reps = LANES // ATTN_HEAD_DIM
    return jnp.tile(cos_h, (1, reps)), jnp.tile(sin_h, (1, reps))


def _lane_row(values, offset):
    return jnp.zeros((1, LANES), F32).at[0, offset:offset + values.shape[0]].set(values.astype(F32))


def kernel(x, c, ctx, c_ctx, w_mod, b_mod, norm1_w, norm2_w, w_in, q_norm_w, k_norm_w, conv_w, a_log, dt_bias,
           o_norm_w, w_out, router_w, w_gate, w_up, w_down):
    bsz, seq, d = x.shape
    n_ctx = ctx.shape[1]
    assert w_mod.shape[0] == 1, "single layer: the last layer's context outputs are never consumed"
    assert bsz < MOD_ROWS and N_EXPERTS == router_w.shape[-1]
    cap = EC_CAPACITY_FACTOR * seq // N_EXPERTS

    cc = jnp.concatenate([c, c_ctx[None, :], jnp.zeros((MOD_ROWS - bsz - 1, d), F32)], axis=0)
    mod = _mod_call(cc, w_mod[0], b_mod[0][None, :])

    w_pad = _wcast_call(w_in[0])
    seg = np.arange(MXU_DIM) // ATTN_HEAD_DIM
    bd = jnp.asarray(seg[:, None] == seg[None, :], BF16)
    qkw = jnp.concatenate([jnp.tile(q_norm_w[0], ATTN_HEADS), jnp.tile(k_norm_w[0], ATTN_KV_HEADS)])[None, :]
    alog_l = _lane_row(a_log[0].reshape(-1), N_GATES // 2)
    dtb_l = _lane_row(dt_bias[0].reshape(-1), N_GATES // 2)
    shared = (mod, norm1_w[0][None, :], w_pad, bd, qkw, alog_l, dtb_l)
    kv_lat, dn_lat, g_lat, gt_lat, q, z = _inproj_call(x, bsz, *shared, rope=_rope_tables(seq))
    kv_ctx, dn_ctx, g_ctx, gt_ctx = _inproj_call(ctx.reshape(1, bsz * n_ctx, d), bsz, *shared)

    att = _attn_call(q, kv_ctx, kv_lat, n_ctx)
    o_dn = _dn_call(dn_ctx, dn_lat, conv_w[0], g_ctx, g_lat, gt_ctx, gt_lat, n_ctx)

    rw = jnp.pad(router_w[0], ((0, 0), (0, LANES - N_EXPERTS))).astype(BF16)
    x1, h2, afft = _outproj_call(x, att, o_dn, z, mod, o_norm_w[0][None, :], w_out[0].astype(BF16),
                                 norm2_w[0][None, :], rw)

    slot, gate, tok, bounds = _route_call(afft, cap)
    n_blk = seq // ROUTE_BLOCK
    cnt = bounds[:, :, 0:n_blk + 1].transpose(0, 2, 1).reshape(bsz, (n_blk + 1) * N_EXPERTS)
    xg, gs = _gather_call(cnt, slot, gate, h2, cap)
    y = _ffn_call(xg, gs, w_gate[0], w_up[0], w_down[0])
    return _combine_call(cnt, x1, tok, y, mod, cap)
```

```python
import functools

import numpy as np
import jax
import jax.numpy as jnp
from jax import lax
from jax.experimental import pallas as pl
from jax.experimental.pallas import tpu as pltpu

F32 = jnp.float32
BF16 = jnp.bfloat16
I32 = jnp.int32

GRID_W = 64
EPS = 1e-6
ATTN_HEADS = 8
ATTN_KV_HEADS = 2
ATTN_HEAD_DIM = 64
ROPE_BASE = 10000.0
DN_HEADS = 4
DN_HEAD_DIM = 128
DN_CONV_W = 5
N_EXPERTS = 16
EC_CAPACITY_FACTOR = 2

ATTN_Q_W = ATTN_HEADS * ATTN_HEAD_DIM
ATTN_KV_W = ATTN_KV_HEADS * ATTN_HEAD_DIM
DN_W = DN_HEADS * DN_HEAD_DIM
QK_W = ATTN_Q_W + ATTN_KV_W
ATT_W = ATTN_Q_W + 2 * ATTN_KV_W
N_GATES = 4 * DN_HEADS

LANES = 128
SUBLANES = 8
MXU_DIM = 256
VMEM_LIMIT = 56 * 1024 * 1024
F32_MIN_EXP = -126
F32_MANTISSA_BITS = 23

TOK_TILE = 512
IN_ROW_CHUNK = 128
Q_TILE = 512
ATTN_KEY_CHUNK = 1024
OUT_TILE = 512
OUT_ROW_CHUNK = 128
DN_CHUNK = 128
DN_GROUP = 9
DN_CONV_UNROLL = 9
DN_HEADS_PER_STEP = 2
ROUTE_BLOCK = 256
SLOT_WINDOW = 64
BF16_ROWS = 16
FFN_ROW_CHUNK = 512
MOD_ROWS = 16


def _params(*sem):
    return pltpu.CompilerParams(dimension_semantics=sem, vmem_limit_bytes=VMEM_LIMIT)


def _silu(v):
    half = 0.5 * v
    return half + half * jnp.tanh(half)


def _dot(a, b):
    return jnp.dot(a, b, preferred_element_type=F32)


def _dot_nt(a, b):
    return lax.dot_general(a, b, (((1,), (1,)), ((), ())), preferred_element_type=F32)


def _mod_kernel(c_ref, w_ref, b_ref, o_ref):
    sc = _silu(c_ref[...]).astype(BF16)
    o_ref[...] = _dot(sc, w_ref[...].astype(BF16)) + b_ref[...]


def _mod_call(cc, w_mod, b_mod):
    d, n = w_mod.shape
    return pl.pallas_call(
        _mod_kernel,
        grid=(n // d,),
        in_specs=[pl.BlockSpec((MOD_ROWS, d), lambda i: (0, 0)),
                  pl.BlockSpec((d, d), lambda i: (0, i)),
                  pl.BlockSpec((1, d), lambda i: (0, i))],
        out_specs=pl.BlockSpec((MOD_ROWS, d), lambda i: (0, i)),
        out_shape=jax.ShapeDtypeStruct((MOD_ROWS, n), F32),
        compiler_params=_params("arbitrary"),
        name="mod",
    )(cc, w_mod, b_mod)


def _inproj_kernel(latent, ctx_row, x_ref, mod_ref, n1_ref, w_ref, bd_ref, qkw_ref, alog_ref, dtb_ref, *refs):
    if latent:
        cos_ref, sin_ref, kv_ref, dn_ref, g_ref, gt_ref, q_ref, z_ref = refs
    else:
        kv_ref, dn_ref, g_ref, gt_ref = refs
    d = x_ref.shape[-1]
    n_rows = x_ref.shape[1]
    row = pl.program_id(0) if latent else ctx_row
    shift = mod_ref[pl.ds(row, 1), 0:d]
    scale = mod_ref[pl.ds(row, 1), d:2 * d]
    lane = lax.broadcasted_iota(I32, (1, LANES), 1)
    k_blk = ATTN_Q_W // LANES

    def project(rows):
        xin = x_ref[0, rows, :]
        hn = xin * lax.rsqrt(jnp.mean(xin * xin, axis=-1, keepdims=True) + EPS) * n1_ref[...]
        h = (hn * (1.0 + scale) + shift).astype(BF16)
        return _dot(h, w_ref[...])

    def finish(rows, p):
        qk = p[:, 0:QK_W]
        sq = qk * qk
        hi = sq.astype(BF16)
        lo = (sq - hi.astype(F32)).astype(BF16)
        seg_w = bd_ref.shape[0]
        ms = []
        for c0 in range(0, QK_W, seg_w):
            w = min(seg_w, QK_W - c0)
            ones = bd_ref[0:w, 0:w]
            ms.append(_dot(hi[:, c0:c0 + w], ones) + _dot(lo[:, c0:c0 + w], ones))
        ms = jnp.concatenate(ms, axis=1) * (1.0 / ATTN_HEAD_DIM)
        qkn = qk * lax.rsqrt(ms + EPS) * qkw_ref[...]

        if latent:
            first_half = (lane % 32) < 16
            cos = cos_ref[rows, :]
            sin = sin_ref[rows, :]
            rot = []
            for i in range(QK_W // LANES):
                blk = qkn[:, i * LANES:(i + 1) * LANES]
                partner = jnp.where(first_half, pltpu.roll(blk, LANES - 16, axis=1), pltpu.roll(blk, 16, axis=1))
                rot.append(blk * cos + partner * sin)
            for i in range(k_blk):
                q_ref[0, rows, i * LANES:(i + 1) * LANES] = (rot[i] * (ATTN_HEAD_DIM ** -0.5)).astype(BF16)
            k2 = rot[k_blk]
            z_ref[0, rows, :] = p[:, ATT_W + 3 * DN_W:ATT_W + 4 * DN_W]
        else:
            k2 = qkn[:, k_blk * LANES:(k_blk + 1) * LANES]

        low = lane < ATTN_HEAD_DIM
        v2 = p[:, QK_W:ATT_W]
        for src, base in ((k2, 0), (v2, 2)):
            swapped = pltpu.roll(src, ATTN_HEAD_DIM, axis=1)
            kv_ref[0, 0, base + 0, rows, :] = jnp.where(low, src, 0.0).astype(BF16)
            kv_ref[0, 0, base + 1, rows, :] = jnp.where(low, 0.0, swapped).astype(BF16)
            kv_ref[0, 1, base + 0, rows, :] = jnp.where(low, swapped, 0.0).astype(BF16)
            kv_ref[0, 1, base + 1, rows, :] = jnp.where(low, 0.0, src).astype(BF16)

        dn_ref[0, rows, :] = p[:, ATT_W:ATT_W + 3 * DN_W]

        gp = p[:, ATT_W + 4 * DN_W:]
        beta = jax.nn.sigmoid(gp)
        xa = gp + dtb_ref[...]
        softplus = jnp.maximum(xa, 0.0) + jnp.log1p(jnp.exp(-jnp.abs(xa)))
        decay = -jnp.exp(alog_ref[...]) * softplus
        gates = jnp.where(lane < N_GATES // 2, beta, jnp.where(lane < N_GATES, decay, 0.0))
        g_ref[0, rows, :] = gates
        gt_ref[0, :, rows] = gates.T[0:N_GATES, :]

    chunks = [slice(r0, r0 + IN_ROW_CHUNK) for r0 in range(0, n_rows, IN_ROW_CHUNK)]
    p_next = project(chunks[0])
    for r, rows in enumerate(chunks):
        p = p_next
        if r + 1 < len(chunks):
            p_next = project(chunks[r + 1])
        finish(rows, p)


def _inproj_call(rows3, ctx_row, mod, n1, w_pad, bd, qkw, alog_l, dtb_l, rope=None):
    latent = rope is not None
    grp, rows, d = rows3.shape
    t = TOK_TILE
    assert rows % t == 0
    nw = w_pad.shape[1]
    full = lambda shape: pl.BlockSpec(shape, lambda b, j: (0,) * len(shape))
    tile = lambda width: pl.BlockSpec((1, t, width), lambda b, j: (b, j, 0))
    in_specs = [tile(d), full((MOD_ROWS, mod.shape[1])), full((1, d)), full((d, nw)), full(bd.shape),
                full((1, QK_W)), full((1, LANES)), full((1, LANES))]
    out_specs = [pl.BlockSpec((1, ATTN_KV_HEADS, 4, t, LANES), lambda b, j: (b, 0, 0, j, 0)),
                 tile(3 * DN_W), tile(LANES), pl.BlockSpec((1, N_GATES, t), lambda b, j: (b, 0, j))]
    out_shape = [jax.ShapeDtypeStruct((grp, ATTN_KV_HEADS, 4, rows, LANES), BF16),
                 jax.ShapeDtypeStruct((grp, rows, 3 * DN_W), F32),
                 jax.ShapeDtypeStruct((grp, rows, LANES), F32),
                 jax.ShapeDtypeStruct((grp, N_GATES, rows), F32)]
    args = [rows3, mod, n1, w_pad, bd, qkw, alog_l, dtb_l]
    if latent:
        in_specs += [pl.BlockSpec((t, LANES), lambda b, j: (j, 0))] * 2
        out_specs += [tile(ATTN_Q_W), tile(DN_W)]
        out_shape += [jax.ShapeDtypeStruct((grp, rows, ATTN_Q_W), BF16), jax.ShapeDtypeStruct((grp, rows, DN_W), F32)]
        args += list(rope)
    return pl.pallas_call(
        functools.partial(_inproj_kernel, latent, ctx_row),
        grid=(grp, rows // t),
        in_specs=in_specs,
        out_specs=out_specs,
        out_shape=out_shape,
        compiler_params=_params("arbitrary", "arbitrary"),
        name="inproj_lat" if latent else "inproj_ctx",
    )(*args)


def _attn_kernel(q_ref, kvc_ref, kvl_ref, o_ref):
    grp = ATTN_HEADS // ATTN_KV_HEADS
    n_lat = kvl_ref.shape[3]

    def scores(h):
        qp = q_ref[0, :, (h // 2) * LANES:(h // 2 + 1) * LANES]
        return (_dot_nt(qp, kvc_ref[0, h // grp, h % 2]),
                _dot_nt(qp, kvl_ref[0, h // grp, h % 2]))

    s_next = scores(0)
    acc = None
    for h in range(ATTN_HEADS):
        s_ctx, s_lat = s_next
        if h + 1 < ATTN_HEADS:
            s_next = scores(h + 1)
        m = jnp.maximum(jnp.max(s_ctx, axis=-1, keepdims=True), jnp.max(s_lat, axis=-1, keepdims=True))
        stages = [(s_ctx, kvc_ref, 0, s_ctx.shape[-1])]
        stages += [(s_lat, kvl_ref, k0, ATTN_KEY_CHUNK) for k0 in range(0, n_lat, ATTN_KEY_CHUNK)]
        o = denom = None
        for s, v_ref, k0, width in stages:
            e = jnp.exp(s[:, k0:k0 + width] - m)
            part = _dot(e.astype(BF16), v_ref[0, h // grp, 2 + h % 2, k0:k0 + width, :])
            part_sum = jnp.sum(e, axis=-1, keepdims=True)
            o = part if o is None else o + part
            denom = part_sum if denom is None else denom + part_sum
        o = o / denom
        if h % 2 == 0:
            acc = o
        else:
            o_ref[0, :, (h // 2) * LANES:(h // 2 + 1) * LANES] = (acc + o).astype(BF16)


def _attn_call(q, kv_ctx, kv_lat, n_ctx):
    bsz, seq, _ = q.shape
    tq = Q_TILE
    assert seq % tq == 0 and seq % ATTN_KEY_CHUNK == 0
    return pl.pallas_call(
        _attn_kernel,
        grid=(bsz, seq // tq),
        in_specs=[pl.BlockSpec((1, tq, ATTN_Q_W), lambda b, i: (b, i, 0)),
                  pl.BlockSpec((1, ATTN_KV_HEADS, 4, n_ctx, LANES), lambda b, i: (0, 0, 0, b, 0)),
                  pl.BlockSpec((1, ATTN_KV_HEADS, 4, seq, LANES), lambda b, i: (b, 0, 0, 0, 0))],
        out_specs=pl.BlockSpec((1, tq, ATTN_Q_W), lambda b, i: (b, i, 0)),
        out_shape=jax.ShapeDtypeStruct((bsz, seq, ATTN_Q_W), BF16),
        compiler_params=_params("arbitrary", "arbitrary"),
        name="attn",
    )(q, kv_ctx, kv_lat)


def _unit_tri_inverses_minus_eye(mats, level_masks):
    strip_masks, last_join, left, right = level_masks
    n = mats[0].shape[0]
    half = n // 2

    def to_strip(m):
        return m[0:half, :] * left + m[half:, :] * right

    def from_strip(s):
        return jnp.concatenate([s * left, s * right], axis=0)

    mats = [a.astype(BF16) for a in mats]
    strips = [to_strip(a) for a in mats]
    corrs = [-(s * strip_masks[0]) for s in strips]
    for joins in strip_masks[1:]:
        xs = [s * joins for s in strips]
        ys = [x.astype(F32) + _dot(corr, from_strip(x)) for corr, x in zip(corrs, xs)]
        corrs = [corr - (y + _dot(y.astype(BF16), from_strip(corr))).astype(BF16) for corr, y in zip(corrs, ys)]
    corrs = [from_strip(corr) for corr in corrs]
    xs = [a * last_join for a in mats]
    ys = [x.astype(F32) + _dot(corr, x) for corr, x in zip(corrs, xs)]
    return [corr - (y + _dot(y.astype(BF16), corr)).astype(BF16) for corr, y in zip(corrs, ys)]


def _tri_level_masks(n):
    ij_xor = lax.broadcasted_iota(I32, (n, n), 0) ^ lax.broadcasted_iota(I32, (n, n), 1)
    half = n // 2
    lane = lax.broadcasted_iota(I32, (1, n), 1)
    left = jnp.where(lane < half, 1.0, 0.0).astype(BF16)
    right = jnp.where(lane >= half, 1.0, 0.0).astype(BF16)
    strip_masks, m = [], 1
    while m < half:
        full = jnp.where((ij_xor >= m) & (ij_xor < 2 * m), 1.0, 0.0).astype(BF16)
        strip_masks.append(full[0:half, :] * left + full[half:, :] * right)
        m *= 2
    last_join = jnp.where((ij_xor >= half) & (ij_xor < n), 1.0, 0.0).astype(BF16)
    return strip_masks, last_join, left, right


def _dn_kernel(dqc_ref, dkc_ref, dvc_ref, dql_ref, dkl_ref, dvl_ref, cq_ref, ck_ref, cv_ref,
               gc_ref, gl_ref, gtc_ref, gtl_ref, o_ref,
               xq_s, xk_s, xv_s, q_s, k_s, v_s, kk_s, qk_s, g_s, gt_s, mq_s, n_s, op_s, gl_s):
    qkv_ctx_refs = (dqc_ref, dkc_ref, dvc_ref)
    qkv_lat_refs = (dql_ref, dkl_ref, dvl_ref)
    conv_refs = (cq_ref, ck_ref, cv_ref)
    x_scr = (xq_s, xk_s, xv_s)
    qkv_scr = (q_s, k_s, v_s)
    c = DN_CHUNK
    n_ctx = gc_ref.shape[1]
    ctx_chunks = n_ctx // c
    lat_chunks = gl_ref.shape[1] // c
    dk = DN_HEAD_DIM
    ii = lax.broadcasted_iota(I32, (c, c), 0)
    jj = lax.broadcasted_iota(I32, (c, c), 1)
    level_masks = _tri_level_masks(c)
    lane = lax.broadcasted_iota(I32, (1, LANES), 1)

    g_s[0:n_ctx, :] = gc_ref[0]
    g_s[n_ctx:, :] = gl_ref[0]
    gt_s[:, 0:ctx_chunks, :] = gtc_ref[0]
    gt_s[:, ctx_chunks:, :] = gtl_ref[0]

    for hh in range(DN_HEADS_PER_STEP):
        _dn_prepare_head(hh, pl.program_id(1) * DN_HEADS_PER_STEP + hh, ii, jj, level_masks, lane,
                         qkv_ctx_refs, qkv_lat_refs, conv_refs, g_s, gt_s,
                         x_scr, qkv_scr, kk_s, qk_s, mq_s, n_s, op_s, gl_s)

    def scan(first_chunk, count, emit, states):
        rows = dk + c if emit else dk

        def step(t, st):
            chains = [(hh, d, first_chunk + t if d == 0 else first_chunk + count - 1 - t)
                      for hh in range(DN_HEADS_PER_STEP) for d in range(2)]
            prods = [_dot(mq_s[hh, d, i, 0:rows, :], s.astype(BF16)) for (hh, d, i), s in zip(chains, st)]
            new = []
            for (hh, d, i), s, r in zip(chains, st, prods):
                if emit:
                    ro = pl.multiple_of((i - ctx_chunks) * c, c)
                    o_ref[0, pl.ds(ro, c), hh * dk:(hh + 1) * dk] += r[dk:dk + c] + op_s[hh, d, i]
                new.append(s * gl_s[hh, d, i][0:1, :] + r[0:dk] + n_s[hh, d, i])
            return tuple(new)
        return lax.fori_loop(0, count, step, states)

    o_ref[...] = jnp.zeros(o_ref.shape, F32)
    zero_state = jnp.zeros((dk, dk), F32)
    states = scan(0, ctx_chunks, False, (zero_state,) * (2 * DN_HEADS_PER_STEP))
    scan(ctx_chunks, lat_chunks, True, states)


def _dn_prepare_head(hh, head, ii, jj, level_masks, lane, qkv_ctx_refs, qkv_lat_refs, conv_refs, g_s, gt_s,
                     x_scr, qkv_scr, kk_s, qk_s, mq_s, n_s, op_s, gl_s):
    c = DN_CHUNK
    n_ctx = qkv_ctx_refs[0].shape[1]
    seq = qkv_lat_refs[0].shape[1]
    n_chunks = (n_ctx + seq) // c
    ctx_chunks = n_ctx // c
    pad = SUBLANES
    dk = DN_HEAD_DIM
    half = DN_CONV_W // 2
    cols = slice(hh * dk, (hh + 1) * dk)
    xq_s, xk_s, xv_s = x_scr
    q_s, k_s, v_s = qkv_scr
    cq_ref, ck_ref, cv_ref = conv_refs

    zeros_pad = jnp.zeros((pad, dk), F32)
    lat0 = 2 * pad + n_ctx
    for src_ctx, src_lat, dst in zip(qkv_ctx_refs, qkv_lat_refs, x_scr):
        dst[0:pad, :] = zeros_pad
        dst[pad:pad + n_ctx, :] = src_ctx[0, :, cols]
        dst[pad + n_ctx:lat0, :] = zeros_pad
        dst[lat0:lat0 + seq, :] = src_lat[0, :, cols]
        dst[lat0 + seq:lat0 + seq + pad, :] = zeros_pad

    def conv_chunk(i, carry):
        r0 = pl.multiple_of(i * c, c)
        rp = r0 + jnp.where(i >= ctx_chunks, 2 * pad, pad)

        def conv(x_s, cw_ref):
            acc = None
            for s in range(DN_CONV_W):
                term = x_s[pl.ds(rp - half + s, c), :] * cw_ref[s:s + 1, cols]
                acc = term if acc is None else acc + term
            return _silu(acc)

        qc = conv(xq_s, cq_ref)
        kc = conv(xk_s, ck_ref)
        vc = conv(xv_s, cv_ref)
        qc = qc * lax.rsqrt(jnp.sum(qc * qc, axis=-1, keepdims=True) + EPS) * (dk ** -0.5)
        kc = kc * lax.rsqrt(jnp.sum(kc * kc, axis=-1, keepdims=True) + EPS)
        q_s[pl.ds(r0, c), :] = qc
        k_s[pl.ds(r0, c), :] = kc
        v_s[pl.ds(r0, c), :] = vc
        kb = kc.astype(BF16)
        kk_s[i] = _dot_nt(kb, kb)
        qk_s[i] = _dot_nt(qc.astype(BF16), kb)
        return carry

    lax.fori_loop(0, n_chunks, conv_chunk, 0, unroll=DN_CONV_UNROLL)

    def local_group(grp, carry):
        chains = []
        for k in range(DN_GROUP):
            i = grp * DN_GROUP + k
            r0 = pl.multiple_of(i * c, c)
            gates = g_s[pl.ds(r0, c), :]
            kc = k_s[pl.ds(r0, c), :]
            qc = q_s[pl.ds(r0, c), :]
            vc = v_s[pl.ds(r0, c), :]
            kk = kk_s[i]
            qk = qk_s[i]
            kt = kc.T
            for d in range(2):
                beta_col = jnp.sum(jnp.where(lane == d * DN_HEADS + head, gates, 0.0), axis=-1, keepdims=True)
                g_col = jnp.sum(jnp.where(lane == N_GATES // 2 + d * DN_HEADS + head, gates, 0.0),
                                axis=-1, keepdims=True)
                g_row = gt_s[pl.ds(N_GATES // 2 + d * DN_HEADS + head, 1), pl.ds(i, 1), :].reshape(1, c)
                incl = (jj <= ii) if d == 0 else (jj >= ii)
                strict = (jj < ii) if d == 0 else (jj > ii)
                incl_t = (ii <= jj) if d == 0 else (ii >= jj)
                gcum_col = jnp.sum(jnp.where(incl, g_row, 0.0), axis=1, keepdims=True)
                gcum_row = jnp.sum(jnp.where(incl_t, g_col, 0.0), axis=0, keepdims=True)
                total = jnp.sum(g_row, axis=1, keepdims=True)
                decay = jnp.exp(jnp.where(incl, gcum_col - gcum_row, -jnp.inf))
                e_col = jnp.exp(gcum_col)
                kdt = (kt * jnp.exp(total - gcum_row)).astype(BF16)
                qkm = (qk * decay).astype(BF16)
                chains.append(dict(
                    d=d, i=i,
                    a=jnp.where(strict, beta_col * kk * decay, 0.0),
                    rhs=jnp.concatenate([vc * beta_col, kc * (beta_col * e_col)], axis=1),
                    lhs=jnp.concatenate([kdt, qkm], axis=0),
                    qd=qc * e_col,
                    g_last=jnp.exp(total)))
        corrs = _unit_tri_inverses_minus_eye([ch["a"] for ch in chains], level_masks)
        sols = [ch["rhs"] + _dot(corr, ch["rhs"].astype(BF16)) for ch, corr in zip(chains, corrs)]
        prods = [_dot(ch["lhs"], sol.astype(BF16)) for ch, sol in zip(chains, sols)]
        for ch, r in zip(chains, prods):
            d, i = ch["d"], ch["i"]
            mq_s[hh, d, i, 0:dk, :] = (-r[0:dk, dk:2 * dk]).astype(BF16)
            mq_s[hh, d, i, dk:dk + c, :] = (ch["qd"] - r[dk:dk + c, dk:2 * dk]).astype(BF16)
            n_s[hh, d, i] = r[0:dk, 0:dk]
            op_s[hh, d, i] = r[dk:dk + c, 0:dk]
            gl_s[hh, d, i] = jnp.broadcast_to(ch["g_last"], (SUBLANES, dk))
        return carry

    lax.fori_loop(0, n_chunks // DN_GROUP, local_group, 0)


def _dn_call(dn_ctx, dn_lat, conv_w, g_ctx, g_lat, gt_ctx, gt_lat, n_ctx):
    bsz, seq, _ = dn_lat.shape
    tot = n_ctx + seq
    c = DN_CHUNK
    assert n_ctx % c == 0 and seq % c == 0
    n_chunks = tot // c
    ctx_chunks = n_ctx // c
    dk = DN_HEAD_DIM
    gtc4 = gt_ctx.reshape(N_GATES, bsz, ctx_chunks, c).transpose(1, 0, 2, 3)
    gtl4 = gt_lat.reshape(bsz, N_GATES, seq // c, c)
    hp = DN_HEADS_PER_STEP
    steps = DN_HEADS // hp
    col_ctx = lambda off: pl.BlockSpec((1, n_ctx, hp * dk), lambda b, h: (0, b, off + h))
    col_lat = lambda off: pl.BlockSpec((1, seq, hp * dk), lambda b, h: (b, 0, off + h))
    cw = lambda off: pl.BlockSpec((DN_CONV_W, hp * dk), lambda b, h: (0, off + h))
    padded = tot + 3 * SUBLANES
    assert n_chunks % DN_GROUP == 0 and n_chunks % DN_CONV_UNROLL == 0 and DN_HEADS % hp == 0
    return pl.pallas_call(
        _dn_kernel,
        grid=(bsz, steps),
        in_specs=[col_ctx(0), col_ctx(steps), col_ctx(2 * steps),
                  col_lat(0), col_lat(steps), col_lat(2 * steps),
                  cw(0), cw(steps), cw(2 * steps),
                  pl.BlockSpec((1, n_ctx, LANES), lambda b, h: (0, b, 0)),
                  pl.BlockSpec((1, seq, LANES), lambda b, h: (b, 0, 0)),
                  pl.BlockSpec((1, N_GATES, ctx_chunks, c), lambda b, h: (b, 0, 0, 0)),
                  pl.BlockSpec((1, N_GATES, seq // c, c), lambda b, h: (b, 0, 0, 0))],
        out_specs=pl.BlockSpec((1, seq, hp * dk), lambda b, h: (b, 0, h)),
        out_shape=jax.ShapeDtypeStruct((bsz, seq, DN_W), F32),
        scratch_shapes=[pltpu.VMEM((padded, dk), F32)] * 3
        + [pltpu.VMEM((tot, dk), F32)] * 3
        + [pltpu.VMEM((n_chunks, c, c), F32)] * 2
        + [pltpu.VMEM((tot, LANES), F32),
           pltpu.VMEM((N_GATES, n_chunks, c), F32)]
        + [pltpu.VMEM((hp, 2, n_chunks, dk + c, dk), BF16),
           pltpu.VMEM((hp, 2, n_chunks, dk, dk), F32),
           pltpu.VMEM((hp, 2, n_chunks, c, dk), F32),
           pltpu.VMEM((hp, 2, n_chunks, SUBLANES, dk), F32)],
        compiler_params=_params("arbitrary", "arbitrary"),
        name="dn",
    )(dn_ctx, dn_ctx, dn_ctx, dn_lat, dn_lat, dn_lat, conv_w, conv_w, conv_w, g_ctx, g_lat, gtc4, gtl4)


def _outproj_kernel(x_ref, att_ref, o_ref, z_ref, mod_ref, onw_ref, wo_ref, n2_ref, rw_ref,
                    x1_ref, h2_ref, afft_ref):
    d = x_ref.shape[-1]
    b = pl.program_id(0)
    gate1 = mod_ref[pl.ds(b, 1), 2 * d:3 * d]
    shift2 = mod_ref[pl.ds(b, 1), 3 * d:4 * d]
    scale2 = mod_ref[pl.ds(b, 1), 4 * d:5 * d]
    lane = lax.broadcasted_iota(I32, (1, LANES), 1)
    n_rows = x_ref.shape[1]

    def mixed(rows):
        parts = [att_ref[0, rows, :]]
        for h in range(DN_HEADS):
            sl = slice(h * DN_HEAD_DIM, (h + 1) * DN_HEAD_DIM)
            oh = o_ref[0, rows, sl]
            on = oh * lax.rsqrt(jnp.mean(oh * oh, axis=-1, keepdims=True) + EPS) * onw_ref[...]
            parts.append((on * _silu(z_ref[0, rows, sl])).astype(BF16))
        return _dot(jnp.concatenate(parts, axis=1), wo_ref[...])

    def finish(rows, y):
        x1 = x_ref[0, rows, :] + gate1 * y
        x1_ref[0, rows, :] = x1
        hn = x1 * lax.rsqrt(jnp.mean(x1 * x1, axis=-1, keepdims=True) + EPS) * n2_ref[...]
        h2 = (hn * (1.0 + scale2) + shift2).astype(BF16)
        h2_ref[0, rows, :] = h2
        logits = _dot(h2, rw_ref[...])
        logits = jnp.where(lane < N_EXPERTS, logits, -jnp.inf)
        e = jnp.exp(logits - jnp.max(logits, axis=-1, keepdims=True))
        aff = e / jnp.sum(e, axis=-1, keepdims=True)
        afft_ref[0, :, rows] = aff.T[0:N_EXPERTS, :]

    chunks = [slice(r0, r0 + OUT_ROW_CHUNK) for r0 in range(0, n_rows, OUT_ROW_CHUNK)]
    y_next = mixed(chunks[0])
    for r, rows in enumerate(chunks):
        y = y_next
        if r + 1 < len(chunks):
            y_next = mixed(chunks[r + 1])
        finish(rows, y)


def _outproj_call(x, att, o_dn, z, mod, onw, wo, n2, rw):
    bsz, seq, d = x.shape
    t = OUT_TILE
    assert seq % t == 0
    full = lambda shape: pl.BlockSpec(shape, lambda b, j: (0,) * len(shape))
    return pl.pallas_call(
        _outproj_kernel,
        grid=(bsz, seq // t),
        in_specs=[pl.BlockSpec((1, t, d), lambda b, j: (b, j, 0)),
                  pl.BlockSpec((1, t, ATTN_Q_W), lambda b, j: (b, j, 0)),
                  pl.BlockSpec((1, t, DN_W), lambda b, j: (b, j, 0)),
                  pl.BlockSpec((1, t, DN_W), lambda b, j: (b, j, 0)),
                  full((MOD_ROWS, mod.shape[1])),
                  full((1, DN_HEAD_DIM)),
                  full(wo.shape),
                  full((1, d)),
                  full((d, LANES))],
        out_specs=[pl.BlockSpec((1, t, d), lambda b, j: (b, j, 0)),
                   pl.BlockSpec((1, t, d), lambda b, j: (b, j, 0)),
                   pl.BlockSpec((1, N_EXPERTS, t), lambda b, j: (b, 0, j))],
        out_shape=[jax.ShapeDtypeStruct((bsz, seq, d), F32),
                   jax.ShapeDtypeStruct((bsz, seq, d), BF16),
                   jax.ShapeDtypeStruct((bsz, N_EXPERTS, seq), F32)],
        compiler_params=_params("arbitrary", "arbitrary"),
        name="outproj",
    )(x, att, o_dn, z, mod, onw, wo, n2, rw)


def _route_kernel(cap, afft_ref, slot_ref, gate_ref, tok_ref, bounds_ref):
    n_b, n_exp, n_t = afft_ref.shape
    n_e = n_b * n_exp
    aff = afft_ref[...].reshape(n_e, n_t)

    def enough(cand):
        return jnp.sum(jnp.where(aff >= cand, 1.0, 0.0), axis=-1, keepdims=True) >= cap

    tiny = 2.0 ** F32_MIN_EXP
    cur = jnp.full((n_e, 1), tiny, F32)
    any_normal = enough(cur)
    shift = 1 << (-F32_MIN_EXP).bit_length()
    while shift > 1:
        shift //= 2
        cand = cur * (2.0 ** shift)
        cur = jnp.where(enough(cand), cand, cur)

    def refine(_, state):
        cur, step = state
        cand = cur + step
        return jnp.where(enough(cand), cand, cur), step * 0.5

    cur, _ = lax.fori_loop(0, F32_MANTISSA_BITS, refine, (cur, cur * 0.5))
    thr = jnp.where(any_normal, cur, 0.0)
    need = cap - jnp.sum(jnp.where(aff > thr, 1.0, 0.0), axis=-1, keepdims=True)

    upper = (lax.broadcasted_iota(I32, (LANES, LANES), 0) < lax.broadcasted_iota(I32, (LANES, LANES), 1))
    upper = jnp.where(upper, 1.0, 0.0).astype(BF16)
    run = jnp.zeros((2 * n_e, 1), F32)
    lane = lax.broadcasted_iota(I32, (1, LANES), 1)
    bounds = jnp.zeros((n_e, LANES), F32)
    per_block = ROUTE_BLOCK // LANES
    for blk in range(n_t // LANES):
        sl = slice(blk * LANES, (blk + 1) * LANES)
        gt = aff[:, sl] > thr
        eq = aff[:, sl] == thr
        x = jnp.concatenate([jnp.where(gt, 1.0, 0.0), jnp.where(eq, 1.0, 0.0)], axis=0)
        cum = _dot(x.astype(BF16), upper) + run
        run = run + jnp.sum(x, axis=-1, keepdims=True)
        cum_gt = cum[0:n_e]
        cum_eq = cum[n_e:2 * n_e]
        sel = gt | (eq & (cum_eq < need))
        slot = jnp.where(sel, cum_gt + jnp.minimum(cum_eq, need), -1.0)
        slot_ref[:, :, sl] = slot.astype(I32).reshape(n_b, n_exp, LANES)
        gate_ref[:, :, sl] = jnp.where(sel, aff[:, sl], 0.0).reshape(n_b, n_exp, LANES)
        if (blk + 1) % per_block == 0:
            taken = run[0:n_e] + jnp.minimum(run[n_e:2 * n_e], need)
            bounds = jnp.where(lane == (blk + 1) // per_block, taken, bounds)
    bounds_ref[...] = bounds.astype(I32).reshape(n_b, n_exp, LANES)

    for b in range(n_b):
        stacked = jnp.concatenate([slot_ref[b].astype(F32), gate_ref[b],
                                   jnp.zeros((LANES - 2 * n_exp, n_t), F32)], axis=0)
        tok_ref[b] = stacked.T


def _route_call(afft, cap):
    bsz, n_e, n_t = afft.shape
    row = pl.BlockSpec((bsz, n_e, n_t), lambda i: (0, 0, 0))
    return pl.pallas_call(
        functools.partial(_route_kernel, cap),
        grid=(1,),
        in_specs=[row],
        out_specs=[row, row, pl.BlockSpec((bsz, n_t, LANES), lambda i: (0, 0, 0)),
                   pl.BlockSpec((bsz, n_e, LANES), lambda i: (0, 0, 0))],
        out_shape=[jax.ShapeDtypeStruct((bsz, n_e, n_t), I32),
                   jax.ShapeDtypeStruct((bsz, n_e, n_t), F32),
                   jax.ShapeDtypeStruct((bsz, n_t, LANES), F32),
                   jax.ShapeDtypeStruct((bsz, n_e, LANES), I32)],
        compiler_params=_params("arbitrary"),
        name="route",
    )(afft)


def _window_starts(cnt_ref, b, tb, n_e, cap):
    starts, fits = [], None
    for e in range(n_e):
        lo = cnt_ref[b, tb * n_e + e]
        hi = cnt_ref[b, (tb + 1) * n_e + e]
        start = jnp.minimum((lo // BF16_ROWS) * BF16_ROWS, cap - SLOT_WINDOW)
        ok = hi - start <= SLOT_WINDOW
        starts.append(pl.multiple_of(start, BF16_ROWS))
        fits = ok if fits is None else jnp.logical_and(fits, ok)
    return starts, fits


def _gather_kernel(cap, cnt_ref, slot_ref, gate_ref, h_ref, xg_ref, gs_ref):
    b = pl.program_id(0)
    n_e = slot_ref.shape[1]
    n_blk = slot_ref.shape[2]
    xg_ref[...] = jnp.zeros(xg_ref.shape, BF16)
    gs_ref[...] = jnp.zeros(gs_ref.shape, F32)

    def token_block(tb, carry):
        h_blk = h_ref[0, pl.ds(pl.multiple_of(tb * ROUTE_BLOCK, ROUTE_BLOCK), ROUTE_BLOCK), :]
        starts, fits = _window_starts(cnt_ref, b, tb, n_e, cap)

        def accumulate(window, first_rows):
            j = lax.broadcasted_iota(I32, (window, 1), 0)
            hits = [slot_ref[0, e, pl.ds(tb, 1), :] == first_rows[e] + j for e in range(n_e)]
            onehot = jnp.concatenate([jnp.where(hit, 1.0, 0.0).astype(BF16) for hit in hits], axis=0)
            rows = _dot(onehot, h_blk)
            for e in range(n_e):
                dst = pl.ds(first_rows[e], window)
                xg_ref[e, 0, dst, :] += rows[e * window:(e + 1) * window].astype(BF16)
                gate = gate_ref[0, e, pl.ds(tb, 1), :]
                gs_ref[e, 0, dst, :] += jnp.sum(jnp.where(hits[e], gate, 0.0), axis=-1, keepdims=True)

        @pl.when(fits)
        def _():
            accumulate(SLOT_WINDOW, starts)

        @pl.when(jnp.logical_not(fits))
        def _():
            accumulate(cap, [0] * n_e)

        return carry

    lax.fori_loop(0, n_blk, token_block, 0)


def _gather_call(cnt, slot, gate, h2, cap):
    bsz, n_e, n_t = slot.shape
    d = h2.shape[-1]
    n_blk = n_t // ROUTE_BLOCK
    blocked = lambda a: a.reshape(bsz, n_e, n_blk, ROUTE_BLOCK)
    row = pl.BlockSpec((1, n_e, n_blk, ROUTE_BLOCK), lambda b, cnt: (b, 0, 0, 0))
    return pl.pallas_call(
        functools.partial(_gather_kernel, cap),
        grid_spec=pltpu.PrefetchScalarGridSpec(
            num_scalar_prefetch=1,
            grid=(bsz,),
            in_specs=[row, row, pl.BlockSpec((1, n_t, d), lambda b, cnt: (b, 0, 0))],
            out_specs=[pl.BlockSpec((n_e, 1, cap, d), lambda b, cnt: (0, b, 0, 0)),
                       pl.BlockSpec((n_e, 1, cap, 1), lambda b, cnt: (0, b, 0, 0))]),
        out_shape=[jax.ShapeDtypeStruct((n_e, bsz, cap, d), BF16),
                   jax.ShapeDtypeStruct((n_e, bsz, cap, 1), F32)],
        compiler_params=_params("arbitrary"),
        name="gather",
    )(cnt, blocked(slot), blocked(gate), h2)


def _ffn_kernel(xg_ref, gs_ref, wg_ref, wu_ref, wd_ref, y_ref):
    bsz, cap, d = xg_ref.shape[1:]
    per = FFN_ROW_CHUNK // cap
    n_chunks = bsz // per
    wg = wg_ref[0].astype(BF16)
    wu = wu_ref[0].astype(BF16)
    wd = wd_ref[0].astype(BF16)

    def up(r):
        x = xg_ref[0, r * per:(r + 1) * per].reshape(per * cap, d)
        return _dot(x, wg), _dot(x, wu)

    nxt = up(0)
    for r in range(n_chunks):
        g, u = nxt
        if r + 1 < n_chunks:
            nxt = up(r + 1)
        y = _dot((_silu(g) * u).astype(BF16), wd) * gs_ref[0, r * per:(r + 1) * per].reshape(per * cap, 1)
        y_ref[0, r * per:(r + 1) * per] = y.astype(BF16).reshape(per, cap, d)


def _ffn_call(xg, gs, w_gate, w_up, w_down):
    n_e, bsz, cap, d = xg.shape
    ff = w_gate.shape[-1]
    rows = pl.BlockSpec((1, bsz, cap, d), lambda e: (e, 0, 0, 0))
    return pl.pallas_call(
        _ffn_kernel,
        grid=(n_e,),
        in_specs=[rows,
                  pl.BlockSpec((1, bsz, cap, 1), lambda e: (e, 0, 0, 0)),
                  pl.BlockSpec((1, d, ff), lambda e: (e, 0, 0)),
                  pl.BlockSpec((1, d, ff), lambda e: (e, 0, 0)),
                  pl.BlockSpec((1, ff, d), lambda e: (e, 0, 0))],
        out_specs=rows,
        out_shape=jax.ShapeDtypeStruct((n_e, bsz, cap, d), BF16),
        compiler_params=_params("arbitrary"),
        name="ffn",
    )(xg, gs, w_gate, w_up, w_down)


def _combine_kernel(cap, cnt_ref, x1_ref, tok_ref, y_ref, mod_ref, o_ref):
    d = x1_ref.shape[-1]
    n_e = y_ref.shape[0]
    b = pl.program_id(0)
    tb = pl.program_id(1)
    gate2 = mod_ref[pl.ds(b, 1), 5 * d:6 * d]
    tok = tok_ref[0]
    starts, fits = _window_starts(cnt_ref, b, tb, n_e, cap)

    @pl.when(fits)
    def _():
        lane = lax.broadcasted_iota(I32, (1, LANES), 1)
        first = lane < SLOT_WINDOW
        groups = []
        for e in range(0, n_e, 2):
            slot = jnp.where(first, tok[:, e:e + 1], tok[:, e + 1:e + 2])
            target = jnp.where(first, starts[e] + lane, starts[e + 1] + lane - SLOT_WINDOW).astype(F32)
            groups.append(jnp.where(slot == target, 1.0, 0.0).astype(BF16))
        onehot = jnp.concatenate(groups, axis=1)
        rows = jnp.concatenate([y_ref[e, 0, pl.ds(starts[e], SLOT_WINDOW), :] for e in range(n_e)], axis=0)
        o_ref[0] = x1_ref[0] + gate2 * _dot(onehot, rows)

    @pl.when(jnp.logical_not(fits))
    def _():
        j = lax.broadcasted_iota(I32, (1, cap), 1).astype(F32)
        acc = None
        for e in range(n_e):
            onehot = jnp.where(tok[:, e:e + 1] == j, 1.0, 0.0).astype(BF16)
            part = _dot(onehot, y_ref[e, 0])
            acc = part if acc is None else acc + part
        o_ref[0] = x1_ref[0] + gate2 * acc


def _combine_call(cnt, x1, tok, y, mod, cap):
    bsz, seq, d = x1.shape
    n_e = y.shape[0]
    t = ROUTE_BLOCK
    assert 2 * SLOT_WINDOW == LANES and n_e % 2 == 0
    return pl.pallas_call(
        functools.partial(_combine_kernel, cap),
        grid_spec=pltpu.PrefetchScalarGridSpec(
            num_scalar_prefetch=1,
            grid=(bsz, seq // t),
            in_specs=[pl.BlockSpec((1, t, d), lambda b, j, cnt: (b, j, 0)),
                      pl.BlockSpec((1, t, LANES), lambda b, j, cnt: (b, j, 0)),
                      pl.BlockSpec((n_e, 1, cap, d), lambda b, j, cnt: (0, b, 0, 0)),
                      pl.BlockSpec((MOD_ROWS, mod.shape[1]), lambda b, j, cnt: (0, 0))],
            out_specs=pl.BlockSpec((1, t, d), lambda b, j, cnt: (b, j, 0))),
        out_shape=jax.ShapeDtypeStruct((bsz, seq, d), F32),
        compiler_params=_params("arbitrary", "arbitrary"),
        name="combine",
    )(cnt, x1, tok, y, mod)


def _rope_tables(seq):
    m = ATTN_HEAD_DIM // 4
    pos = jnp.arange(seq, dtype=jnp.int32)
    rows = (pos // GRID_W).astype(F32)
    cols = (pos % GRID_W).astype(F32)
    freqs = ROPE_BASE ** (-jnp.arange(m, dtype=F32) / m)
    ang_r = rows[:, None] * freqs[None, :]
    ang_c = cols[:, None] * freqs[None, :]
    cos_h = jnp.concatenate([jnp.cos(ang_r), jnp.cos(ang_r), jnp.cos(ang_c), jnp.cos(ang_c)], axis=-1)
    sin_h = jnp.concatenate([-jnp.sin(ang_r), jnp.sin(ang_r), -jnp.sin(ang_c), jnp.sin(ang_c)], axis=-1)
    reps = LANES // ATTN_HEAD_DIM
    return jnp.tile(cos_h, (1, reps)), jnp.tile(sin_h, (1, reps))


def _lane_row(values, offset):
    return jnp.zeros((1, LANES), F32).at[0, offset:offset + values.shape[0]].set(values.astype(F32))


def kernel(x, c, ctx, c_ctx, w_mod, b_mod, norm1_w, norm2_w, w_in, q_norm_w, k_norm_w, conv_w, a_log, dt_bias,
           o_norm_w, w_out, router_w, w_gate, w_up, w_down):
    bsz, seq, d = x.shape
    n_ctx = ctx.shape[1]
    assert w_mod.shape[0] == 1, "single layer: the last layer's context outputs are never consumed"
    assert bsz < MOD_ROWS and N_EXPERTS == router_w.shape[-1]
    cap = EC_CAPACITY_FACTOR * seq // N_EXPERTS

    cc = jnp.concatenate([c, c_ctx[None, :], jnp.zeros((MOD_ROWS - bsz - 1, d), F32)], axis=0)
    mod = _mod_call(cc, w_mod[0], b_mod[0][None, :])

    w_pad = jnp.concatenate([w_in[0].astype(BF16), jnp.zeros((d, LANES - N_GATES), BF16)], axis=1)
    seg = np.arange(MXU_DIM) // ATTN_HEAD_DIM
    bd = jnp.asarray(seg[:, None] == seg[None, :], BF16)
    qkw = jnp.concatenate([jnp.tile(q_norm_w[0], ATTN_HEADS), jnp.tile(k_norm_w[0], ATTN_KV_HEADS)])[None, :]
    alog_l = _lane_row(a_log[0].reshape(-1), N_GATES // 2)
    dtb_l = _lane_row(dt_bias[0].reshape(-1), N_GATES // 2)
    shared = (mod, norm1_w[0][None, :], w_pad, bd, qkw, alog_l, dtb_l)
    kv_lat, dn_lat, g_lat, gt_lat, q, z = _inproj_call(x, bsz, *shared, rope=_rope_tables(seq))
    kv_ctx, dn_ctx, g_ctx, gt_ctx = _inproj_call(ctx.reshape(1, bsz * n_ctx, d), bsz, *shared)

    att = _attn_call(q, kv_ctx, kv_lat, n_ctx)
    o_dn = _dn_call(dn_ctx, dn_lat, conv_w[0], g_ctx, g_lat, gt_ctx, gt_lat, n_ctx)

    rw = jnp.pad(router_w[0], ((0, 0), (0, LANES - N_EXPERTS))).astype(BF16)
    x1, h2, afft = _outproj_call(x, att, o_dn, z, mod, o_norm_w[0][None, :], w_out[0].astype(BF16),
                                 norm2_w[0][None, :], rw)

    slot, gate, tok, bounds = _route_call(afft, cap)
    n_blk = seq // ROUTE_BLOCK
    cnt = bounds[:, :, 0:n_blk + 1].transpose(0, 2, 1).reshape(bsz, (n_blk + 1) * N_EXPERTS)
    xg, gs = _gather_call(cnt, slot, gate, h2, cap)
    y = _ffn_call(xg, gs, w_gate[0], w_up[0], w_down[0])
    return _combine_call(cnt, x1, tok, y, mod, cap)
```

```python
import functools

import numpy as np
import jax
import jax.numpy as jnp
from jax import lax
from jax.experimental import pallas as pl
from jax.experimental.pallas import tpu as pltpu

F32 = jnp.float32
BF16 = jnp.bfloat16
I32 = jnp.int32

GRID_W = 64
EPS = 1e-6
ATTN_HEADS = 8
ATTN_KV_HEADS = 2
ATTN_HEAD_DIM = 64
ROPE_BASE = 10000.0
DN_HEADS = 4
DN_HEAD_DIM = 128
DN_CONV_W = 5
N_EXPERTS = 16
EC_CAPACITY_FACTOR = 2

ATTN_Q_W = ATTN_HEADS * ATTN_HEAD_DIM
ATTN_KV_W = ATTN_KV_HEADS * ATTN_HEAD_DIM
DN_W = DN_HEADS * DN_HEAD_DIM
QK_W = ATTN_Q_W + ATTN_KV_W
ATT_W = ATTN_Q_W + 2 * ATTN_KV_W
N_GATES = 4 * DN_HEADS

LANES = 128
SUBLANES = 8
MXU_DIM = 256
VMEM_LIMIT = 56 * 1024 * 1024
F32_MIN_EXP = -126
F32_MANTISSA_BITS = 23

TOK_TILE = 1024
IN_ROW_CHUNK = 128
Q_TILE = 512
ATTN_KEY_CHUNK = 1024
OUT_TILE = 512
OUT_ROW_CHUNK = 128
DN_CHUNK = 128
DN_GROUP = 9
DN_CONV_UNROLL = 9
DN_HEADS_PER_STEP = 2
ROUTE_BLOCK = 256
SLOT_WINDOW = 64
BF16_ROWS = 16
FFN_ROW_CHUNK = 512
MOD_ROWS = 16


def _params(*sem):
    return pltpu.CompilerParams(dimension_semantics=sem, vmem_limit_bytes=VMEM_LIMIT)


def _silu(v):
    half = 0.5 * v
    return half + half * jnp.tanh(half)


def _dot(a, b):
    return jnp.dot(a, b, preferred_element_type=F32)


def _dot_nt(a, b):
    return lax.dot_general(a, b, (((1,), (1,)), ((), ())), preferred_element_type=F32)


def _mod_kernel(c_ref, w_ref, b_ref, o_ref):
    sc = _silu(c_ref[...]).astype(BF16)
    o_ref[...] = _dot(sc, w_ref[...].astype(BF16)) + b_ref[...]


def _mod_call(cc, w_mod, b_mod):
    d, n = w_mod.shape
    return pl.pallas_call(
        _mod_kernel,
        grid=(n // d,),
        in_specs=[pl.BlockSpec((MOD_ROWS, d), lambda i: (0, 0)),
                  pl.BlockSpec((d, d), lambda i: (0, i)),
                  pl.BlockSpec((1, d), lambda i: (0, i))],
        out_specs=pl.BlockSpec((MOD_ROWS, d), lambda i: (0, i)),
        out_shape=jax.ShapeDtypeStruct((MOD_ROWS, n), F32),
        compiler_params=_params("arbitrary"),
        name="mod",
    )(cc, w_mod, b_mod)


def _inproj_kernel(latent, ctx_row, x_ref, mod_ref, n1_ref, w_ref, bd_ref, qkw_ref, alog_ref, dtb_ref, *refs):
    if latent:
        cos_ref, sin_ref, kv_ref, dn_ref, g_ref, gt_ref, q_ref, z_ref = refs
    else:
        kv_ref, dn_ref, g_ref, gt_ref = refs
    d = x_ref.shape[-1]
    n_rows = x_ref.shape[1]
    row = pl.program_id(0) if latent else ctx_row
    shift = mod_ref[pl.ds(row, 1), 0:d]
    scale = mod_ref[pl.ds(row, 1), d:2 * d]
    lane = lax.broadcasted_iota(I32, (1, LANES), 1)
    k_blk = ATTN_Q_W // LANES

    def project(rows):
        xin = x_ref[0, rows, :]
        hn = xin * lax.rsqrt(jnp.mean(xin * xin, axis=-1, keepdims=True) + EPS) * n1_ref[...]
        h = (hn * (1.0 + scale) + shift).astype(BF16)
        return _dot(h, w_ref[...])

    def finish(rows, p):
        qk = p[:, 0:QK_W]
        sq = qk * qk
        hi = sq.astype(BF16)
        lo = (sq - hi.astype(F32)).astype(BF16)
        seg_w = bd_ref.shape[0]
        ms = []
        for c0 in range(0, QK_W, seg_w):
            w = min(seg_w, QK_W - c0)
            ones = bd_ref[0:w, 0:w]
            ms.append(_dot(hi[:, c0:c0 + w], ones) + _dot(lo[:, c0:c0 + w], ones))
        ms = jnp.concatenate(ms, axis=1) * (1.0 / ATTN_HEAD_DIM)
        qkn = qk * lax.rsqrt(ms + EPS) * qkw_ref[...]

        if latent:
            first_half = (lane % 32) < 16
            cos = cos_ref[rows, :]
            sin = sin_ref[rows, :]
            rot = []
            for i in range(QK_W // LANES):
                blk = qkn[:, i * LANES:(i + 1) * LANES]
                partner = jnp.where(first_half, pltpu.roll(blk, LANES - 16, axis=1), pltpu.roll(blk, 16, axis=1))
                rot.append(blk * cos + partner * sin)
            for i in range(k_blk):
                q_ref[0, rows, i * LANES:(i + 1) * LANES] = (rot[i] * (ATTN_HEAD_DIM ** -0.5)).astype(BF16)
            k2 = rot[k_blk]
            z_ref[0, rows, :] = p[:, ATT_W + 3 * DN_W:ATT_W + 4 * DN_W]
        else:
            k2 = qkn[:, k_blk * LANES:(k_blk + 1) * LANES]

        low = lane < ATTN_HEAD_DIM
        v2 = p[:, QK_W:ATT_W]
        for src, base in ((k2, 0), (v2, 2)):
            swapped = pltpu.roll(src, ATTN_HEAD_DIM, axis=1)
            kv_ref[0, 0, base + 0, rows, :] = jnp.where(low, src, 0.0).astype(BF16)
            kv_ref[0, 0, base + 1, rows, :] = jnp.where(low, 0.0, swapped).astype(BF16)
            kv_ref[0, 1, base + 0, rows, :] = jnp.where(low, swapped, 0.0).astype(BF16)
            kv_ref[0, 1, base + 1, rows, :] = jnp.where(low, 0.0, src).astype(BF16)

        dn_ref[0, rows, :] = p[:, ATT_W:ATT_W + 3 * DN_W]

        gp = p[:, ATT_W + 4 * DN_W:]
        beta = jax.nn.sigmoid(gp)
        xa = gp + dtb_ref[...]
        softplus = jnp.maximum(xa, 0.0) + jnp.log1p(jnp.exp(-jnp.abs(xa)))
        decay = -jnp.exp(alog_ref[...]) * softplus
        gates = jnp.where(lane < N_GATES // 2, beta, jnp.where(lane < N_GATES, decay, 0.0))
        g_ref[0, rows, :] = gates
        gt_ref[0, :, rows] = gates.T[0:N_GATES, :]

    chunks = [slice(r0, r0 + IN_ROW_CHUNK) for r0 in range(0, n_rows, IN_ROW_CHUNK)]
    p_next = project(chunks[0])
    for r, rows in enumerate(chunks):
        p = p_next
        if r + 1 < len(chunks):
            p_next = project(chunks[r + 1])
        finish(rows, p)


def _inproj_call(rows3, ctx_row, mod, n1, w_pad, bd, qkw, alog_l, dtb_l, rope=None):
    latent = rope is not None
    grp, rows, d = rows3.shape
    t = TOK_TILE
    assert rows % t == 0
    nw = w_pad.shape[1]
    full = lambda shape: pl.BlockSpec(shape, lambda b, j: (0,) * len(shape))
    tile = lambda width: pl.BlockSpec((1, t, width), lambda b, j: (b, j, 0))
    in_specs = [tile(d), full((MOD_ROWS, mod.shape[1])), full((1, d)), full((d, nw)), full(bd.shape),
                full((1, QK_W)), full((1, LANES)), full((1, LANES))]
    out_specs = [pl.BlockSpec((1, ATTN_KV_HEADS, 4, t, LANES), lambda b, j: (b, 0, 0, j, 0)),
                 tile(3 * DN_W), tile(LANES), pl.BlockSpec((1, N_GATES, t), lambda b, j: (b, 0, j))]
    out_shape = [jax.ShapeDtypeStruct((grp, ATTN_KV_HEADS, 4, rows, LANES), BF16),
                 jax.ShapeDtypeStruct((grp, rows, 3 * DN_W), F32),
                 jax.ShapeDtypeStruct((grp, rows, LANES), F32),
                 jax.ShapeDtypeStruct((grp, N_GATES, rows), F32)]
    args = [rows3, mod, n1, w_pad, bd, qkw, alog_l, dtb_l]
    if latent:
        in_specs += [pl.BlockSpec((t, LANES), lambda b, j: (j, 0))] * 2
        out_specs += [tile(ATTN_Q_W), tile(DN_W)]
        out_shape += [jax.ShapeDtypeStruct((grp, rows, ATTN_Q_W), BF16), jax.ShapeDtypeStruct((grp, rows, DN_W), F32)]
        args += list(rope)
    return pl.pallas_call(
        functools.partial(_inproj_kernel, latent, ctx_row),
        grid=(grp, rows // t),
        in_specs=in_specs,
        out_specs=out_specs,
        out_shape=out_shape,
        compiler_params=_params("arbitrary", "arbitrary"),
        name="inproj_lat" if latent else "inproj_ctx",
    )(*args)


def _attn_kernel(q_ref, kvc_ref, kvl_ref, o_ref):
    grp = ATTN_HEADS // ATTN_KV_HEADS
    n_lat = kvl_ref.shape[3]

    def scores(h):
        qp = q_ref[0, :, (h // 2) * LANES:(h // 2 + 1) * LANES]
        return (_dot_nt(qp, kvc_ref[0, h // grp, h % 2]),
                _dot_nt(qp, kvl_ref[0, h // grp, h % 2]))

    s_next = scores(0)
    acc = None
    for h in range(ATTN_HEADS):
        s_ctx, s_lat = s_next
        if h + 1 < ATTN_HEADS:
            s_next = scores(h + 1)
        m = jnp.maximum(jnp.max(s_ctx, axis=-1, keepdims=True), jnp.max(s_lat, axis=-1, keepdims=True))
        stages = [(s_ctx, kvc_ref, 0, s_ctx.shape[-1])]
        stages += [(s_lat, kvl_ref, k0, ATTN_KEY_CHUNK) for k0 in range(0, n_lat, ATTN_KEY_CHUNK)]
        o = denom = None
        for s, v_ref, k0, width in stages:
            e = jnp.exp(s[:, k0:k0 + width] - m)
            part = _dot(e.astype(BF16), v_ref[0, h // grp, 2 + h % 2, k0:k0 + width, :])
            part_sum = jnp.sum(e, axis=-1, keepdims=True)
            o = part if o is None else o + part
            denom = part_sum if denom is None else denom + part_sum
        o = o / denom
        if h % 2 == 0:
            acc = o
        else:
            o_ref[0, :, (h // 2) * LANES:(h // 2 + 1) * LANES] = (acc + o).astype(BF16)


def _attn_call(q, kv_ctx, kv_lat, n_ctx):
    bsz, seq, _ = q.shape
    tq = Q_TILE
    assert seq % tq == 0 and seq % ATTN_KEY_CHUNK == 0
    return pl.pallas_call(
        _attn_kernel,
        grid=(bsz, seq // tq),
        in_specs=[pl.BlockSpec((1, tq, ATTN_Q_W), lambda b, i: (b, i, 0)),
                  pl.BlockSpec((1, ATTN_KV_HEADS, 4, n_ctx, LANES), lambda b, i: (0, 0, 0, b, 0)),
                  pl.BlockSpec((1, ATTN_KV_HEADS, 4, seq, LANES), lambda b, i: (b, 0, 0, 0, 0))],
        out_specs=pl.BlockSpec((1, tq, ATTN_Q_W), lambda b, i: (b, i, 0)),
        out_shape=jax.ShapeDtypeStruct((bsz, seq, ATTN_Q_W), BF16),
        compiler_params=_params("arbitrary", "arbitrary"),
        name="attn",
    )(q, kv_ctx, kv_lat)


def _unit_tri_inverses_minus_eye(mats, level_masks):
    strip_masks, last_join, left, right = level_masks
    n = mats[0].shape[0]
    half = n // 2

    def to_strip(m):
        return m[0:half, :] * left + m[half:, :] * right

    def from_strip(s):
        return jnp.concatenate([s * left, s * right], axis=0)

    mats = [a.astype(BF16) for a in mats]
    strips = [to_strip(a) for a in mats]
    corrs = [-(s * strip_masks[0]) for s in strips]
    for joins in strip_masks[1:]:
        xs = [s * joins for s in strips]
        ys = [x.astype(F32) + _dot(corr, from_strip(x)) for corr, x in zip(corrs, xs)]
        corrs = [corr - (y + _dot(y.astype(BF16), from_strip(corr))).astype(BF16) for corr, y in zip(corrs, ys)]
    corrs = [from_strip(corr) for corr in corrs]
    xs = [a * last_join for a in mats]
    ys = [x.astype(F32) + _dot(corr, x) for corr, x in zip(corrs, xs)]
    return [corr - (y + _dot(y.astype(BF16), corr)).astype(BF16) for corr, y in zip(corrs, ys)]


def _tri_level_masks(n):
    ij_xor = lax.broadcasted_iota(I32, (n, n), 0) ^ lax.broadcasted_iota(I32, (n, n), 1)
    half = n // 2
    lane = lax.broadcasted_iota(I32, (1, n), 1)
    left = jnp.where(lane < half, 1.0, 0.0).astype(BF16)
    right = jnp.where(lane >= half, 1.0, 0.0).astype(BF16)
    strip_masks, m = [], 1
    while m < half:
        full = jnp.where((ij_xor >= m) & (ij_xor < 2 * m), 1.0, 0.0).astype(BF16)
        strip_masks.append(full[0:half, :] * left + full[half:, :] * right)
        m *= 2
    last_join = jnp.where((ij_xor >= half) & (ij_xor < n), 1.0, 0.0).astype(BF16)
    return strip_masks, last_join, left, right


def _dn_kernel(dqc_ref, dkc_ref, dvc_ref, dql_ref, dkl_ref, dvl_ref, cq_ref, ck_ref, cv_ref,
               gc_ref, gl_ref, gtc_ref, gtl_ref, o_ref,
               xq_s, xk_s, xv_s, q_s, k_s, v_s, kk_s, qk_s, g_s, gt_s, mq_s, n_s, op_s, gl_s):
    qkv_ctx_refs = (dqc_ref, dkc_ref, dvc_ref)
    qkv_lat_refs = (dql_ref, dkl_ref, dvl_ref)
    conv_refs = (cq_ref, ck_ref, cv_ref)
    x_scr = (xq_s, xk_s, xv_s)
    qkv_scr = (q_s, k_s, v_s)
    c = DN_CHUNK
    n_ctx = gc_ref.shape[1]
    ctx_chunks = n_ctx // c
    lat_chunks = gl_ref.shape[1] // c
    dk = DN_HEAD_DIM
    ii = lax.broadcasted_iota(I32, (c, c), 0)
    jj = lax.broadcasted_iota(I32, (c, c), 1)
    level_masks = _tri_level_masks(c)
    lane = lax.broadcasted_iota(I32, (1, LANES), 1)

    g_s[0:n_ctx, :] = gc_ref[0]
    g_s[n_ctx:, :] = gl_ref[0]
    gt_s[:, 0:ctx_chunks, :] = gtc_ref[0]
    gt_s[:, ctx_chunks:, :] = gtl_ref[0]

    for hh in range(DN_HEADS_PER_STEP):
        _dn_prepare_head(hh, pl.program_id(1) * DN_HEADS_PER_STEP + hh, ii, jj, level_masks, lane,
                         qkv_ctx_refs, qkv_lat_refs, conv_refs, g_s, gt_s,
                         x_scr, qkv_scr, kk_s, qk_s, mq_s, n_s, op_s, gl_s)

    def scan(first_chunk, count, emit, states):
        rows = dk + c if emit else dk

        def step(t, st):
            chains = [(hh, d, first_chunk + t if d == 0 else first_chunk + count - 1 - t)
                      for hh in range(DN_HEADS_PER_STEP) for d in range(2)]
            prods = [_dot(mq_s[hh, d, i, 0:rows, :], s.astype(BF16)) for (hh, d, i), s in zip(chains, st)]
            new = []
            for (hh, d, i), s, r in zip(chains, st, prods):
                if emit:
                    ro = pl.multiple_of((i - ctx_chunks) * c, c)
                    o_ref[0, pl.ds(ro, c), hh * dk:(hh + 1) * dk] += r[dk:dk + c] + op_s[hh, d, i]
                new.append(s * gl_s[hh, d, i][0:1, :] + r[0:dk] + n_s[hh, d, i])
            return tuple(new)
        return lax.fori_loop(0, count, step, states)

    o_ref[...] = jnp.zeros(o_ref.shape, F32)
    zero_state = jnp.zeros((dk, dk), F32)
    states = scan(0, ctx_chunks, False, (zero_state,) * (2 * DN_HEADS_PER_STEP))
    scan(ctx_chunks, lat_chunks, True, states)


def _dn_prepare_head(hh, head, ii, jj, level_masks, lane, qkv_ctx_refs, qkv_lat_refs, conv_refs, g_s, gt_s,
                     x_scr, qkv_scr, kk_s, qk_s, mq_s, n_s, op_s, gl_s):
    c = DN_CHUNK
    n_ctx = qkv_ctx_refs[0].shape[1]
    seq = qkv_lat_refs[0].shape[1]
    n_chunks = (n_ctx + seq) // c
    ctx_chunks = n_ctx // c
    pad = SUBLANES
    dk = DN_HEAD_DIM
    half = DN_CONV_W // 2
    cols = slice(hh * dk, (hh + 1) * dk)
    xq_s, xk_s, xv_s = x_scr
    q_s, k_s, v_s = qkv_scr
    cq_ref, ck_ref, cv_ref = conv_refs

    zeros_pad = jnp.zeros((pad, dk), F32)
    lat0 = 2 * pad + n_ctx
    for src_ctx, src_lat, dst in zip(qkv_ctx_refs, qkv_lat_refs, x_scr):
        dst[0:pad, :] = zeros_pad
        dst[pad:pad + n_ctx, :] = src_ctx[0, :, cols]
        dst[pad + n_ctx:lat0, :] = zeros_pad
        dst[lat0:lat0 + seq, :] = src_lat[0, :, cols]
        dst[lat0 + seq:lat0 + seq + pad, :] = zeros_pad

    def conv_chunk(i, carry):
        r0 = pl.multiple_of(i * c, c)
        rp = r0 + jnp.where(i >= ctx_chunks, 2 * pad, pad)

        def conv(x_s, cw_ref):
            acc = None
            for s in range(DN_CONV_W):
                term = x_s[pl.ds(rp - half + s, c), :] * cw_ref[s:s + 1, cols]
                acc = term if acc is None else acc + term
            return _silu(acc)

        qc = conv(xq_s, cq_ref)
        kc = conv(xk_s, ck_ref)
        vc = conv(xv_s, cv_ref)
        qc = qc * lax.rsqrt(jnp.sum(qc * qc, axis=-1, keepdims=True) + EPS) * (dk ** -0.5)
        kc = kc * lax.rsqrt(jnp.sum(kc * kc, axis=-1, keepdims=True) + EPS)
        q_s[pl.ds(r0, c), :] = qc
        k_s[pl.ds(r0, c), :] = kc
        v_s[pl.ds(r0, c), :] = vc
        kb = kc.astype(BF16)
        kk_s[i] = _dot_nt(kb, kb)
        qk_s[i] = _dot_nt(qc.astype(BF16), kb)
        return carry

    lax.fori_loop(0, n_chunks, conv_chunk, 0, unroll=DN_CONV_UNROLL)

    def local_group(grp, carry):
        chains = []
        for k in range(DN_GROUP):
            i = grp * DN_GROUP + k
            r0 = pl.multiple_of(i * c, c)
            gates = g_s[pl.ds(r0, c), :]
            kc = k_s[pl.ds(r0, c), :]
            qc = q_s[pl.ds(r0, c), :]
            vc = v_s[pl.ds(r0, c), :]
            kk = kk_s[i]
            qk = qk_s[i]
            kt = kc.T
            for d in range(2):
                beta_col = jnp.sum(jnp.where(lane == d * DN_HEADS + head, gates, 0.0), axis=-1, keepdims=True)
                g_col = jnp.sum(jnp.where(lane == N_GATES // 2 + d * DN_HEADS + head, gates, 0.0),
                                axis=-1, keepdims=True)
                g_row = gt_s[pl.ds(N_GATES // 2 + d * DN_HEADS + head, 1), pl.ds(i, 1), :].reshape(1, c)
                incl = (jj <= ii) if d == 0 else (jj >= ii)
                strict = (jj < ii) if d == 0 else (jj > ii)
                incl_t = (ii <= jj) if d == 0 else (ii >= jj)
                gcum_col = jnp.sum(jnp.where(incl, g_row, 0.0), axis=1, keepdims=True)
                gcum_row = jnp.sum(jnp.where(incl_t, g_col, 0.0), axis=0, keepdims=True)
                total = jnp.sum(g_row, axis=1, keepdims=True)
                decay = jnp.exp(jnp.where(incl, gcum_col - gcum_row, -jnp.inf))
                e_col = jnp.exp(gcum_col)
                kdt = (kt * jnp.exp(total - gcum_row)).astype(BF16)
                qkm = (qk * decay).astype(BF16)
                chains.append(dict(
                    d=d, i=i,
                    a=jnp.where(strict, beta_col * kk * decay, 0.0),
                    rhs=jnp.concatenate([vc * beta_col, kc * (beta_col * e_col)], axis=1),
                    lhs=jnp.concatenate([kdt, qkm], axis=0),
                    qd=qc * e_col,
                    g_last=jnp.exp(total)))
        corrs = _unit_tri_inverses_minus_eye([ch["a"] for ch in chains], level_masks)
        sols = [ch["rhs"] + _dot(corr, ch["rhs"].astype(BF16)) for ch, corr in zip(chains, corrs)]
        prods = [_dot(ch["lhs"], sol.astype(BF16)) for ch, sol in zip(chains, sols)]
        for ch, r in zip(chains, prods):
            d, i = ch["d"], ch["i"]
            mq_s[hh, d, i, 0:dk, :] = (-r[0:dk, dk:2 * dk]).astype(BF16)
            mq_s[hh, d, i, dk:dk + c, :] = (ch["qd"] - r[dk:dk + c, dk:2 * dk]).astype(BF16)
            n_s[hh, d, i] = r[0:dk, 0:dk]
            op_s[hh, d, i] = r[dk:dk + c, 0:dk]
            gl_s[hh, d, i] = jnp.broadcast_to(ch["g_last"], (SUBLANES, dk))
        return carry

    lax.fori_loop(0, n_chunks // DN_GROUP, local_group, 0)


def _dn_call(dn_ctx, dn_lat, conv_w, g_ctx, g_lat, gt_ctx, gt_lat, n_ctx):
    bsz, seq, _ = dn_lat.shape
    tot = n_ctx + seq
    c = DN_CHUNK
    assert n_ctx % c == 0 and seq % c == 0
    n_chunks = tot // c
    ctx_chunks = n_ctx // c
    dk = DN_HEAD_DIM
    gtc4 = gt_ctx.reshape(N_GATES, bsz, ctx_chunks, c).transpose(1, 0, 2, 3)
    gtl4 = gt_lat.reshape(bsz, N_GATES, seq // c, c)
    hp = DN_HEADS_PER_STEP
    steps = DN_HEADS // hp
    col_ctx = lambda off: pl.BlockSpec((1, n_ctx, hp * dk), lambda b, h: (0, b, off + h))
    col_lat = lambda off: pl.BlockSpec((1, seq, hp * dk), lambda b, h: (b, 0, off + h))
    cw = lambda off: pl.BlockSpec((DN_CONV_W, hp * dk), lambda b, h: (0, off + h))
    padded = tot + 3 * SUBLANES
    assert n_chunks % DN_GROUP == 0 and n_chunks % DN_CONV_UNROLL == 0 and DN_HEADS % hp == 0
    return pl.pallas_call(
        _dn_kernel,
        grid=(bsz, steps),
        in_specs=[col_ctx(0), col_ctx(steps), col_ctx(2 * steps),
                  col_lat(0), col_lat(steps), col_lat(2 * steps),
                  cw(0), cw(steps), cw(2 * steps),
                  pl.BlockSpec((1, n_ctx, LANES), lambda b, h: (0, b, 0)),
                  pl.BlockSpec((1, seq, LANES), lambda b, h: (b, 0, 0)),
                  pl.BlockSpec((1, N_GATES, ctx_chunks, c), lambda b, h: (b, 0, 0, 0)),
                  pl.BlockSpec((1, N_GATES, seq // c, c), lambda b, h: (b, 0, 0, 0))],
        out_specs=pl.BlockSpec((1, seq, hp * dk), lambda b, h: (b, 0, h)),
        out_shape=jax.ShapeDtypeStruct((bsz, seq, DN_W), F32),
        scratch_shapes=[pltpu.VMEM((padded, dk), F32)] * 3
        + [pltpu.VMEM((tot, dk), F32)] * 3
        + [pltpu.VMEM((n_chunks, c, c), F32)] * 2
        + [pltpu.VMEM((tot, LANES), F32),
           pltpu.VMEM((N_GATES, n_chunks, c), F32)]
        + [pltpu.VMEM((hp, 2, n_chunks, dk + c, dk), BF16),
           pltpu.VMEM((hp, 2, n_chunks, dk, dk), F32),
           pltpu.VMEM((hp, 2, n_chunks, c, dk), F32),
           pltpu.VMEM((hp, 2, n_chunks, SUBLANES, dk), F32)],
        compiler_params=_params("arbitrary", "arbitrary"),
        name="dn",
    )(dn_ctx, dn_ctx, dn_ctx, dn_lat, dn_lat, dn_lat, conv_w, conv_w, conv_w, g_ctx, g_lat, gtc4, gtl4)


def _outproj_kernel(x_ref, att_ref, o_ref, z_ref, mod_ref, onw_ref, wo_ref, n2_ref, rw_ref,
                    x1_ref, h2_ref, afft_ref):
    d = x_ref.shape[-1]
    b = pl.program_id(0)
    gate1 = mod_ref[pl.ds(b, 1), 2 * d:3 * d]
    shift2 = mod_ref[pl.ds(b, 1), 3 * d:4 * d]
    scale2 = mod_ref[pl.ds(b, 1), 4 * d:5 * d]
    lane = lax.broadcasted_iota(I32, (1, LANES), 1)
    n_rows = x_ref.shape[1]

    def mixed(rows):
        parts = [att_ref[0, rows, :]]
        for h in range(DN_HEADS):
            sl = slice(h * DN_HEAD_DIM, (h + 1) * DN_HEAD_DIM)
            oh = o_ref[0, rows, sl]
            on = oh * lax.rsqrt(jnp.mean(oh * oh, axis=-1, keepdims=True) + EPS) * onw_ref[...]
            parts.append((on * _silu(z_ref[0, rows, sl])).astype(BF16))
        return _dot(jnp.concatenate(parts, axis=1), wo_ref[...])

    def finish(rows, y):
        x1 = x_ref[0, rows, :] + gate1 * y
        x1_ref[0, rows, :] = x1
        hn = x1 * lax.rsqrt(jnp.mean(x1 * x1, axis=-1, keepdims=True) + EPS) * n2_ref[...]
        h2 = (hn * (1.0 + scale2) + shift2).astype(BF16)
        h2_ref[0, rows, :] = h2
        logits = _dot(h2, rw_ref[...])
        logits = jnp.where(lane < N_EXPERTS, logits, -jnp.inf)
        e = jnp.exp(logits - jnp.max(logits, axis=-1, keepdims=True))
        aff = e / jnp.sum(e, axis=-1, keepdims=True)
        afft_ref[0, :, rows] = aff.T[0:N_EXPERTS, :]

    chunks = [slice(r0, r0 + OUT_ROW_CHUNK) for r0 in range(0, n_rows, OUT_ROW_CHUNK)]
    y_next = mixed(chunks[0])
    for r, rows in enumerate(chunks):
        y = y_next
        if r + 1 < len(chunks):
            y_next = mixed(chunks[r + 1])
        finish(rows, y)


def _outproj_call(x, att, o_dn, z, mod, onw, wo, n2, rw):
    bsz, seq, d = x.shape
    t = OUT_TILE
    assert seq % t == 0
    full = lambda shape: pl.BlockSpec(shape, lambda b, j: (0,) * len(shape))
    return pl.pallas_call(
        _outproj_kernel,
        grid=(bsz, seq // t),
        in_specs=[pl.BlockSpec((1, t, d), lambda b, j: (b, j, 0)),
                  pl.BlockSpec((1, t, ATTN_Q_W), lambda b, j: (b, j, 0)),
                  pl.BlockSpec((1, t, DN_W), lambda b, j: (b, j, 0)),
                  pl.BlockSpec((1, t, DN_W), lambda b, j: (b, j, 0)),
                  full((MOD_ROWS, mod.shape[1])),
                  full((1, DN_HEAD_DIM)),
                  full(wo.shape),
                  full((1, d)),
                  full((d, LANES))],
        out_specs=[pl.BlockSpec((1, t, d), lambda b, j: (b, j, 0)),
                   pl.BlockSpec((1, t, d), lambda b, j: (b, j, 0)),
                   pl.BlockSpec((1, N_EXPERTS, t), lambda b, j: (b, 0, j))],
        out_shape=[jax.ShapeDtypeStruct((bsz, seq, d), F32),
                   jax.ShapeDtypeStruct((bsz, seq, d), BF16),
                   jax.ShapeDtypeStruct((bsz, N_EXPERTS, seq), F32)],
        compiler_params=_params("arbitrary", "arbitrary"),
        name="outproj",
    )(x, att, o_dn, z, mod, onw, wo, n2, rw)


def _route_kernel(cap, afft_ref, slot_ref, gate_ref, tok_ref, bounds_ref):
    n_b, n_exp, n_t = afft_ref.shape
    n_e = n_b * n_exp
    aff = afft_ref[...].reshape(n_e, n_t)

    def enough(cand):
        return jnp.sum(jnp.where(aff >= cand, 1.0, 0.0), axis=-1, keepdims=True) >= cap

    tiny = 2.0 ** F32_MIN_EXP
    cur = jnp.full((n_e, 1), tiny, F32)
    any_normal = enough(cur)
    shift = 1 << (-F32_MIN_EXP).bit_length()
    while shift > 1:
        shift //= 2
        cand = cur * (2.0 ** shift)
        cur = jnp.where(enough(cand), cand, cur)

    def refine(_, state):
        cur, step = state
        cand = cur + step
        return jnp.where(enough(cand), cand, cur), step * 0.5

    cur, _ = lax.fori_loop(0, F32_MANTISSA_BITS, refine, (cur, cur * 0.5))
    thr = jnp.where(any_normal, cur, 0.0)
    need = cap - jnp.sum(jnp.where(aff > thr, 1.0, 0.0), axis=-1, keepdims=True)

    upper = (lax.broadcasted_iota(I32, (LANES, LANES), 0) < lax.broadcasted_iota(I32, (LANES, LANES), 1))
    upper = jnp.where(upper, 1.0, 0.0).astype(BF16)
    run = jnp.zeros((2 * n_e, 1), F32)
    lane = lax.broadcasted_iota(I32, (1, LANES), 1)
    bounds = jnp.zeros((n_e, LANES), F32)
    per_block = ROUTE_BLOCK // LANES
    for blk in range(n_t // LANES):
        sl = slice(blk * LANES, (blk + 1) * LANES)
        gt = aff[:, sl] > thr
        eq = aff[:, sl] == thr
        x = jnp.concatenate([jnp.where(gt, 1.0, 0.0), jnp.where(eq, 1.0, 0.0)], axis=0)
        cum = _dot(x.astype(BF16), upper) + run
        run = run + jnp.sum(x, axis=-1, keepdims=True)
        cum_gt = cum[0:n_e]
        cum_eq = cum[n_e:2 * n_e]
        sel = gt | (eq & (cum_eq < need))
        slot = jnp.where(sel, cum_gt + jnp.minimum(cum_eq, need), -1.0)
        slot_ref[:, :, sl] = slot.astype(I32).reshape(n_b, n_exp, LANES)
        gate_ref[:, :, sl] = jnp.where(sel, aff[:, sl], 0.0).reshape(n_b, n_exp, LANES)
        if (blk + 1) % per_block == 0:
            taken = run[0:n_e] + jnp.minimum(run[n_e:2 * n_e], need)
            bounds = jnp.where(lane == (blk + 1) // per_block, taken, bounds)
    bounds_ref[...] = bounds.astype(I32).reshape(n_b, n_exp, LANES)

    for b in range(n_b):
        stacked = jnp.concatenate([slot_ref[b].astype(F32), gate_ref[b],
                                   jnp.zeros((LANES - 2 * n_exp, n_t), F32)], axis=0)
        tok_ref[b] = stacked.T


def _route_call(afft, cap):
    bsz, n_e, n_t = afft.shape
    row = pl.BlockSpec((bsz, n_e, n_t), lambda i: (0, 0, 0))
    return pl.pallas_call(
        functools.partial(_route_kernel, cap),
        grid=(1,),
        in_specs=[row],
        out_specs=[row, row, pl.BlockSpec((bsz, n_t, LANES), lambda i: (0, 0, 0)),
                   pl.BlockSpec((bsz, n_e, LANES), lambda i: (0, 0, 0))],
        out_shape=[jax.ShapeDtypeStruct((bsz, n_e, n_t), I32),
                   jax.ShapeDtypeStruct((bsz, n_e, n_t), F32),
                   jax.ShapeDtypeStruct((bsz, n_t, LANES), F32),
                   jax.ShapeDtypeStruct((bsz, n_e, LANES), I32)],
        compiler_params=_params("arbitrary"),
        name="route",
    )(afft)


def _window_starts(cnt_ref, b, tb, n_e, cap):
    starts, fits = [], None
    for e in range(n_e):
        lo = cnt_ref[b, tb * n_e + e]
        hi = cnt_ref[b, (tb + 1) * n_e + e]
        start = jnp.minimum((lo // BF16_ROWS) * BF16_ROWS, cap - SLOT_WINDOW)
        ok = hi - start <= SLOT_WINDOW
        starts.append(pl.multiple_of(start, BF16_ROWS))
        fits = ok if fits is None else jnp.logical_and(fits, ok)
    return starts, fits


def _gather_kernel(cap, cnt_ref, slot_ref, gate_ref, h_ref, xg_ref, gs_ref):
    b = pl.program_id(0)
    n_e = slot_ref.shape[1]
    n_blk = slot_ref.shape[2]
    xg_ref[...] = jnp.zeros(xg_ref.shape, BF16)
    gs_ref[...] = jnp.zeros(gs_ref.shape, F32)

    def token_block(tb, carry):
        h_blk = h_ref[0, pl.ds(pl.multiple_of(tb * ROUTE_BLOCK, ROUTE_BLOCK), ROUTE_BLOCK), :]
        starts, fits = _window_starts(cnt_ref, b, tb, n_e, cap)

        def accumulate(window, first_rows):
            j = lax.broadcasted_iota(I32, (window, 1), 0)
            hits = [slot_ref[0, e, pl.ds(tb, 1), :] == first_rows[e] + j for e in range(n_e)]
            onehot = jnp.concatenate([jnp.where(hit, 1.0, 0.0).astype(BF16) for hit in hits], axis=0)
            rows = _dot(onehot, h_blk)
            for e in range(n_e):
                dst = pl.ds(first_rows[e], window)
                xg_ref[e, 0, dst, :] += rows[e * window:(e + 1) * window].astype(BF16)
                gate = gate_ref[0, e, pl.ds(tb, 1), :]
                gs_ref[e, 0, dst, :] += jnp.sum(jnp.where(hits[e], gate, 0.0), axis=-1, keepdims=True)

        @pl.when(fits)
        def _():
            accumulate(SLOT_WINDOW, starts)

        @pl.when(jnp.logical_not(fits))
        def _():
            accumulate(cap, [0] * n_e)

        return carry

    lax.fori_loop(0, n_blk, token_block, 0)


def _gather_call(cnt, slot, gate, h2, cap):
    bsz, n_e, n_t = slot.shape
    d = h2.shape[-1]
    n_blk = n_t // ROUTE_BLOCK
    blocked = lambda a: a.reshape(bsz, n_e, n_blk, ROUTE_BLOCK)
    row = pl.BlockSpec((1, n_e, n_blk, ROUTE_BLOCK), lambda b, cnt: (b, 0, 0, 0))
    return pl.pallas_call(
        functools.partial(_gather_kernel, cap),
        grid_spec=pltpu.PrefetchScalarGridSpec(
            num_scalar_prefetch=1,
            grid=(bsz,),
            in_specs=[row, row, pl.BlockSpec((1, n_t, d), lambda b, cnt: (b, 0, 0))],
            out_specs=[pl.BlockSpec((n_e, 1, cap, d), lambda b, cnt: (0, b, 0, 0)),
                       pl.BlockSpec((n_e, 1, cap, 1), lambda b, cnt: (0, b, 0, 0))]),
        out_shape=[jax.ShapeDtypeStruct((n_e, bsz, cap, d), BF16),
                   jax.ShapeDtypeStruct((n_e, bsz, cap, 1), F32)],
        compiler_params=_params("arbitrary"),
        name="gather",
    )(cnt, blocked(slot), blocked(gate), h2)


def _ffn_kernel(xg_ref, gs_ref, wg_ref, wu_ref, wd_ref, y_ref):
    bsz, cap, d = xg_ref.shape[1:]
    per = FFN_ROW_CHUNK // cap
    n_chunks = bsz // per
    wg = wg_ref[0].astype(BF16)
    wu = wu_ref[0].astype(BF16)
    wd = wd_ref[0].astype(BF16)

    def up(r):
        x = xg_ref[0, r * per:(r + 1) * per].reshape(per * cap, d)
        return _dot(x, wg), _dot(x, wu)

    nxt = up(0)
    for r in range(n_chunks):
        g, u = nxt
        if r + 1 < n_chunks:
            nxt = up(r + 1)
        y = _dot((_silu(g) * u).astype(BF16), wd) * gs_ref[0, r * per:(r + 1) * per].reshape(per * cap, 1)
        y_ref[0, r * per:(r + 1) * per] = y.astype(BF16).reshape(per, cap, d)


def _ffn_call(xg, gs, w_gate, w_up, w_down):
    n_e, bsz, cap, d = xg.shape
    ff = w_gate.shape[-1]
    rows = pl.BlockSpec((1, bsz, cap, d), lambda e: (e, 0, 0, 0))
    return pl.pallas_call(
        _ffn_kernel,
        grid=(n_e,),
        in_specs=[rows,
                  pl.BlockSpec((1, bsz, cap, 1), lambda e: (e, 0, 0, 0)),
                  pl.BlockSpec((1, d, ff), lambda e: (e, 0, 0)),
                  pl.BlockSpec((1, d, ff), lambda e: (e, 0, 0)),
                  pl.BlockSpec((1, ff, d), lambda e: (e, 0, 0))],
        out_specs=rows,
        out_shape=jax.ShapeDtypeStruct((n_e, bsz, cap, d), BF16),
        compiler_params=_params("arbitrary"),
        name="ffn",
    )(xg, gs, w_gate, w_up, w_down)


def _combine_kernel(cap, cnt_ref, x1_ref, tok_ref, y_ref, mod_ref, o_ref):
    d = x1_ref.shape[-1]
    n_e = y_ref.shape[0]
    b = pl.program_id(0)
    tb = pl.program_id(1)
    gate2 = mod_ref[pl.ds(b, 1), 5 * d:6 * d]
    tok = tok_ref[0]
    starts, fits = _window_starts(cnt_ref, b, tb, n_e, cap)

    @pl.when(fits)
    def _():
        lane = lax.broadcasted_iota(I32, (1, LANES), 1)
        first = lane < SLOT_WINDOW
        groups = []
        for e in range(0, n_e, 2):
            slot = jnp.where(first, tok[:, e:e + 1], tok[:, e + 1:e + 2])
            target = jnp.where(first, starts[e] + lane, starts[e + 1] + lane - SLOT_WINDOW).astype(F32)
            groups.append(jnp.where(slot == target, 1.0, 0.0).astype(BF16))
        onehot = jnp.concatenate(groups, axis=1)
        rows = jnp.concatenate([y_ref[e, 0, pl.ds(starts[e], SLOT_WINDOW), :] for e in range(n_e)], axis=0)
        o_ref[0] = x1_ref[0] + gate2 * _dot(onehot, rows)

    @pl.when(jnp.logical_not(fits))
    def _():
        j = lax.broadcasted_iota(I32, (1, cap), 1).astype(F32)
        acc = None
        for e in range(n_e):
            onehot = jnp.where(tok[:, e:e + 1] == j, 1.0, 0.0).astype(BF16)
            part = _dot(onehot, y_ref[e, 0])
            acc = part if acc is None else acc + part
        o_ref[0] = x1_ref[0] + gate2 * acc


def _combine_call(cnt, x1, tok, y, mod, cap):
    bsz, seq, d = x1.shape
    n_e = y.shape[0]
    t = ROUTE_BLOCK
    assert 2 * SLOT_WINDOW == LANES and n_e % 2 == 0
    return pl.pallas_call(
        functools.partial(_combine_kernel, cap),
        grid_spec=pltpu.PrefetchScalarGridSpec(
            num_scalar_prefetch=1,
            grid=(bsz, seq // t),
            in_specs=[pl.BlockSpec((1, t, d), lambda b, j, cnt: (b, j, 0)),
                      pl.BlockSpec((1, t, LANES), lambda b, j, cnt: (b, j, 0)),
                      pl.BlockSpec((n_e, 1, cap, d), lambda b, j, cnt: (0, b, 0, 0)),
                      pl.BlockSpec((MOD_ROWS, mod.shape[1]), lambda b, j, cnt: (0, 0))],
            out_specs=pl.BlockSpec((1, t, d), lambda b, j, cnt: (b, j, 0))),
        out_shape=jax.ShapeDtypeStruct((bsz, seq, d), F32),
        compiler_params=_params("arbitrary", "arbitrary"),
        name="combine",
    )(cnt, x1, tok, y, mod)


def _rope_tables(seq):
    m = ATTN_HEAD_DIM // 4
    pos = jnp.arange(seq, dtype=jnp.int32)
    rows = (pos // GRID_W).astype(F32)
    cols = (pos % GRID_W).astype(F32)
    freqs = ROPE_BASE ** (-jnp.arange(m, dtype=F32) / m)
    ang_r = rows[:, None] * freqs[None, :]
    ang_c = cols[:, None] * freqs[None, :]
    cos_h = jnp.concatenate([jnp.cos(ang_r), jnp.cos(ang_r), jnp.cos(ang_c), jnp.cos(ang_c)], axis=-1)
    sin_h = jnp.concatenate([-jnp.sin(ang_r), jnp.sin(ang_r), -jnp.sin(ang_c), jnp.sin(ang_c)], axis=-1)
    reps = LANES // ATTN_HEAD_DIM
    return jnp.tile(cos_h, (1, reps)), jnp.tile(sin_h, (1, reps))


def _lane_row(values, offset):
    return jnp.zeros((1, LANES), F32).at[0, offset:offset + values.shape[0]].set(values.astype(F32))


def kernel(x, c, ctx, c_ctx, w_mod, b_mod, norm1_w, norm2_w, w_in, q_norm_w, k_norm_w, conv_w, a_log, dt_bias,
           o_norm_w, w_out, router_w, w_gate, w_up, w_down):
    bsz, seq, d = x.shape
    n_ctx = ctx.shape[1]
    assert w_mod.shape[0] == 1, "single layer: the last layer's context outputs are never consumed"
    assert bsz < MOD_ROWS and N_EXPERTS == router_w.shape[-1]
    cap = EC_CAPACITY_FACTOR * seq // N_EXPERTS

    cc = jnp.concatenate([c, c_ctx[None, :], jnp.zeros((MOD_ROWS - bsz - 1, d), F32)], axis=0)
    mod = _mod_call(cc, w_mod[0], b_mod[0][None, :])

    w_pad = jnp.concatenate([w_in[0].astype(BF16), jnp.zeros((d, LANES - N_GATES), BF16)], axis=1)
    seg = np.arange(MXU_DIM) // ATTN_HEAD_DIM
    bd = jnp.asarray(seg[:, None] == seg[None, :], BF16)
    qkw = jnp.concatenate([jnp.tile(q_norm_w[0], ATTN_HEADS), jnp.tile(k_norm_w[0], ATTN_KV_HEADS)])[None, :]
    alog_l = _lane_row(a_log[0].reshape(-1), N_GATES // 2)
    dtb_l = _lane_row(dt_bias[0].reshape(-1), N_GATES // 2)
    shared = (mod, norm1_w[0][None, :], w_pad, bd, qkw, alog_l, dtb_l)
    kv_lat, dn_lat, g_lat, gt_lat, q, z = _inproj_call(x, bsz, *shared, rope=_rope_tables(seq))
    kv_ctx, dn_ctx, g_ctx, gt_ctx = _inproj_call(ctx.reshape(1, bsz * n_ctx, d), bsz, *shared)

    att = _attn_call(q, kv_ctx, kv_lat, n_ctx)
    o_dn = _dn_call(dn_ctx, dn_lat, conv_w[0], g_ctx, g_lat, gt_ctx, gt_lat, n_ctx)

    rw = jnp.pad(router_w[0], ((0, 0), (0, LANES - N_EXPERTS))).astype(BF16)
    x1, h2, afft = _outproj_call(x, att, o_dn, z, mod, o_norm_w[0][None, :], w_out[0].astype(BF16),
                                 norm2_w[0][None, :], rw)

    slot, gate, tok, bounds = _route_call(afft, cap)
    n_blk = seq // ROUTE_BLOCK
    cnt = bounds[:, :, 0:n_blk + 1].transpose(0, 2, 1).reshape(bsz, (n_blk + 1) * N_EXPERTS)
    xg, gs = _gather_call(cnt, slot, gate, h2, cap)
    y = _ffn_call(xg, gs, w_gate[0], w_up[0], w_down[0])
    return _combine_call(cnt, x1, tok, y, mod, cap)
```

```python
import functools

import numpy as np
import jax
import jax.numpy as jnp
from jax import lax
from jax.experimental import pallas as pl
from jax.experimental.pallas import tpu as pltpu

F32 = jnp.float32
BF16 = jnp.bfloat16
I32 = jnp.int32

GRID_W = 64
EPS = 1e-6
ATTN_HEADS = 8
ATTN_KV_HEADS = 2
ATTN_HEAD_DIM = 64
ROPE_BASE = 10000.0
DN_HEADS = 4
DN_HEAD_DIM = 128
DN_CONV_W = 5
N_EXPERTS = 16
EC_CAPACITY_FACTOR = 2

ATTN_Q_W = ATTN_HEADS * ATTN_HEAD_DIM
ATTN_KV_W = ATTN_KV_HEADS * ATTN_HEAD_DIM
DN_W = DN_HEADS * DN_HEAD_DIM
QK_W = ATTN_Q_W + ATTN_KV_W
ATT_W = ATTN_Q_W + 2 * ATTN_KV_W
N_GATES = 4 * DN_HEADS

LANES = 128
SUBLANES = 8
MXU_DIM = 256
VMEM_LIMIT = 56 * 1024 * 1024
F32_MIN_EXP = -126
F32_MANTISSA_BITS = 23

TOK_TILE = 1024
IN_ROW_CHUNK = 128
Q_TILE = 512
ATTN_KEY_CHUNK = 1024
OUT_TILE = 1024
OUT_ROW_CHUNK = 128
DN_CHUNK = 128
DN_GROUP = 9
DN_CONV_UNROLL = 9
DN_HEADS_PER_STEP = 2
ROUTE_BLOCK = 256
SLOT_WINDOW = 64
BF16_ROWS = 16
FFN_ROW_CHUNK = 512
MOD_ROWS = 16


def _params(*sem):
    return pltpu.CompilerParams(dimension_semantics=sem, vmem_limit_bytes=VMEM_LIMIT)


def _silu(v):
    half = 0.5 * v
    return half + half * jnp.tanh(half)


def _dot(a, b):
    return jnp.dot(a, b, preferred_element_type=F32)


def _dot_nt(a, b):
    return lax.dot_general(a, b, (((1,), (1,)), ((), ())), preferred_element_type=F32)


def _mod_kernel(c_ref, w_ref, b_ref, o_ref):
    sc = _silu(c_ref[...]).astype(BF16)
    o_ref[...] = _dot(sc, w_ref[...].astype(BF16)) + b_ref[...]


def _mod_call(cc, w_mod, b_mod):
    d, n = w_mod.shape
    return pl.pallas_call(
        _mod_kernel,
        grid=(n // d,),
        in_specs=[pl.BlockSpec((MOD_ROWS, d), lambda i: (0, 0)),
                  pl.BlockSpec((d, d), lambda i: (0, i)),
                  pl.BlockSpec((1, d), lambda i: (0, i))],
        out_specs=pl.BlockSpec((MOD_ROWS, d), lambda i: (0, i)),
        out_shape=jax.ShapeDtypeStruct((MOD_ROWS, n), F32),
        compiler_params=_params("arbitrary"),
        name="mod",
    )(cc, w_mod, b_mod)


def _inproj_kernel(latent, ctx_row, x_ref, mod_ref, n1_ref, w_ref, bd_ref, qkw_ref, alog_ref, dtb_ref, *refs):
    if latent:
        cos_ref, sin_ref, kv_ref, dn_ref, g_ref, gt_ref, q_ref, z_ref = refs
    else:
        kv_ref, dn_ref, g_ref, gt_ref = refs
    d = x_ref.shape[-1]
    n_rows = x_ref.shape[1]
    row = pl.program_id(0) if latent else ctx_row
    shift = mod_ref[pl.ds(row, 1), 0:d]
    scale = mod_ref[pl.ds(row, 1), d:2 * d]
    lane = lax.broadcasted_iota(I32, (1, LANES), 1)
    k_blk = ATTN_Q_W // LANES

    def project(rows):
        xin = x_ref[0, rows, :]
        hn = xin * lax.rsqrt(jnp.mean(xin * xin, axis=-1, keepdims=True) + EPS) * n1_ref[...]
        h = (hn * (1.0 + scale) + shift).astype(BF16)
        return _dot(h, w_ref[...])

    def finish(rows, p):
        qk = p[:, 0:QK_W]
        sq = qk * qk
        hi = sq.astype(BF16)
        lo = (sq - hi.astype(F32)).astype(BF16)
        seg_w = bd_ref.shape[0]
        ms = []
        for c0 in range(0, QK_W, seg_w):
            w = min(seg_w, QK_W - c0)
            ones = bd_ref[0:w, 0:w]
            ms.append(_dot(hi[:, c0:c0 + w], ones) + _dot(lo[:, c0:c0 + w], ones))
        ms = jnp.concatenate(ms, axis=1) * (1.0 / ATTN_HEAD_DIM)
        qkn = qk * lax.rsqrt(ms + EPS) * qkw_ref[...]

        if latent:
            first_half = (lane % 32) < 16
            cos = cos_ref[rows, :]
            sin = sin_ref[rows, :]
            rot = []
            for i in range(QK_W // LANES):
                blk = qkn[:, i * LANES:(i + 1) * LANES]
                partner = jnp.where(first_half, pltpu.roll(blk, LANES - 16, axis=1), pltpu.roll(blk, 16, axis=1))
                rot.append(blk * cos + partner * sin)
            for i in range(k_blk):
                q_ref[0, rows, i * LANES:(i + 1) * LANES] = (rot[i] * (ATTN_HEAD_DIM ** -0.5)).astype(BF16)
            k2 = rot[k_blk]
            z_ref[0, rows, :] = p[:, ATT_W + 3 * DN_W:ATT_W + 4 * DN_W]
        else:
            k2 = qkn[:, k_blk * LANES:(k_blk + 1) * LANES]

        low = lane < ATTN_HEAD_DIM
        v2 = p[:, QK_W:ATT_W]
        for src, base in ((k2, 0), (v2, 2)):
            swapped = pltpu.roll(src, ATTN_HEAD_DIM, axis=1)
            kv_ref[0, 0, base + 0, rows, :] = jnp.where(low, src, 0.0).astype(BF16)
            kv_ref[0, 0, base + 1, rows, :] = jnp.where(low, 0.0, swapped).astype(BF16)
            kv_ref[0, 1, base + 0, rows, :] = jnp.where(low, swapped, 0.0).astype(BF16)
            kv_ref[0, 1, base + 1, rows, :] = jnp.where(low, 0.0, src).astype(BF16)

        dn_ref[0, rows, :] = p[:, ATT_W:ATT_W + 3 * DN_W]

        gp = p[:, ATT_W + 4 * DN_W:]
        beta = jax.nn.sigmoid(gp)
        xa = gp + dtb_ref[...]
        softplus = jnp.maximum(xa, 0.0) + jnp.log1p(jnp.exp(-jnp.abs(xa)))
        decay = -jnp.exp(alog_ref[...]) * softplus
        gates = jnp.where(lane < N_GATES // 2, beta, jnp.where(lane < N_GATES, decay, 0.0))
        g_ref[0, rows, :] = gates
        gt_ref[0, :, rows] = gates.T[0:N_GATES, :]

    chunks = [slice(r0, r0 + IN_ROW_CHUNK) for r0 in range(0, n_rows, IN_ROW_CHUNK)]
    p_next = project(chunks[0])
    for r, rows in enumerate(chunks):
        p = p_next
        if r + 1 < len(chunks):
            p_next = project(chunks[r + 1])
        finish(rows, p)


def _inproj_call(rows3, ctx_row, mod, n1, w_pad, bd, qkw, alog_l, dtb_l, rope=None):
    latent = rope is not None
    grp, rows, d = rows3.shape
    t = TOK_TILE
    assert rows % t == 0
    nw = w_pad.shape[1]
    full = lambda shape: pl.BlockSpec(shape, lambda b, j: (0,) * len(shape))
    tile = lambda width: pl.BlockSpec((1, t, width), lambda b, j: (b, j, 0))
    in_specs = [tile(d), full((MOD_ROWS, mod.shape[1])), full((1, d)), full((d, nw)), full(bd.shape),
                full((1, QK_W)), full((1, LANES)), full((1, LANES))]
    out_specs = [pl.BlockSpec((1, ATTN_KV_HEADS, 4, t, LANES), lambda b, j: (b, 0, 0, j, 0)),
                 tile(3 * DN_W), tile(LANES), pl.BlockSpec((1, N_GATES, t), lambda b, j: (b, 0, j))]
    out_shape = [jax.ShapeDtypeStruct((grp, ATTN_KV_HEADS, 4, rows, LANES), BF16),
                 jax.ShapeDtypeStruct((grp, rows, 3 * DN_W), F32),
                 jax.ShapeDtypeStruct((grp, rows, LANES), F32),
                 jax.ShapeDtypeStruct((grp, N_GATES, rows), F32)]
    args = [rows3, mod, n1, w_pad, bd, qkw, alog_l, dtb_l]
    if latent:
        in_specs += [pl.BlockSpec((t, LANES), lambda b, j: (j, 0))] * 2
        out_specs += [tile(ATTN_Q_W), tile(DN_W)]
        out_shape += [jax.ShapeDtypeStruct((grp, rows, ATTN_Q_W), BF16), jax.ShapeDtypeStruct((grp, rows, DN_W), F32)]
        args += list(rope)
    return pl.pallas_call(
        functools.partial(_inproj_kernel, latent, ctx_row),
        grid=(grp, rows // t),
        in_specs=in_specs,
        out_specs=out_specs,
        out_shape=out_shape,
        compiler_params=_params("arbitrary", "arbitrary"),
        name="inproj_lat" if latent else "inproj_ctx",
    )(*args)


def _attn_kernel(q_ref, kvc_ref, kvl_ref, o_ref):
    grp = ATTN_HEADS // ATTN_KV_HEADS
    n_lat = kvl_ref.shape[3]

    def scores(h):
        qp = q_ref[0, :, (h // 2) * LANES:(h // 2 + 1) * LANES]
        return (_dot_nt(qp, kvc_ref[0, h // grp, h % 2]),
                _dot_nt(qp, kvl_ref[0, h // grp, h % 2]))

    s_next = scores(0)
    acc = None
    for h in range(ATTN_HEADS):
        s_ctx, s_lat = s_next
        if h + 1 < ATTN_HEADS:
            s_next = scores(h + 1)
        m = jnp.maximum(jnp.max(s_ctx, axis=-1, keepdims=True), jnp.max(s_lat, axis=-1, keepdims=True))
        stages = [(s_ctx, kvc_ref, 0, s_ctx.shape[-1])]
        stages += [(s_lat, kvl_ref, k0, ATTN_KEY_CHUNK) for k0 in range(0, n_lat, ATTN_KEY_CHUNK)]
        o = denom = None
        for s, v_ref, k0, width in stages:
            e = jnp.exp(s[:, k0:k0 + width] - m)
            part = _dot(e.astype(BF16), v_ref[0, h // grp, 2 + h % 2, k0:k0 + width, :])
            part_sum = jnp.sum(e, axis=-1, keepdims=True)
            o = part if o is None else o + part
            denom = part_sum if denom is None else denom + part_sum
        o = o / denom
        if h % 2 == 0:
            acc = o
        else:
            o_ref[0, :, (h // 2) * LANES:(h // 2 + 1) * LANES] = (acc + o).astype(BF16)


def _attn_call(q, kv_ctx, kv_lat, n_ctx):
    bsz, seq, _ = q.shape
    tq = Q_TILE
    assert seq % tq == 0 and seq % ATTN_KEY_CHUNK == 0
    return pl.pallas_call(
        _attn_kernel,
        grid=(bsz, seq // tq),
        in_specs=[pl.BlockSpec((1, tq, ATTN_Q_W), lambda b, i: (b, i, 0)),
                  pl.BlockSpec((1, ATTN_KV_HEADS, 4, n_ctx, LANES), lambda b, i: (0, 0, 0, b, 0)),
                  pl.BlockSpec((1, ATTN_KV_HEADS, 4, seq, LANES), lambda b, i: (b, 0, 0, 0, 0))],
        out_specs=pl.BlockSpec((1, tq, ATTN_Q_W), lambda b, i: (b, i, 0)),
        out_shape=jax.ShapeDtypeStruct((bsz, seq, ATTN_Q_W), BF16),
        compiler_params=_params("arbitrary", "arbitrary"),
        name="attn",
    )(q, kv_ctx, kv_lat)


def _unit_tri_inverses_minus_eye(mats, level_masks):
    strip_masks, last_join, left, right = level_masks
    n = mats[0].shape[0]
    half = n // 2

    def to_strip(m):
        return m[0:half, :] * left + m[half:, :] * right

    def from_strip(s):
        return jnp.concatenate([s * left, s * right], axis=0)

    mats = [a.astype(BF16) for a in mats]
    strips = [to_strip(a) for a in mats]
    corrs = [-(s * strip_masks[0]) for s in strips]
    for joins in strip_masks[1:]:
        xs = [s * joins for s in strips]
        ys = [x.astype(F32) + _dot(corr, from_strip(x)) for corr, x in zip(corrs, xs)]
        corrs = [corr - (y + _dot(y.astype(BF16), from_strip(corr))).astype(BF16) for corr, y in zip(corrs, ys)]
    corrs = [from_strip(corr) for corr in corrs]
    xs = [a * last_join for a in mats]
    ys = [x.astype(F32) + _dot(corr, x) for corr, x in zip(corrs, xs)]
    return [corr - (y + _dot(y.astype(BF16), corr)).astype(BF16) for corr, y in zip(corrs, ys)]


def _tri_level_masks(n):
    ij_xor = lax.broadcasted_iota(I32, (n, n), 0) ^ lax.broadcasted_iota(I32, (n, n), 1)
    half = n // 2
    lane = lax.broadcasted_iota(I32, (1, n), 1)
    left = jnp.where(lane < half, 1.0, 0.0).astype(BF16)
    right = jnp.where(lane >= half, 1.0, 0.0).astype(BF16)
    strip_masks, m = [], 1
    while m < half:
        full = jnp.where((ij_xor >= m) & (ij_xor < 2 * m), 1.0, 0.0).astype(BF16)
        strip_masks.append(full[0:half, :] * left + full[half:, :] * right)
        m *= 2
    last_join = jnp.where((ij_xor >= half) & (ij_xor < n), 1.0, 0.0).astype(BF16)
    return strip_masks, last_join, left, right


def _dn_kernel(dqc_ref, dkc_ref, dvc_ref, dql_ref, dkl_ref, dvl_ref, cq_ref, ck_ref, cv_ref,
               gc_ref, gl_ref, gtc_ref, gtl_ref, o_ref,
               xq_s, xk_s, xv_s, q_s, k_s, v_s, kk_s, qk_s, g_s, gt_s, mq_s, n_s, op_s, gl_s):
    qkv_ctx_refs = (dqc_ref, dkc_ref, dvc_ref)
    qkv_lat_refs = (dql_ref, dkl_ref, dvl_ref)
    conv_refs = (cq_ref, ck_ref, cv_ref)
    x_scr = (xq_s, xk_s, xv_s)
    qkv_scr = (q_s, k_s, v_s)
    c = DN_CHUNK
    n_ctx = gc_ref.shape[1]
    ctx_chunks = n_ctx // c
    lat_chunks = gl_ref.shape[1] // c
    dk = DN_HEAD_DIM
    ii = lax.broadcasted_iota(I32, (c, c), 0)
    jj = lax.broadcasted_iota(I32, (c, c), 1)
    level_masks = _tri_level_masks(c)
    lane = lax.broadcasted_iota(I32, (1, LANES), 1)

    g_s[0:n_ctx, :] = gc_ref[0]
    g_s[n_ctx:, :] = gl_ref[0]
    gt_s[:, 0:ctx_chunks, :] = gtc_ref[0]
    gt_s[:, ctx_chunks:, :] = gtl_ref[0]

    for hh in range(DN_HEADS_PER_STEP):
        _dn_prepare_head(hh, pl.program_id(1) * DN_HEADS_PER_STEP + hh, ii, jj, level_masks, lane,
                         qkv_ctx_refs, qkv_lat_refs, conv_refs, g_s, gt_s,
                         x_scr, qkv_scr, kk_s, qk_s, mq_s, n_s, op_s, gl_s)

    def scan(first_chunk, count, emit, states):
        rows = dk + c if emit else dk

        def step(t, st):
            chains = [(hh, d, first_chunk + t if d == 0 else first_chunk + count - 1 - t)
                      for hh in range(DN_HEADS_PER_STEP) for d in range(2)]
            prods = [_dot(mq_s[hh, d, i, 0:rows, :], s.astype(BF16)) for (hh, d, i), s in zip(chains, st)]
            new = []
            for (hh, d, i), s, r in zip(chains, st, prods):
                if emit:
                    ro = pl.multiple_of((i - ctx_chunks) * c, c)
                    o_ref[0, pl.ds(ro, c), hh * dk:(hh + 1) * dk] += r[dk:dk + c] + op_s[hh, d, i]
                new.append(s * gl_s[hh, d, i][0:1, :] + r[0:dk] + n_s[hh, d, i])
            return tuple(new)
        return lax.fori_loop(0, count, step, states)

    o_ref[...] = jnp.zeros(o_ref.shape, F32)
    zero_state = jnp.zeros((dk, dk), F32)
    states = scan(0, ctx_chunks, False, (zero_state,) * (2 * DN_HEADS_PER_STEP))
    scan(ctx_chunks, lat_chunks, True, states)


def _dn_prepare_head(hh, head, ii, jj, level_masks, lane, qkv_ctx_refs, qkv_lat_refs, conv_refs, g_s, gt_s,
                     x_scr, qkv_scr, kk_s, qk_s, mq_s, n_s, op_s, gl_s):
    c = DN_CHUNK
    n_ctx = qkv_ctx_refs[0].shape[1]
    seq = qkv_lat_refs[0].shape[1]
    n_chunks = (n_ctx + seq) // c
    ctx_chunks = n_ctx // c
    pad = SUBLANES
    dk = DN_HEAD_DIM
    half = DN_CONV_W // 2
    cols = slice(hh * dk, (hh + 1) * dk)
    xq_s, xk_s, xv_s = x_scr
    q_s, k_s, v_s = qkv_scr
    cq_ref, ck_ref, cv_ref = conv_refs

    zeros_pad = jnp.zeros((pad, dk), F32)
    lat0 = 2 * pad + n_ctx
    for src_ctx, src_lat, dst in zip(qkv_ctx_refs, qkv_lat_refs, x_scr):
        dst[0:pad, :] = zeros_pad
        dst[pad:pad + n_ctx, :] = src_ctx[0, :, cols]
        dst[pad + n_ctx:lat0, :] = zeros_pad
        dst[lat0:lat0 + seq, :] = src_lat[0, :, cols]
        dst[lat0 + seq:lat0 + seq + pad, :] = zeros_pad

    def conv_chunk(i, carry):
        r0 = pl.multiple_of(i * c, c)
        rp = r0 + jnp.where(i >= ctx_chunks, 2 * pad, pad)

        def conv(x_s, cw_ref):
            acc = None
            for s in range(DN_CONV_W):
                term = x_s[pl.ds(rp - half + s, c), :] * cw_ref[s:s + 1, cols]
                acc = term if acc is None else acc + term
            return _silu(acc)

        qc = conv(xq_s, cq_ref)
        kc = conv(xk_s, ck_ref)
        vc = conv(xv_s, cv_ref)
        qc = qc * lax.rsqrt(jnp.sum(qc * qc, axis=-1, keepdims=True) + EPS) * (dk ** -0.5)
        kc = kc * lax.rsqrt(jnp.sum(kc * kc, axis=-1, keepdims=True) + EPS)
        q_s[pl.ds(r0, c), :] = qc
        k_s[pl.ds(r0, c), :] = kc
        v_s[pl.ds(r0, c), :] = vc
        kb = kc.astype(BF16)
        kk_s[i] = _dot_nt(kb, kb)
        qk_s[i] = _dot_nt(qc.astype(BF16), kb)
        return carry

    lax.fori_loop(0, n_chunks, conv_chunk, 0, unroll=DN_CONV_UNROLL)

    def local_group(grp, carry):
        chains = []
        for k in range(DN_GROUP):
            i = grp * DN_GROUP + k
            r0 = pl.multiple_of(i * c, c)
            gates = g_s[pl.ds(r0, c), :]
            kc = k_s[pl.ds(r0, c), :]
            qc = q_s[pl.ds(r0, c), :]
            vc = v_s[pl.ds(r0, c), :]
            kk = kk_s[i]
            qk = qk_s[i]
            kt = kc.T
            for d in range(2):
                beta_col = jnp.sum(jnp.where(lane == d * DN_HEADS + head, gates, 0.0), axis=-1, keepdims=True)
                g_col = jnp.sum(jnp.where(lane == N_GATES // 2 + d * DN_HEADS + head, gates, 0.0),
                                axis=-1, keepdims=True)
                g_row = gt_s[pl.ds(N_GATES // 2 + d * DN_HEADS + head, 1), pl.ds(i, 1), :].reshape(1, c)
                incl = (jj <= ii) if d == 0 else (jj >= ii)
                strict = (jj < ii) if d == 0 else (jj > ii)
                incl_t = (ii <= jj) if d == 0 else (ii >= jj)
                gcum_col = jnp.sum(jnp.where(incl, g_row, 0.0), axis=1, keepdims=True)
                gcum_row = jnp.sum(jnp.where(incl_t, g_col, 0.0), axis=0, keepdims=True)
                total = jnp.sum(g_row, axis=1, keepdims=True)
                decay = jnp.exp(jnp.where(incl, gcum_col - gcum_row, -jnp.inf))
                e_col = jnp.exp(gcum_col)
                kdt = (kt * jnp.exp(total - gcum_row)).astype(BF16)
                qkm = (qk * decay).astype(BF16)
                chains.append(dict(
                    d=d, i=i,
                    a=jnp.where(strict, beta_col * kk * decay, 0.0),
                    rhs=jnp.concatenate([vc * beta_col, kc * (beta_col * e_col)], axis=1),
                    lhs=jnp.concatenate([kdt, qkm], axis=0),
                    qd=qc * e_col,
                    g_last=jnp.exp(total)))
        corrs = _unit_tri_inverses_minus_eye([ch["a"] for ch in chains], level_masks)
        sols = [ch["rhs"] + _dot(corr, ch["rhs"].astype(BF16)) for ch, corr in zip(chains, corrs)]
        prods = [_dot(ch["lhs"], sol.astype(BF16)) for ch, sol in zip(chains, sols)]
        for ch, r in zip(chains, prods):
            d, i = ch["d"], ch["i"]
            mq_s[hh, d, i, 0:dk, :] = (-r[0:dk, dk:2 * dk]).astype(BF16)
            mq_s[hh, d, i, dk:dk + c, :] = (ch["qd"] - r[dk:dk + c, dk:2 * dk]).astype(BF16)
            n_s[hh, d, i] = r[0:dk, 0:dk]
            op_s[hh, d, i] = r[dk:dk + c, 0:dk]
            gl_s[hh, d, i] = jnp.broadcast_to(ch["g_last"], (SUBLANES, dk))
        return carry

    lax.fori_loop(0, n_chunks // DN_GROUP, local_group, 0)


def _dn_call(dn_ctx, dn_lat, conv_w, g_ctx, g_lat, gt_ctx, gt_lat, n_ctx):
    bsz, seq, _ = dn_lat.shape
    tot = n_ctx + seq
    c = DN_CHUNK
    assert n_ctx % c == 0 and seq % c == 0
    n_chunks = tot // c
    ctx_chunks = n_ctx // c
    dk = DN_HEAD_DIM
    gtc4 = gt_ctx.reshape(N_GATES, bsz, ctx_chunks, c).transpose(1, 0, 2, 3)
    gtl4 = gt_lat.reshape(bsz, N_GATES, seq // c, c)
    hp = DN_HEADS_PER_STEP
    steps = DN_HEADS // hp
    col_ctx = lambda off: pl.BlockSpec((1, n_ctx, hp * dk), lambda b, h: (0, b, off + h))
    col_lat = lambda off: pl.BlockSpec((1, seq, hp * dk), lambda b, h: (b, 0, off + h))
    cw = lambda off: pl.BlockSpec((DN_CONV_W, hp * dk), lambda b, h: (0, off + h))
    padded = tot + 3 * SUBLANES
    assert n_chunks % DN_GROUP == 0 and n_chunks % DN_CONV_UNROLL == 0 and DN_HEADS % hp == 0
    return pl.pallas_call(
        _dn_kernel,
        grid=(bsz, steps),
        in_specs=[col_ctx(0), col_ctx(steps), col_ctx(2 * steps),
                  col_lat(0), col_lat(steps), col_lat(2 * steps),
                  cw(0), cw(steps), cw(2 * steps),
                  pl.BlockSpec((1, n_ctx, LANES), lambda b, h: (0, b, 0)),
                  pl.BlockSpec((1, seq, LANES), lambda b, h: (b, 0, 0)),
                  pl.BlockSpec((1, N_GATES, ctx_chunks, c), lambda b, h: (b, 0, 0, 0)),
                  pl.BlockSpec((1, N_GATES, seq // c, c), lambda b, h: (b, 0, 0, 0))],
        out_specs=pl.BlockSpec((1, seq, hp * dk), lambda b, h: (b, 0, h)),
        out_shape=jax.ShapeDtypeStruct((bsz, seq, DN_W), F32),
        scratch_shapes=[pltpu.VMEM((padded, dk), F32)] * 3
        + [pltpu.VMEM((tot, dk), F32)] * 3
        + [pltpu.VMEM((n_chunks, c, c), F32)] * 2
        + [pltpu.VMEM((tot, LANES), F32),
           pltpu.VMEM((N_GATES, n_chunks, c), F32)]
        + [pltpu.VMEM((hp, 2, n_chunks, dk + c, dk), BF16),
           pltpu.VMEM((hp, 2, n_chunks, dk, dk), F32),
           pltpu.VMEM((hp, 2, n_chunks, c, dk), F32),
           pltpu.VMEM((hp, 2, n_chunks, SUBLANES, dk), F32)],
        compiler_params=_params("arbitrary", "arbitrary"),
        name="dn",
    )(dn_ctx, dn_ctx, dn_ctx, dn_lat, dn_lat, dn_lat, conv_w, conv_w, conv_w, g_ctx, g_lat, gtc4, gtl4)


def _outproj_kernel(x_ref, att_ref, o_ref, z_ref, mod_ref, onw_ref, wo_ref, n2_ref, rw_ref,
                    x1_ref, h2_ref, afft_ref):
    d = x_ref.shape[-1]
    b = pl.program_id(0)
    gate1 = mod_ref[pl.ds(b, 1), 2 * d:3 * d]
    shift2 = mod_ref[pl.ds(b, 1), 3 * d:4 * d]
    scale2 = mod_ref[pl.ds(b, 1), 4 * d:5 * d]
    lane = lax.broadcasted_iota(I32, (1, LANES), 1)
    n_rows = x_ref.shape[1]

    def mixed(rows):
        parts = [att_ref[0, rows, :]]
        for h in range(DN_HEADS):
            sl = slice(h * DN_HEAD_DIM, (h + 1) * DN_HEAD_DIM)
            oh = o_ref[0, rows, sl]
            on = oh * lax.rsqrt(jnp.mean(oh * oh, axis=-1, keepdims=True) + EPS) * onw_ref[...]
            parts.append((on * _silu(z_ref[0, rows, sl])).astype(BF16))
        return _dot(jnp.concatenate(parts, axis=1), wo_ref[...])

    def finish(rows, y):
        x1 = x_ref[0, rows, :] + gate1 * y
        x1_ref[0, rows, :] = x1
        hn = x1 * lax.rsqrt(jnp.mean(x1 * x1, axis=-1, keepdims=True) + EPS) * n2_ref[...]
        h2 = (hn * (1.0 + scale2) + shift2).astype(BF16)
        h2_ref[0, rows, :] = h2
        logits = _dot(h2, rw_ref[...])
        logits = jnp.where(lane < N_EXPERTS, logits, -jnp.inf)
        e = jnp.exp(logits - jnp.max(logits, axis=-1, keepdims=True))
        aff = e / jnp.sum(e, axis=-1, keepdims=True)
        afft_ref[0, :, rows] = aff.T[0:N_EXPERTS, :]

    chunks = [slice(r0, r0 + OUT_ROW_CHUNK) for r0 in range(0, n_rows, OUT_ROW_CHUNK)]
    y_next = mixed(chunks[0])
    for r, rows in enumerate(chunks):
        y = y_next
        if r + 1 < len(chunks):
            y_next = mixed(chunks[r + 1])
        finish(rows, y)


def _outproj_call(x, att, o_dn, z, mod, onw, wo, n2, rw):
    bsz, seq, d = x.shape
    t = OUT_TILE
    assert seq % t == 0
    full = lambda shape: pl.BlockSpec(shape, lambda b, j: (0,) * len(shape))
    return pl.pallas_call(
        _outproj_kernel,
        grid=(bsz, seq // t),
        in_specs=[pl.BlockSpec((1, t, d), lambda b, j: (b, j, 0)),
                  pl.BlockSpec((1, t, ATTN_Q_W), lambda b, j: (b, j, 0)),
                  pl.BlockSpec((1, t, DN_W), lambda b, j: (b, j, 0)),
                  pl.BlockSpec((1, t, DN_W), lambda b, j: (b, j, 0)),
                  full((MOD_ROWS, mod.shape[1])),
                  full((1, DN_HEAD_DIM)),
                  full(wo.shape),
                  full((1, d)),
                  full((d, LANES))],
        out_specs=[pl.BlockSpec((1, t, d), lambda b, j: (b, j, 0)),
                   pl.BlockSpec((1, t, d), lambda b, j: (b, j, 0)),
                   pl.BlockSpec((1, N_EXPERTS, t), lambda b, j: (b, 0, j))],
        out_shape=[jax.ShapeDtypeStruct((bsz, seq, d), F32),
                   jax.ShapeDtypeStruct((bsz, seq, d), BF16),
                   jax.ShapeDtypeStruct((bsz, N_EXPERTS, seq), F32)],
        compiler_params=_params("arbitrary", "arbitrary"),
        name="outproj",
    )(x, att, o_dn, z, mod, onw, wo, n2, rw)


def _route_kernel(cap, afft_ref, slot_ref, gate_ref, tok_ref, bounds_ref):
    n_b, n_exp, n_t = afft_ref.shape
    n_e = n_b * n_exp
    aff = afft_ref[...].reshape(n_e, n_t)

    def enough(cand):
        return jnp.sum(jnp.where(aff >= cand, 1.0, 0.0), axis=-1, keepdims=True) >= cap

    tiny = 2.0 ** F32_MIN_EXP
    cur = jnp.full((n_e, 1), tiny, F32)
    any_normal = enough(cur)
    shift = 1 << (-F32_MIN_EXP).bit_length()
    while shift > 1:
        shift //= 2
        cand = cur * (2.0 ** shift)
        cur = jnp.where(enough(cand), cand, cur)

    def refine(_, state):
        cur, step = state
        cand = cur + step
        return jnp.where(enough(cand), cand, cur), step * 0.5

    cur, _ = lax.fori_loop(0, F32_MANTISSA_BITS, refine, (cur, cur * 0.5))
    thr = jnp.where(any_normal, cur, 0.0)
    need = cap - jnp.sum(jnp.where(aff > thr, 1.0, 0.0), axis=-1, keepdims=True)

    upper = (lax.broadcasted_iota(I32, (LANES, LANES), 0) < lax.broadcasted_iota(I32, (LANES, LANES), 1))
    upper = jnp.where(upper, 1.0, 0.0).astype(BF16)
    run = jnp.zeros((2 * n_e, 1), F32)
    lane = lax.broadcasted_iota(I32, (1, LANES), 1)
    bounds = jnp.zeros((n_e, LANES), F32)
    per_block = ROUTE_BLOCK // LANES
    for blk in range(n_t // LANES):
        sl = slice(blk * LANES, (blk + 1) * LANES)
        gt = aff[:, sl] > thr
        eq = aff[:, sl] == thr
        x = jnp.concatenate([jnp.where(gt, 1.0, 0.0), jnp.where(eq, 1.0, 0.0)], axis=0)
        cum = _dot(x.astype(BF16), upper) + run
        run = run + jnp.sum(x, axis=-1, keepdims=True)
        cum_gt = cum[0:n_e]
        cum_eq = cum[n_e:2 * n_e]
        sel = gt | (eq & (cum_eq < need))
        slot = jnp.where(sel, cum_gt + jnp.minimum(cum_eq, need), -1.0)
        slot_ref[:, :, sl] = slot.astype(I32).reshape(n_b, n_exp, LANES)
        gate_ref[:, :, sl] = jnp.where(sel, aff[:, sl], 0.0).reshape(n_b, n_exp, LANES)
        if (blk + 1) % per_block == 0:
            taken = run[0:n_e] + jnp.minimum(run[n_e:2 * n_e], need)
            bounds = jnp.where(lane == (blk + 1) // per_block, taken, bounds)
    bounds_ref[...] = bounds.astype(I32).reshape(n_b, n_exp, LANES)

    for b in range(n_b):
        stacked = jnp.concatenate([slot_ref[b].astype(F32), gate_ref[b],
                                   jnp.zeros((LANES - 2 * n_exp, n_t), F32)], axis=0)
        tok_ref[b] = stacked.T


def _route_call(afft, cap):
    bsz, n_e, n_t = afft.shape
    row = pl.BlockSpec((bsz, n_e, n_t), lambda i: (0, 0, 0))
    return pl.pallas_call(
        functools.partial(_route_kernel, cap),
        grid=(1,),
        in_specs=[row],
        out_specs=[row, row, pl.BlockSpec((bsz, n_t, LANES), lambda i: (0, 0, 0)),
                   pl.BlockSpec((bsz, n_e, LANES), lambda i: (0, 0, 0))],
        out_shape=[jax.ShapeDtypeStruct((bsz, n_e, n_t), I32),
                   jax.ShapeDtypeStruct((bsz, n_e, n_t), F32),
                   jax.ShapeDtypeStruct((bsz, n_t, LANES), F32),
                   jax.ShapeDtypeStruct((bsz, n_e, LANES), I32)],
        compiler_params=_params("arbitrary"),
        name="route",
    )(afft)


def _window_starts(cnt_ref, b, tb, n_e, cap):
    starts, fits = [], None
    for e in range(n_e):
        lo = cnt_ref[b, tb * n_e + e]
        hi = cnt_ref[b, (tb + 1) * n_e + e]
        start = jnp.minimum((lo // BF16_ROWS) * BF16_ROWS, cap - SLOT_WINDOW)
        ok = hi - start <= SLOT_WINDOW
        starts.append(pl.multiple_of(start, BF16_ROWS))
        fits = ok if fits is None else jnp.logical_and(fits, ok)
    return starts, fits


def _gather_kernel(cap, cnt_ref, slot_ref, gate_ref, h_ref, xg_ref, gs_ref):
    b = pl.program_id(0)
    n_e = slot_ref.shape[1]
    n_blk = slot_ref.shape[2]
    xg_ref[...] = jnp.zeros(xg_ref.shape, BF16)
    gs_ref[...] = jnp.zeros(gs_ref.shape, F32)

    def token_block(tb, carry):
        h_blk = h_ref[0, pl.ds(pl.multiple_of(tb * ROUTE_BLOCK, ROUTE_BLOCK), ROUTE_BLOCK), :]
        starts, fits = _window_starts(cnt_ref, b, tb, n_e, cap)

        def accumulate(window, first_rows):
            j = lax.broadcasted_iota(I32, (window, 1), 0)
            hits = [slot_ref[0, e, pl.ds(tb, 1), :] == first_rows[e] + j for e in range(n_e)]
            onehot = jnp.concatenate([jnp.where(hit, 1.0, 0.0).astype(BF16) for hit in hits], axis=0)
            rows = _dot(onehot, h_blk)
            for e in range(n_e):
                dst = pl.ds(first_rows[e], window)
                xg_ref[e, 0, dst, :] += rows[e * window:(e + 1) * window].astype(BF16)
                gate = gate_ref[0, e, pl.ds(tb, 1), :]
                gs_ref[e, 0, dst, :] += jnp.sum(jnp.where(hits[e], gate, 0.0), axis=-1, keepdims=True)

        @pl.when(fits)
        def _():
            accumulate(SLOT_WINDOW, starts)

        @pl.when(jnp.logical_not(fits))
        def _():
            accumulate(cap, [0] * n_e)

        return carry

    lax.fori_loop(0, n_blk, token_block, 0)


def _gather_call(cnt, slot, gate, h2, cap):
    bsz, n_e, n_t = slot.shape
    d = h2.shape[-1]
    n_blk = n_t // ROUTE_BLOCK
    blocked = lambda a: a.reshape(bsz, n_e, n_blk, ROUTE_BLOCK)
    row = pl.BlockSpec((1, n_e, n_blk, ROUTE_BLOCK), lambda b, cnt: (b, 0, 0, 0))
    return pl.pallas_call(
        functools.partial(_gather_kernel, cap),
        grid_spec=pltpu.PrefetchScalarGridSpec(
            num_scalar_prefetch=1,
            grid=(bsz,),
            in_specs=[row, row, pl.BlockSpec((1, n_t, d), lambda b, cnt: (b, 0, 0))],
            out_specs=[pl.BlockSpec((n_e, 1, cap, d), lambda b, cnt: (0, b, 0, 0)),
                       pl.BlockSpec((n_e, 1, cap, 1), lambda b, cnt: (0, b, 0, 0))]),
        out_shape=[jax.ShapeDtypeStruct((n_e, bsz, cap, d), BF16),
                   jax.ShapeDtypeStruct((n_e, bsz, cap, 1), F32)],
        compiler_params=_params("arbitrary"),
        name="gather",
    )(cnt, blocked(slot), blocked(gate), h2)


def _ffn_kernel(xg_ref, gs_ref, wg_ref, wu_ref, wd_ref, y_ref):
    bsz, cap, d = xg_ref.shape[1:]
    per = FFN_ROW_CHUNK // cap
    n_chunks = bsz // per
    wg = wg_ref[0].astype(BF16)
    wu = wu_ref[0].astype(BF16)
    wd = wd_ref[0].astype(BF16)

    def up(r):
        x = xg_ref[0, r * per:(r + 1) * per].reshape(per * cap, d)
        return _dot(x, wg), _dot(x, wu)

    nxt = up(0)
    for r in range(n_chunks):
        g, u = nxt
        if r + 1 < n_chunks:
            nxt = up(r + 1)
        y = _dot((_silu(g) * u).astype(BF16), wd) * gs_ref[0, r * per:(r + 1) * per].reshape(per * cap, 1)
        y_ref[0, r * per:(r + 1) * per] = y.astype(BF16).reshape(per, cap, d)


def _ffn_call(xg, gs, w_gate, w_up, w_down):
    n_e, bsz, cap, d = xg.shape
    ff = w_gate.shape[-1]
    rows = pl.BlockSpec((1, bsz, cap, d), lambda e: (e, 0, 0, 0))
    return pl.pallas_call(
        _ffn_kernel,
        grid=(n_e,),
        in_specs=[rows,
                  pl.BlockSpec((1, bsz, cap, 1), lambda e: (e, 0, 0, 0)),
                  pl.BlockSpec((1, d, ff), lambda e: (e, 0, 0)),
                  pl.BlockSpec((1, d, ff), lambda e: (e, 0, 0)),
                  pl.BlockSpec((1, ff, d), lambda e: (e, 0, 0))],
        out_specs=rows,
        out_shape=jax.ShapeDtypeStruct((n_e, bsz, cap, d), BF16),
        compiler_params=_params("arbitrary"),
        name="ffn",
    )(xg, gs, w_gate, w_up, w_down)


def _combine_kernel(cap, cnt_ref, x1_ref, tok_ref, y_ref, mod_ref, o_ref):
    d = x1_ref.shape[-1]
    n_e = y_ref.shape[0]
    b = pl.program_id(0)
    tb = pl.program_id(1)
    gate2 = mod_ref[pl.ds(b, 1), 5 * d:6 * d]
    tok = tok_ref[0]
    starts, fits = _window_starts(cnt_ref, b, tb, n_e, cap)

    @pl.when(fits)
    def _():
        lane = lax.broadcasted_iota(I32, (1, LANES), 1)
        first = lane < SLOT_WINDOW
        groups = []
        for e in range(0, n_e, 2):
            slot = jnp.where(first, tok[:, e:e + 1], tok[:, e + 1:e + 2])
            target = jnp.where(first, starts[e] + lane, starts[e + 1] + lane - SLOT_WINDOW).astype(F32)
            groups.append(jnp.where(slot == target, 1.0, 0.0).astype(BF16))
        onehot = jnp.concatenate(groups, axis=1)
        rows = jnp.concatenate([y_ref[e, 0, pl.ds(starts[e], SLOT_WINDOW), :] for e in range(n_e)], axis=0)
        o_ref[0] = x1_ref[0] + gate2 * _dot(onehot, rows)

    @pl.when(jnp.logical_not(fits))
    def _():
        j = lax.broadcasted_iota(I32, (1, cap), 1).astype(F32)
        acc = None
        for e in range(n_e):
            onehot = jnp.where(tok[:, e:e + 1] == j, 1.0, 0.0).astype(BF16)
            part = _dot(onehot, y_ref[e, 0])
            acc = part if acc is None else acc + part
        o_ref[0] = x1_ref[0] + gate2 * acc


def _combine_call(cnt, x1, tok, y, mod, cap):
    bsz, seq, d = x1.shape
    n_e = y.shape[0]
    t = ROUTE_BLOCK
    assert 2 * SLOT_WINDOW == LANES and n_e % 2 == 0
    return pl.pallas_call(
        functools.partial(_combine_kernel, cap),
        grid_spec=pltpu.PrefetchScalarGridSpec(
            num_scalar_prefetch=1,
            grid=(bsz, seq // t),
            in_specs=[pl.BlockSpec((1, t, d), lambda b, j, cnt: (b, j, 0)),
                      pl.BlockSpec((1, t, LANES), lambda b, j, cnt: (b, j, 0)),
                      pl.BlockSpec((n_e, 1, cap, d), lambda b, j, cnt: (0, b, 0, 0)),
                      pl.BlockSpec((MOD_ROWS, mod.shape[1]), lambda b, j, cnt: (0, 0))],
            out_specs=pl.BlockSpec((1, t, d), lambda b, j, cnt: (b, j, 0))),
        out_shape=jax.ShapeDtypeStruct((bsz, seq, d), F32),
        compiler_params=_params("arbitrary", "arbitrary"),
        name="combine",
    )(cnt, x1, tok, y, mod)


def _rope_tables(seq):
    m = ATTN_HEAD_DIM // 4
    pos = jnp.arange(seq, dtype=jnp.int32)
    rows = (pos // GRID_W).astype(F32)
    cols = (pos % GRID_W).astype(F32)
    freqs = ROPE_BASE ** (-jnp.arange(m, dtype=F32) / m)
    ang_r = rows[:, None] * freqs[None, :]
    ang_c = cols[:, None] * freqs[None, :]
    cos_h = jnp.concatenate([jnp.cos(ang_r), jnp.cos(ang_r), jnp.cos(ang_c), jnp.cos(ang_c)], axis=-1)
    sin_h = jnp.concatenate([-jnp.sin(ang_r), jnp.sin(ang_r), -jnp.sin(ang_c), jnp.sin(ang_c)], axis=-1)
    reps = LANES // ATTN_HEAD_DIM
    return jnp.tile(cos_h, (1, reps)), jnp.tile(sin_h, (1, reps))


def _lane_row(values, offset):
    return jnp.zeros((1, LANES), F32).at[0, offset:offset + values.shape[0]].set(values.astype(F32))


def kernel(x, c, ctx, c_ctx, w_mod, b_mod, norm1_w, norm2_w, w_in, q_norm_w, k_norm_w, conv_w, a_log, dt_bias,
           o_norm_w, w_out, router_w, w_gate, w_up, w_down):
    bsz, seq, d = x.shape
    n_ctx = ctx.shape[1]
    assert w_mod.shape[0] == 1, "single layer: the last layer's context outputs are never consumed"
    assert bsz < MOD_ROWS and N_EXPERTS == router_w.shape[-1]
    cap = EC_CAPACITY_FACTOR * seq // N_EXPERTS

    cc = jnp.concatenate([c, c_ctx[None, :], jnp.zeros((MOD_ROWS - bsz - 1, d), F32)], axis=0)
    mod = _mod_call(cc, w_mod[0], b_mod[0][None, :])

    w_pad = jnp.concatenate([w_in[0].astype(BF16), jnp.zeros((d, LANES - N_GATES), BF16)], axis=1)
    seg = np.arange(MXU_DIM) // ATTN_HEAD_DIM
    bd = jnp.asarray(seg[:, None] == seg[None, :], BF16)
    qkw = jnp.concatenate([jnp.tile(q_norm_w[0], ATTN_HEADS), jnp.tile(k_norm_w[0], ATTN_KV_HEADS)])[None, :]
    alog_l = _lane_row(a_log[0].reshape(-1), N_GATES // 2)
    dtb_l = _lane_row(dt_bias[0].reshape(-1), N_GATES // 2)
    shared = (mod, norm1_w[0][None, :], w_pad, bd, qkw, alog_l, dtb_l)
    kv_lat, dn_lat, g_lat, gt_lat, q, z = _inproj_call(x, bsz, *shared, rope=_rope_tables(seq))
    kv_ctx, dn_ctx, g_ctx, gt_ctx = _inproj_call(ctx.reshape(1, bsz * n_ctx, d), bsz, *shared)

    att = _attn_call(q, kv_ctx, kv_lat, n_ctx)
    o_dn = _dn_call(dn_ctx, dn_lat, conv_w[0], g_ctx, g_lat, gt_ctx, gt_lat, n_ctx)

    rw = jnp.pad(router_w[0], ((0, 0), (0, LANES - N_EXPERTS))).astype(BF16)
    x1, h2, afft = _outproj_call(x, att, o_dn, z, mod, o_norm_w[0][None, :], w_out[0].astype(BF16),
                                 norm2_w[0][None, :], rw)

    slot, gate, tok, bounds = _route_call(afft, cap)
    n_blk = seq // ROUTE_BLOCK
    cnt = bounds[:, :, 0:n_blk + 1].transpose(0, 2, 1).reshape(bsz, (n_blk + 1) * N_EXPERTS)
    xg, gs = _gather_call(cnt, slot, gate, h2, cap)
    y = _ffn_call(xg, gs, w_gate[0], w_up[0], w_down[0])
    return _combine_call(cnt, x1, tok, y, mod, cap)
```

```python
import functools

import numpy as np
import jax
import jax.numpy as jnp
from jax import lax
from jax.experimental import pallas as pl
from jax.experimental.pallas import tpu as pltpu

F32 = jnp.float32
BF16 = jnp.bfloat16
I32 = jnp.int32

GRID_W = 64
EPS = 1e-6
ATTN_HEADS = 8
ATTN_KV_HEADS = 2
ATTN_HEAD_DIM = 64
ROPE_BASE = 10000.0
DN_HEADS = 4
DN_HEAD_DIM = 128
DN_CONV_W = 5
N_EXPERTS = 16
EC_CAPACITY_FACTOR = 2

ATTN_Q_W = ATTN_HEADS * ATTN_HEAD_DIM
ATTN_KV_W = ATTN_KV_HEADS * ATTN_HEAD_DIM
DN_W = DN_HEADS * DN_HEAD_DIM
QK_W = ATTN_Q_W + ATTN_KV_W
ATT_W = ATTN_Q_W + 2 * ATTN_KV_W
N_GATES = 4 * DN_HEADS

LANES = 128
SUBLANES = 8
MXU_DIM = 256
VMEM_LIMIT = 56 * 1024 * 1024
F32_MIN_EXP = -126
F32_MANTISSA_BITS = 23

TOK_TILE = 1024
IN_ROW_CHUNK = 128
Q_TILE = 512
ATTN_KEY_CHUNK = 1024
OUT_TILE = 1024
OUT_ROW_CHUNK = 256
DN_CHUNK = 128
DN_GROUP = 9
DN_CONV_UNROLL = 9
DN_HEADS_PER_STEP = 2
ROUTE_BLOCK = 256
SLOT_WINDOW = 64
BF16_ROWS = 16
FFN_ROW_CHUNK = 512
MOD_ROWS = 16


def _params(*sem):
    return pltpu.CompilerParams(dimension_semantics=sem, vmem_limit_bytes=VMEM_LIMIT)


def _silu(v):
    half = 0.5 * v
    return half + half * jnp.tanh(half)


def _dot(a, b):
    return jnp.dot(a, b, preferred_element_type=F32)


def _dot_nt(a, b):
    return lax.dot_general(a, b, (((1,), (1,)), ((), ())), preferred_element_type=F32)


def _mod_kernel(c_ref, w_ref, b_ref, o_ref):
    sc = _silu(c_ref[...]).astype(BF16)
    o_ref[...] = _dot(sc, w_ref[...].astype(BF16)) + b_ref[...]


def _mod_call(cc, w_mod, b_mod):
    d, n = w_mod.shape
    return pl.pallas_call(
        _mod_kernel,
        grid=(n // d,),
        in_specs=[pl.BlockSpec((MOD_ROWS, d), lambda i: (0, 0)),
                  pl.BlockSpec((d, d), lambda i: (0, i)),
                  pl.BlockSpec((1, d), lambda i: (0, i))],
        out_specs=pl.BlockSpec((MOD_ROWS, d), lambda i: (0, i)),
        out_shape=jax.ShapeDtypeStruct((MOD_ROWS, n), F32),
        compiler_params=_params("arbitrary"),
        name="mod",
    )(cc, w_mod, b_mod)


def _inproj_kernel(latent, ctx_row, x_ref, mod_ref, n1_ref, w_ref, bd_ref, qkw_ref, alog_ref, dtb_ref, *refs):
    if latent:
        cos_ref, sin_ref, kv_ref, dn_ref, g_ref, gt_ref, q_ref, z_ref = refs
    else:
        kv_ref, dn_ref, g_ref, gt_ref = refs
    d = x_ref.shape[-1]
    n_rows = x_ref.shape[1]
    row = pl.program_id(0) if latent else ctx_row
    shift = mod_ref[pl.ds(row, 1), 0:d]
    scale = mod_ref[pl.ds(row, 1), d:2 * d]
    lane = lax.broadcasted_iota(I32, (1, LANES), 1)
    k_blk = ATTN_Q_W // LANES

    def project(rows):
        xin = x_ref[0, rows, :]
        hn = xin * lax.rsqrt(jnp.mean(xin * xin, axis=-1, keepdims=True) + EPS) * n1_ref[...]
        h = (hn * (1.0 + scale) + shift).astype(BF16)
        return _dot(h, w_ref[...])

    def finish(rows, p):
        qk = p[:, 0:QK_W]
        sq = qk * qk
        hi = sq.astype(BF16)
        lo = (sq - hi.astype(F32)).astype(BF16)
        seg_w = bd_ref.shape[0]
        ms = []
        for c0 in range(0, QK_W, seg_w):
            w = min(seg_w, QK_W - c0)
            ones = bd_ref[0:w, 0:w]
            ms.append(_dot(hi[:, c0:c0 + w], ones) + _dot(lo[:, c0:c0 + w], ones))
        ms = jnp.concatenate(ms, axis=1) * (1.0 / ATTN_HEAD_DIM)
        qkn = qk * lax.rsqrt(ms + EPS) * qkw_ref[...]

        if latent:
            first_half = (lane % 32) < 16
            cos = cos_ref[rows, :]
            sin = sin_ref[rows, :]
            rot = []
            for i in range(QK_W // LANES):
                blk = qkn[:, i * LANES:(i + 1) * LANES]
                partner = jnp.where(first_half, pltpu.roll(blk, LANES - 16, axis=1), pltpu.roll(blk, 16, axis=1))
                rot.append(blk * cos + partner * sin)
            for i in range(k_blk):
                q_ref[0, rows, i * LANES:(i + 1) * LANES] = (rot[i] * (ATTN_HEAD_DIM ** -0.5)).astype(BF16)
            k2 = rot[k_blk]
            z_ref[0, rows, :] = p[:, ATT_W + 3 * DN_W:ATT_W + 4 * DN_W]
        else:
            k2 = qkn[:, k_blk * LANES:(k_blk + 1) * LANES]

        low = lane < ATTN_HEAD_DIM
        v2 = p[:, QK_W:ATT_W]
        for src, base in ((k2, 0), (v2, 2)):
            swapped = pltpu.roll(src, ATTN_HEAD_DIM, axis=1)
            kv_ref[0, 0, base + 0, rows, :] = jnp.where(low, src, 0.0).astype(BF16)
            kv_ref[0, 0, base + 1, rows, :] = jnp.where(low, 0.0, swapped).astype(BF16)
            kv_ref[0, 1, base + 0, rows, :] = jnp.where(low, swapped, 0.0).astype(BF16)
            kv_ref[0, 1, base + 1, rows, :] = jnp.where(low, 0.0, src).astype(BF16)

        dn_ref[0, rows, :] = p[:, ATT_W:ATT_W + 3 * DN_W]

        gp = p[:, ATT_W + 4 * DN_W:]
        beta = jax.nn.sigmoid(gp)
        xa = gp + dtb_ref[...]
        softplus = jnp.maximum(xa, 0.0) + jnp.log1p(jnp.exp(-jnp.abs(xa)))
        decay = -jnp.exp(alog_ref[...]) * softplus
        gates = jnp.where(lane < N_GATES // 2, beta, jnp.where(lane < N_GATES, decay, 0.0))
        g_ref[0, rows, :] = gates
        gt_ref[0, :, rows] = gates.T[0:N_GATES, :]

    chunks = [slice(r0, r0 + IN_ROW_CHUNK) for r0 in range(0, n_rows, IN_ROW_CHUNK)]
    p_next = project(chunks[0])
    for r, rows in enumerate(chunks):
        p = p_next
        if r + 1 < len(chunks):
            p_next = project(chunks[r + 1])
        finish(rows, p)


def _inproj_call(rows3, ctx_row, mod, n1, w_pad, bd, qkw, alog_l, dtb_l, rope=None):
    latent = rope is not None
    grp, rows, d = rows3.shape
    t = TOK_TILE
    assert rows % t == 0
    nw = w_pad.shape[1]
    full = lambda shape: pl.BlockSpec(shape, lambda b, j: (0,) * len(shape))
    tile = lambda width: pl.BlockSpec((1, t, width), lambda b, j: (b, j, 0))
    in_specs = [tile(d), full((MOD_ROWS, mod.shape[1])), full((1, d)), full((d, nw)), full(bd.shape),
                full((1, QK_W)), full((1, LANES)), full((1, LANES))]
    out_specs = [pl.BlockSpec((1, ATTN_KV_HEADS, 4, t, LANES), lambda b, j: (b, 0, 0, j, 0)),
                 tile(3 * DN_W), tile(LANES), pl.BlockSpec((1, N_GATES, t), lambda b, j: (b, 0, j))]
    out_shape = [jax.ShapeDtypeStruct((grp, ATTN_KV_HEADS, 4, rows, LANES), BF16),
                 jax.ShapeDtypeStruct((grp, rows, 3 * DN_W), F32),
                 jax.ShapeDtypeStruct((grp, rows, LANES), F32),
                 jax.ShapeDtypeStruct((grp, N_GATES, rows), F32)]
    args = [rows3, mod, n1, w_pad, bd, qkw, alog_l, dtb_l]
    if latent:
        in_specs += [pl.BlockSpec((t, LANES), lambda b, j: (j, 0))] * 2
        out_specs += [tile(ATTN_Q_W), tile(DN_W)]
        out_shape += [jax.ShapeDtypeStruct((grp, rows, ATTN_Q_W), BF16), jax.ShapeDtypeStruct((grp, rows, DN_W), F32)]
        args += list(rope)
    return pl.pallas_call(
        functools.partial(_inproj_kernel, latent, ctx_row),
        grid=(grp, rows // t),
        in_specs=in_specs,
        out_specs=out_specs,
        out_shape=out_shape,
        compiler_params=_params("arbitrary", "arbitrary"),
        name="inproj_lat" if latent else "inproj_ctx",
    )(*args)


def _attn_kernel(q_ref, kvc_ref, kvl_ref, o_ref):
    grp = ATTN_HEADS // ATTN_KV_HEADS
    n_lat = kvl_ref.shape[3]

    def scores(h):
        qp = q_ref[0, :, (h // 2) * LANES:(h // 2 + 1) * LANES]
        return (_dot_nt(qp, kvc_ref[0, h // grp, h % 2]),
                _dot_nt(qp, kvl_ref[0, h // grp, h % 2]))

    s_next = scores(0)
    acc = None
    for h in range(ATTN_HEADS):
        s_ctx, s_lat = s_next
        if h + 1 < ATTN_HEADS:
            s_next = scores(h + 1)
        m = jnp.maximum(jnp.max(s_ctx, axis=-1, keepdims=True), jnp.max(s_lat, axis=-1, keepdims=True))
        stages = [(s_ctx, kvc_ref, 0, s_ctx.shape[-1])]
        stages += [(s_lat, kvl_ref, k0, ATTN_KEY_CHUNK) for k0 in range(0, n_lat, ATTN_KEY_CHUNK)]
        o = denom = None
        for s, v_ref, k0, width in stages:
            e = jnp.exp(s[:, k0:k0 + width] - m)
            part = _dot(e.astype(BF16), v_ref[0, h // grp, 2 + h % 2, k0:k0 + width, :])
            part_sum = jnp.sum(e, axis=-1, keepdims=True)
            o = part if o is None else o + part
            denom = part_sum if denom is None else denom + part_sum
        o = o / denom
        if h % 2 == 0:
            acc = o
        else:
            o_ref[0, :, (h // 2) * LANES:(h // 2 + 1) * LANES] = (acc + o).astype(BF16)


def _attn_call(q, kv_ctx, kv_lat, n_ctx):
    bsz, seq, _ = q.shape
    tq = Q_TILE
    assert seq % tq == 0 and seq % ATTN_KEY_CHUNK == 0
    return pl.pallas_call(
        _attn_kernel,
        grid=(bsz, seq // tq),
        in_specs=[pl.BlockSpec((1, tq, ATTN_Q_W), lambda b, i: (b, i, 0)),
                  pl.BlockSpec((1, ATTN_KV_HEADS, 4, n_ctx, LANES), lambda b, i: (0, 0, 0, b, 0)),
                  pl.BlockSpec((1, ATTN_KV_HEADS, 4, seq, LANES), lambda b, i: (b, 0, 0, 0, 0))],
        out_specs=pl.BlockSpec((1, tq, ATTN_Q_W), lambda b, i: (b, i, 0)),
        out_shape=jax.ShapeDtypeStruct((bsz, seq, ATTN_Q_W), BF16),
        compiler_params=_params("arbitrary", "arbitrary"),
        name="attn",
    )(q, kv_ctx, kv_lat)


def _unit_tri_inverses_minus_eye(mats, level_masks):
    strip_masks, last_join, left, right = level_masks
    n = mats[0].shape[0]
    half = n // 2

    def to_strip(m):
        return m[0:half, :] * left + m[half:, :] * right

    def from_strip(s):
        return jnp.concatenate([s * left, s * right], axis=0)

    mats = [a.astype(BF16) for a in mats]
    strips = [to_strip(a) for a in mats]
    corrs = [-(s * strip_masks[0]) for s in strips]
    for joins in strip_masks[1:]:
        xs = [s * joins for s in strips]
        ys = [x.astype(F32) + _dot(corr, from_strip(x)) for corr, x in zip(corrs, xs)]
        corrs = [corr - (y + _dot(y.astype(BF16), from_strip(corr))).astype(BF16) for corr, y in zip(corrs, ys)]
    corrs = [from_strip(corr) for corr in corrs]
    xs = [a * last_join for a in mats]
    ys = [x.astype(F32) + _dot(corr, x) for corr, x in zip(corrs, xs)]
    return [corr - (y + _dot(y.astype(BF16), corr)).astype(BF16) for corr, y in zip(corrs, ys)]


def _tri_level_masks(n):
    ij_xor = lax.broadcasted_iota(I32, (n, n), 0) ^ lax.broadcasted_iota(I32, (n, n), 1)
    half = n // 2
    lane = lax.broadcasted_iota(I32, (1, n), 1)
    left = jnp.where(lane < half, 1.0, 0.0).astype(BF16)
    right = jnp.where(lane >= half, 1.0, 0.0).astype(BF16)
    strip_masks, m = [], 1
    while m < half:
        full = jnp.where((ij_xor >= m) & (ij_xor < 2 * m), 1.0, 0.0).astype(BF16)
        strip_masks.append(full[0:half, :] * left + full[half:, :] * right)
        m *= 2
    last_join = jnp.where((ij_xor >= half) & (ij_xor < n), 1.0, 0.0).astype(BF16)
    return strip_masks, last_join, left, right


def _dn_kernel(dqc_ref, dkc_ref, dvc_ref, dql_ref, dkl_ref, dvl_ref, cq_ref, ck_ref, cv_ref,
               gc_ref, gl_ref, gtc_ref, gtl_ref, o_ref,
               xq_s, xk_s, xv_s, q_s, k_s, v_s, kk_s, qk_s, g_s, gt_s, mq_s, n_s, op_s, gl_s):
    qkv_ctx_refs = (dqc_ref, dkc_ref, dvc_ref)
    qkv_lat_refs = (dql_ref, dkl_ref, dvl_ref)
    conv_refs = (cq_ref, ck_ref, cv_ref)
    x_scr = (xq_s, xk_s, xv_s)
    qkv_scr = (q_s, k_s, v_s)
    c = DN_CHUNK
    n_ctx = gc_ref.shape[1]
    ctx_chunks = n_ctx // c
    lat_chunks = gl_ref.shape[1] // c
    dk = DN_HEAD_DIM
    ii = lax.broadcasted_iota(I32, (c, c), 0)
    jj = lax.broadcasted_iota(I32, (c, c), 1)
    level_masks = _tri_level_masks(c)
    lane = lax.broadcasted_iota(I32, (1, LANES), 1)

    g_s[0:n_ctx, :] = gc_ref[0]
    g_s[n_ctx:, :] = gl_ref[0]
    gt_s[:, 0:ctx_chunks, :] = gtc_ref[0]
    gt_s[:, ctx_chunks:, :] = gtl_ref[0]

    for hh in range(DN_HEADS_PER_STEP):
        _dn_prepare_head(hh, pl.program_id(1) * DN_HEADS_PER_STEP + hh, ii, jj, level_masks, lane,
                         qkv_ctx_refs, qkv_lat_refs, conv_refs, g_s, gt_s,
                         x_scr, qkv_scr, kk_s, qk_s, mq_s, n_s, op_s, gl_s)

    def scan(first_chunk, count, emit, states):
        rows = dk + c if emit else dk

        def step(t, st):
            chains = [(hh, d, first_chunk + t if d == 0 else first_chunk + count - 1 - t)
                      for hh in range(DN_HEADS_PER_STEP) for d in range(2)]
            prods = [_dot(mq_s[hh, d, i, 0:rows, :], s.astype(BF16)) for (hh, d, i), s in zip(chains, st)]
            new = []
            for (hh, d, i), s, r in zip(chains, st, prods):
                if emit:
                    ro = pl.multiple_of((i - ctx_chunks) * c, c)
                    o_ref[0, pl.ds(ro, c), hh * dk:(hh + 1) * dk] += r[dk:dk + c] + op_s[hh, d, i]
                new.append(s * gl_s[hh, d, i][0:1, :] + r[0:dk] + n_s[hh, d, i])
            return tuple(new)
        return lax.fori_loop(0, count, step, states)

    o_ref[...] = jnp.zeros(o_ref.shape, F32)
    zero_state = jnp.zeros((dk, dk), F32)
    states = scan(0, ctx_chunks, False, (zero_state,) * (2 * DN_HEADS_PER_STEP))
    scan(ctx_chunks, lat_chunks, True, states)


def _dn_prepare_head(hh, head, ii, jj, level_masks, lane, qkv_ctx_refs, qkv_lat_refs, conv_refs, g_s, gt_s,
                     x_scr, qkv_scr, kk_s, qk_s, mq_s, n_s, op_s, gl_s):
    c = DN_CHUNK
    n_ctx = qkv_ctx_refs[0].shape[1]
    seq = qkv_lat_refs[0].shape[1]
    n_chunks = (n_ctx + seq) // c
    ctx_chunks = n_ctx // c
    pad = SUBLANES
    dk = DN_HEAD_DIM
    half = DN_CONV_W // 2
    cols = slice(hh * dk, (hh + 1) * dk)
    xq_s, xk_s, xv_s = x_scr
    q_s, k_s, v_s = qkv_scr
    cq_ref, ck_ref, cv_ref = conv_refs

    zeros_pad = jnp.zeros((pad, dk), F32)
    lat0 = 2 * pad + n_ctx
    for src_ctx, src_lat, dst in zip(qkv_ctx_refs, qkv_lat_refs, x_scr):
        dst[0:pad, :] = zeros_pad
        dst[pad:pad + n_ctx, :] = src_ctx[0, :, cols]
        dst[pad + n_ctx:lat0, :] = zeros_pad
        dst[lat0:lat0 + seq, :] = src_lat[0, :, cols]
        dst[lat0 + seq:lat0 + seq + pad, :] = zeros_pad

    def conv_chunk(i, carry):
        r0 = pl.multiple_of(i * c, c)
        rp = r0 + jnp.where(i >= ctx_chunks, 2 * pad, pad)

        def conv(x_s, cw_ref):
            acc = None
            for s in range(DN_CONV_W):
                term = x_s[pl.ds(rp - half + s, c), :] * cw_ref[s:s + 1, cols]
                acc = term if acc is None else acc + term
            return _silu(acc)

        qc = conv(xq_s, cq_ref)
        kc = conv(xk_s, ck_ref)
        vc = conv(xv_s, cv_ref)
        qc = qc * lax.rsqrt(jnp.sum(qc * qc, axis=-1, keepdims=True) + EPS) * (dk ** -0.5)
        kc = kc * lax.rsqrt(jnp.sum(kc * kc, axis=-1, keepdims=True) + EPS)
        q_s[pl.ds(r0, c), :] = qc
        k_s[pl.ds(r0, c), :] = kc
        v_s[pl.ds(r0, c), :] = vc
        kb = kc.astype(BF16)
        kk_s[i] = _dot_nt(kb, kb)
        qk_s[i] = _dot_nt(qc.astype(BF16), kb)
        return carry

    lax.fori_loop(0, n_chunks, conv_chunk, 0, unroll=DN_CONV_UNROLL)

    def local_group(grp, carry):
        chains = []
        for k in range(DN_GROUP):
            i = grp * DN_GROUP + k
            r0 = pl.multiple_of(i * c, c)
            gates = g_s[pl.ds(r0, c), :]
            kc = k_s[pl.ds(r0, c), :]
            qc = q_s[pl.ds(r0, c), :]
            vc = v_s[pl.ds(r0, c), :]
            kk = kk_s[i]
            qk = qk_s[i]
            kt = kc.T
            for d in range(2):
                beta_col = jnp.sum(jnp.where(lane == d * DN_HEADS + head, gates, 0.0), axis=-1, keepdims=True)
                g_col = jnp.sum(jnp.where(lane == N_GATES // 2 + d * DN_HEADS + head, gates, 0.0),
                                axis=-1, keepdims=True)
                g_row = gt_s[pl.ds(N_GATES // 2 + d * DN_HEADS + head, 1), pl.ds(i, 1), :].reshape(1, c)
                incl = (jj <= ii) if d == 0 else (jj >= ii)
                strict = (jj < ii) if d == 0 else (jj > ii)
                incl_t = (ii <= jj) if d == 0 else (ii >= jj)
                gcum_col = jnp.sum(jnp.where(incl, g_row, 0.0), axis=1, keepdims=True)
                gcum_row = jnp.sum(jnp.where(incl_t, g_col, 0.0), axis=0, keepdims=True)
                total = jnp.sum(g_row, axis=1, keepdims=True)
                decay = jnp.exp(jnp.where(incl, gcum_col - gcum_row, -jnp.inf))
                e_col = jnp.exp(gcum_col)
                kdt = (kt * jnp.exp(total - gcum_row)).astype(BF16)
                qkm = (qk * decay).astype(BF16)
                chains.append(dict(
                    d=d, i=i,
                    a=jnp.where(strict, beta_col * kk * decay, 0.0),
                    rhs=jnp.concatenate([vc * beta_col, kc * (beta_col * e_col)], axis=1),
                    lhs=jnp.concatenate([kdt, qkm], axis=0),
                    qd=qc * e_col,
                    g_last=jnp.exp(total)))
        corrs = _unit_tri_inverses_minus_eye([ch["a"] for ch in chains], level_masks)
        sols = [ch["rhs"] + _dot(corr, ch["rhs"].astype(BF16)) for ch, corr in zip(chains, corrs)]
        prods = [_dot(ch["lhs"], sol.astype(BF16)) for ch, sol in zip(chains, sols)]
        for ch, r in zip(chains, prods):
            d, i = ch["d"], ch["i"]
            mq_s[hh, d, i, 0:dk, :] = (-r[0:dk, dk:2 * dk]).astype(BF16)
            mq_s[hh, d, i, dk:dk + c, :] = (ch["qd"] - r[dk:dk + c, dk:2 * dk]).astype(BF16)
            n_s[hh, d, i] = r[0:dk, 0:dk]
            op_s[hh, d, i] = r[dk:dk + c, 0:dk]
            gl_s[hh, d, i] = jnp.broadcast_to(ch["g_last"], (SUBLANES, dk))
        return carry

    lax.fori_loop(0, n_chunks // DN_GROUP, local_group, 0)


def _dn_call(dn_ctx, dn_lat, conv_w, g_ctx, g_lat, gt_ctx, gt_lat, n_ctx):
    bsz, seq, _ = dn_lat.shape
    tot = n_ctx + seq
    c = DN_CHUNK
    assert n_ctx % c == 0 and seq % c == 0
    n_chunks = tot // c
    ctx_chunks = n_ctx // c
    dk = DN_HEAD_DIM
    gtc4 = gt_ctx.reshape(N_GATES, bsz, ctx_chunks, c).transpose(1, 0, 2, 3)
    gtl4 = gt_lat.reshape(bsz, N_GATES, seq // c, c)
    hp = DN_HEADS_PER_STEP
    steps = DN_HEADS // hp
    col_ctx = lambda off: pl.BlockSpec((1, n_ctx, hp * dk), lambda b, h: (0, b, off + h))
    col_lat = lambda off: pl.BlockSpec((1, seq, hp * dk), lambda b, h: (b, 0, off + h))
    cw = lambda off: pl.BlockSpec((DN_CONV_W, hp * dk), lambda b, h: (0, off + h))
    padded = tot + 3 * SUBLANES
    assert n_chunks % DN_GROUP == 0 and n_chunks % DN_CONV_UNROLL == 0 and DN_HEADS % hp == 0
    return pl.pallas_call(
        _dn_kernel,
        grid=(bsz, steps),
        in_specs=[col_ctx(0), col_ctx(steps), col_ctx(2 * steps),
                  col_lat(0), col_lat(steps), col_lat(2 * steps),
                  cw(0), cw(steps), cw(2 * steps),
                  pl.BlockSpec((1, n_ctx, LANES), lambda b, h: (0, b, 0)),
                  pl.BlockSpec((1, seq, LANES), lambda b, h: (b, 0, 0)),
                  pl.BlockSpec((1, N_GATES, ctx_chunks, c), lambda b, h: (b, 0, 0, 0)),
                  pl.BlockSpec((1, N_GATES, seq // c, c), lambda b, h: (b, 0, 0, 0))],
        out_specs=pl.BlockSpec((1, seq, hp * dk), lambda b, h: (b, 0, h)),
        out_shape=jax.ShapeDtypeStruct((bsz, seq, DN_W), F32),
        scratch_shapes=[pltpu.VMEM((padded, dk), F32)] * 3
        + [pltpu.VMEM((tot, dk), F32)] * 3
        + [pltpu.VMEM((n_chunks, c, c), F32)] * 2
        + [pltpu.VMEM((tot, LANES), F32),
           pltpu.VMEM((N_GATES, n_chunks, c), F32)]
        + [pltpu.VMEM((hp, 2, n_chunks, dk + c, dk), BF16),
           pltpu.VMEM((hp, 2, n_chunks, dk, dk), F32),
           pltpu.VMEM((hp, 2, n_chunks, c, dk), F32),
           pltpu.VMEM((hp, 2, n_chunks, SUBLANES, dk), F32)],
        compiler_params=_params("arbitrary", "arbitrary"),
        name="dn",
    )(dn_ctx, dn_ctx, dn_ctx, dn_lat, dn_lat, dn_lat, conv_w, conv_w, conv_w, g_ctx, g_lat, gtc4, gtl4)


def _outproj_kernel(x_ref, att_ref, o_ref, z_ref, mod_ref, onw_ref, wo_ref, n2_ref, rw_ref,
                    x1_ref, h2_ref, afft_ref):
    d = x_ref.shape[-1]
    b = pl.program_id(0)
    gate1 = mod_ref[pl.ds(b, 1), 2 * d:3 * d]
    shift2 = mod_ref[pl.ds(b, 1), 3 * d:4 * d]
    scale2 = mod_ref[pl.ds(b, 1), 4 * d:5 * d]
    lane = lax.broadcasted_iota(I32, (1, LANES), 1)
    n_rows = x_ref.shape[1]

    def mixed(rows):
        parts = [att_ref[0, rows, :]]
        for h in range(DN_HEADS):
            sl = slice(h * DN_HEAD_DIM, (h + 1) * DN_HEAD_DIM)
            oh = o_ref[0, rows, sl]
            on = oh * lax.rsqrt(jnp.mean(oh * oh, axis=-1, keepdims=True) + EPS) * onw_ref[...]
            parts.append((on * _silu(z_ref[0, rows, sl])).astype(BF16))
        return _dot(jnp.concatenate(parts, axis=1), wo_ref[...])

    def finish(rows, y):
        x1 = x_ref[0, rows, :] + gate1 * y
        x1_ref[0, rows, :] = x1
        hn = x1 * lax.rsqrt(jnp.mean(x1 * x1, axis=-1, keepdims=True) + EPS) * n2_ref[...]
        h2 = (hn * (1.0 + scale2) + shift2).astype(BF16)
        h2_ref[0, rows, :] = h2
        logits = _dot(h2, rw_ref[...])
        logits = jnp.where(lane < N_EXPERTS, logits, -jnp.inf)
        e = jnp.exp(logits - jnp.max(logits, axis=-1, keepdims=True))
        aff = e / jnp.sum(e, axis=-1, keepdims=True)
        afft_ref[0, :, rows] = aff.T[0:N_EXPERTS, :]

    chunks = [slice(r0, r0 + OUT_ROW_CHUNK) for r0 in range(0, n_rows, OUT_ROW_CHUNK)]
    y_next = mixed(chunks[0])
    for r, rows in enumerate(chunks):
        y = y_next
        if r + 1 < len(chunks):
            y_next = mixed(chunks[r + 1])
        finish(rows, y)


def _outproj_call(x, att, o_dn, z, mod, onw, wo, n2, rw):
    bsz, seq, d = x.shape
    t = OUT_TILE
    assert seq % t == 0
    full = lambda shape: pl.BlockSpec(shape, lambda b, j: (0,) * len(shape))
    return pl.pallas_call(
        _outproj_kernel,
        grid=(bsz, seq // t),
        in_specs=[pl.BlockSpec((1, t, d), lambda b, j: (b, j, 0)),
                  pl.BlockSpec((1, t, ATTN_Q_W), lambda b, j: (b, j, 0)),
                  pl.BlockSpec((1, t, DN_W), lambda b, j: (b, j, 0)),
                  pl.BlockSpec((1, t, DN_W), lambda b, j: (b, j, 0)),
                  full((MOD_ROWS, mod.shape[1])),
                  full((1, DN_HEAD_DIM)),
                  full(wo.shape),
                  full((1, d)),
                  full((d, LANES))],
        out_specs=[pl.BlockSpec((1, t, d), lambda b, j: (b, j, 0)),
                   pl.BlockSpec((1, t, d), lambda b, j: (b, j, 0)),
                   pl.BlockSpec((1, N_EXPERTS, t), lambda b, j: (b, 0, j))],
        out_shape=[jax.ShapeDtypeStruct((bsz, seq, d), F32),
                   jax.ShapeDtypeStruct((bsz, seq, d), BF16),
                   jax.ShapeDtypeStruct((bsz, N_EXPERTS, seq), F32)],
        compiler_params=_params("arbitrary", "arbitrary"),
        name="outproj",
    )(x, att, o_dn, z, mod, onw, wo, n2, rw)


def _route_kernel(cap, afft_ref, slot_ref, gate_ref, tok_ref, bounds_ref):
    n_b, n_exp, n_t = afft_ref.shape
    n_e = n_b * n_exp
    aff = afft_ref[...].reshape(n_e, n_t)

    def enough(cand):
        return jnp.sum(jnp.where(aff >= cand, 1.0, 0.0), axis=-1, keepdims=True) >= cap

    tiny = 2.0 ** F32_MIN_EXP
    cur = jnp.full((n_e, 1), tiny, F32)
    any_normal = enough(cur)
    shift = 1 << (-F32_MIN_EXP).bit_length()
    while shift > 1:
        shift //= 2
        cand = cur * (2.0 ** shift)
        cur = jnp.where(enough(cand), cand, cur)

    def refine(_, state):
        cur, step = state
        cand = cur + step
        return jnp.where(enough(cand), cand, cur), step * 0.5

    cur, _ = lax.fori_loop(0, F32_MANTISSA_BITS, refine, (cur, cur * 0.5))
    thr = jnp.where(any_normal, cur, 0.0)
    need = cap - jnp.sum(jnp.where(aff > thr, 1.0, 0.0), axis=-1, keepdims=True)

    upper = (lax.broadcasted_iota(I32, (LANES, LANES), 0) < lax.broadcasted_iota(I32, (LANES, LANES), 1))
    upper = jnp.where(upper, 1.0, 0.0).astype(BF16)
    run = jnp.zeros((2 * n_e, 1), F32)
    lane = lax.broadcasted_iota(I32, (1, LANES), 1)
    bounds = jnp.zeros((n_e, LANES), F32)
    per_block = ROUTE_BLOCK // LANES
    for blk in range(n_t // LANES):
        sl = slice(blk * LANES, (blk + 1) * LANES)
        gt = aff[:, sl] > thr
        eq = aff[:, sl] == thr
        x = jnp.concatenate([jnp.where(gt, 1.0, 0.0), jnp.where(eq, 1.0, 0.0)], axis=0)
        cum = _dot(x.astype(BF16), upper) + run
        run = run + jnp.sum(x, axis=-1, keepdims=True)
        cum_gt = cum[0:n_e]
        cum_eq = cum[n_e:2 * n_e]
        sel = gt | (eq & (cum_eq < need))
        slot = jnp.where(sel, cum_gt + jnp.minimum(cum_eq, need), -1.0)
        slot_ref[:, :, sl] = slot.astype(I32).reshape(n_b, n_exp, LANES)
        gate_ref[:, :, sl] = jnp.where(sel, aff[:, sl], 0.0).reshape(n_b, n_exp, LANES)
        if (blk + 1) % per_block == 0:
            taken = run[0:n_e] + jnp.minimum(run[n_e:2 * n_e], need)
            bounds = jnp.where(lane == (blk + 1) // per_block, taken, bounds)
    bounds_ref[...] = bounds.astype(I32).reshape(n_b, n_exp, LANES)

    for b in range(n_b):
        stacked = jnp.concatenate([slot_ref[b].astype(F32), gate_ref[b],
                                   jnp.zeros((LANES - 2 * n_exp, n_t), F32)], axis=0)
        tok_ref[b] = stacked.T


def _route_call(afft, cap):
    bsz, n_e, n_t = afft.shape
    row = pl.BlockSpec((bsz, n_e, n_t), lambda i: (0, 0, 0))
    return pl.pallas_call(
        functools.partial(_route_kernel, cap),
        grid=(1,),
        in_specs=[row],
        out_specs=[row, row, pl.BlockSpec((bsz, n_t, LANES), lambda i: (0, 0, 0)),
                   pl.BlockSpec((bsz, n_e, LANES), lambda i: (0, 0, 0))],
        out_shape=[jax.ShapeDtypeStruct((bsz, n_e, n_t), I32),
                   jax.ShapeDtypeStruct((bsz, n_e, n_t), F32),
                   jax.ShapeDtypeStruct((bsz, n_t, LANES), F32),
                   jax.ShapeDtypeStruct((bsz, n_e, LANES), I32)],
        compiler_params=_params("arbitrary"),
        name="route",
    )(afft)


def _window_starts(cnt_ref, b, tb, n_e, cap):
    starts, fits = [], None
    for e in range(n_e):
        lo = cnt_ref[b, tb * n_e + e]
        hi = cnt_ref[b, (tb + 1) * n_e + e]
        start = jnp.minimum((lo // BF16_ROWS) * BF16_ROWS, cap - SLOT_WINDOW)
        ok = hi - start <= SLOT_WINDOW
        starts.append(pl.multiple_of(start, BF16_ROWS))
        fits = ok if fits is None else jnp.logical_and(fits, ok)
    return starts, fits


def _gather_kernel(cap, cnt_ref, slot_ref, gate_ref, h_ref, xg_ref, gs_ref):
    b = pl.program_id(0)
    n_e = slot_ref.shape[1]
    n_blk = slot_ref.shape[2]
    xg_ref[...] = jnp.zeros(xg_ref.shape, BF16)
    gs_ref[...] = jnp.zeros(gs_ref.shape, F32)

    def token_block(tb, carry):
        h_blk = h_ref[0, pl.ds(pl.multiple_of(tb * ROUTE_BLOCK, ROUTE_BLOCK), ROUTE_BLOCK), :]
        starts, fits = _window_starts(cnt_ref, b, tb, n_e, cap)

        def accumulate(window, first_rows):
            j = lax.broadcasted_iota(I32, (window, 1), 0)
            hits = [slot_ref[0, e, pl.ds(tb, 1), :] == first_rows[e] + j for e in range(n_e)]
            onehot = jnp.concatenate([jnp.where(hit, 1.0, 0.0).astype(BF16) for hit in hits], axis=0)
            rows = _dot(onehot, h_blk)
            for e in range(n_e):
                dst = pl.ds(first_rows[e], window)
                xg_ref[e, 0, dst, :] += rows[e * window:(e + 1) * window].astype(BF16)
                gate = gate_ref[0, e, pl.ds(tb, 1), :]
                gs_ref[e, 0, dst, :] += jnp.sum(jnp.where(hits[e], gate, 0.0), axis=-1, keepdims=True)

        @pl.when(fits)
        def _():
            accumulate(SLOT_WINDOW, starts)

        @pl.when(jnp.logical_not(fits))
        def _():
            accumulate(cap, [0] * n_e)

        return carry

    lax.fori_loop(0, n_blk, token_block, 0)


def _gather_call(cnt, slot, gate, h2, cap):
    bsz, n_e, n_t = slot.shape
    d = h2.shape[-1]
    n_blk = n_t // ROUTE_BLOCK
    blocked = lambda a: a.reshape(bsz, n_e, n_blk, ROUTE_BLOCK)
    row = pl.BlockSpec((1, n_e, n_blk, ROUTE_BLOCK), lambda b, cnt: (b, 0, 0, 0))
    return pl.pallas_call(
        functools.partial(_gather_kernel, cap),
        grid_spec=pltpu.PrefetchScalarGridSpec(
            num_scalar_prefetch=1,
            grid=(bsz,),
            in_specs=[row, row, pl.BlockSpec((1, n_t, d), lambda b, cnt: (b, 0, 0))],
            out_specs=[pl.BlockSpec((n_e, 1, cap, d), lambda b, cnt: (0, b, 0, 0)),
                       pl.BlockSpec((n_e, 1, cap, 1), lambda b, cnt: (0, b, 0, 0))]),
        out_shape=[jax.ShapeDtypeStruct((n_e, bsz, cap, d), BF16),
                   jax.ShapeDtypeStruct((n_e, bsz, cap, 1), F32)],
        compiler_params=_params("arbitrary"),
        name="gather",
    )(cnt, blocked(slot), blocked(gate), h2)


def _ffn_kernel(xg_ref, gs_ref, wg_ref, wu_ref, wd_ref, y_ref):
    bsz, cap, d = xg_ref.shape[1:]
    per = FFN_ROW_CHUNK // cap
    n_chunks = bsz // per
    wg = wg_ref[0].astype(BF16)
    wu = wu_ref[0].astype(BF16)
    wd = wd_ref[0].astype(BF16)

    def up(r):
        x = xg_ref[0, r * per:(r + 1) * per].reshape(per * cap, d)
        return _dot(x, wg), _dot(x, wu)

    nxt = up(0)
    for r in range(n_chunks):
        g, u = nxt
        if r + 1 < n_chunks:
            nxt = up(r + 1)
        y = _dot((_silu(g) * u).astype(BF16), wd) * gs_ref[0, r * per:(r + 1) * per].reshape(per * cap, 1)
        y_ref[0, r * per:(r + 1) * per] = y.astype(BF16).reshape(per, cap, d)


def _ffn_call(xg, gs, w_gate, w_up, w_down):
    n_e, bsz, cap, d = xg.shape
    ff = w_gate.shape[-1]
    rows = pl.BlockSpec((1, bsz, cap, d), lambda e: (e, 0, 0, 0))
    return pl.pallas_call(
        _ffn_kernel,
        grid=(n_e,),
        in_specs=[rows,
                  pl.BlockSpec((1, bsz, cap, 1), lambda e: (e, 0, 0, 0)),
                  pl.BlockSpec((1, d, ff), lambda e: (e, 0, 0)),
                  pl.BlockSpec((1, d, ff), lambda e: (e, 0, 0)),
                  pl.BlockSpec((1, ff, d), lambda e: (e, 0, 0))],
        out_specs=rows,
        out_shape=jax.ShapeDtypeStruct((n_e, bsz, cap, d), BF16),
        compiler_params=_params("arbitrary"),
        name="ffn",
    )(xg, gs, w_gate, w_up, w_down)


def _combine_kernel(cap, cnt_ref, x1_ref, tok_ref, y_ref, mod_ref, o_ref):
    d = x1_ref.shape[-1]
    n_e = y_ref.shape[0]
    b = pl.program_id(0)
    tb = pl.program_id(1)
    gate2 = mod_ref[pl.ds(b, 1), 5 * d:6 * d]
    tok = tok_ref[0]
    starts, fits = _window_starts(cnt_ref, b, tb, n_e, cap)

    @pl.when(fits)
    def _():
        lane = lax.broadcasted_iota(I32, (1, LANES), 1)
        first = lane < SLOT_WINDOW
        groups = []
        for e in range(0, n_e, 2):
            slot = jnp.where(first, tok[:, e:e + 1], tok[:, e + 1:e + 2])
            target = jnp.where(first, starts[e] + lane, starts[e + 1] + lane - SLOT_WINDOW).astype(F32)
            groups.append(jnp.where(slot == target, 1.0, 0.0).astype(BF16))
        onehot = jnp.concatenate(groups, axis=1)
        rows = jnp.concatenate([y_ref[e, 0, pl.ds(starts[e], SLOT_WINDOW), :] for e in range(n_e)], axis=0)
        o_ref[0] = x1_ref[0] + gate2 * _dot(onehot, rows)

    @pl.when(jnp.logical_not(fits))
    def _():
        j = lax.broadcasted_iota(I32, (1, cap), 1).astype(F32)
        acc = None
        for e in range(n_e):
            onehot = jnp.where(tok[:, e:e + 1] == j, 1.0, 0.0).astype(BF16)
            part = _dot(onehot, y_ref[e, 0])
            acc = part if acc is None else acc + part
        o_ref[0] = x1_ref[0] + gate2 * acc


def _combine_call(cnt, x1, tok, y, mod, cap):
    bsz, seq, d = x1.shape
    n_e = y.shape[0]
    t = ROUTE_BLOCK
    assert 2 * SLOT_WINDOW == LANES and n_e % 2 == 0
    return pl.pallas_call(
        functools.partial(_combine_kernel, cap),
        grid_spec=pltpu.PrefetchScalarGridSpec(
            num_scalar_prefetch=1,
            grid=(bsz, seq // t),
            in_specs=[pl.BlockSpec((1, t, d), lambda b, j, cnt: (b, j, 0)),
                      pl.BlockSpec((1, t, LANES), lambda b, j, cnt: (b, j, 0)),
                      pl.BlockSpec((n_e, 1, cap, d), lambda b, j, cnt: (0, b, 0, 0)),
                      pl.BlockSpec((MOD_ROWS, mod.shape[1]), lambda b, j, cnt: (0, 0))],
            out_specs=pl.BlockSpec((1, t, d), lambda b, j, cnt: (b, j, 0))),
        out_shape=jax.ShapeDtypeStruct((bsz, seq, d), F32),
        compiler_params=_params("arbitrary", "arbitrary"),
        name="combine",
    )(cnt, x1, tok, y, mod)


def _rope_tables(seq):
    m = ATTN_HEAD_DIM // 4
    pos = jnp.arange(seq, dtype=jnp.int32)
    rows = (pos // GRID_W).astype(F32)
    cols = (pos % GRID_W).astype(F32)
    freqs = ROPE_BASE ** (-jnp.arange(m, dtype=F32) / m)
    ang_r = rows[:, None] * freqs[None, :]
    ang_c = cols[:, None] * freqs[None, :]
    cos_h = jnp.concatenate([jnp.cos(ang_r), jnp.cos(ang_r), jnp.cos(ang_c), jnp.cos(ang_c)], axis=-1)
    sin_h = jnp.concatenate([-jnp.sin(ang_r), jnp.sin(ang_r), -jnp.sin(ang_c), jnp.sin(ang_c)], axis=-1)
    reps = LANES // ATTN_HEAD_DIM
    return jnp.tile(cos_h, (1, reps)), jnp.tile(sin_h, (1, reps))


def _lane_row(values, offset):
    return jnp.zeros((1, LANES), F32).at[0, offset:offset + values.shape[0]].set(values.astype(F32))


def kernel(x, c, ctx, c_ctx, w_mod, b_mod, norm1_w, norm2_w, w_in, q_norm_w, k_norm_w, conv_w, a_log, dt_bias,
           o_norm_w, w_out, router_w, w_gate, w_up, w_down):
    bsz, seq, d = x.shape
    n_ctx = ctx.shape[1]
    assert w_mod.shape[0] == 1, "single layer: the last layer's context outputs are never consumed"
    assert bsz < MOD_ROWS and N_EXPERTS == router_w.shape[-1]
    cap = EC_CAPACITY_FACTOR * seq // N_EXPERTS

    cc = jnp.concatenate([c, c_ctx[None, :], jnp.zeros((MOD_ROWS - bsz - 1, d), F32)], axis=0)
    mod = _mod_call(cc, w_mod[0], b_mod[0][None, :])

    w_pad = jnp.concatenate([w_in[0].astype(BF16), jnp.zeros((d, LANES - N_GATES), BF16)], axis=1)
    seg = np.arange(MXU_DIM) // ATTN_HEAD_DIM
    bd = jnp.asarray(seg[:, None] == seg[None, :], BF16)
    qkw = jnp.concatenate([jnp.tile(q_norm_w[0], ATTN_HEADS), jnp.tile(k_norm_w[0], ATTN_KV_HEADS)])[None, :]
    alog_l = _lane_row(a_log[0].reshape(-1), N_GATES // 2)
    dtb_l = _lane_row(dt_bias[0].reshape(-1), N_GATES // 2)
    shared = (mod, norm1_w[0][None, :], w_pad, bd, qkw, alog_l, dtb_l)
    kv_lat, dn_lat, g_lat, gt_lat, q, z = _inproj_call(x, bsz, *shared, rope=_rope_tables(seq))
    kv_ctx, dn_ctx, g_ctx, gt_ctx = _inproj_call(ctx.reshape(1, bsz * n_ctx, d), bsz, *shared)

    att = _attn_call(q, kv_ctx, kv_lat, n_ctx)
    o_dn = _dn_call(dn_ctx, dn_lat, conv_w[0], g_ctx, g_lat, gt_ctx, gt_lat, n_ctx)

    rw = jnp.pad(router_w[0], ((0, 0), (0, LANES - N_EXPERTS))).astype(BF16)
    x1, h2, afft = _outproj_call(x, att, o_dn, z, mod, o_norm_w[0][None, :], w_out[0].astype(BF16),
                                 norm2_w[0][None, :], rw)

    slot, gate, tok, bounds = _route_call(afft, cap)
    n_blk = seq // ROUTE_BLOCK
    cnt = bounds[:, :, 0:n_blk + 1].transpose(0, 2, 1).reshape(bsz, (n_blk + 1) * N_EXPERTS)
    xg, gs = _gather_call(cnt, slot, gate, h2, cap)
    y = _ffn_call(xg, gs, w_gate[0], w_up[0], w_down[0])
    return _combine_call(cnt, x1, tok, y, mod, cap)
```

```python
import functools

import numpy as np
import jax
import jax.numpy as jnp
from jax import lax
from jax.experimental import pallas as pl
from jax.experimental.pallas import tpu as pltpu

F32 = jnp.float32
BF16 = jnp.bfloat16
I32 = jnp.int32

GRID_W = 64
EPS = 1e-6
ATTN_HEADS = 8
ATTN_KV_HEADS = 2
ATTN_HEAD_DIM = 64
ROPE_BASE = 10000.0
DN_HEADS = 4
DN_HEAD_DIM = 128
DN_CONV_W = 5
N_EXPERTS = 16
EC_CAPACITY_FACTOR = 2

ATTN_Q_W = ATTN_HEADS * ATTN_HEAD_DIM
ATTN_KV_W = ATTN_KV_HEADS * ATTN_HEAD_DIM
DN_W = DN_HEADS * DN_HEAD_DIM
QK_W = ATTN_Q_W + ATTN_KV_W
ATT_W = ATTN_Q_W + 2 * ATTN_KV_W
N_GATES = 4 * DN_HEADS

LANES = 128
SUBLANES = 8
MXU_DIM = 256
VMEM_LIMIT = 56 * 1024 * 1024
F32_MIN_EXP = -126
F32_MANTISSA_BITS = 23

TOK_TILE = 1024
IN_ROW_CHUNK = 128
Q_TILE = 512
ATTN_KEY_CHUNK = 1024
OUT_TILE = 1024
STREAM_BUFFERS = 3
OUT_ROW_CHUNK = 128
DN_CHUNK = 128
DN_GROUP = 9
DN_CONV_UNROLL = 9
DN_HEADS_PER_STEP = 2
ROUTE_BLOCK = 256
SLOT_WINDOW = 64
BF16_ROWS = 16
FFN_ROW_CHUNK = 512
MOD_ROWS = 16


def _params(*sem):
    return pltpu.CompilerParams(dimension_semantics=sem, vmem_limit_bytes=VMEM_LIMIT)


def _silu(v):
    half = 0.5 * v
    return half + half * jnp.tanh(half)


def _dot(a, b):
    return jnp.dot(a, b, preferred_element_type=F32)


def _dot_nt(a, b):
    return lax.dot_general(a, b, (((1,), (1,)), ((), ())), preferred_element_type=F32)


def _mod_kernel(c_ref, w_ref, b_ref, o_ref):
    sc = _silu(c_ref[...]).astype(BF16)
    o_ref[...] = _dot(sc, w_ref[...].astype(BF16)) + b_ref[...]


def _mod_call(cc, w_mod, b_mod):
    d, n = w_mod.shape
    return pl.pallas_call(
        _mod_kernel,
        grid=(n // d,),
        in_specs=[pl.BlockSpec((MOD_ROWS, d), lambda i: (0, 0)),
                  pl.BlockSpec((d, d), lambda i: (0, i)),
                  pl.BlockSpec((1, d), lambda i: (0, i))],
        out_specs=pl.BlockSpec((MOD_ROWS, d), lambda i: (0, i)),
        out_shape=jax.ShapeDtypeStruct((MOD_ROWS, n), F32),
        compiler_params=_params("arbitrary"),
        name="mod",
    )(cc, w_mod, b_mod)


def _inproj_kernel(latent, ctx_row, x_ref, mod_ref, n1_ref, w_ref, bd_ref, qkw_ref, alog_ref, dtb_ref, *refs):
    if latent:
        cos_ref, sin_ref, kv_ref, dn_ref, g_ref, gt_ref, q_ref, z_ref = refs
    else:
        kv_ref, dn_ref, g_ref, gt_ref = refs
    d = x_ref.shape[-1]
    n_rows = x_ref.shape[1]
    row = pl.program_id(0) if latent else ctx_row
    shift = mod_ref[pl.ds(row, 1), 0:d]
    scale = mod_ref[pl.ds(row, 1), d:2 * d]
    lane = lax.broadcasted_iota(I32, (1, LANES), 1)
    k_blk = ATTN_Q_W // LANES

    def project(rows):
        xin = x_ref[0, rows, :]
        hn = xin * lax.rsqrt(jnp.mean(xin * xin, axis=-1, keepdims=True) + EPS) * n1_ref[...]
        h = (hn * (1.0 + scale) + shift).astype(BF16)
        return _dot(h, w_ref[...])

    def finish(rows, p):
        qk = p[:, 0:QK_W]
        sq = qk * qk
        hi = sq.astype(BF16)
        lo = (sq - hi.astype(F32)).astype(BF16)
        seg_w = bd_ref.shape[0]
        ms = []
        for c0 in range(0, QK_W, seg_w):
            w = min(seg_w, QK_W - c0)
            ones = bd_ref[0:w, 0:w]
            ms.append(_dot(hi[:, c0:c0 + w], ones) + _dot(lo[:, c0:c0 + w], ones))
        ms = jnp.concatenate(ms, axis=1) * (1.0 / ATTN_HEAD_DIM)
        qkn = qk * lax.rsqrt(ms + EPS) * qkw_ref[...]

        if latent:
            first_half = (lane % 32) < 16
            cos = cos_ref[rows, :]
            sin = sin_ref[rows, :]
            rot = []
            for i in range(QK_W // LANES):
                blk = qkn[:, i * LANES:(i + 1) * LANES]
                partner = jnp.where(first_half, pltpu.roll(blk, LANES - 16, axis=1), pltpu.roll(blk, 16, axis=1))
                rot.append(blk * cos + partner * sin)
            for i in range(k_blk):
                q_ref[0, rows, i * LANES:(i + 1) * LANES] = (rot[i] * (ATTN_HEAD_DIM ** -0.5)).astype(BF16)
            k2 = rot[k_blk]
            z_ref[0, rows, :] = p[:, ATT_W + 3 * DN_W:ATT_W + 4 * DN_W]
        else:
            k2 = qkn[:, k_blk * LANES:(k_blk + 1) * LANES]

        low = lane < ATTN_HEAD_DIM
        v2 = p[:, QK_W:ATT_W]
        for src, base in ((k2, 0), (v2, 2)):
            swapped = pltpu.roll(src, ATTN_HEAD_DIM, axis=1)
            kv_ref[0, 0, base + 0, rows, :] = jnp.where(low, src, 0.0).astype(BF16)
            kv_ref[0, 0, base + 1, rows, :] = jnp.where(low, 0.0, swapped).astype(BF16)
            kv_ref[0, 1, base + 0, rows, :] = jnp.where(low, swapped, 0.0).astype(BF16)
            kv_ref[0, 1, base + 1, rows, :] = jnp.where(low, 0.0, src).astype(BF16)

        dn_ref[0, rows, :] = p[:, ATT_W:ATT_W + 3 * DN_W]

        gp = p[:, ATT_W + 4 * DN_W:]
        beta = jax.nn.sigmoid(gp)
        xa = gp + dtb_ref[...]
        softplus = jnp.maximum(xa, 0.0) + jnp.log1p(jnp.exp(-jnp.abs(xa)))
        decay = -jnp.exp(alog_ref[...]) * softplus
        gates = jnp.where(lane < N_GATES // 2, beta, jnp.where(lane < N_GATES, decay, 0.0))
        g_ref[0, rows, :] = gates
        gt_ref[0, :, rows] = gates.T[0:N_GATES, :]

    chunks = [slice(r0, r0 + IN_ROW_CHUNK) for r0 in range(0, n_rows, IN_ROW_CHUNK)]
    p_next = project(chunks[0])
    for r, rows in enumerate(chunks):
        p = p_next
        if r + 1 < len(chunks):
            p_next = project(chunks[r + 1])
        finish(rows, p)


def _inproj_call(rows3, ctx_row, mod, n1, w_pad, bd, qkw, alog_l, dtb_l, rope=None):
    latent = rope is not None
    grp, rows, d = rows3.shape
    t = TOK_TILE
    assert rows % t == 0
    nw = w_pad.shape[1]
    full = lambda shape: pl.BlockSpec(shape, lambda b, j: (0,) * len(shape))
    tile = lambda width: pl.BlockSpec((1, t, width), lambda b, j: (b, j, 0))
    in_specs = [tile(d), full((MOD_ROWS, mod.shape[1])), full((1, d)), full((d, nw)), full(bd.shape),
                full((1, QK_W)), full((1, LANES)), full((1, LANES))]
    out_specs = [pl.BlockSpec((1, ATTN_KV_HEADS, 4, t, LANES), lambda b, j: (b, 0, 0, j, 0)),
                 tile(3 * DN_W), tile(LANES), pl.BlockSpec((1, N_GATES, t), lambda b, j: (b, 0, j))]
    out_shape = [jax.ShapeDtypeStruct((grp, ATTN_KV_HEADS, 4, rows, LANES), BF16),
                 jax.ShapeDtypeStruct((grp, rows, 3 * DN_W), F32),
                 jax.ShapeDtypeStruct((grp, rows, LANES), F32),
                 jax.ShapeDtypeStruct((grp, N_GATES, rows), F32)]
    args = [rows3, mod, n1, w_pad, bd, qkw, alog_l, dtb_l]
    if latent:
        in_specs += [pl.BlockSpec((t, LANES), lambda b, j: (j, 0))] * 2
        out_specs += [tile(ATTN_Q_W), tile(DN_W)]
        out_shape += [jax.ShapeDtypeStruct((grp, rows, ATTN_Q_W), BF16), jax.ShapeDtypeStruct((grp, rows, DN_W), F32)]
        args += list(rope)
    return pl.pallas_call(
        functools.partial(_inproj_kernel, latent, ctx_row),
        grid=(grp, rows // t),
        in_specs=in_specs,
        out_specs=out_specs,
        out_shape=out_shape,
        compiler_params=_params("arbitrary", "arbitrary"),
        name="inproj_lat" if latent else "inproj_ctx",
    )(*args)


def _attn_kernel(q_ref, kvc_ref, kvl_ref, o_ref):
    grp = ATTN_HEADS // ATTN_KV_HEADS
    n_lat = kvl_ref.shape[3]

    def scores(h):
        qp = q_ref[0, :, (h // 2) * LANES:(h // 2 + 1) * LANES]
        return (_dot_nt(qp, kvc_ref[0, h // grp, h % 2]),
                _dot_nt(qp, kvl_ref[0, h // grp, h % 2]))

    s_next = scores(0)
    acc = None
    for h in range(ATTN_HEADS):
        s_ctx, s_lat = s_next
        if h + 1 < ATTN_HEADS:
            s_next = scores(h + 1)
        m = jnp.maximum(jnp.max(s_ctx, axis=-1, keepdims=True), jnp.max(s_lat, axis=-1, keepdims=True))
        stages = [(s_ctx, kvc_ref, 0, s_ctx.shape[-1])]
        stages += [(s_lat, kvl_ref, k0, ATTN_KEY_CHUNK) for k0 in range(0, n_lat, ATTN_KEY_CHUNK)]
        o = denom = None
        for s, v_ref, k0, width in stages:
            e = jnp.exp(s[:, k0:k0 + width] - m)
            part = _dot(e.astype(BF16), v_ref[0, h // grp, 2 + h % 2, k0:k0 + width, :])
            part_sum = jnp.sum(e, axis=-1, keepdims=True)
            o = part if o is None else o + part
            denom = part_sum if denom is None else denom + part_sum
        o = o / denom
        if h % 2 == 0:
            acc = o
        else:
            o_ref[0, :, (h // 2) * LANES:(h // 2 + 1) * LANES] = (acc + o).astype(BF16)


def _attn_call(q, kv_ctx, kv_lat, n_ctx):
    bsz, seq, _ = q.shape
    tq = Q_TILE
    assert seq % tq == 0 and seq % ATTN_KEY_CHUNK == 0
    return pl.pallas_call(
        _attn_kernel,
        grid=(bsz, seq // tq),
        in_specs=[pl.BlockSpec((1, tq, ATTN_Q_W), lambda b, i: (b, i, 0)),
                  pl.BlockSpec((1, ATTN_KV_HEADS, 4, n_ctx, LANES), lambda b, i: (0, 0, 0, b, 0)),
                  pl.BlockSpec((1, ATTN_KV_HEADS, 4, seq, LANES), lambda b, i: (b, 0, 0, 0, 0))],
        out_specs=pl.BlockSpec((1, tq, ATTN_Q_W), lambda b, i: (b, i, 0)),
        out_shape=jax.ShapeDtypeStruct((bsz, seq, ATTN_Q_W), BF16),
        compiler_params=_params("arbitrary", "arbitrary"),
        name="attn",
    )(q, kv_ctx, kv_lat)


def _unit_tri_inverses_minus_eye(mats, level_masks):
    strip_masks, last_join, left, right = level_masks
    n = mats[0].shape[0]
    half = n // 2

    def to_strip(m):
        return m[0:half, :] * left + m[half:, :] * right

    def from_strip(s):
        return jnp.concatenate([s * left, s * right], axis=0)

    mats = [a.astype(BF16) for a in mats]
    strips = [to_strip(a) for a in mats]
    corrs = [-(s * strip_masks[0]) for s in strips]
    for joins in strip_masks[1:]:
        xs = [s * joins for s in strips]
        ys = [x.astype(F32) + _dot(corr, from_strip(x)) for corr, x in zip(corrs, xs)]
        corrs = [corr - (y + _dot(y.astype(BF16), from_strip(corr))).astype(BF16) for corr, y in zip(corrs, ys)]
    corrs = [from_strip(corr) for corr in corrs]
    xs = [a * last_join for a in mats]
    ys = [x.astype(F32) + _dot(corr, x) for corr, x in zip(corrs, xs)]
    return [corr - (y + _dot(y.astype(BF16), corr)).astype(BF16) for corr, y in zip(corrs, ys)]


def _tri_level_masks(n):
    ij_xor = lax.broadcasted_iota(I32, (n, n), 0) ^ lax.broadcasted_iota(I32, (n, n), 1)
    half = n // 2
    lane = lax.broadcasted_iota(I32, (1, n), 1)
    left = jnp.where(lane < half, 1.0, 0.0).astype(BF16)
    right = jnp.where(lane >= half, 1.0, 0.0).astype(BF16)
    strip_masks, m = [], 1
    while m < half:
        full = jnp.where((ij_xor >= m) & (ij_xor < 2 * m), 1.0, 0.0).astype(BF16)
        strip_masks.append(full[0:half, :] * left + full[half:, :] * right)
        m *= 2
    last_join = jnp.where((ij_xor >= half) & (ij_xor < n), 1.0, 0.0).astype(BF16)
    return strip_masks, last_join, left, right


def _dn_kernel(dqc_ref, dkc_ref, dvc_ref, dql_ref, dkl_ref, dvl_ref, cq_ref, ck_ref, cv_ref,
               gc_ref, gl_ref, gtc_ref, gtl_ref, o_ref,
               xq_s, xk_s, xv_s, q_s, k_s, v_s, kk_s, qk_s, g_s, gt_s, mq_s, n_s, op_s, gl_s):
    qkv_ctx_refs = (dqc_ref, dkc_ref, dvc_ref)
    qkv_lat_refs = (dql_ref, dkl_ref, dvl_ref)
    conv_refs = (cq_ref, ck_ref, cv_ref)
    x_scr = (xq_s, xk_s, xv_s)
    qkv_scr = (q_s, k_s, v_s)
    c = DN_CHUNK
    n_ctx = gc_ref.shape[1]
    ctx_chunks = n_ctx // c
    lat_chunks = gl_ref.shape[1] // c
    dk = DN_HEAD_DIM
    ii = lax.broadcasted_iota(I32, (c, c), 0)
    jj = lax.broadcasted_iota(I32, (c, c), 1)
    level_masks = _tri_level_masks(c)
    lane = lax.broadcasted_iota(I32, (1, LANES), 1)

    g_s[0:n_ctx, :] = gc_ref[0]
    g_s[n_ctx:, :] = gl_ref[0]
    gt_s[:, 0:ctx_chunks, :] = gtc_ref[0]
    gt_s[:, ctx_chunks:, :] = gtl_ref[0]

    for hh in range(DN_HEADS_PER_STEP):
        _dn_prepare_head(hh, pl.program_id(1) * DN_HEADS_PER_STEP + hh, ii, jj, level_masks, lane,
                         qkv_ctx_refs, qkv_lat_refs, conv_refs, g_s, gt_s,
                         x_scr, qkv_scr, kk_s, qk_s, mq_s, n_s, op_s, gl_s)

    def scan(first_chunk, count, emit, states):
        rows = dk + c if emit else dk

        def step(t, st):
            chains = [(hh, d, first_chunk + t if d == 0 else first_chunk + count - 1 - t)
                      for hh in range(DN_HEADS_PER_STEP) for d in range(2)]
            prods = [_dot(mq_s[hh, d, i, 0:rows, :], s.astype(BF16)) for (hh, d, i), s in zip(chains, st)]
            new = []
            for (hh, d, i), s, r in zip(chains, st, prods):
                if emit:
                    ro = pl.multiple_of((i - ctx_chunks) * c, c)
                    o_ref[0, pl.ds(ro, c), hh * dk:(hh + 1) * dk] += r[dk:dk + c] + op_s[hh, d, i]
                new.append(s * gl_s[hh, d, i][0:1, :] + r[0:dk] + n_s[hh, d, i])
            return tuple(new)
        return lax.fori_loop(0, count, step, states)

    o_ref[...] = jnp.zeros(o_ref.shape, F32)
    zero_state = jnp.zeros((dk, dk), F32)
    states = scan(0, ctx_chunks, False, (zero_state,) * (2 * DN_HEADS_PER_STEP))
    scan(ctx_chunks, lat_chunks, True, states)


def _dn_prepare_head(hh, head, ii, jj, level_masks, lane, qkv_ctx_refs, qkv_lat_refs, conv_refs, g_s, gt_s,
                     x_scr, qkv_scr, kk_s, qk_s, mq_s, n_s, op_s, gl_s):
    c = DN_CHUNK
    n_ctx = qkv_ctx_refs[0].shape[1]
    seq = qkv_lat_refs[0].shape[1]
    n_chunks = (n_ctx + seq) // c
    ctx_chunks = n_ctx // c
    pad = SUBLANES
    dk = DN_HEAD_DIM
    half = DN_CONV_W // 2
    cols = slice(hh * dk, (hh + 1) * dk)
    xq_s, xk_s, xv_s = x_scr
    q_s, k_s, v_s = qkv_scr
    cq_ref, ck_ref, cv_ref = conv_refs

    zeros_pad = jnp.zeros((pad, dk), F32)
    lat0 = 2 * pad + n_ctx
    for src_ctx, src_lat, dst in zip(qkv_ctx_refs, qkv_lat_refs, x_scr):
        dst[0:pad, :] = zeros_pad
        dst[pad:pad + n_ctx, :] = src_ctx[0, :, cols]
        dst[pad + n_ctx:lat0, :] = zeros_pad
        dst[lat0:lat0 + seq, :] = src_lat[0, :, cols]
        dst[lat0 + seq:lat0 + seq + pad, :] = zeros_pad

    def conv_chunk(i, carry):
        r0 = pl.multiple_of(i * c, c)
        rp = r0 + jnp.where(i >= ctx_chunks, 2 * pad, pad)

        def conv(x_s, cw_ref):
            acc = None
            for s in range(DN_CONV_W):
                term = x_s[pl.ds(rp - half + s, c), :] * cw_ref[s:s + 1, cols]
                acc = term if acc is None else acc + term
            return _silu(acc)

        qc = conv(xq_s, cq_ref)
        kc = conv(xk_s, ck_ref)
        vc = conv(xv_s, cv_ref)
        qc = qc * lax.rsqrt(jnp.sum(qc * qc, axis=-1, keepdims=True) + EPS) * (dk ** -0.5)
        kc = kc * lax.rsqrt(jnp.sum(kc * kc, axis=-1, keepdims=True) + EPS)
        q_s[pl.ds(r0, c), :] = qc
        k_s[pl.ds(r0, c), :] = kc
        v_s[pl.ds(r0, c), :] = vc
        kb = kc.astype(BF16)
        kk_s[i] = _dot_nt(kb, kb)
        qk_s[i] = _dot_nt(qc.astype(BF16), kb)
        return carry

    lax.fori_loop(0, n_chunks, conv_chunk, 0, unroll=DN_CONV_UNROLL)

    def local_group(grp, carry):
        chains = []
        for k in range(DN_GROUP):
            i = grp * DN_GROUP + k
            r0 = pl.multiple_of(i * c, c)
            gates = g_s[pl.ds(r0, c), :]
            kc = k_s[pl.ds(r0, c), :]
            qc = q_s[pl.ds(r0, c), :]
            vc = v_s[pl.ds(r0, c), :]
            kk = kk_s[i]
            qk = qk_s[i]
            kt = kc.T
            for d in range(2):
                beta_col = jnp.sum(jnp.where(lane == d * DN_HEADS + head, gates, 0.0), axis=-1, keepdims=True)
                g_col = jnp.sum(jnp.where(lane == N_GATES // 2 + d * DN_HEADS + head, gates, 0.0),
                                axis=-1, keepdims=True)
                g_row = gt_s[pl.ds(N_GATES // 2 + d * DN_HEADS + head, 1), pl.ds(i, 1), :].reshape(1, c)
                incl = (jj <= ii) if d == 0 else (jj >= ii)
                strict = (jj < ii) if d == 0 else (jj > ii)
                incl_t = (ii <= jj) if d == 0 else (ii >= jj)
                gcum_col = jnp.sum(jnp.where(incl, g_row, 0.0), axis=1, keepdims=True)
                gcum_row = jnp.sum(jnp.where(incl_t, g_col, 0.0), axis=0, keepdims=True)
                total = jnp.sum(g_row, axis=1, keepdims=True)
                decay = jnp.exp(jnp.where(incl, gcum_col - gcum_row, -jnp.inf))
                e_col = jnp.exp(gcum_col)
                kdt = (kt * jnp.exp(total - gcum_row)).astype(BF16)
                qkm = (qk * decay).astype(BF16)
                chains.append(dict(
                    d=d, i=i,
                    a=jnp.where(strict, beta_col * kk * decay, 0.0),
                    rhs=jnp.concatenate([vc * beta_col, kc * (beta_col * e_col)], axis=1),
                    lhs=jnp.concatenate([kdt, qkm], axis=0),
                    qd=qc * e_col,
                    g_last=jnp.exp(total)))
        corrs = _unit_tri_inverses_minus_eye([ch["a"] for ch in chains], level_masks)
        sols = [ch["rhs"] + _dot(corr, ch["rhs"].astype(BF16)) for ch, corr in zip(chains, corrs)]
        prods = [_dot(ch["lhs"], sol.astype(BF16)) for ch, sol in zip(chains, sols)]
        for ch, r in zip(chains, prods):
            d, i = ch["d"], ch["i"]
            mq_s[hh, d, i, 0:dk, :] = (-r[0:dk, dk:2 * dk]).astype(BF16)
            mq_s[hh, d, i, dk:dk + c, :] = (ch["qd"] - r[dk:dk + c, dk:2 * dk]).astype(BF16)
            n_s[hh, d, i] = r[0:dk, 0:dk]
            op_s[hh, d, i] = r[dk:dk + c, 0:dk]
            gl_s[hh, d, i] = jnp.broadcast_to(ch["g_last"], (SUBLANES, dk))
        return carry

    lax.fori_loop(0, n_chunks // DN_GROUP, local_group, 0)


def _dn_call(dn_ctx, dn_lat, conv_w, g_ctx, g_lat, gt_ctx, gt_lat, n_ctx):
    bsz, seq, _ = dn_lat.shape
    tot = n_ctx + seq
    c = DN_CHUNK
    assert n_ctx % c == 0 and seq % c == 0
    n_chunks = tot // c
    ctx_chunks = n_ctx // c
    dk = DN_HEAD_DIM
    gtc4 = gt_ctx.reshape(N_GATES, bsz, ctx_chunks, c).transpose(1, 0, 2, 3)
    gtl4 = gt_lat.reshape(bsz, N_GATES, seq // c, c)
    hp = DN_HEADS_PER_STEP
    steps = DN_HEADS // hp
    col_ctx = lambda off: pl.BlockSpec((1, n_ctx, hp * dk), lambda b, h: (0, b, off + h))
    col_lat = lambda off: pl.BlockSpec((1, seq, hp * dk), lambda b, h: (b, 0, off + h))
    cw = lambda off: pl.BlockSpec((DN_CONV_W, hp * dk), lambda b, h: (0, off + h))
    padded = tot + 3 * SUBLANES
    assert n_chunks % DN_GROUP == 0 and n_chunks % DN_CONV_UNROLL == 0 and DN_HEADS % hp == 0
    return pl.pallas_call(
        _dn_kernel,
        grid=(bsz, steps),
        in_specs=[col_ctx(0), col_ctx(steps), col_ctx(2 * steps),
                  col_lat(0), col_lat(steps), col_lat(2 * steps),
                  cw(0), cw(steps), cw(2 * steps),
                  pl.BlockSpec((1, n_ctx, LANES), lambda b, h: (0, b, 0)),
                  pl.BlockSpec((1, seq, LANES), lambda b, h: (b, 0, 0)),
                  pl.BlockSpec((1, N_GATES, ctx_chunks, c), lambda b, h: (b, 0, 0, 0)),
                  pl.BlockSpec((1, N_GATES, seq // c, c), lambda b, h: (b, 0, 0, 0))],
        out_specs=pl.BlockSpec((1, seq, hp * dk), lambda b, h: (b, 0, h)),
        out_shape=jax.ShapeDtypeStruct((bsz, seq, DN_W), F32),
        scratch_shapes=[pltpu.VMEM((padded, dk), F32)] * 3
        + [pltpu.VMEM((tot, dk), F32)] * 3
        + [pltpu.VMEM((n_chunks, c, c), F32)] * 2
        + [pltpu.VMEM((tot, LANES), F32),
           pltpu.VMEM((N_GATES, n_chunks, c), F32)]
        + [pltpu.VMEM((hp, 2, n_chunks, dk + c, dk), BF16),
           pltpu.VMEM((hp, 2, n_chunks, dk, dk), F32),
           pltpu.VMEM((hp, 2, n_chunks, c, dk), F32),
           pltpu.VMEM((hp, 2, n_chunks, SUBLANES, dk), F32)],
        compiler_params=_params("arbitrary", "arbitrary"),
        name="dn",
    )(dn_ctx, dn_ctx, dn_ctx, dn_lat, dn_lat, dn_lat, conv_w, conv_w, conv_w, g_ctx, g_lat, gtc4, gtl4)


def _outproj_kernel(x_hbm, att_ref, o_ref, z_ref, mod_ref, onw_ref, wo_ref, n2_ref, rw_ref,
                    x1_ref, h2_ref, afft_ref, x_ring, x_sem):
    d = x_hbm.shape[-1]
    n_rows = x1_ref.shape[1]
    b = pl.program_id(0)
    n_j = pl.num_programs(1)
    step = b * n_j + pl.program_id(1)
    n_steps = pl.num_programs(0) * n_j

    def x_copy(s):
        return pltpu.make_async_copy(x_hbm.at[s // n_j, pl.ds((s % n_j) * n_rows, n_rows), :],
                                     x_ring.at[s % STREAM_BUFFERS], x_sem.at[s % STREAM_BUFFERS])

    @pl.when(step == 0)
    def _():
        for s in range(STREAM_BUFFERS - 1):
            x_copy(s).start()

    @pl.when(step + STREAM_BUFFERS - 1 < n_steps)
    def _():
        x_copy(step + STREAM_BUFFERS - 1).start()

    x_copy(step).wait()
    x_ref = x_ring.at[step % STREAM_BUFFERS]

    gate1 = mod_ref[pl.ds(b, 1), 2 * d:3 * d]
    shift2 = mod_ref[pl.ds(b, 1), 3 * d:4 * d]
    scale2 = mod_ref[pl.ds(b, 1), 4 * d:5 * d]
    lane = lax.broadcasted_iota(I32, (1, LANES), 1)

    def mixed(rows):
        parts = [att_ref[0, rows, :]]
        for h in range(DN_HEADS):
            sl = slice(h * DN_HEAD_DIM, (h + 1) * DN_HEAD_DIM)
            oh = o_ref[0, rows, sl]
            on = oh * lax.rsqrt(jnp.mean(oh * oh, axis=-1, keepdims=True) + EPS) * onw_ref[...]
            parts.append((on * _silu(z_ref[0, rows, sl])).astype(BF16))
        return _dot(jnp.concatenate(parts, axis=1), wo_ref[...])

    def finish(rows, y):
        x1 = x_ref[rows, :] + gate1 * y
        x1_ref[0, rows, :] = x1
        hn = x1 * lax.rsqrt(jnp.mean(x1 * x1, axis=-1, keepdims=True) + EPS) * n2_ref[...]
        h2 = (hn * (1.0 + scale2) + shift2).astype(BF16)
        h2_ref[0, rows, :] = h2
        logits = _dot(h2, rw_ref[...])
        logits = jnp.where(lane < N_EXPERTS, logits, -jnp.inf)
        e = jnp.exp(logits - jnp.max(logits, axis=-1, keepdims=True))
        aff = e / jnp.sum(e, axis=-1, keepdims=True)
        afft_ref[0, :, rows] = aff.T[0:N_EXPERTS, :]

    chunks = [slice(r0, r0 + OUT_ROW_CHUNK) for r0 in range(0, n_rows, OUT_ROW_CHUNK)]
    y_next = mixed(chunks[0])
    for r, rows in enumerate(chunks):
        y = y_next
        if r + 1 < len(chunks):
            y_next = mixed(chunks[r + 1])
        finish(rows, y)


def _outproj_call(x, att, o_dn, z, mod, onw, wo, n2, rw):
    bsz, seq, d = x.shape
    t = OUT_TILE
    assert seq % t == 0
    full = lambda shape: pl.BlockSpec(shape, lambda b, j: (0,) * len(shape))
    return pl.pallas_call(
        _outproj_kernel,
        grid=(bsz, seq // t),
        in_specs=[pl.BlockSpec(memory_space=pl.ANY),
                  pl.BlockSpec((1, t, ATTN_Q_W), lambda b, j: (b, j, 0)),
                  pl.BlockSpec((1, t, DN_W), lambda b, j: (b, j, 0)),
                  pl.BlockSpec((1, t, DN_W), lambda b, j: (b, j, 0)),
                  full((MOD_ROWS, mod.shape[1])),
                  full((1, DN_HEAD_DIM)),
                  full(wo.shape),
                  full((1, d)),
                  full((d, LANES))],
        out_specs=[pl.BlockSpec((1, t, d), lambda b, j: (b, j, 0)),
                   pl.BlockSpec((1, t, d), lambda b, j: (b, j, 0)),
                   pl.BlockSpec((1, N_EXPERTS, t), lambda b, j: (b, 0, j))],
        out_shape=[jax.ShapeDtypeStruct((bsz, seq, d), F32),
                   jax.ShapeDtypeStruct((bsz, seq, d), BF16),
                   jax.ShapeDtypeStruct((bsz, N_EXPERTS, seq), F32)],
        scratch_shapes=[pltpu.VMEM((STREAM_BUFFERS, t, d), F32), pltpu.SemaphoreType.DMA((STREAM_BUFFERS,))],
        compiler_params=_params("arbitrary", "arbitrary"),
        name="outproj",
    )(x, att, o_dn, z, mod, onw, wo, n2, rw)


def _route_kernel(cap, afft_ref, slot_ref, gate_ref, tok_ref, bounds_ref):
    n_b, n_exp, n_t = afft_ref.shape
    n_e = n_b * n_exp
    aff = afft_ref[...].reshape(n_e, n_t)

    def enough(cand):
        return jnp.sum(jnp.where(aff >= cand, 1.0, 0.0), axis=-1, keepdims=True) >= cap

    tiny = 2.0 ** F32_MIN_EXP
    cur = jnp.full((n_e, 1), tiny, F32)
    any_normal = enough(cur)
    shift = 1 << (-F32_MIN_EXP).bit_length()
    while shift > 1:
        shift //= 2
        cand = cur * (2.0 ** shift)
        cur = jnp.where(enough(cand), cand, cur)

    def refine(_, state):
        cur, step = state
        cand = cur + step
        return jnp.where(enough(cand), cand, cur), step * 0.5

    cur, _ = lax.fori_loop(0, F32_MANTISSA_BITS, refine, (cur, cur * 0.5))
    thr = jnp.where(any_normal, cur, 0.0)
    need = cap - jnp.sum(jnp.where(aff > thr, 1.0, 0.0), axis=-1, keepdims=True)

    upper = (lax.broadcasted_iota(I32, (LANES, LANES), 0) < lax.broadcasted_iota(I32, (LANES, LANES), 1))
    upper = jnp.where(upper, 1.0, 0.0).astype(BF16)
    run = jnp.zeros((2 * n_e, 1), F32)
    lane = lax.broadcasted_iota(I32, (1, LANES), 1)
    bounds = jnp.zeros((n_e, LANES), F32)
    per_block = ROUTE_BLOCK // LANES
    for blk in range(n_t // LANES):
        sl = slice(blk * LANES, (blk + 1) * LANES)
        gt = aff[:, sl] > thr
        eq = aff[:, sl] == thr
        x = jnp.concatenate([jnp.where(gt, 1.0, 0.0), jnp.where(eq, 1.0, 0.0)], axis=0)
        cum = _dot(x.astype(BF16), upper) + run
        run = run + jnp.sum(x, axis=-1, keepdims=True)
        cum_gt = cum[0:n_e]
        cum_eq = cum[n_e:2 * n_e]
        sel = gt | (eq & (cum_eq < need))
        slot = jnp.where(sel, cum_gt + jnp.minimum(cum_eq, need), -1.0)
        slot_ref[:, :, sl] = slot.astype(I32).reshape(n_b, n_exp, LANES)
        gate_ref[:, :, sl] = jnp.where(sel, aff[:, sl], 0.0).reshape(n_b, n_exp, LANES)
        if (blk + 1) % per_block == 0:
            taken = run[0:n_e] + jnp.minimum(run[n_e:2 * n_e], need)
            bounds = jnp.where(lane == (blk + 1) // per_block, taken, bounds)
    bounds_ref[...] = bounds.astype(I32).reshape(n_b, n_exp, LANES)

    for b in range(n_b):
        stacked = jnp.concatenate([slot_ref[b].astype(F32), gate_ref[b],
                                   jnp.zeros((LANES - 2 * n_exp, n_t), F32)], axis=0)
        tok_ref[b] = stacked.T


def _route_call(afft, cap):
    bsz, n_e, n_t = afft.shape
    row = pl.BlockSpec((bsz, n_e, n_t), lambda i: (0, 0, 0))
    return pl.pallas_call(
        functools.partial(_route_kernel, cap),
        grid=(1,),
        in_specs=[row],
        out_specs=[row, row, pl.BlockSpec((bsz, n_t, LANES), lambda i: (0, 0, 0)),
                   pl.BlockSpec((bsz, n_e, LANES), lambda i: (0, 0, 0))],
        out_shape=[jax.ShapeDtypeStruct((bsz, n_e, n_t), I32),
                   jax.ShapeDtypeStruct((bsz, n_e, n_t), F32),
                   jax.ShapeDtypeStruct((bsz, n_t, LANES), F32),
                   jax.ShapeDtypeStruct((bsz, n_e, LANES), I32)],
        compiler_params=_params("arbitrary"),
        name="route",
    )(afft)


def _window_starts(cnt_ref, b, tb, n_e, cap):
    starts, fits = [], None
    for e in range(n_e):
        lo = cnt_ref[b, tb * n_e + e]
        hi = cnt_ref[b, (tb + 1) * n_e + e]
        start = jnp.minimum((lo // BF16_ROWS) * BF16_ROWS, cap - SLOT_WINDOW)
        ok = hi - start <= SLOT_WINDOW
        starts.append(pl.multiple_of(start, BF16_ROWS))
        fits = ok if fits is None else jnp.logical_and(fits, ok)
    return starts, fits


def _gather_kernel(cap, cnt_ref, slot_ref, gate_ref, h_ref, xg_ref, gs_ref):
    b = pl.program_id(0)
    n_e = slot_ref.shape[1]
    n_blk = slot_ref.shape[2]
    xg_ref[...] = jnp.zeros(xg_ref.shape, BF16)
    gs_ref[...] = jnp.zeros(gs_ref.shape, F32)

    def token_block(tb, carry):
        h_blk = h_ref[0, pl.ds(pl.multiple_of(tb * ROUTE_BLOCK, ROUTE_BLOCK), ROUTE_BLOCK), :]
        starts, fits = _window_starts(cnt_ref, b, tb, n_e, cap)

        def accumulate(window, first_rows):
            j = lax.broadcasted_iota(I32, (window, 1), 0)
            hits = [slot_ref[0, e, pl.ds(tb, 1), :] == first_rows[e] + j for e in range(n_e)]
            onehot = jnp.concatenate([jnp.where(hit, 1.0, 0.0).astype(BF16) for hit in hits], axis=0)
            rows = _dot(onehot, h_blk)
            for e in range(n_e):
                dst = pl.ds(first_rows[e], window)
                xg_ref[e, 0, dst, :] += rows[e * window:(e + 1) * window].astype(BF16)
                gate = gate_ref[0, e, pl.ds(tb, 1), :]
                gs_ref[e, 0, dst, :] += jnp.sum(jnp.where(hits[e], gate, 0.0), axis=-1, keepdims=True)

        @pl.when(fits)
        def _():
            accumulate(SLOT_WINDOW, starts)

        @pl.when(jnp.logical_not(fits))
        def _():
            accumulate(cap, [0] * n_e)

        return carry

    lax.fori_loop(0, n_blk, token_block, 0)


def _gather_call(cnt, slot, gate, h2, cap):
    bsz, n_e, n_t = slot.shape
    d = h2.shape[-1]
    n_blk = n_t // ROUTE_BLOCK
    blocked = lambda a: a.reshape(bsz, n_e, n_blk, ROUTE_BLOCK)
    row = pl.BlockSpec((1, n_e, n_blk, ROUTE_BLOCK), lambda b, cnt: (b, 0, 0, 0))
    return pl.pallas_call(
        functools.partial(_gather_kernel, cap),
        grid_spec=pltpu.PrefetchScalarGridSpec(
            num_scalar_prefetch=1,
            grid=(bsz,),
            in_specs=[row, row, pl.BlockSpec((1, n_t, d), lambda b, cnt: (b, 0, 0))],
            out_specs=[pl.BlockSpec((n_e, 1, cap, d), lambda b, cnt: (0, b, 0, 0)),
                       pl.BlockSpec((n_e, 1, cap, 1), lambda b, cnt: (0, b, 0, 0))]),
        out_shape=[jax.ShapeDtypeStruct((n_e, bsz, cap, d), BF16),
                   jax.ShapeDtypeStruct((n_e, bsz, cap, 1), F32)],
        compiler_params=_params("arbitrary"),
        name="gather",
    )(cnt, blocked(slot), blocked(gate), h2)


def _ffn_kernel(xg_ref, gs_ref, wg_ref, wu_ref, wd_ref, y_ref):
    bsz, cap, d = xg_ref.shape[1:]
    per = FFN_ROW_CHUNK // cap
    n_chunks = bsz // per
    wg = wg_ref[0].astype(BF16)
    wu = wu_ref[0].astype(BF16)
    wd = wd_ref[0].astype(BF16)

    def up(r):
        x = xg_ref[0, r * per:(r + 1) * per].reshape(per * cap, d)
        return _dot(x, wg), _dot(x, wu)

    nxt = up(0)
    for r in range(n_chunks):
        g, u = nxt
        if r + 1 < n_chunks:
            nxt = up(r + 1)
        y = _dot((_silu(g) * u).astype(BF16), wd) * gs_ref[0, r * per:(r + 1) * per].reshape(per * cap, 1)
        y_ref[0, r * per:(r + 1) * per] = y.astype(BF16).reshape(per, cap, d)


def _ffn_call(xg, gs, w_gate, w_up, w_down):
    n_e, bsz, cap, d = xg.shape
    ff = w_gate.shape[-1]
    rows = pl.BlockSpec((1, bsz, cap, d), lambda e: (e, 0, 0, 0))
    return pl.pallas_call(
        _ffn_kernel,
        grid=(n_e,),
        in_specs=[rows,
                  pl.BlockSpec((1, bsz, cap, 1), lambda e: (e, 0, 0, 0)),
                  pl.BlockSpec((1, d, ff), lambda e: (e, 0, 0)),
                  pl.BlockSpec((1, d, ff), lambda e: (e, 0, 0)),
                  pl.BlockSpec((1, ff, d), lambda e: (e, 0, 0))],
        out_specs=rows,
        out_shape=jax.ShapeDtypeStruct((n_e, bsz, cap, d), BF16),
        compiler_params=_params("arbitrary"),
        name="ffn",
    )(xg, gs, w_gate, w_up, w_down)


def _combine_kernel(cap, cnt_ref, x1_ref, tok_ref, y_ref, mod_ref, o_ref):
    d = x1_ref.shape[-1]
    n_e = y_ref.shape[0]
    b = pl.program_id(0)
    tb = pl.program_id(1)
    gate2 = mod_ref[pl.ds(b, 1), 5 * d:6 * d]
    tok = tok_ref[0]
    starts, fits = _window_starts(cnt_ref, b, tb, n_e, cap)

    @pl.when(fits)
    def _():
        lane = lax.broadcasted_iota(I32, (1, LANES), 1)
        first = lane < SLOT_WINDOW
        groups = []
        for e in range(0, n_e, 2):
            slot = jnp.where(first, tok[:, e:e + 1], tok[:, e + 1:e + 2])
            target = jnp.where(first, starts[e] + lane, starts[e + 1] + lane - SLOT_WINDOW).astype(F32)
            groups.append(jnp.where(slot == target, 1.0, 0.0).astype(BF16))
        onehot = jnp.concatenate(groups, axis=1)
        rows = jnp.concatenate([y_ref[e, 0, pl.ds(starts[e], SLOT_WINDOW), :] for e in range(n_e)], axis=0)
        o_ref[0] = x1_ref[0] + gate2 * _dot(onehot, rows)

    @pl.when(jnp.logical_not(fits))
    def _():
        j = lax.broadcasted_iota(I32, (1, cap), 1).astype(F32)
        acc = None
        for e in range(n_e):
            onehot = jnp.where(tok[:, e:e + 1] == j, 1.0, 0.0).astype(BF16)
            part = _dot(onehot, y_ref[e, 0])
            acc = part if acc is None else acc + part
        o_ref[0] = x1_ref[0] + gate2 * acc


def _combine_call(cnt, x1, tok, y, mod, cap):
    bsz, seq, d = x1.shape
    n_e = y.shape[0]
    t = ROUTE_BLOCK
    assert 2 * SLOT_WINDOW == LANES and n_e % 2 == 0
    return pl.pallas_call(
        functools.partial(_combine_kernel, cap),
        grid_spec=pltpu.PrefetchScalarGridSpec(
            num_scalar_prefetch=1,
            grid=(bsz, seq // t),
            in_specs=[pl.BlockSpec((1, t, d), lambda b, j, cnt: (b, j, 0)),
                      pl.BlockSpec((1, t, LANES), lambda b, j, cnt: (b, j, 0)),
                      pl.BlockSpec((n_e, 1, cap, d), lambda b, j, cnt: (0, b, 0, 0)),
                      pl.BlockSpec((MOD_ROWS, mod.shape[1]), lambda b, j, cnt: (0, 0))],
            out_specs=pl.BlockSpec((1, t, d), lambda b, j, cnt: (b, j, 0))),
        out_shape=jax.ShapeDtypeStruct((bsz, seq, d), F32),
        compiler_params=_params("arbitrary", "arbitrary"),
        name="combine",
    )(cnt, x1, tok, y, mod)


def _rope_tables(seq):
    m = ATTN_HEAD_DIM // 4
    pos = jnp.arange(seq, dtype=jnp.int32)
    rows = (pos // GRID_W).astype(F32)
    cols = (pos % GRID_W).astype(F32)
    freqs = ROPE_BASE ** (-jnp.arange(m, dtype=F32) / m)
    ang_r = rows[:, None] * freqs[None, :]
    ang_c = cols[:, None] * freqs[None, :]
    cos_h = jnp.concatenate([jnp.cos(ang_r), jnp.cos(ang_r), jnp.cos(ang_c), jnp.cos(ang_c)], axis=-1)
    sin_h = jnp.concatenate([-jnp.sin(ang_r), jnp.sin(ang_r), -jnp.sin(ang_c), jnp.sin(ang_c)], axis=-1)
    reps = LANES // ATTN_HEAD_DIM
    return jnp.tile(cos_h, (1, reps)), jnp.tile(sin_h, (1, reps))


def _lane_row(values, offset):
    return jnp.zeros((1, LANES), F32).at[0, offset:offset + values.shape[0]].set(values.astype(F32))


def kernel(x, c, ctx, c_ctx, w_mod, b_mod, norm1_w, norm2_w, w_in, q_norm_w, k_norm_w, conv_w, a_log, dt_bias,
           o_norm_w, w_out, router_w, w_gate, w_up, w_down):
    bsz, seq, d = x.shape
    n_ctx = ctx.shape[1]
    assert w_mod.shape[0] == 1, "single layer: the last layer's context outputs are never consumed"
    assert bsz < MOD_ROWS and N_EXPERTS == router_w.shape[-1]
    cap = EC_CAPACITY_FACTOR * seq // N_EXPERTS

    cc = jnp.concatenate([c, c_ctx[None, :], jnp.zeros((MOD_ROWS - bsz - 1, d), F32)], axis=0)
    mod = _mod_call(cc, w_mod[0], b_mod[0][None, :])

    w_pad = jnp.concatenate([w_in[0].astype(BF16), jnp.zeros((d, LANES - N_GATES), BF16)], axis=1)
    seg = np.arange(MXU_DIM) // ATTN_HEAD_DIM
    bd = jnp.asarray(seg[:, None] == seg[None, :], BF16)
    qkw = jnp.concatenate([jnp.tile(q_norm_w[0], ATTN_HEADS), jnp.tile(k_norm_w[0], ATTN_KV_HEADS)])[None, :]
    alog_l = _lane_row(a_log[0].reshape(-1), N_GATES // 2)
    dtb_l = _lane_row(dt_bias[0].reshape(-1), N_GATES // 2)
    shared = (mod, norm1_w[0][None, :], w_pad, bd, qkw, alog_l, dtb_l)
    kv_lat, dn_lat, g_lat, gt_lat, q, z = _inproj_call(x, bsz, *shared, rope=_rope_tables(seq))
    kv_ctx, dn_ctx, g_ctx, gt_ctx = _inproj_call(ctx.reshape(1, bsz * n_ctx, d), bsz, *shared)

    att = _attn_call(q, kv_ctx, kv_lat, n_ctx)
    o_dn = _dn_call(dn_ctx, dn_lat, conv_w[0], g_ctx, g_lat, gt_ctx, gt_lat, n_ctx)

    rw = jnp.pad(router_w[0], ((0, 0), (0, LANES - N_EXPERTS))).astype(BF16)
    x1, h2, afft = _outproj_call(x, att, o_dn, z, mod, o_norm_w[0][None, :], w_out[0].astype(BF16),
                                 norm2_w[0][None, :], rw)

    slot, gate, tok, bounds = _route_call(afft, cap)
    n_blk = seq // ROUTE_BLOCK
    cnt = bounds[:, :, 0:n_blk + 1].transpose(0, 2, 1).reshape(bsz, (n_blk + 1) * N_EXPERTS)
    xg, gs = _gather_call(cnt, slot, gate, h2, cap)
    y = _ffn_call(xg, gs, w_gate[0], w_up[0], w_down[0])
    return _combine_call(cnt, x1, tok, y, mod, cap)
```
